```python
import math
import jax, jax.numpy as jnp
from jax import lax
import numpy as np

D_MODEL = 1024
BATCH = 8
SEQ = 2048
DEPTH = 1

D_MIX = D_MODEL
MOBA_HEADS = 8
MOBA_HEAD_DIM = 64
D_MOBA = MOBA_HEADS * MOBA_HEAD_DIM
GMLP_GROUPS = 8
GMLP_GROUP_DIM = 64
D_GMLP = GMLP_GROUPS * GMLP_GROUP_DIM
D_IN_PROJ = 3 * D_MOBA + 2 * D_GMLP

MOBA_BLOCK = 256
MOBA_TOPK = 3
Q_CHUNK = 128
GMLP_CHUNK = 128

NUM_BUCKETS = 32
REL_MAX_DISTANCE = 128

D_FF = -(-(8 * D_MODEL) // (3 * 256)) * 256
EPS = 1e-6
NEG = -1e9

kernel_name = 'hybrid_moba_chunked_gmlp_block'


def rms_norm(x, g):
    xf = x.astype(jnp.float32)
    y = xf * lax.rsqrt(jnp.mean(xf * xf, axis=-1, keepdims=True) + EPS)
    return (y * g.astype(jnp.float32)).astype(x.dtype)


def layer_norm(x, g, b):
    xf = x.astype(jnp.float32)
    mu = jnp.mean(xf, axis=-1, keepdims=True)
    var = jnp.mean(jnp.square(xf - mu), axis=-1, keepdims=True)
    y = (xf - mu) * lax.rsqrt(var + EPS)
    return (y * g.astype(jnp.float32) + b.astype(jnp.float32)).astype(x.dtype)


def t5_bucket(dist):
    n = jnp.maximum(dist, 0)
    max_exact = NUM_BUCKETS // 2
    nf = jnp.maximum(n, max_exact).astype(jnp.float32)
    large = max_exact + (jnp.log(nf / max_exact) / math.log(REL_MAX_DISTANCE / max_exact)
                         * (NUM_BUCKETS - max_exact)).astype(jnp.int32)
    large = jnp.minimum(large, NUM_BUCKETS - 1)
    return jnp.where(n < max_exact, n, large)


def moba_attention(q, k, v, rel_bias):
    B, S, H, D = q.shape
    S_pad = -(-S // MOBA_BLOCK) * MOBA_BLOCK
    pad = S_pad - S
    q, k, v = [jnp.pad(t, ((0, 0), (0, pad), (0, 0), (0, 0))).transpose(0, 2, 1, 3)
               for t in (q, k, v)]
    NB = S_pad // MOBA_BLOCK
    k_blk = k.reshape(B, H, NB, MOBA_BLOCK, D)
    v_blk = v.reshape(B, H, NB, MOBA_BLOCK, D)
    k_mean = jnp.mean(k_blk.astype(jnp.float32), axis=3)
    gate = jnp.einsum('bhsd,bhnd->bhsn', q.astype(jnp.float32), k_mean)
    q_block = jnp.arange(S_pad) // MOBA_BLOCK
    past = jnp.arange(NB)[None, :] < q_block[:, None]
    gate = jnp.where(past[None, None], gate, NEG)
    topk = min(MOBA_TOPK, NB)
    _, top_idx = lax.top_k(gate, topk)
    valid = jnp.arange(topk)[None, :] < q_block[:, None]
    bias_hb = rel_bias.T.astype(jnp.float32)
    NC = S_pad // Q_CHUNK
    scale = D ** -0.5
    h_ix = jnp.arange(H)[:, None, None]
    l_idx = jnp.arange(MOBA_BLOCK)

    def step(idx):
        b, c = idx
        q0 = c * Q_CHUNK
        q_c = lax.dynamic_slice(q, (b, 0, q0, 0), (1, H, Q_CHUNK, D))[0]
        sel = lax.dynamic_slice(top_idx, (b, 0, q0, 0), (1, H, Q_CHUNK, topk))[0]
        ok = lax.dynamic_slice(valid, (q0, 0), (Q_CHUNK, topk))
        kb = k_blk[b]
        vb = v_blk[b]
        k_sel = kb[h_ix, sel]
        v_sel = vb[h_ix, sel]
        own = q0 // MOBA_BLOCK
        k_own = lax.dynamic_index_in_dim(kb, own, axis=1, keepdims=False)
        v_own = lax.dynamic_index_in_dim(vb, own, axis=1, keepdims=False)
        q_pos = q0 + jnp.arange(Q_CHUNK)
        pos_sel = sel[..., None] * MOBA_BLOCK + l_idx
        dist_sel = q_pos[None, :, None, None] - pos_sel
        b_sel = bias_hb[h_ix[..., None], t5_bucket(dist_sel)]
        dist_own = q_pos[:, None] - (own * MOBA_BLOCK + l_idx)[None, :]
        b_own = bias_hb[:, t5_bucket(dist_own)]
        l_sel = jnp.einsum('hqd,hqkld->hqkl', q_c, k_sel).astype(jnp.float32) * scale + b_sel
        l_sel = jnp.where(ok[None, :, :, None], l_sel, NEG)
        l_own = jnp.einsum('hqd,hld->hql', q_c, k_own).astype(jnp.float32) * scale + b_own
        l_own = jnp.where((dist_own >= 0)[None], l_own, NEG)
        logits = jnp.concatenate([l_sel.reshape(H, Q_CHUNK, topk * MOBA_BLOCK), l_own], axis=-1)
        p = jax.nn.softmax(logits, axis=-1).astype(v.dtype)
        p_sel = p[..., :topk * MOBA_BLOCK].reshape(H, Q_CHUNK, topk, MOBA_BLOCK)
        p_own = p[..., topk * MOBA_BLOCK:]
        return (jnp.einsum('hqkl,hqkld->hqd', p_sel, v_sel)
                + jnp.einsum('hql,hld->hqd', p_own, v_own))

    b_ids = jnp.repeat(jnp.arange(B), NC)
    c_ids = jnp.tile(jnp.arange(NC), B)
    out = lax.map(step, (b_ids, c_ids))
    out = out.reshape(B, NC, H, Q_CHUNK, D).transpose(0, 1, 3, 2, 4).reshape(B, S_pad, H * D)
    return out[:, :S]


def chunked_gmlp(u, z, ln_g, ln_b, w_spatial, b_spatial):
    B, S, _ = z.shape
    u = jax.nn.gelu(u)
    z = layer_norm(jax.nn.gelu(z), ln_g, ln_b)
    n_ch = S // GMLP_CHUNK
    z = z.reshape(B, n_ch, GMLP_CHUNK, GMLP_GROUPS, GMLP_GROUP_DIM)
    causal = jnp.tril(jnp.ones((GMLP_CHUNK, GMLP_CHUNK), dtype=z.dtype))
    w = w_spatial.astype(z.dtype) * causal
    s = jnp.einsum('gts,bcsgd->bctgd', w, z) + b_spatial.T.astype(z.dtype)[:, :, None]
    return u * s.reshape(B, S, D_GMLP)


def setup_inputs(seed: int = 0) -> dict:
    key = jax.random.key(seed)
    ks = jax.random.split(key, 16)

    def nrm(k, shape, scale):
        return jax.random.normal(k, shape, jnp.float32) * scale

    return {
        'x': nrm(ks[0], (BATCH, SEQ, D_MODEL), 1.0),
        'rel_bias': nrm(ks[1], (NUM_BUCKETS, MOBA_HEADS), 0.3),
        'norm_mix': 1.0 + nrm(ks[2], (DEPTH, D_MODEL), 0.05),
        'w_in': nrm(ks[3], (DEPTH, D_MODEL, D_IN_PROJ), D_MODEL ** -0.5),
        'gmlp_ln_g': 1.0 + nrm(ks[4], (DEPTH, D_GMLP), 0.05),
        'gmlp_ln_b': nrm(ks[5], (DEPTH, D_GMLP), 0.02),
        'w_spatial': nrm(ks[6], (DEPTH, GMLP_GROUPS, GMLP_CHUNK, GMLP_CHUNK), GMLP_CHUNK ** -0.5),
        'b_spatial': 1.0 + nrm(ks[7], (DEPTH, GMLP_GROUPS, GMLP_CHUNK), 0.1),
        'out_norm_b': 1.0 + nrm(ks[8], (DEPTH, D_MOBA), 0.05),
        'out_norm_a': 1.0 + nrm(ks[9], (DEPTH, D_GMLP), 0.05),
        'w_out': nrm(ks[10], (DEPTH, D_MIX, D_MODEL), D_MIX ** -0.5),
        'norm_ffn': 1.0 + nrm(ks[11], (DEPTH, D_MODEL), 0.05),
        'w_gate': nrm(ks[12], (DEPTH, D_MODEL, D_FF), D_MODEL ** -0.5),
        'w_up': nrm(ks[13], (DEPTH, D_MODEL, D_FF), D_MODEL ** -0.5),
        'w_down': nrm(ks[14], (DEPTH, D_FF, D_MODEL), D_FF ** -0.5),
        'norm_final': 1.0 + nrm(ks[15], (D_MODEL,), 0.05),
    }


def reference(x, rel_bias, norm_mix, w_in, gmlp_ln_g, gmlp_ln_b, w_spatial, b_spatial,
              out_norm_b, out_norm_a, w_out, norm_ffn, w_gate, w_up, w_down, norm_final):
    B, S, _ = x.shape
    for l in range(DEPTH):
        h = rms_norm(x, norm_mix[l])
        proj = h @ w_in[l]
        q, k, v, u, z = jnp.split(
            proj, [D_MOBA, 2 * D_MOBA, 3 * D_MOBA, 3 * D_MOBA + D_GMLP], axis=-1)
        shp = (B, S, MOBA_HEADS, MOBA_HEAD_DIM)
        y_b = moba_attention(q.reshape(shp), k.reshape(shp), v.reshape(shp), rel_bias)
        y_a = chunked_gmlp(u, z, gmlp_ln_g[l], gmlp_ln_b[l], w_spatial[l], b_spatial[l])
        y = jnp.concatenate([rms_norm(y_b, out_norm_b[l]), rms_norm(y_a, out_norm_a[l])], axis=-1)
        x = x + y @ w_out[l]
        h = rms_norm(x, norm_ffn[l])
        x = x + (jax.nn.silu(h @ w_gate[l]) * (h @ w_up[l])) @ w_down[l]
    return rms_norm(x, norm_final)
```

```python
import functools
import math

import numpy as np
import jax
import jax.numpy as jnp
from jax import lax
from jax.experimental import pallas as pl
from jax.experimental.pallas import tpu as pltpu

F32 = jnp.float32
BF16 = jnp.bfloat16

MOBA_HEADS = 8
MOBA_HEAD_DIM = 64
D_MOBA = MOBA_HEADS * MOBA_HEAD_DIM
GMLP_GROUPS = 8
GMLP_GROUP_DIM = 64
D_GMLP = GMLP_GROUPS * GMLP_GROUP_DIM
MOBA_BLOCK = 256
MOBA_TOPK = 3
GMLP_CHUNK = 128
NUM_BUCKETS = 32
REL_MAX_DISTANCE = 128
EPS = 1e-6
NEG = -1e9

LANES = 128
HEADS_PER_STEP = LANES // MOBA_HEAD_DIM
ONES_ROWS = 16
VMEM_LIMIT = 56 * 1024 * 1024

LOG2E = math.log2(math.e)
Q_SCALE = MOBA_HEAD_DIM ** -0.5 * LOG2E


def _t5_bucket_np(dist):
    n = np.maximum(dist, 0)
    max_exact = NUM_BUCKETS // 2
    nf = np.maximum(n, max_exact).astype(np.float32)
    ratio = np.log(nf / np.float32(max_exact)) / np.float32(math.log(REL_MAX_DISTANCE / max_exact))
    large = max_exact + (ratio * np.float32(NUM_BUCKETS - max_exact)).astype(np.int32)
    large = np.minimum(large, NUM_BUCKETS - 1)
    return np.where(n < max_exact, n, large).astype(np.int32)


def _bias_kernel(bucket_ref, rel_ref, out_ref):
    h = pl.program_id(0)
    L = MOBA_BLOCK
    key = lax.broadcasted_iota(jnp.int32, (L, L), 0)
    qry = lax.broadcasted_iota(jnp.int32, (L, L), 1)
    for delta in range(2):
        bucket = bucket_ref[delta]
        acc = jnp.zeros((L, L), F32)
        for b in range(NUM_BUCKETS):
            acc = jnp.where(bucket == b, rel_ref[b, h] * LOG2E, acc)
        if delta == 0:
            acc = jnp.where(key > qry, NEG, acc)
        out_ref[0, delta] = acc


def _bias_tiles(rel_bias):
    L = MOBA_BLOCK
    kk = np.arange(L)[:, None]
    qq = np.arange(L)[None, :]
    bucket = np.stack([_t5_bucket_np(d * L + qq - kk) for d in range(2)]).astype(np.int32)
    return pl.pallas_call(
        _bias_kernel,
        grid=(MOBA_HEADS,),
        in_specs=[
            pl.BlockSpec((2, L, L), lambda h: (0, 0, 0)),
            pl.BlockSpec(memory_space=pltpu.SMEM),
        ],
        out_specs=pl.BlockSpec((1, 2, L, L), lambda h: (h, 0, 0, 0)),
        out_shape=jax.ShapeDtypeStruct((MOBA_HEADS, 2, L, L), F32),
        name="bias_tiles",
    )(jnp.asarray(bucket), rel_bias)


def _inproj_kernel(x_ref, g_ref, w_ref, qkv_ref, uz_ref):
    x = x_ref[...]
    ms = jnp.mean(x * x, axis=-1, keepdims=True)
    h = (x * lax.rsqrt(ms + EPS) * g_ref[...]).astype(BF16)
    n_qkv = 3 * D_MOBA
    qkv = jnp.dot(h, w_ref[:, :n_qkv], preferred_element_type=F32)
    qkv_ref[:, :D_MOBA] = (qkv[:, :D_MOBA] * Q_SCALE).astype(BF16)
    qkv_ref[:, D_MOBA:] = qkv[:, D_MOBA:].astype(BF16)
    uz_ref[...] = jnp.dot(h, w_ref[:, n_qkv:], preferred_element_type=F32)


def _inproj(x2, g, w_bf, tm):
    T, D = x2.shape
    n_all = w_bf.shape[1]
    n_qkv = 3 * D_MOBA
    return pl.pallas_call(
        _inproj_kernel,
        grid=(T // tm,),
        in_specs=[
            pl.BlockSpec((tm, D), lambda i: (i, 0)),
            pl.BlockSpec((1, D), lambda i: (0, 0)),
            pl.BlockSpec((D, n_all), lambda i: (0, 0), pipeline_mode=pl.Buffered(1)),
        ],
        out_specs=[
            pl.BlockSpec((tm, n_qkv), lambda i: (i, 0)),
            pl.BlockSpec((tm, n_all - n_qkv), lambda i: (i, 0)),
        ],
        out_shape=[
            jax.ShapeDtypeStruct((T, n_qkv), BF16),
            jax.ShapeDtypeStruct((T, n_all - n_qkv), F32),
        ],
        compiler_params=pltpu.CompilerParams(
            dimension_semantics=("arbitrary",), vmem_limit_bytes=VMEM_LIMIT),
        name="inproj",
    )(x2, g, w_bf)


def _dot_nt(a, b):
    return lax.dot_general(a, b, (((1,), (1,)), ((), ())), preferred_element_type=F32)


def _moba_kernel(q_ref, k_ref, v_ref, bias_ref, cfar_ref, o_ref, t_scr, p_scr, vt_scr):
    L = MOBA_BLOCK
    S = q_ref.shape[1]
    NB = S // L
    hp = pl.program_id(1)
    lane = lax.broadcasted_iota(jnp.int32, (1, LANES), 1)
    head_lanes = [(lane >= hh * MOBA_HEAD_DIM) & (lane < (hh + 1) * MOBA_HEAD_DIM)
                  for hh in range(HEADS_PER_STEP)]

    vt_scr[0:LANES, :] = v_ref[0].astype(F32).T.astype(BF16)
    vt_scr[LANES:LANES + ONES_ROWS, :] = jnp.ones((ONES_ROWS, S), BF16)

    km = jnp.concatenate(
        [jnp.sum(k_ref[0, n * L:(n + 1) * L, :].astype(F32), axis=0, keepdims=True) * (1.0 / L)
         for n in range(NB)], axis=0)
    kmm = jnp.concatenate([jnp.where(head_lanes[hh], km, 0.0) for hh in range(HEADS_PER_STEP)], axis=0)
    km_hi = kmm.astype(BF16)
    km_lo = (kmm - km_hi.astype(F32)).astype(BF16)

    row = lax.broadcasted_iota(jnp.int32, (NB, L), 0)

    for i in range(NB):
        q_i = q_ref[0, i * L:(i + 1) * L, :]
        nk = (i + 1) * L
        if i > MOBA_TOPK:
            gate = _dot_nt(km_hi, q_i) + _dot_nt(km_lo, q_i)
        outs = []
        for hh in range(HEADS_PER_STEP):
            cfar = cfar_ref[hp * HEADS_PER_STEP + hh]
            if i > MOBA_TOPK:
                g = gate[hh * NB:(hh + 1) * NB]
                cnt = jnp.zeros((NB, L), jnp.int32)
                for m in range(i):
                    gm = g[m:m + 1, :]
                    beats = (gm > g) | ((gm == g) & (m < row))
                    cnt = cnt + jnp.where(beats, 1, 0)
                sel = cnt < MOBA_TOPK
                m_far = jnp.where(sel, cfar, NEG)
                m_prev = jnp.where(sel, 0.0, NEG)
            qm = jnp.where(head_lanes[hh], q_i, jnp.zeros_like(q_i))
            st = _dot_nt(k_ref[0, 0:nk, :], qm)
            mx = None
            for j in range(i + 1):
                t = st[j * L:(j + 1) * L]
                if j == i:
                    t = t + bias_ref[hh, 0]
                elif j == i - 1:
                    t = t + bias_ref[hh, 1]
                    if i > MOBA_TOPK:
                        t = t + m_prev[j:j + 1, :]
                else:
                    t = t + (m_far[j:j + 1, :] if i > MOBA_TOPK else cfar)
                t_scr[j * L:(j + 1) * L, :] = t
                tm = jnp.max(t.reshape(L // 8, 8, L), axis=0)
                mx = tm if mx is None else jnp.maximum(mx, tm)
            mrow = jnp.max(mx, axis=0, keepdims=True)
            for j in range(i + 1):
                p_scr[j * L:(j + 1) * L, :] = jnp.exp2(t_scr[j * L:(j + 1) * L, :] - mrow).astype(BF16)
            acc = jnp.dot(vt_scr[:, 0:nk], p_scr[0:nk, :], preferred_element_type=F32)
            outs.append(acc[hh * MOBA_HEAD_DIM:(hh + 1) * MOBA_HEAD_DIM] / acc[LANES:LANES + 1])
        o_ref[0, i * L:(i + 1) * L, :] = jnp.concatenate(outs, axis=0).T


def _moba(qkv3, bias_t, cfar):
    B, S, _ = qkv3.shape
    L = MOBA_BLOCK
    n_hp = MOBA_HEADS // HEADS_PER_STEP
    blk = lambda off: pl.BlockSpec((1, S, LANES), lambda b, hp: (b, 0, off + hp))
    return pl.pallas_call(
        _moba_kernel,
        grid=(B, n_hp),
        in_specs=[
            blk(0), blk(n_hp), blk(2 * n_hp),
            pl.BlockSpec((HEADS_PER_STEP, 2, L, L), lambda b, hp: (hp, 0, 0, 0)),
            pl.BlockSpec(memory_space=pltpu.SMEM),
        ],
        out_specs=pl.BlockSpec((1, S, LANES), lambda b, hp: (b, 0, hp)),
        out_shape=jax.ShapeDtypeStruct((B, S, D_MOBA), F32),
        scratch_shapes=[
            pltpu.VMEM((S, L), F32),
            pltpu.VMEM((S, L), BF16),
            pltpu.VMEM((LANES + ONES_ROWS, S), BF16),
        ],
        compiler_params=pltpu.CompilerParams(
            dimension_semantics=("arbitrary", "arbitrary"), vmem_limit_bytes=VMEM_LIMIT),
        name="moba",
    )(qkv3, qkv3, qkv3, bias_t, cfar)


def _gmlp_kernel(u_ref, z_ref, lng_ref, lnb_ref, w_ref, bsp_ref, gn_ref, o_ref):
    C = GMLP_CHUNK
    tm = u_ref.shape[0]
    z = jax.nn.gelu(z_ref[...])
    mu = jnp.mean(z, axis=-1, keepdims=True)
    zc = z - mu
    var = jnp.mean(zc * zc, axis=-1, keepdims=True)
    zn = (zc * lax.rsqrt(var + EPS) * lng_ref[...] + lnb_ref[...]).astype(BF16)

    t_ix = lax.broadcasted_iota(jnp.int32, (C, C), 0)
    s_ix = lax.broadcasted_iota(jnp.int32, (C, C), 1)
    w = [jnp.where(t_ix >= s_ix, w_ref[g], 0.0).astype(BF16) for g in range(GMLP_GROUPS)]
    lane = lax.broadcasted_iota(jnp.int32, (1, LANES), 1)
    first = lane < GMLP_GROUP_DIM

    for c in range(tm // C):
        cols = []
        for gp in range(D_GMLP // LANES):
            zp = zn[c * C:(c + 1) * C, gp * LANES:(gp + 1) * LANES]
            r0 = jnp.dot(w[2 * gp], zp, preferred_element_type=F32)
            r1 = jnp.dot(w[2 * gp + 1], zp, preferred_element_type=F32)
            cols.append(jnp.where(first, r0, r1))
        s = jnp.concatenate(cols, axis=1) + bsp_ref[...]
        y = jax.nn.gelu(u_ref[c * C:(c + 1) * C, :]) * s
        ms = jnp.mean(y * y, axis=-1, keepdims=True)
        o_ref[c * C:(c + 1) * C, :] = (y * lax.rsqrt(ms + EPS) * gn_ref[...]).astype(BF16)


def _gmlp(uz, ln_g, ln_b, w_sp, b_sp_full, gn, tm):
    T = uz.shape[0]
    C = GMLP_CHUNK
    vec = pl.BlockSpec((1, D_GMLP), lambda i: (0, 0))
    return pl.pallas_call(
        _gmlp_kernel,
        grid=(T // tm,),
        in_specs=[
            pl.BlockSpec((tm, D_GMLP), lambda i: (i, 0)),
            pl.BlockSpec((tm, D_GMLP), lambda i: (i, 1)),
            vec, vec,
            pl.BlockSpec((GMLP_GROUPS, C, C), lambda i: (0, 0, 0)),
            pl.BlockSpec((C, D_GMLP), lambda i: (0, 0)),
            vec,
        ],
        out_specs=pl.BlockSpec((tm, D_GMLP), lambda i: (i, 0)),
        out_shape=jax.ShapeDtypeStruct((T, D_GMLP), BF16),
        compiler_params=pltpu.CompilerParams(
            dimension_semantics=("arbitrary",), vmem_limit_bytes=VMEM_LIMIT),
        name="gmlp",
    )(uz, uz, ln_g, ln_b, w_sp, b_sp_full, gn)


def _ffn_kernel(x_ref, yb_ref, ya_ref, gb_ref, wo_ref, gf_ref, wg_ref, wu_ref, wd_ref, gl_ref,
                o_ref, *, final_norm):
    yb = yb_ref[...]
    ms = jnp.mean(yb * yb, axis=-1, keepdims=True)
    ybn = (yb * lax.rsqrt(ms + EPS) * gb_ref[...]).astype(BF16)
    x1 = (x_ref[...]
          + jnp.dot(ybn, wo_ref[:D_MOBA, :], preferred_element_type=F32)
          + jnp.dot(ya_ref[...], wo_ref[D_MOBA:, :], preferred_element_type=F32))
    ms = jnp.mean(x1 * x1, axis=-1, keepdims=True)
    h = (x1 * lax.rsqrt(ms + EPS) * gf_ref[...]).astype(BF16)
    g = jnp.dot(h, wg_ref[...], preferred_element_type=F32)
    u = jnp.dot(h, wu_ref[...], preferred_element_type=F32)
    a = (jax.nn.silu(g) * u).astype(BF16)
    x2 = x1 + jnp.dot(a, wd_ref[...], preferred_element_type=F32)
    if final_norm:
        ms = jnp.mean(x2 * x2, axis=-1, keepdims=True)
        x2 = x2 * lax.rsqrt(ms + EPS) * gl_ref[...]
    o_ref[...] = x2


def _ffn(x2, yb, ya, gb, wo, gf, wg, wu, wd, gl, tm, final_norm):
    T, D = x2.shape
    d_ff = wg.shape[1]
    const = lambda shape: pl.BlockSpec(shape, lambda i: (0, 0), pipeline_mode=pl.Buffered(1))
    return pl.pallas_call(
        functools.partial(_ffn_kernel, final_norm=final_norm),
        grid=(T // tm,),
        in_specs=[
            pl.BlockSpec((tm, D), lambda i: (i, 0)),
            pl.BlockSpec((tm, D_MOBA), lambda i: (i, 0)),
            pl.BlockSpec((tm, D_GMLP), lambda i: (i, 0)),
            pl.BlockSpec((1, D_MOBA), lambda i: (0, 0)),
            const((D_MOBA + D_GMLP, D)),
            pl.BlockSpec((1, D), lambda i: (0, 0)),
            const((D, d_ff)), const((D, d_ff)), const((d_ff, D)),
            pl.BlockSpec((1, D), lambda i: (0, 0)),
        ],
        out_specs=pl.BlockSpec((tm, D), lambda i: (i, 0)),
        out_shape=jax.ShapeDtypeStruct((T, D), F32),
        compiler_params=pltpu.CompilerParams(
            dimension_semantics=("arbitrary",), vmem_limit_bytes=VMEM_LIMIT),
        name="outproj_ffn",
    )(x2, yb, ya, gb, wo, gf, wg, wu, wd, gl)


def kernel(x, rel_bias, norm_mix, w_in, gmlp_ln_g, gmlp_ln_b, w_spatial, b_spatial, out_norm_b,
           out_norm_a, w_out, norm_ffn, w_gate, w_up, w_down, norm_final):
    B, S, D = x.shape
    depth = w_in.shape[0]
    assert S % MOBA_BLOCK == 0 and S // MOBA_BLOCK > MOBA_TOPK
    far = _t5_bucket_np(np.arange(MOBA_BLOCK + 1, S + MOBA_BLOCK))
    assert (far == far[0]).all()
    far_bucket = int(far[0])

    tm = 512
    T = B * S
    x2 = x.reshape(T, D)
    bias_t = _bias_tiles(rel_bias)
    cfar = rel_bias[far_bucket, :] * LOG2E
    row = lambda v: v.reshape(1, -1)
    for l in range(depth):
        qkv, uz = _inproj(x2, row(norm_mix[l]), w_in[l].astype(BF16), tm)
        yb = _moba(qkv.reshape(B, S, -1), bias_t, cfar).reshape(T, D_MOBA)
        b_sp_full = jnp.repeat(b_spatial[l].T, GMLP_GROUP_DIM, axis=1)
        ya = _gmlp(uz, row(gmlp_ln_g[l]), row(gmlp_ln_b[l]), w_spatial[l], b_sp_full,
                   row(out_norm_a[l]), tm)
        x2 = _ffn(x2, yb, ya, row(out_norm_b[l]), w_out[l].astype(BF16), row(norm_ffn[l]),
                  w_gate[l].astype(BF16), w_up[l].astype(BF16), w_down[l].astype(BF16),
                  row(norm_final), tm, l == depth - 1)
    return x2.reshape(B, S, D)
```

```python
import functools
import math

import numpy as np
import jax
import jax.numpy as jnp
from jax import lax
from jax.experimental import pallas as pl
from jax.experimental.pallas import tpu as pltpu

F32 = jnp.float32
BF16 = jnp.bfloat16

MOBA_HEADS = 8
MOBA_HEAD_DIM = 64
D_MOBA = MOBA_HEADS * MOBA_HEAD_DIM
GMLP_GROUPS = 8
GMLP_GROUP_DIM = 64
D_GMLP = GMLP_GROUPS * GMLP_GROUP_DIM
MOBA_BLOCK = 256
MOBA_TOPK = 3
GMLP_CHUNK = 128
NUM_BUCKETS = 32
REL_MAX_DISTANCE = 128
EPS = 1e-6
NEG = -1e9

LANES = 128
HEADS_PER_STEP = LANES // MOBA_HEAD_DIM
ONES_ROWS = 16
VMEM_LIMIT = 56 * 1024 * 1024
N_SLOTS = 4

LOG2E = math.log2(math.e)
Q_SCALE = MOBA_HEAD_DIM ** -0.5 * LOG2E


def _t5_bucket_np(dist):
    n = np.maximum(dist, 0)
    max_exact = NUM_BUCKETS // 2
    nf = np.maximum(n, max_exact).astype(np.float32)
    ratio = np.log(nf / np.float32(max_exact)) / np.float32(math.log(REL_MAX_DISTANCE / max_exact))
    large = max_exact + (ratio * np.float32(NUM_BUCKETS - max_exact)).astype(np.int32)
    large = np.minimum(large, NUM_BUCKETS - 1)
    return np.where(n < max_exact, n, large).astype(np.int32)


def _bias_kernel(bucket_ref, rel_ref, out_ref):
    h = pl.program_id(0)
    L = MOBA_BLOCK
    key = lax.broadcasted_iota(jnp.int32, (L, L), 0)
    qry = lax.broadcasted_iota(jnp.int32, (L, L), 1)
    for delta in range(2):
        bucket = bucket_ref[delta]
        acc = jnp.zeros((L, L), F32)
        for b in range(NUM_BUCKETS):
            acc = jnp.where(bucket == b, rel_ref[b, h] * LOG2E, acc)
        if delta == 0:
            acc = jnp.where(key > qry, NEG, acc)
        out_ref[0, delta] = acc


def _bias_tiles(rel_bias):
    L = MOBA_BLOCK
    kk = np.arange(L)[:, None]
    qq = np.arange(L)[None, :]
    bucket = np.stack([_t5_bucket_np(d * L + qq - kk) for d in range(2)]).astype(np.int32)
    return pl.pallas_call(
        _bias_kernel,
        grid=(MOBA_HEADS,),
        in_specs=[
            pl.BlockSpec((2, L, L), lambda h: (0, 0, 0)),
            pl.BlockSpec(memory_space=pltpu.SMEM),
        ],
        out_specs=pl.BlockSpec((1, 2, L, L), lambda h: (h, 0, 0, 0)),
        out_shape=jax.ShapeDtypeStruct((MOBA_HEADS, 2, L, L), F32),
        name="bias_tiles",
    )(jnp.asarray(bucket), rel_bias)


def _inproj_kernel(x_ref, g_ref, w_ref, qkv_ref, uz_ref):
    x = x_ref[...]
    ms = jnp.mean(x * x, axis=-1, keepdims=True)
    h = (x * lax.rsqrt(ms + EPS) * g_ref[...]).astype(BF16)
    n_qkv = 3 * D_MOBA
    qkv = jnp.dot(h, w_ref[:, :n_qkv], preferred_element_type=F32)
    qkv_ref[:, :D_MOBA] = (qkv[:, :D_MOBA] * Q_SCALE).astype(BF16)
    qkv_ref[:, D_MOBA:] = qkv[:, D_MOBA:].astype(BF16)
    uz_ref[...] = jnp.dot(h, w_ref[:, n_qkv:], preferred_element_type=F32)


def _inproj(x2, g, w_bf, tm):
    T, D = x2.shape
    n_all = w_bf.shape[1]
    n_qkv = 3 * D_MOBA
    return pl.pallas_call(
        _inproj_kernel,
        grid=(T // tm,),
        in_specs=[
            pl.BlockSpec((tm, D), lambda i: (i, 0)),
            pl.BlockSpec((1, D), lambda i: (0, 0)),
            pl.BlockSpec((D, n_all), lambda i: (0, 0), pipeline_mode=pl.Buffered(1)),
        ],
        out_specs=[
            pl.BlockSpec((tm, n_qkv), lambda i: (i, 0)),
            pl.BlockSpec((tm, n_all - n_qkv), lambda i: (i, 0)),
        ],
        out_shape=[
            jax.ShapeDtypeStruct((T, n_qkv), BF16),
            jax.ShapeDtypeStruct((T, n_all - n_qkv), F32),
        ],
        compiler_params=pltpu.CompilerParams(
            dimension_semantics=("arbitrary",), vmem_limit_bytes=VMEM_LIMIT),
        name="inproj",
    )(x2, g, w_bf)


def _dot_nt(a, b):
    return lax.dot_general(a, b, (((1,), (1,)), ((), ())), preferred_element_type=F32)


def _moba_kernel(q_ref, k_ref, v_ref, bias_ref, cfar_ref, o_ref, vt_scr, *slots):
    t_scr, p_scr = slots[:len(slots) // 2], slots[len(slots) // 2:]
    L = MOBA_BLOCK
    S = q_ref.shape[1]
    NB = S // L
    hp = pl.program_id(1)
    lane = lax.broadcasted_iota(jnp.int32, (1, LANES), 1)
    head_lanes = [(lane >= hh * MOBA_HEAD_DIM) & (lane < (hh + 1) * MOBA_HEAD_DIM)
                  for hh in range(HEADS_PER_STEP)]

    vt_scr[0:LANES, :] = v_ref[0].astype(F32).T.astype(BF16)
    vt_scr[LANES:LANES + ONES_ROWS, :] = jnp.ones((ONES_ROWS, S), BF16)

    km = jnp.concatenate(
        [jnp.sum(k_ref[0, n * L:(n + 1) * L, :].astype(F32), axis=0, keepdims=True) * (1.0 / L)
         for n in range(NB)], axis=0)
    kmm = jnp.concatenate([jnp.where(head_lanes[hh], km, 0.0) for hh in range(HEADS_PER_STEP)], axis=0)
    km_hi = kmm.astype(BF16)
    km_lo = (kmm - km_hi.astype(F32)).astype(BF16)

    row = lax.broadcasted_iota(jnp.int32, (NB, L), 0)

    bodies = [(i, hh) for i in range(NB) for hh in range(HEADS_PER_STEP)]
    gates, scores, mrows, outs = {}, {}, {}, {}

    def slot_of(n):
        return n % len(t_scr)

    def qk_stage(n):
        i, hh = bodies[n]
        q_i = q_ref[0, i * L:(i + 1) * L, :]
        if i > MOBA_TOPK and i not in gates:
            gates[i] = _dot_nt(km_hi, q_i) + _dot_nt(km_lo, q_i)
        qm = jnp.where(head_lanes[hh], q_i, jnp.zeros_like(q_i))
        scores[n] = _dot_nt(k_ref[0, 0:(i + 1) * L, :], qm)

    def mask_stage(n):
        i, hh = bodies[n]
        cfar = cfar_ref[hp * HEADS_PER_STEP + hh]
        m_far = m_prev = None
        if i > MOBA_TOPK:
            g = gates[i][hh * NB:(hh + 1) * NB]
            cnt = jnp.zeros((NB, L), jnp.int32)
            for m in range(i):
                gm = g[m:m + 1, :]
                beats = (gm > g) | ((gm == g) & (m < row))
                cnt = cnt + jnp.where(beats, 1, 0)
            sel = cnt < MOBA_TOPK
            m_far = jnp.where(sel, cfar, NEG)
            m_prev = jnp.where(sel, 0.0, NEG)
        st = scores.pop(n)
        t_ref = t_scr[slot_of(n)]
        mx = None
        for j in range(i + 1):
            t = st[j * L:(j + 1) * L]
            if j == i:
                t = t + bias_ref[hh, 0]
            elif j == i - 1:
                t = t + bias_ref[hh, 1]
                if i > MOBA_TOPK:
                    t = t + m_prev[j:j + 1, :]
            else:
                t = t + (m_far[j:j + 1, :] if i > MOBA_TOPK else cfar)
            t_ref[j * L:(j + 1) * L, :] = t
            tm = jnp.max(t.reshape(L // 8, 8, L), axis=0)
            mx = tm if mx is None else jnp.maximum(mx, tm)
        mrows[n] = jnp.max(mx, axis=0, keepdims=True)

    def exp_stage(n):
        i, hh = bodies[n]
        t_ref, p_ref = t_scr[slot_of(n)], p_scr[slot_of(n)]
        mrow = mrows.pop(n)
        for j in range(i + 1):
            p_ref[j * L:(j + 1) * L, :] = jnp.exp2(t_ref[j * L:(j + 1) * L, :] - mrow).astype(BF16)

    def pv_stage(n):
        i, hh = bodies[n]
        nk = (i + 1) * L
        acc = jnp.dot(vt_scr[:, 0:nk], p_scr[slot_of(n)][0:nk, :], preferred_element_type=F32)
        outs[n] = acc[hh * MOBA_HEAD_DIM:(hh + 1) * MOBA_HEAD_DIM] / acc[LANES:LANES + 1]
        if hh == HEADS_PER_STEP - 1:
            pair = [outs.pop(n - HEADS_PER_STEP + 1 + h) for h in range(HEADS_PER_STEP)]
            o_ref[0, i * L:(i + 1) * L, :] = jnp.concatenate(pair, axis=0).T

    nb = len(bodies)
    qk_stage(0)
    qk_stage(1)
    for n in range(nb):
        mask_stage(n)
        if n + 2 < nb:
            qk_stage(n + 2)
        exp_stage(n)
        pv_stage(n)


def _moba(qkv3, bias_t, cfar):
    B, S, _ = qkv3.shape
    L = MOBA_BLOCK
    n_hp = MOBA_HEADS // HEADS_PER_STEP
    blk = lambda off: pl.BlockSpec((1, S, LANES), lambda b, hp: (b, 0, off + hp))
    return pl.pallas_call(
        _moba_kernel,
        grid=(B, n_hp),
        in_specs=[
            blk(0), blk(n_hp), blk(2 * n_hp),
            pl.BlockSpec((HEADS_PER_STEP, 2, L, L), lambda b, hp: (hp, 0, 0, 0)),
            pl.BlockSpec(memory_space=pltpu.SMEM),
        ],
        out_specs=pl.BlockSpec((1, S, LANES), lambda b, hp: (b, 0, hp)),
        out_shape=jax.ShapeDtypeStruct((B, S, D_MOBA), F32),
        scratch_shapes=[
            pltpu.VMEM((LANES + ONES_ROWS, S), BF16),
        ] + [pltpu.VMEM((S, L), F32)] * N_SLOTS + [pltpu.VMEM((S, L), BF16)] * N_SLOTS,
        compiler_params=pltpu.CompilerParams(
            dimension_semantics=("arbitrary", "arbitrary"), vmem_limit_bytes=VMEM_LIMIT),
        name="moba",
    )(qkv3, qkv3, qkv3, bias_t, cfar)


def _gmlp_kernel(u_ref, z_ref, lng_ref, lnb_ref, w_ref, bsp_ref, gn_ref, o_ref):
    C = GMLP_CHUNK
    tm = u_ref.shape[0]
    z = jax.nn.gelu(z_ref[...])
    mu = jnp.mean(z, axis=-1, keepdims=True)
    zc = z - mu
    var = jnp.mean(zc * zc, axis=-1, keepdims=True)
    zn = (zc * lax.rsqrt(var + EPS) * lng_ref[...] + lnb_ref[...]).astype(BF16)

    t_ix = lax.broadcasted_iota(jnp.int32, (C, C), 0)
    s_ix = lax.broadcasted_iota(jnp.int32, (C, C), 1)
    w = [jnp.where(t_ix >= s_ix, w_ref[g], 0.0).astype(BF16) for g in range(GMLP_GROUPS)]
    lane = lax.broadcasted_iota(jnp.int32, (1, LANES), 1)
    first = lane < GMLP_GROUP_DIM

    for c in range(tm // C):
        cols = []
        for gp in range(D_GMLP // LANES):
            zp = zn[c * C:(c + 1) * C, gp * LANES:(gp + 1) * LANES]
            r0 = jnp.dot(w[2 * gp], zp, preferred_element_type=F32)
            r1 = jnp.dot(w[2 * gp + 1], zp, preferred_element_type=F32)
            cols.append(jnp.where(first, r0, r1))
        s = jnp.concatenate(cols, axis=1) + bsp_ref[...]
        y = jax.nn.gelu(u_ref[c * C:(c + 1) * C, :]) * s
        ms = jnp.mean(y * y, axis=-1, keepdims=True)
        o_ref[c * C:(c + 1) * C, :] = (y * lax.rsqrt(ms + EPS) * gn_ref[...]).astype(BF16)


def _gmlp(uz, ln_g, ln_b, w_sp, b_sp_full, gn, tm):
    T = uz.shape[0]
    C = GMLP_CHUNK
    vec = pl.BlockSpec((1, D_GMLP), lambda i: (0, 0))
    return pl.pallas_call(
        _gmlp_kernel,
        grid=(T // tm,),
        in_specs=[
            pl.BlockSpec((tm, D_GMLP), lambda i: (i, 0)),
            pl.BlockSpec((tm, D_GMLP), lambda i: (i, 1)),
            vec, vec,
            pl.BlockSpec((GMLP_GROUPS, C, C), lambda i: (0, 0, 0)),
            pl.BlockSpec((C, D_GMLP), lambda i: (0, 0)),
            vec,
        ],
        out_specs=pl.BlockSpec((tm, D_GMLP), lambda i: (i, 0)),
        out_shape=jax.ShapeDtypeStruct((T, D_GMLP), BF16),
        compiler_params=pltpu.CompilerParams(
            dimension_semantics=("arbitrary",), vmem_limit_bytes=VMEM_LIMIT),
        name="gmlp",
    )(uz, uz, ln_g, ln_b, w_sp, b_sp_full, gn)


def _ffn_kernel(x_ref, yb_ref, ya_ref, gb_ref, wo_ref, gf_ref, wg_ref, wu_ref, wd_ref, gl_ref,
                o_ref, *, final_norm):
    yb = yb_ref[...]
    ms = jnp.mean(yb * yb, axis=-1, keepdims=True)
    ybn = (yb * lax.rsqrt(ms + EPS) * gb_ref[...]).astype(BF16)
    x1 = (x_ref[...]
          + jnp.dot(ybn, wo_ref[:D_MOBA, :], preferred_element_type=F32)
          + jnp.dot(ya_ref[...], wo_ref[D_MOBA:, :], preferred_element_type=F32))
    ms = jnp.mean(x1 * x1, axis=-1, keepdims=True)
    h = (x1 * lax.rsqrt(ms + EPS) * gf_ref[...]).astype(BF16)
    g = jnp.dot(h, wg_ref[...], preferred_element_type=F32)
    u = jnp.dot(h, wu_ref[...], preferred_element_type=F32)
    a = (jax.nn.silu(g) * u).astype(BF16)
    x2 = x1 + jnp.dot(a, wd_ref[...], preferred_element_type=F32)
    if final_norm:
        ms = jnp.mean(x2 * x2, axis=-1, keepdims=True)
        x2 = x2 * lax.rsqrt(ms + EPS) * gl_ref[...]
    o_ref[...] = x2


def _ffn(x2, yb, ya, gb, wo, gf, wg, wu, wd, gl, tm, final_norm):
    T, D = x2.shape
    d_ff = wg.shape[1]
    const = lambda shape: pl.BlockSpec(shape, lambda i: (0, 0), pipeline_mode=pl.Buffered(1))
    return pl.pallas_call(
        functools.partial(_ffn_kernel, final_norm=final_norm),
        grid=(T // tm,),
        in_specs=[
            pl.BlockSpec((tm, D), lambda i: (i, 0)),
            pl.BlockSpec((tm, D_MOBA), lambda i: (i, 0)),
            pl.BlockSpec((tm, D_GMLP), lambda i: (i, 0)),
            pl.BlockSpec((1, D_MOBA), lambda i: (0, 0)),
            const((D_MOBA + D_GMLP, D)),
            pl.BlockSpec((1, D), lambda i: (0, 0)),
            const((D, d_ff)), const((D, d_ff)), const((d_ff, D)),
            pl.BlockSpec((1, D), lambda i: (0, 0)),
        ],
        out_specs=pl.BlockSpec((tm, D), lambda i: (i, 0)),
        out_shape=jax.ShapeDtypeStruct((T, D), F32),
        compiler_params=pltpu.CompilerParams(
            dimension_semantics=("arbitrary",), vmem_limit_bytes=VMEM_LIMIT),
        name="outproj_ffn",
    )(x2, yb, ya, gb, wo, gf, wg, wu, wd, gl)


def kernel(x, rel_bias, norm_mix, w_in, gmlp_ln_g, gmlp_ln_b, w_spatial, b_spatial, out_norm_b,
           out_norm_a, w_out, norm_ffn, w_gate, w_up, w_down, norm_final):
    B, S, D = x.shape
    depth = w_in.shape[0]
    assert S % MOBA_BLOCK == 0 and S // MOBA_BLOCK > MOBA_TOPK
    far = _t5_bucket_np(np.arange(MOBA_BLOCK + 1, S + MOBA_BLOCK))
    assert (far == far[0]).all()
    far_bucket = int(far[0])

    tm = 512
    T = B * S
    x2 = x.reshape(T, D)
    bias_t = _bias_tiles(rel_bias)
    cfar = rel_bias[far_bucket, :] * LOG2E
    row = lambda v: v.reshape(1, -1)
    for l in range(depth):
        qkv, uz = _inproj(x2, row(norm_mix[l]), w_in[l].astype(BF16), tm)
        yb = _moba(qkv.reshape(B, S, -1), bias_t, cfar).reshape(T, D_MOBA)
        b_sp_full = jnp.repeat(b_spatial[l].T, GMLP_GROUP_DIM, axis=1)
        ya = _gmlp(uz, row(gmlp_ln_g[l]), row(gmlp_ln_b[l]), w_spatial[l], b_sp_full,
                   row(out_norm_a[l]), tm)
        x2 = _ffn(x2, yb, ya, row(out_norm_b[l]), w_out[l].astype(BF16), row(norm_ffn[l]),
                  w_gate[l].astype(BF16), w_up[l].astype(BF16), w_down[l].astype(BF16),
                  row(norm_final), tm, l == depth - 1)
    return x2.reshape(B, S, D)
```

```python
import functools
import math

import numpy as np
import jax
import jax.numpy as jnp
from jax import lax
from jax.experimental import pallas as pl
from jax.experimental.pallas import tpu as pltpu

F32 = jnp.float32
BF16 = jnp.bfloat16

MOBA_HEADS = 8
MOBA_HEAD_DIM = 64
D_MOBA = MOBA_HEADS * MOBA_HEAD_DIM
GMLP_GROUPS = 8
GMLP_GROUP_DIM = 64
D_GMLP = GMLP_GROUPS * GMLP_GROUP_DIM
MOBA_BLOCK = 256
MOBA_TOPK = 3
GMLP_CHUNK = 128
NUM_BUCKETS = 32
REL_MAX_DISTANCE = 128
EPS = 1e-6
NEG = -1e9

LANES = 128
HEADS_PER_STEP = LANES // MOBA_HEAD_DIM
ONES_ROWS = 16
VMEM_LIMIT = 56 * 1024 * 1024
N_SLOTS = 4

LOG2E = math.log2(math.e)
Q_SCALE = MOBA_HEAD_DIM ** -0.5 * LOG2E


def _t5_bucket_np(dist):
    n = np.maximum(dist, 0)
    max_exact = NUM_BUCKETS // 2
    nf = np.maximum(n, max_exact).astype(np.float32)
    ratio = np.log(nf / np.float32(max_exact)) / np.float32(math.log(REL_MAX_DISTANCE / max_exact))
    large = max_exact + (ratio * np.float32(NUM_BUCKETS - max_exact)).astype(np.int32)
    large = np.minimum(large, NUM_BUCKETS - 1)
    return np.where(n < max_exact, n, large).astype(np.int32)


def _bias_kernel(bucket_ref, rel_ref, out_ref):
    h = pl.program_id(0)
    L = MOBA_BLOCK
    key = lax.broadcasted_iota(jnp.int32, (L, L), 0)
    qry = lax.broadcasted_iota(jnp.int32, (L, L), 1)
    for delta in range(2):
        bucket = bucket_ref[delta]
        acc = jnp.zeros((L, L), F32)
        for b in range(NUM_BUCKETS):
            acc = jnp.where(bucket == b, rel_ref[b, h] * LOG2E, acc)
        if delta == 0:
            acc = jnp.where(key > qry, NEG, acc)
        out_ref[0, delta] = acc


def _bias_tiles(rel_bias):
    L = MOBA_BLOCK
    kk = np.arange(L)[:, None]
    qq = np.arange(L)[None, :]
    bucket = np.stack([_t5_bucket_np(d * L + qq - kk) for d in range(2)]).astype(np.int32)
    return pl.pallas_call(
        _bias_kernel,
        grid=(MOBA_HEADS,),
        in_specs=[
            pl.BlockSpec((2, L, L), lambda h: (0, 0, 0)),
            pl.BlockSpec(memory_space=pltpu.SMEM),
        ],
        out_specs=pl.BlockSpec((1, 2, L, L), lambda h: (h, 0, 0, 0)),
        out_shape=jax.ShapeDtypeStruct((MOBA_HEADS, 2, L, L), F32),
        name="bias_tiles",
    )(jnp.asarray(bucket), rel_bias)


def _inproj_kernel(x_ref, g_ref, w_ref, lng_ref, lnb_ref, wsp_ref, bsp_ref, gn_ref,
                   qkv_ref, ya_ref, uz_scr):
    C = GMLP_CHUNK
    tm = x_ref.shape[0]
    n_qkv = 3 * D_MOBA
    x = x_ref[...]
    ms = jnp.mean(x * x, axis=-1, keepdims=True)
    h = (x * lax.rsqrt(ms + EPS) * g_ref[...]).astype(BF16)
    uz_scr[...] = jnp.dot(h, w_ref[:, n_qkv:], preferred_element_type=F32)

    t_ix = lax.broadcasted_iota(jnp.int32, (C, C), 0)
    s_ix = lax.broadcasted_iota(jnp.int32, (C, C), 1)
    w_sp = [jnp.where(t_ix >= s_ix, wsp_ref[g], 0.0).astype(BF16) for g in range(GMLP_GROUPS)]
    first = lax.broadcasted_iota(jnp.int32, (1, LANES), 1) < GMLP_GROUP_DIM

    def qkv_cols(c0, c1):
        acc = jnp.dot(h, w_ref[:, c0:c1], preferred_element_type=F32)
        if c1 <= D_MOBA:
            acc = acc * Q_SCALE
        qkv_ref[:, c0:c1] = acc.astype(BF16)

    def gmlp_rows(c):
        rows = slice(c * C, (c + 1) * C)
        z = jax.nn.gelu(uz_scr[rows, D_GMLP:])
        mu = jnp.mean(z, axis=-1, keepdims=True)
        zc = z - mu
        var = jnp.mean(zc * zc, axis=-1, keepdims=True)
        zn = (zc * lax.rsqrt(var + EPS) * lng_ref[...] + lnb_ref[...]).astype(BF16)
        cols = []
        for gp in range(D_GMLP // LANES):
            zp = zn[:, gp * LANES:(gp + 1) * LANES]
            r0 = jnp.dot(w_sp[2 * gp], zp, preferred_element_type=F32)
            r1 = jnp.dot(w_sp[2 * gp + 1], zp, preferred_element_type=F32)
            cols.append(jnp.where(first, r0, r1))
        s = jnp.concatenate(cols, axis=1) + bsp_ref[...]
        y = jax.nn.gelu(uz_scr[rows, :D_GMLP]) * s
        ms_y = jnp.mean(y * y, axis=-1, keepdims=True)
        ya_ref[rows, :] = (y * lax.rsqrt(ms_y + EPS) * gn_ref[...]).astype(BF16)

    col_blocks = [(c, c + 2 * LANES) for c in range(0, n_qkv, 2 * LANES)]
    n_chunks = tm // C
    for step in range(max(len(col_blocks), n_chunks)):
        if step < len(col_blocks):
            qkv_cols(*col_blocks[step])
        if step < n_chunks:
            gmlp_rows(step)


def _inproj(x2, g, w_bf, ln_g, ln_b, w_sp, b_sp_full, gn, tm):
    T, D = x2.shape
    n_all = w_bf.shape[1]
    n_qkv = 3 * D_MOBA
    C = GMLP_CHUNK
    vec = pl.BlockSpec((1, D_GMLP), lambda i: (0, 0))
    return pl.pallas_call(
        _inproj_kernel,
        grid=(T // tm,),
        in_specs=[
            pl.BlockSpec((tm, D), lambda i: (i, 0)),
            pl.BlockSpec((1, D), lambda i: (0, 0)),
            pl.BlockSpec((D, n_all), lambda i: (0, 0), pipeline_mode=pl.Buffered(1)),
            vec, vec,
            pl.BlockSpec((GMLP_GROUPS, C, C), lambda i: (0, 0, 0)),
            pl.BlockSpec((C, D_GMLP), lambda i: (0, 0)),
            vec,
        ],
        out_specs=[
            pl.BlockSpec((tm, n_qkv), lambda i: (i, 0)),
            pl.BlockSpec((tm, D_GMLP), lambda i: (i, 0)),
        ],
        out_shape=[
            jax.ShapeDtypeStruct((T, n_qkv), BF16),
            jax.ShapeDtypeStruct((T, D_GMLP), BF16),
        ],
        scratch_shapes=[pltpu.VMEM((tm, n_all - n_qkv), F32)],
        compiler_params=pltpu.CompilerParams(
            dimension_semantics=("arbitrary",), vmem_limit_bytes=VMEM_LIMIT),
        name="inproj_gmlp",
    )(x2, g, w_bf, ln_g, ln_b, w_sp, b_sp_full, gn)


def _dot_nt(a, b):
    return lax.dot_general(a, b, (((1,), (1,)), ((), ())), preferred_element_type=F32)


def _moba_kernel(q_ref, k_ref, v_ref, bias_ref, cfar_ref, o_ref, vt_scr, *slots):
    t_scr, p_scr = slots[:len(slots) // 2], slots[len(slots) // 2:]
    L = MOBA_BLOCK
    S = q_ref.shape[1]
    NB = S // L
    hp = pl.program_id(1)
    lane = lax.broadcasted_iota(jnp.int32, (1, LANES), 1)
    head_lanes = [(lane >= hh * MOBA_HEAD_DIM) & (lane < (hh + 1) * MOBA_HEAD_DIM)
                  for hh in range(HEADS_PER_STEP)]

    vt_scr[0:LANES, :] = v_ref[0].astype(F32).T.astype(BF16)
    vt_scr[LANES:LANES + ONES_ROWS, :] = jnp.ones((ONES_ROWS, S), BF16)

    km = jnp.concatenate(
        [jnp.sum(k_ref[0, n * L:(n + 1) * L, :].astype(F32), axis=0, keepdims=True) * (1.0 / L)
         for n in range(NB)], axis=0)
    kmm = jnp.concatenate([jnp.where(head_lanes[hh], km, 0.0) for hh in range(HEADS_PER_STEP)], axis=0)
    km_hi = kmm.astype(BF16)
    km_lo = (kmm - km_hi.astype(F32)).astype(BF16)

    row = lax.broadcasted_iota(jnp.int32, (NB, L), 0)

    bodies = [(i, hh) for i in range(NB) for hh in range(HEADS_PER_STEP)]
    gates, masks, qms, maxes, mrows, outs = {}, {}, {}, {}, {}, {}

    def slot_of(n):
        return n % len(t_scr)

    def begin_body(n):
        i, hh = bodies[n]
        q_i = q_ref[0, i * L:(i + 1) * L, :]
        qms[n] = jnp.where(head_lanes[hh], q_i, jnp.zeros_like(q_i))
        if i <= MOBA_TOPK:
            return
        if i not in gates:
            gates[i] = _dot_nt(km_hi, q_i) + _dot_nt(km_lo, q_i)
        g = gates[i][hh * NB:(hh + 1) * NB]
        cnt = jnp.zeros((NB, L), jnp.int32)
        for m in range(i):
            gm = g[m:m + 1, :]
            beats = (gm > g) | ((gm == g) & (m < row))
            cnt = cnt + jnp.where(beats, 1, 0)
        sel = cnt < MOBA_TOPK
        cfar = cfar_ref[hp * HEADS_PER_STEP + hh]
        masks[n] = (jnp.where(sel, cfar, NEG),
                    jnp.where(sel, 0.0, NEG))

    def score_tile(n, j):
        i, hh = bodies[n]
        t = _dot_nt(k_ref[0, j * L:(j + 1) * L, :], qms[n])
        if j == i:
            t = t + bias_ref[hh, 0]
        elif j == i - 1:
            t = t + bias_ref[hh, 1]
            if i > MOBA_TOPK:
                t = t + masks[n][1][j:j + 1, :]
        elif i > MOBA_TOPK:
            t = t + masks[n][0][j:j + 1, :]
        else:
            t = t + cfar_ref[hp * HEADS_PER_STEP + hh]
        t_scr[slot_of(n)][j * L:(j + 1) * L, :] = t
        tm = jnp.max(t.reshape(L // 8, 8, L), axis=0)
        maxes[n] = tm if j == 0 else jnp.maximum(maxes[n], tm)
        if j == i:
            mrows[n] = jnp.max(maxes.pop(n), axis=0, keepdims=True)
            qms.pop(n)
            masks.pop(n, None)

    def exp_tile(n, j):
        t_ref, p_ref = t_scr[slot_of(n)], p_scr[slot_of(n)]
        p_ref[j * L:(j + 1) * L, :] = jnp.exp2(t_ref[j * L:(j + 1) * L, :] - mrows[n]).astype(BF16)

    def pv_stage(n):
        i, hh = bodies[n]
        nk = (i + 1) * L
        mrows.pop(n)
        acc = jnp.dot(vt_scr[:, 0:nk], p_scr[slot_of(n)][0:nk, :], preferred_element_type=F32)
        outs[n] = acc[hh * MOBA_HEAD_DIM:(hh + 1) * MOBA_HEAD_DIM] / acc[LANES:LANES + 1]
        if hh == HEADS_PER_STEP - 1:
            pair = [outs.pop(n - HEADS_PER_STEP + 1 + h) for h in range(HEADS_PER_STEP)]
            o_ref[0, i * L:(i + 1) * L, :] = jnp.concatenate(pair, axis=0).T

    nb = len(bodies)
    begin_body(0)
    for j in range(bodies[0][0] + 1):
        score_tile(0, j)
    for n in range(nb):
        n_exp = bodies[n][0] + 1
        n_score = bodies[n + 1][0] + 1 if n + 1 < nb else 0
        if n_score:
            begin_body(n + 1)
        for j in range(max(n_exp, n_score)):
            if j < n_score:
                score_tile(n + 1, j)
            if j < n_exp:
                exp_tile(n, j)
        pv_stage(n)


def _moba(qkv3, bias_t, cfar):
    B, S, _ = qkv3.shape
    L = MOBA_BLOCK
    n_hp = MOBA_HEADS // HEADS_PER_STEP
    blk = lambda off: pl.BlockSpec((1, S, LANES), lambda b, hp: (b, 0, off + hp))
    return pl.pallas_call(
        _moba_kernel,
        grid=(B, n_hp),
        in_specs=[
            blk(0), blk(n_hp), blk(2 * n_hp),
            pl.BlockSpec((HEADS_PER_STEP, 2, L, L), lambda b, hp: (hp, 0, 0, 0)),
            pl.BlockSpec(memory_space=pltpu.SMEM),
        ],
        out_specs=pl.BlockSpec((1, S, LANES), lambda b, hp: (b, 0, hp)),
        out_shape=jax.ShapeDtypeStruct((B, S, D_MOBA), F32),
        scratch_shapes=[
            pltpu.VMEM((LANES + ONES_ROWS, S), BF16),
        ] + [pltpu.VMEM((S, L), F32)] * N_SLOTS + [pltpu.VMEM((S, L), BF16)] * N_SLOTS,
        compiler_params=pltpu.CompilerParams(
            dimension_semantics=("arbitrary", "arbitrary"), vmem_limit_bytes=VMEM_LIMIT),
        name="moba",
    )(qkv3, qkv3, qkv3, bias_t, cfar)


def _ffn_kernel(x_ref, yb_ref, ya_ref, gb_ref, wo_ref, gf_ref, wg_ref, wu_ref, wd_ref, gl_ref,
                o_ref, *, final_norm):
    yb = yb_ref[...]
    ms = jnp.mean(yb * yb, axis=-1, keepdims=True)
    ybn = (yb * lax.rsqrt(ms + EPS) * gb_ref[...]).astype(BF16)
    x1 = (x_ref[...]
          + jnp.dot(ybn, wo_ref[:D_MOBA, :], preferred_element_type=F32)
          + jnp.dot(ya_ref[...], wo_ref[D_MOBA:, :], preferred_element_type=F32))
    ms = jnp.mean(x1 * x1, axis=-1, keepdims=True)
    h = (x1 * lax.rsqrt(ms + EPS) * gf_ref[...]).astype(BF16)
    g = jnp.dot(h, wg_ref[...], preferred_element_type=F32)
    u = jnp.dot(h, wu_ref[...], preferred_element_type=F32)
    a = (jax.nn.silu(g) * u).astype(BF16)
    x2 = x1 + jnp.dot(a, wd_ref[...], preferred_element_type=F32)
    if final_norm:
        ms = jnp.mean(x2 * x2, axis=-1, keepdims=True)
        x2 = x2 * lax.rsqrt(ms + EPS) * gl_ref[...]
    o_ref[...] = x2


def _ffn(x2, yb, ya, gb, wo, gf, wg, wu, wd, gl, tm, final_norm):
    T, D = x2.shape
    d_ff = wg.shape[1]
    const = lambda shape: pl.BlockSpec(shape, lambda i: (0, 0), pipeline_mode=pl.Buffered(1))
    return pl.pallas_call(
        functools.partial(_ffn_kernel, final_norm=final_norm),
        grid=(T // tm,),
        in_specs=[
            pl.BlockSpec((tm, D), lambda i: (i, 0)),
            pl.BlockSpec((tm, D_MOBA), lambda i: (i, 0)),
            pl.BlockSpec((tm, D_GMLP), lambda i: (i, 0)),
            pl.BlockSpec((1, D_MOBA), lambda i: (0, 0)),
            const((D_MOBA + D_GMLP, D)),
            pl.BlockSpec((1, D), lambda i: (0, 0)),
            const((D, d_ff)), const((D, d_ff)), const((d_ff, D)),
            pl.BlockSpec((1, D), lambda i: (0, 0)),
        ],
        out_specs=pl.BlockSpec((tm, D), lambda i: (i, 0)),
        out_shape=jax.ShapeDtypeStruct((T, D), F32),
        compiler_params=pltpu.CompilerParams(
            dimension_semantics=("arbitrary",), vmem_limit_bytes=VMEM_LIMIT),
        name="outproj_ffn",
    )(x2, yb, ya, gb, wo, gf, wg, wu, wd, gl)


def kernel(x, rel_bias, norm_mix, w_in, gmlp_ln_g, gmlp_ln_b, w_spatial, b_spatial, out_norm_b,
           out_norm_a, w_out, norm_ffn, w_gate, w_up, w_down, norm_final):
    B, S, D = x.shape
    depth = w_in.shape[0]
    assert S % MOBA_BLOCK == 0 and S // MOBA_BLOCK > MOBA_TOPK
    far = _t5_bucket_np(np.arange(MOBA_BLOCK + 1, S + MOBA_BLOCK))
    assert (far == far[0]).all()
    far_bucket = int(far[0])

    tm = 512
    T = B * S
    x2 = x.reshape(T, D)
    bias_t = _bias_tiles(rel_bias)
    cfar = rel_bias[far_bucket, :] * LOG2E
    row = lambda v: v.reshape(1, -1)
    for l in range(depth):
        b_sp_full = jnp.repeat(b_spatial[l].T, GMLP_GROUP_DIM, axis=1)
        qkv, ya = _inproj(x2, row(norm_mix[l]), w_in[l].astype(BF16), row(gmlp_ln_g[l]),
                          row(gmlp_ln_b[l]), w_spatial[l], b_sp_full, row(out_norm_a[l]), tm)
        yb = _moba(qkv.reshape(B, S, -1), bias_t, cfar).reshape(T, D_MOBA)
        x2 = _ffn(x2, yb, ya, row(out_norm_b[l]), w_out[l].astype(BF16), row(norm_ffn[l]),
                  w_gate[l].astype(BF16), w_up[l].astype(BF16), w_down[l].astype(BF16),
                  row(norm_final), tm, l == depth - 1)
    return x2.reshape(B, S, D)
```

```python
import functools
import math

import numpy as np
import jax
import jax.numpy as jnp
from jax import lax
from jax.experimental import pallas as pl
from jax.experimental.pallas import tpu as pltpu

F32 = jnp.float32
BF16 = jnp.bfloat16

MOBA_HEADS = 8
MOBA_HEAD_DIM = 64
D_MOBA = MOBA_HEADS * MOBA_HEAD_DIM
GMLP_GROUPS = 8
GMLP_GROUP_DIM = 64
D_GMLP = GMLP_GROUPS * GMLP_GROUP_DIM
MOBA_BLOCK = 256
MOBA_TOPK = 3
GMLP_CHUNK = 128
NUM_BUCKETS = 32
REL_MAX_DISTANCE = 128
EPS = 1e-6
NEG = -1e9

LANES = 128
HEADS_PER_STEP = LANES // MOBA_HEAD_DIM
BF16_SUBLANES = 16
ONES_ROWS = BF16_SUBLANES
VMEM_LIMIT = 56 * 1024 * 1024
N_SLOTS = 4
LOOKAHEAD = 3

LOG2E = math.log2(math.e)
Q_SCALE = MOBA_HEAD_DIM ** -0.5 * LOG2E


def _t5_bucket_np(dist):
    n = np.maximum(dist, 0)
    max_exact = NUM_BUCKETS // 2
    nf = np.maximum(n, max_exact).astype(np.float32)
    ratio = np.log(nf / np.float32(max_exact)) / np.float32(math.log(REL_MAX_DISTANCE / max_exact))
    large = max_exact + (ratio * np.float32(NUM_BUCKETS - max_exact)).astype(np.int32)
    large = np.minimum(large, NUM_BUCKETS - 1)
    return np.where(n < max_exact, n, large).astype(np.int32)


def _cast_specs(weights, n_steps, step_of):
    in_specs, out_specs, out_shapes = [], [], []
    for w in weights:
        rows, cols = w.shape
        period = next(p for p in range(1, n_steps + 1)
                      if n_steps % p == 0 and (rows * p) % (n_steps * BF16_SUBLANES) == 0)
        spec = pl.BlockSpec((rows * period // n_steps, cols),
                            functools.partial(lambda p, *ids: (step_of(*ids) // p, 0), period))
        in_specs.append(spec)
        out_specs.append(spec)
        out_shapes.append(jax.ShapeDtypeStruct(w.shape, BF16))
    return in_specs, out_specs, out_shapes


def _cast_slabs(src_refs, dst_refs):
    for src, dst in zip(src_refs, dst_refs):
        dst[...] = src[...].astype(BF16)


def _bias_kernel(bucket_ref, rel_ref, *refs):
    n_cast = (len(refs) - 1) // 2
    out_ref = refs[n_cast]
    _cast_slabs(refs[:n_cast], refs[n_cast + 1:])
    h = pl.program_id(0)
    L = MOBA_BLOCK
    key = lax.broadcasted_iota(jnp.int32, (L, L), 0)
    qry = lax.broadcasted_iota(jnp.int32, (L, L), 1)
    for delta in range(2):
        bucket = bucket_ref[delta]
        acc = jnp.zeros((L, L), F32)
        for b in range(NUM_BUCKETS):
            acc = jnp.where(bucket == b, rel_ref[b, h] * LOG2E, acc)
        if delta == 0:
            acc = jnp.where(key > qry, NEG, acc)
        out_ref[0, delta] = acc


def _bias_tiles(rel_bias, cast_weights):
    L = MOBA_BLOCK
    kk = np.arange(L)[:, None]
    qq = np.arange(L)[None, :]
    bucket = np.stack([_t5_bucket_np(d * L + qq - kk) for d in range(2)]).astype(np.int32)
    c_in, c_out, c_shapes = _cast_specs(cast_weights, MOBA_HEADS, lambda h: h)
    outs = pl.pallas_call(
        _bias_kernel,
        grid=(MOBA_HEADS,),
        in_specs=[
            pl.BlockSpec((2, L, L), lambda h: (0, 0, 0)),
            pl.BlockSpec(memory_space=pltpu.SMEM),
        ] + c_in,
        out_specs=[pl.BlockSpec((1, 2, L, L), lambda h: (h, 0, 0, 0))] + c_out,
        out_shape=[jax.ShapeDtypeStruct((MOBA_HEADS, 2, L, L), F32)] + c_shapes,
        compiler_params=pltpu.CompilerParams(
            dimension_semantics=("arbitrary",), vmem_limit_bytes=VMEM_LIMIT),
        name="bias_tiles",
    )(jnp.asarray(bucket), rel_bias, *cast_weights)
    return outs[0], outs[1:]


def _inproj_kernel(x_ref, g_ref, w_ref, lng_ref, lnb_ref, wsp_ref, bsp_ref, gn_ref,
                   qkv_ref, ya_ref, uz_scr):
    C = GMLP_CHUNK
    tm = x_ref.shape[0]
    n_qkv = 3 * D_MOBA
    x = x_ref[...]
    ms = jnp.mean(x * x, axis=-1, keepdims=True)
    h = (x * lax.rsqrt(ms + EPS) * g_ref[...]).astype(BF16)
    uz_scr[...] = jnp.dot(h, w_ref[:, n_qkv:], preferred_element_type=F32)

    t_ix = lax.broadcasted_iota(jnp.int32, (C, C), 0)
    s_ix = lax.broadcasted_iota(jnp.int32, (C, C), 1)
    w_sp = [jnp.where(t_ix >= s_ix, wsp_ref[g], 0.0).astype(BF16) for g in range(GMLP_GROUPS)]
    first = lax.broadcasted_iota(jnp.int32, (1, LANES), 1) < GMLP_GROUP_DIM

    def qkv_cols(c0, c1):
        acc = jnp.dot(h, w_ref[:, c0:c1], preferred_element_type=F32)
        if c1 <= D_MOBA:
            acc = acc * Q_SCALE
        qkv_ref[:, c0:c1] = acc.astype(BF16)

    def gmlp_rows(c):
        rows = slice(c * C, (c + 1) * C)
        z = jax.nn.gelu(uz_scr[rows, D_GMLP:])
        mu = jnp.mean(z, axis=-1, keepdims=True)
        zc = z - mu
        var = jnp.mean(zc * zc, axis=-1, keepdims=True)
        zn = (zc * lax.rsqrt(var + EPS) * lng_ref[...] + lnb_ref[...]).astype(BF16)
        cols = []
        for gp in range(D_GMLP // LANES):
            zp = zn[:, gp * LANES:(gp + 1) * LANES]
            r0 = jnp.dot(w_sp[2 * gp], zp, preferred_element_type=F32)
            r1 = jnp.dot(w_sp[2 * gp + 1], zp, preferred_element_type=F32)
            cols.append(jnp.where(first, r0, r1))
        s = jnp.concatenate(cols, axis=1) + bsp_ref[...]
        y = jax.nn.gelu(uz_scr[rows, :D_GMLP]) * s
        ms_y = jnp.mean(y * y, axis=-1, keepdims=True)
        ya_ref[rows, :] = (y * lax.rsqrt(ms_y + EPS) * gn_ref[...]).astype(BF16)

    col_blocks = [(c, c + 2 * LANES) for c in range(0, n_qkv, 2 * LANES)]
    n_chunks = tm // C
    for step in range(max(len(col_blocks), n_chunks)):
        if step < len(col_blocks):
            qkv_cols(*col_blocks[step])
        if step < n_chunks:
            gmlp_rows(step)


def _inproj(x2, g, w_bf, ln_g, ln_b, w_sp, b_sp_full, gn, tm):
    T, D = x2.shape
    n_all = w_bf.shape[1]
    n_qkv = 3 * D_MOBA
    C = GMLP_CHUNK
    vec = pl.BlockSpec((1, D_GMLP), lambda i: (0, 0))
    return pl.pallas_call(
        _inproj_kernel,
        grid=(T // tm,),
        in_specs=[
            pl.BlockSpec((tm, D), lambda i: (i, 0)),
            pl.BlockSpec((1, D), lambda i: (0, 0)),
            pl.BlockSpec((D, n_all), lambda i: (0, 0), pipeline_mode=pl.Buffered(1)),
            vec, vec,
            pl.BlockSpec((GMLP_GROUPS, C, C), lambda i: (0, 0, 0)),
            pl.BlockSpec((C, D_GMLP), lambda i: (0, 0)),
            vec,
        ],
        out_specs=[
            pl.BlockSpec((tm, n_qkv), lambda i: (i, 0)),
            pl.BlockSpec((tm, D_GMLP), lambda i: (i, 0)),
        ],
        out_shape=[
            jax.ShapeDtypeStruct((T, n_qkv), BF16),
            jax.ShapeDtypeStruct((T, D_GMLP), BF16),
        ],
        scratch_shapes=[pltpu.VMEM((tm, n_all - n_qkv), F32)],
        compiler_params=pltpu.CompilerParams(
            dimension_semantics=("arbitrary",), vmem_limit_bytes=VMEM_LIMIT),
        name="inproj_gmlp",
    )(x2, g, w_bf, ln_g, ln_b, w_sp, b_sp_full, gn)


def _dot_nt(a, b):
    return lax.dot_general(a, b, (((1,), (1,)), ((), ())), preferred_element_type=F32)


def _moba_kernel(q_ref, k_ref, v_ref, bias_ref, cfar_ref, *refs, n_cast):
    cast_src, o_ref, cast_dst = refs[:n_cast], refs[n_cast], refs[n_cast + 1:2 * n_cast + 1]
    vt_scr, slots = refs[2 * n_cast + 1], refs[2 * n_cast + 2:]
    t_scr, p_scr = slots[:len(slots) // 2], slots[len(slots) // 2:]
    _cast_slabs(cast_src, cast_dst)
    L = MOBA_BLOCK
    S = q_ref.shape[1]
    NB = S // L
    hp = pl.program_id(1)
    lane = lax.broadcasted_iota(jnp.int32, (1, LANES), 1)
    head_lanes = [(lane >= hh * MOBA_HEAD_DIM) & (lane < (hh + 1) * MOBA_HEAD_DIM)
                  for hh in range(HEADS_PER_STEP)]

    vt_scr[0:LANES, :] = v_ref[0].astype(F32).T.astype(BF16)
    vt_scr[LANES:LANES + ONES_ROWS, :] = jnp.ones((ONES_ROWS, S), BF16)

    km = jnp.concatenate(
        [jnp.sum(k_ref[0, n * L:(n + 1) * L, :].astype(F32), axis=0, keepdims=True) * (1.0 / L)
         for n in range(NB)], axis=0)
    kmm = jnp.concatenate([jnp.where(head_lanes[hh], km, 0.0) for hh in range(HEADS_PER_STEP)], axis=0)
    km_hi = kmm.astype(BF16)
    km_lo = (kmm - km_hi.astype(F32)).astype(BF16)

    row = lax.broadcasted_iota(jnp.int32, (NB, L), 0)

    bodies = [(i, hh) for i in range(NB) for hh in range(HEADS_PER_STEP)]
    gates, masks, qms, maxes, mrows, outs = {}, {}, {}, {}, {}, {}

    def slot_of(n):
        return n % len(t_scr)

    def begin_body(n):
        i, hh = bodies[n]
        q_i = q_ref[0, i * L:(i + 1) * L, :]
        qms[n] = jnp.where(head_lanes[hh], q_i, jnp.zeros_like(q_i))
        if i <= MOBA_TOPK:
            return
        if i not in gates:
            gates[i] = _dot_nt(km_hi, q_i) + _dot_nt(km_lo, q_i)
        g = gates[i][hh * NB:(hh + 1) * NB]
        cnt = jnp.zeros((NB, L), jnp.int32)
        for m in range(i):
            gm = g[m:m + 1, :]
            beats = (gm > g) | ((gm == g) & (m < row))
            cnt = cnt + jnp.where(beats, 1, 0)
        sel = cnt < MOBA_TOPK
        cfar = cfar_ref[hp * HEADS_PER_STEP + hh]
        masks[n] = (jnp.where(sel, cfar, NEG),
                    jnp.where(sel, 0.0, NEG))

    def score_tile(n, j):
        i, hh = bodies[n]
        t = _dot_nt(k_ref[0, j * L:(j + 1) * L, :], qms[n])
        if j == i:
            t = t + bias_ref[hh, 0]
        elif j == i - 1:
            t = t + bias_ref[hh, 1]
            if i > MOBA_TOPK:
                t = t + masks[n][1][j:j + 1, :]
        elif i > MOBA_TOPK:
            t = t + masks[n][0][j:j + 1, :]
        else:
            t = t + cfar_ref[hp * HEADS_PER_STEP + hh]
        t_scr[slot_of(n)][j * L:(j + 1) * L, :] = t
        tm = jnp.max(t.reshape(L // 8, 8, L), axis=0)
        maxes[n] = tm if j == 0 else jnp.maximum(maxes[n], tm)
        if j == i:
            mrows[n] = jnp.max(maxes.pop(n), axis=0, keepdims=True)
            qms.pop(n)
            masks.pop(n, None)

    def exp_tile(n, j):
        t_ref, p_ref = t_scr[slot_of(n)], p_scr[slot_of(n)]
        p_ref[j * L:(j + 1) * L, :] = jnp.exp2(t_ref[j * L:(j + 1) * L, :] - mrows[n]).astype(BF16)

    def pv_stage(n):
        i, hh = bodies[n]
        nk = (i + 1) * L
        mrows.pop(n)
        acc = jnp.dot(vt_scr[:, 0:nk], p_scr[slot_of(n)][0:nk, :], preferred_element_type=F32)
        outs[n] = acc[hh * MOBA_HEAD_DIM:(hh + 1) * MOBA_HEAD_DIM] / acc[LANES:LANES + 1]
        if hh == HEADS_PER_STEP - 1:
            pair = [outs.pop(n - HEADS_PER_STEP + 1 + h) for h in range(HEADS_PER_STEP)]
            o_ref[0, i * L:(i + 1) * L, :] = jnp.concatenate(pair, axis=0).T

    nb = len(bodies)
    for n0 in range(min(LOOKAHEAD, nb)):
        begin_body(n0)
        for j in range(bodies[n0][0] + 1):
            score_tile(n0, j)
    for n in range(nb):
        n_exp = bodies[n][0] + 1
        ahead = n + LOOKAHEAD
        n_score = bodies[ahead][0] + 1 if ahead < nb else 0
        if n_score:
            begin_body(ahead)
        for j in range(max(n_exp, n_score)):
            if j < n_score:
                score_tile(ahead, j)
            if j < n_exp:
                exp_tile(n, j)
        pv_stage(n)


def _moba(qkv3, bias_t, cfar, cast_weights):
    B, S, _ = qkv3.shape
    L = MOBA_BLOCK
    n_hp = MOBA_HEADS // HEADS_PER_STEP
    blk = lambda off: pl.BlockSpec((1, S, LANES), lambda b, hp: (b, 0, off + hp))
    c_in, c_out, c_shapes = _cast_specs(cast_weights, B * n_hp, lambda b, hp: b * n_hp + hp)
    outs = pl.pallas_call(
        functools.partial(_moba_kernel, n_cast=len(cast_weights)),
        grid=(B, n_hp),
        in_specs=[
            blk(0), blk(n_hp), blk(2 * n_hp),
            pl.BlockSpec((HEADS_PER_STEP, 2, L, L), lambda b, hp: (hp, 0, 0, 0)),
            pl.BlockSpec(memory_space=pltpu.SMEM),
        ] + c_in,
        out_specs=[pl.BlockSpec((1, S, LANES), lambda b, hp: (b, 0, hp))] + c_out,
        out_shape=[jax.ShapeDtypeStruct((B, S, D_MOBA), F32)] + c_shapes,
        scratch_shapes=[
            pltpu.VMEM((LANES + ONES_ROWS, S), BF16),
        ] + [pltpu.VMEM((S, L), F32)] * N_SLOTS + [pltpu.VMEM((S, L), BF16)] * N_SLOTS,
        compiler_params=pltpu.CompilerParams(
            dimension_semantics=("arbitrary", "arbitrary"), vmem_limit_bytes=VMEM_LIMIT),
        name="moba",
    )(qkv3, qkv3, qkv3, bias_t, cfar, *cast_weights)
    return outs[0], outs[1:]


def _ffn_kernel(x_ref, yb_ref, ya_ref, gb_ref, wo_ref, gf_ref, wg_ref, wu_ref, wd_ref, gl_ref,
                o_ref, *, final_norm):
    yb = yb_ref[...]
    ms = jnp.mean(yb * yb, axis=-1, keepdims=True)
    ybn = (yb * lax.rsqrt(ms + EPS) * gb_ref[...]).astype(BF16)
    x1 = (x_ref[...]
          + jnp.dot(ybn, wo_ref[:D_MOBA, :], preferred_element_type=F32)
          + jnp.dot(ya_ref[...], wo_ref[D_MOBA:, :], preferred_element_type=F32))
    ms = jnp.mean(x1 * x1, axis=-1, keepdims=True)
    h = (x1 * lax.rsqrt(ms + EPS) * gf_ref[...]).astype(BF16)
    g = jnp.dot(h, wg_ref[...], preferred_element_type=F32)
    u = jnp.dot(h, wu_ref[...], preferred_element_type=F32)
    a = (jax.nn.silu(g) * u).astype(BF16)
    x2 = x1 + jnp.dot(a, wd_ref[...], preferred_element_type=F32)
    if final_norm:
        ms = jnp.mean(x2 * x2, axis=-1, keepdims=True)
        x2 = x2 * lax.rsqrt(ms + EPS) * gl_ref[...]
    o_ref[...] = x2


def _ffn(x2, yb, ya, gb, wo, gf, wg, wu, wd, gl, tm, final_norm):
    T, D = x2.shape
    d_ff = wg.shape[1]
    const = lambda shape: pl.BlockSpec(shape, lambda i: (0, 0), pipeline_mode=pl.Buffered(1))
    return pl.pallas_call(
        functools.partial(_ffn_kernel, final_norm=final_norm),
        grid=(T // tm,),
        in_specs=[
            pl.BlockSpec((tm, D), lambda i: (i, 0)),
            pl.BlockSpec((tm, D_MOBA), lambda i: (i, 0)),
            pl.BlockSpec((tm, D_GMLP), lambda i: (i, 0)),
            pl.BlockSpec((1, D_MOBA), lambda i: (0, 0)),
            const((D_MOBA + D_GMLP, D)),
            pl.BlockSpec((1, D), lambda i: (0, 0)),
            const((D, d_ff)), const((D, d_ff)), const((d_ff, D)),
            pl.BlockSpec((1, D), lambda i: (0, 0)),
        ],
        out_specs=pl.BlockSpec((tm, D), lambda i: (i, 0)),
        out_shape=jax.ShapeDtypeStruct((T, D), F32),
        compiler_params=pltpu.CompilerParams(
            dimension_semantics=("arbitrary",), vmem_limit_bytes=VMEM_LIMIT),
        name="outproj_ffn",
    )(x2, yb, ya, gb, wo, gf, wg, wu, wd, gl)


def kernel(x, rel_bias, norm_mix, w_in, gmlp_ln_g, gmlp_ln_b, w_spatial, b_spatial, out_norm_b,
           out_norm_a, w_out, norm_ffn, w_gate, w_up, w_down, norm_final):
    B, S, D = x.shape
    depth = w_in.shape[0]
    assert S % MOBA_BLOCK == 0 and S // MOBA_BLOCK > MOBA_TOPK
    far = _t5_bucket_np(np.arange(MOBA_BLOCK + 1, S + MOBA_BLOCK))
    assert (far == far[0]).all()
    far_bucket = int(far[0])

    assert depth == 1, "weight casts ride along the single layer's kernels"
    tm = 512
    T = B * S
    x2 = x.reshape(T, D)
    row = lambda v: v.reshape(1, -1)
    bias_t, (w_in_bf,) = _bias_tiles(rel_bias, [w_in[0]])
    cfar = rel_bias[far_bucket, :] * LOG2E
    b_sp_full = jnp.repeat(b_spatial[0].T, GMLP_GROUP_DIM, axis=1)
    qkv, ya = _inproj(x2, row(norm_mix[0]), w_in_bf, row(gmlp_ln_g[0]), row(gmlp_ln_b[0]),
                      w_spatial[0], b_sp_full, row(out_norm_a[0]), tm)
    yb, (wo_bf, wg_bf, wu_bf, wd_bf) = _moba(qkv.reshape(B, S, -1), bias_t, cfar,
                                             [w_out[0], w_gate[0], w_up[0], w_down[0]])
    out = _ffn(x2, yb.reshape(T, D_MOBA), ya, row(out_norm_b[0]), wo_bf, row(norm_ffn[0]),
               wg_bf, wu_bf, wd_bf, row(norm_final), tm, True)
    return out.reshape(B, S, D)
```

```python
import functools
import math

import numpy as np
import jax
import jax.numpy as jnp
from jax import lax
from jax.experimental import pallas as pl
from jax.experimental.pallas import tpu as pltpu

F32 = jnp.float32
BF16 = jnp.bfloat16

MOBA_HEADS = 8
MOBA_HEAD_DIM = 64
D_MOBA = MOBA_HEADS * MOBA_HEAD_DIM
GMLP_GROUPS = 8
GMLP_GROUP_DIM = 64
D_GMLP = GMLP_GROUPS * GMLP_GROUP_DIM
MOBA_BLOCK = 256
MOBA_TOPK = 3
GMLP_CHUNK = 128
NUM_BUCKETS = 32
REL_MAX_DISTANCE = 128
EPS = 1e-6
NEG = -1e9

LANES = 128
HEADS_PER_STEP = LANES // MOBA_HEAD_DIM
N_HEAD_PAIRS = MOBA_HEADS // HEADS_PER_STEP
BF16_SUBLANES = 16
ONES_ROWS = BF16_SUBLANES
VMEM_LIMIT = 56 * 1024 * 1024
N_SLOTS = 4
LOOKAHEAD = 3

LOG2E = math.log2(math.e)
Q_SCALE = MOBA_HEAD_DIM ** -0.5 * LOG2E


def _t5_bucket_np(dist):
    n = np.maximum(dist, 0)
    max_exact = NUM_BUCKETS // 2
    nf = np.maximum(n, max_exact).astype(np.float32)
    ratio = np.log(nf / np.float32(max_exact)) / np.float32(math.log(REL_MAX_DISTANCE / max_exact))
    large = max_exact + (ratio * np.float32(NUM_BUCKETS - max_exact)).astype(np.int32)
    large = np.minimum(large, NUM_BUCKETS - 1)
    return np.where(n < max_exact, n, large).astype(np.int32)


def _cast_specs(weights, n_steps, step_of):
    in_specs, out_specs, out_shapes = [], [], []
    for w in weights:
        rows, cols = w.shape
        period = next(p for p in range(1, n_steps + 1)
                      if n_steps % p == 0 and (rows * p) % (n_steps * BF16_SUBLANES) == 0)
        spec = pl.BlockSpec((rows * period // n_steps, cols),
                            functools.partial(lambda p, *ids: (step_of(*ids) // p, 0), period))
        in_specs.append(spec)
        out_specs.append(spec)
        out_shapes.append(jax.ShapeDtypeStruct(w.shape, BF16))
    return in_specs, out_specs, out_shapes


def _cast_slabs(src_refs, dst_refs):
    for src, dst in zip(src_refs, dst_refs):
        dst[...] = src[...].astype(BF16)


def _bias_kernel(bucket_ref, rel_ref, *refs):
    n_cast = (len(refs) - 1) // 2
    out_ref = refs[n_cast]
    _cast_slabs(refs[:n_cast], refs[n_cast + 1:])
    h = pl.program_id(0)
    L = MOBA_BLOCK
    key = lax.broadcasted_iota(jnp.int32, (L, L), 0)
    qry = lax.broadcasted_iota(jnp.int32, (L, L), 1)
    for delta in range(2):
        bucket = bucket_ref[delta]
        acc = jnp.zeros((L, L), F32)
        for b in range(NUM_BUCKETS):
            acc = jnp.where(bucket == b, rel_ref[b, h] * LOG2E, acc)
        if delta == 0:
            acc = jnp.where(key > qry, NEG, acc)
        out_ref[0, delta] = acc


def _bias_tiles(rel_bias, cast_weights):
    L = MOBA_BLOCK
    kk = np.arange(L)[:, None]
    qq = np.arange(L)[None, :]
    bucket = np.stack([_t5_bucket_np(d * L + qq - kk) for d in range(2)]).astype(np.int32)
    c_in, c_out, c_shapes = _cast_specs(cast_weights, MOBA_HEADS, lambda h: h)
    outs = pl.pallas_call(
        _bias_kernel,
        grid=(MOBA_HEADS,),
        in_specs=[
            pl.BlockSpec((2, L, L), lambda h: (0, 0, 0)),
            pl.BlockSpec(memory_space=pltpu.SMEM),
        ] + c_in,
        out_specs=[pl.BlockSpec((1, 2, L, L), lambda h: (h, 0, 0, 0))] + c_out,
        out_shape=[jax.ShapeDtypeStruct((MOBA_HEADS, 2, L, L), F32)] + c_shapes,
        compiler_params=pltpu.CompilerParams(
            dimension_semantics=("arbitrary",), vmem_limit_bytes=VMEM_LIMIT),
        name="bias_tiles",
    )(jnp.asarray(bucket), rel_bias, *cast_weights)
    return outs[0], outs[1:]


def _dot_nt(a, b):
    return lax.dot_general(a, b, (((1,), (1,)), ((), ())), preferred_element_type=F32)


def _inproj_ops(hp, x_ref, g_ref, w_ref, lng_ref, lnb_ref, wsp_ref, bsp_ref, gn_ref,
                ya_ref, qkv_w, uz_scr):
    C = GMLP_CHUNK
    tm = x_ref.shape[0]
    n_qkv = 3 * D_MOBA
    n_all = w_ref.shape[1]
    row0 = pl.multiple_of(hp * tm, tm)
    st = {}

    def norm():
        x = x_ref[...]
        ms = jnp.mean(x * x, axis=-1, keepdims=True)
        st["h"] = (x * lax.rsqrt(ms + EPS) * g_ref[...]).astype(BF16)
        t_ix = lax.broadcasted_iota(jnp.int32, (C, C), 0)
        s_ix = lax.broadcasted_iota(jnp.int32, (C, C), 1)
        st["w_sp"] = [jnp.where(t_ix >= s_ix, wsp_ref[g], 0.0).astype(BF16)
                      for g in range(GMLP_GROUPS)]

    def uz_cols(c0, c1):
        uz_scr[:, c0 - n_qkv:c1 - n_qkv] = jnp.dot(st["h"], w_ref[:, c0:c1],
                                                    preferred_element_type=F32)

    def qkv_cols(c0, c1):
        acc = jnp.dot(st["h"], w_ref[:, c0:c1], preferred_element_type=F32)
        if c1 <= D_MOBA:
            acc = acc * Q_SCALE
        acc = acc.astype(BF16)
        for c in range(c0, c1, LANES):
            qkv_w[c // LANES, pl.ds(row0, tm), :] = acc[:, c - c0:c - c0 + LANES]

    def gmlp_rows(c):
        rows = slice(c * C, (c + 1) * C)
        first = lax.broadcasted_iota(jnp.int32, (1, LANES), 1) < GMLP_GROUP_DIM
        z = jax.nn.gelu(uz_scr[rows, D_GMLP:])
        mu = jnp.mean(z, axis=-1, keepdims=True)
        zc = z - mu
        var = jnp.mean(zc * zc, axis=-1, keepdims=True)
        zn = (zc * lax.rsqrt(var + EPS) * lng_ref[...] + lnb_ref[...]).astype(BF16)
        cols = []
        for gp in range(D_GMLP // LANES):
            zp = zn[:, gp * LANES:(gp + 1) * LANES]
            r0 = jnp.dot(st["w_sp"][2 * gp], zp, preferred_element_type=F32)
            r1 = jnp.dot(st["w_sp"][2 * gp + 1], zp, preferred_element_type=F32)
            cols.append(jnp.where(first, r0, r1))
        s = jnp.concatenate(cols, axis=1) + bsp_ref[...]
        y = jax.nn.gelu(uz_scr[rows, :D_GMLP]) * s
        ms_y = jnp.mean(y * y, axis=-1, keepdims=True)
        ya_ref[rows, :] = (y * lax.rsqrt(ms_y + EPS) * gn_ref[...]).astype(BF16)

    ops = [norm]
    ops += [functools.partial(uz_cols, c, c + 2 * LANES) for c in range(n_qkv, n_all, 2 * LANES)]
    col_blocks = [(c, c + 2 * LANES) for c in range(0, n_qkv, 2 * LANES)]
    n_chunks = tm // C
    for step in range(max(len(col_blocks), n_chunks)):
        if step < len(col_blocks):
            ops.append(functools.partial(qkv_cols, *col_blocks[step]))
        if step < n_chunks:
            ops.append(functools.partial(gmlp_rows, step))
    return ops


def _moba_ops(hp, qkv_r, bias_ref, cfar_ref, o_ref, vt_scr, t_scr, p_scr):
    L = MOBA_BLOCK
    S = qkv_r.shape[1]
    NB = S // L
    q_ref, k_ref, v_ref = (qkv_r.at[hp], qkv_r.at[N_HEAD_PAIRS + hp], qkv_r.at[2 * N_HEAD_PAIRS + hp])
    lane = lax.broadcasted_iota(jnp.int32, (1, LANES), 1)
    head_lanes = [(lane >= hh * MOBA_HEAD_DIM) & (lane < (hh + 1) * MOBA_HEAD_DIM)
                  for hh in range(HEADS_PER_STEP)]
    row = lax.broadcasted_iota(jnp.int32, (NB, L), 0)
    bodies = [(i, hh) for i in range(NB) for hh in range(HEADS_PER_STEP)]
    st, gates, masks, qms, maxes, mrows, outs = {}, {}, {}, {}, {}, {}, {}

    def slot_of(n):
        return n % len(t_scr)

    def setup():
        vt_scr[0:LANES, :] = v_ref[...].astype(F32).T.astype(BF16)
        vt_scr[LANES:LANES + ONES_ROWS, :] = jnp.ones((ONES_ROWS, S), BF16)
        km = jnp.concatenate(
            [jnp.sum(k_ref[n * L:(n + 1) * L, :].astype(F32), axis=0, keepdims=True) * (1.0 / L)
             for n in range(NB)], axis=0)
        kmm = jnp.concatenate([jnp.where(head_lanes[hh], km, 0.0) for hh in range(HEADS_PER_STEP)],
                              axis=0)
        st["km_hi"] = kmm.astype(BF16)
        st["km_lo"] = (kmm - st["km_hi"].astype(F32)).astype(BF16)

    def begin_body(n):
        i, hh = bodies[n]
        q_i = q_ref[i * L:(i + 1) * L, :]
        qms[n] = jnp.where(head_lanes[hh], q_i, jnp.zeros_like(q_i))
        if i <= MOBA_TOPK:
            return
        if i not in gates:
            gates[i] = _dot_nt(st["km_hi"], q_i) + _dot_nt(st["km_lo"], q_i)
        g = gates[i][hh * NB:(hh + 1) * NB]
        cnt = jnp.zeros((NB, L), jnp.int32)
        for m in range(i):
            gm = g[m:m + 1, :]
            beats = (gm > g) | ((gm == g) & (m < row))
            cnt = cnt + jnp.where(beats, 1, 0)
        sel = cnt < MOBA_TOPK
        cfar = cfar_ref[hp * HEADS_PER_STEP + hh]
        masks[n] = (jnp.where(sel, cfar, NEG),
                    jnp.where(sel, 0.0, NEG))

    def score_tile(n, j):
        i, hh = bodies[n]
        t = _dot_nt(k_ref[j * L:(j + 1) * L, :], qms[n])
        if j == i:
            t = t + bias_ref[hh, 0]
        elif j == i - 1:
            t = t + bias_ref[hh, 1]
            if i > MOBA_TOPK:
                t = t + masks[n][1][j:j + 1, :]
        elif i > MOBA_TOPK:
            t = t + masks[n][0][j:j + 1, :]
        else:
            t = t + cfar_ref[hp * HEADS_PER_STEP + hh]
        t_scr[slot_of(n)][j * L:(j + 1) * L, :] = t
        tm = jnp.max(t.reshape(L // 8, 8, L), axis=0)
        maxes[n] = tm if j == 0 else jnp.maximum(maxes[n], tm)
        if j == i:
            mrows[n] = jnp.max(maxes.pop(n), axis=0, keepdims=True)
            qms.pop(n)
            masks.pop(n, None)

    def exp_tile(n, j):
        t_ref, p_ref = t_scr[slot_of(n)], p_scr[slot_of(n)]
        p_ref[j * L:(j + 1) * L, :] = jnp.exp2(t_ref[j * L:(j + 1) * L, :] - mrows[n]).astype(BF16)

    def pv_stage(n):
        i, hh = bodies[n]
        nk = (i + 1) * L
        mrows.pop(n)
        acc = jnp.dot(vt_scr[:, 0:nk], p_scr[slot_of(n)][0:nk, :], preferred_element_type=F32)
        outs[n] = acc[hh * MOBA_HEAD_DIM:(hh + 1) * MOBA_HEAD_DIM] / acc[LANES:LANES + 1]
        if hh == HEADS_PER_STEP - 1:
            pair = [outs.pop(n - HEADS_PER_STEP + 1 + h) for h in range(HEADS_PER_STEP)]
            o_ref[0, i * L:(i + 1) * L, :] = jnp.concatenate(pair, axis=0).T

    ops = [setup]
    nb = len(bodies)
    for n0 in range(min(LOOKAHEAD, nb)):
        ops.append(functools.partial(begin_body, n0))
        ops += [functools.partial(score_tile, n0, j) for j in range(bodies[n0][0] + 1)]
    for n in range(nb):
        n_exp = bodies[n][0] + 1
        ahead = n + LOOKAHEAD
        n_score = bodies[ahead][0] + 1 if ahead < nb else 0
        if n_score:
            ops.append(functools.partial(begin_body, ahead))
        for j in range(max(n_exp, n_score)):
            if j < n_score:
                ops.append(functools.partial(score_tile, ahead, j))
            if j < n_exp:
                ops.append(functools.partial(exp_tile, n, j))
        ops.append(functools.partial(pv_stage, n))
    return ops


def _merge(major, minor):
    out, k = [], 0
    for idx, op in enumerate(major):
        out.append(op)
        while k < len(minor) and (k + 1) * len(major) <= (idx + 1) * (len(minor) + 1):
            out.append(minor[k])
            k += 1
    return out + minor[k:]


def _mix_kernel(x_ref, g_ref, w_ref, lng_ref, lnb_ref, wsp_ref, bsp_ref, gn_ref, bias_ref, cfar_ref,
                *refs, n_cast, n_batch):
    cast_src, ya_ref, yb_ref = refs[:n_cast], refs[n_cast], refs[n_cast + 1]
    cast_dst = refs[n_cast + 2:2 * n_cast + 2]
    qkv_w, qkv_r, uz_scr, vt_scr = refs[2 * n_cast + 2:2 * n_cast + 6]
    slots = refs[2 * n_cast + 6:]
    t_scr, p_scr = slots[:len(slots) // 2], slots[len(slots) // 2:]
    b = pl.program_id(0)
    hp = pl.program_id(1)
    _cast_slabs(cast_src, cast_dst)

    def inproj():
        return _inproj_ops(hp, x_ref, g_ref, w_ref, lng_ref, lnb_ref, wsp_ref, bsp_ref, gn_ref,
                           ya_ref, qkv_w, uz_scr)

    def moba():
        return _moba_ops(hp, qkv_r, bias_ref, cfar_ref, yb_ref, vt_scr, t_scr, p_scr)

    @pl.when((b > 0) & (hp == 0))
    def _():
        qkv_r[...] = qkv_w[...]

    @pl.when(b == 0)
    def _():
        for op in inproj():
            op()

    @pl.when((b > 0) & (b < n_batch))
    def _():
        for op in _merge(moba(), inproj()):
            op()

    @pl.when(b == n_batch)
    def _():
        for op in moba():
            op()


def _token_mix(x2, n_batch, g, w_bf, ln_g, ln_b, w_sp, b_sp_full, gn, bias_t, cfar, cast_weights, tm):
    T, D = x2.shape
    S = T // n_batch
    assert S == N_HEAD_PAIRS * tm
    n_all = w_bf.shape[1]
    n_qkv = 3 * D_MOBA
    L, C = MOBA_BLOCK, GMLP_CHUNK
    last_tile = T // tm - 1
    tile = lambda b, hp: (jnp.minimum(b * N_HEAD_PAIRS + hp, last_tile), 0)
    const2 = lambda b, hp: (0, 0)
    vec = pl.BlockSpec((1, D_GMLP), const2)
    c_in, c_out, c_shapes = _cast_specs(
        cast_weights, T // tm, lambda b, hp: jnp.minimum(b * N_HEAD_PAIRS + hp, last_tile))
    outs = pl.pallas_call(
        functools.partial(_mix_kernel, n_cast=len(cast_weights), n_batch=n_batch),
        grid=(n_batch + 1, N_HEAD_PAIRS),
        in_specs=[
            pl.BlockSpec((tm, D), tile),
            pl.BlockSpec((1, D), const2),
            pl.BlockSpec((D, n_all), const2, pipeline_mode=pl.Buffered(1)),
            vec, vec,
            pl.BlockSpec((GMLP_GROUPS, C, C), lambda b, hp: (0, 0, 0)),
            pl.BlockSpec((C, D_GMLP), const2),
            vec,
            pl.BlockSpec((HEADS_PER_STEP, 2, L, L), lambda b, hp: (hp, 0, 0, 0)),
            pl.BlockSpec(memory_space=pltpu.SMEM),
        ] + c_in,
        out_specs=[
            pl.BlockSpec((tm, D_GMLP), tile),
            pl.BlockSpec((1, S, LANES),
                         lambda b, hp: (jnp.maximum(b - 1, 0), 0, jnp.where(b == 0, 0, hp))),
        ] + c_out,
        out_shape=[
            jax.ShapeDtypeStruct((T, D_GMLP), BF16),
            jax.ShapeDtypeStruct((n_batch, S, D_MOBA), F32),
        ] + c_shapes,
        scratch_shapes=[
            pltpu.VMEM((n_qkv // LANES, S, LANES), BF16),
            pltpu.VMEM((n_qkv // LANES, S, LANES), BF16),
            pltpu.VMEM((tm, n_all - n_qkv), F32),
            pltpu.VMEM((LANES + ONES_ROWS, S), BF16),
        ] + [pltpu.VMEM((S, L), F32)] * N_SLOTS + [pltpu.VMEM((S, L), BF16)] * N_SLOTS,
        compiler_params=pltpu.CompilerParams(
            dimension_semantics=("arbitrary", "arbitrary"), vmem_limit_bytes=VMEM_LIMIT),
        name="token_mix",
    )(x2, g, w_bf, ln_g, ln_b, w_sp, b_sp_full, gn, bias_t, cfar, *cast_weights)
    return outs[0], outs[1], outs[2:]


def _ffn_kernel(x_ref, yb_ref, ya_ref, gb_ref, wo_ref, gf_ref, wg_ref, wu_ref, wd_ref, gl_ref,
                o_ref, *, final_norm):
    yb = yb_ref[...]
    ms = jnp.mean(yb * yb, axis=-1, keepdims=True)
    ybn = (yb * lax.rsqrt(ms + EPS) * gb_ref[...]).astype(BF16)
    x1 = (x_ref[...]
          + jnp.dot(ybn, wo_ref[:D_MOBA, :], preferred_element_type=F32)
          + jnp.dot(ya_ref[...], wo_ref[D_MOBA:, :], preferred_element_type=F32))
    ms = jnp.mean(x1 * x1, axis=-1, keepdims=True)
    h = (x1 * lax.rsqrt(ms + EPS) * gf_ref[...]).astype(BF16)
    g = jnp.dot(h, wg_ref[...], preferred_element_type=F32)
    u = jnp.dot(h, wu_ref[...], preferred_element_type=F32)
    a = (jax.nn.silu(g) * u).astype(BF16)
    x2 = x1 + jnp.dot(a, wd_ref[...], preferred_element_type=F32)
    if final_norm:
        ms = jnp.mean(x2 * x2, axis=-1, keepdims=True)
        x2 = x2 * lax.rsqrt(ms + EPS) * gl_ref[...]
    o_ref[...] = x2


def _ffn(x2, yb, ya, gb, wo, gf, wg, wu, wd, gl, tm, final_norm):
    T, D = x2.shape
    d_ff = wg.shape[1]
    const = lambda shape: pl.BlockSpec(shape, lambda i: (0, 0), pipeline_mode=pl.Buffered(1))
    return pl.pallas_call(
        functools.partial(_ffn_kernel, final_norm=final_norm),
        grid=(T // tm,),
        in_specs=[
            pl.BlockSpec((tm, D), lambda i: (i, 0)),
            pl.BlockSpec((tm, D_MOBA), lambda i: (i, 0)),
            pl.BlockSpec((tm, D_GMLP), lambda i: (i, 0)),
            pl.BlockSpec((1, D_MOBA), lambda i: (0, 0)),
            const((D_MOBA + D_GMLP, D)),
            pl.BlockSpec((1, D), lambda i: (0, 0)),
            const((D, d_ff)), const((D, d_ff)), const((d_ff, D)),
            pl.BlockSpec((1, D), lambda i: (0, 0)),
        ],
        out_specs=pl.BlockSpec((tm, D), lambda i: (i, 0)),
        out_shape=jax.ShapeDtypeStruct((T, D), F32),
        compiler_params=pltpu.CompilerParams(
            dimension_semantics=("arbitrary",), vmem_limit_bytes=VMEM_LIMIT),
        name="outproj_ffn",
    )(x2, yb, ya, gb, wo, gf, wg, wu, wd, gl)


def kernel(x, rel_bias, norm_mix, w_in, gmlp_ln_g, gmlp_ln_b, w_spatial, b_spatial, out_norm_b,
           out_norm_a, w_out, norm_ffn, w_gate, w_up, w_down, norm_final):
    B, S, D = x.shape
    depth = w_in.shape[0]
    assert S % MOBA_BLOCK == 0 and S // MOBA_BLOCK > MOBA_TOPK
    far = _t5_bucket_np(np.arange(MOBA_BLOCK + 1, S + MOBA_BLOCK))
    assert (far == far[0]).all()
    far_bucket = int(far[0])

    assert depth == 1, "weight casts ride along the single layer's kernels"
    tm = S // N_HEAD_PAIRS
    T = B * S
    x2 = x.reshape(T, D)
    row = lambda v: v.reshape(1, -1)
    bias_t, (w_in_bf,) = _bias_tiles(rel_bias, [w_in[0]])
    cfar = rel_bias[far_bucket, :] * LOG2E
    b_sp_full = jnp.repeat(b_spatial[0].T, GMLP_GROUP_DIM, axis=1)
    ya, yb, (wo_bf, wg_bf, wu_bf, wd_bf) = _token_mix(
        x2, B, row(norm_mix[0]), w_in_bf, row(gmlp_ln_g[0]), row(gmlp_ln_b[0]), w_spatial[0],
        b_sp_full, row(out_norm_a[0]), bias_t, cfar, [w_out[0], w_gate[0], w_up[0], w_down[0]], tm)
    out = _ffn(x2, yb.reshape(T, D_MOBA), ya, row(out_norm_b[0]), wo_bf, row(norm_ffn[0]),
               wg_bf, wu_bf, wd_bf, row(norm_final), tm, True)
    return out.reshape(B, S, D)
```

```python
import functools
import math

import numpy as np
import jax
import jax.numpy as jnp
from jax import lax
from jax.experimental import pallas as pl
from jax.experimental.pallas import tpu as pltpu

F32 = jnp.float32
BF16 = jnp.bfloat16

MOBA_HEADS = 8
MOBA_HEAD_DIM = 64
D_MOBA = MOBA_HEADS * MOBA_HEAD_DIM
GMLP_GROUPS = 8
GMLP_GROUP_DIM = 64
D_GMLP = GMLP_GROUPS * GMLP_GROUP_DIM
MOBA_BLOCK = 256
MOBA_TOPK = 3
GMLP_CHUNK = 128
NUM_BUCKETS = 32
REL_MAX_DISTANCE = 128
EPS = 1e-6
NEG = -1e9

LANES = 128
HEADS_PER_STEP = LANES // MOBA_HEAD_DIM
BF16_SUBLANES = 16
ONES_ROWS = BF16_SUBLANES
VT_ROWS = MOBA_HEAD_DIM + ONES_ROWS
VMEM_LIMIT = 56 * 1024 * 1024
N_SLOTS = 4
LOOKAHEAD = 3
FAST_LOOKAHEAD = 2

LOG2E = math.log2(math.e)
Q_SCALE = MOBA_HEAD_DIM ** -0.5 * LOG2E

BOUND_SLACK = 1.0 + 2.0 ** -6
SAFE_SPAN = 100.0


def _t5_bucket_np(dist):
    n = np.maximum(dist, 0)
    max_exact = NUM_BUCKETS // 2
    nf = np.maximum(n, max_exact).astype(np.float32)
    ratio = np.log(nf / np.float32(max_exact)) / np.float32(math.log(REL_MAX_DISTANCE / max_exact))
    large = max_exact + (ratio * np.float32(NUM_BUCKETS - max_exact)).astype(np.int32)
    large = np.minimum(large, NUM_BUCKETS - 1)
    return np.where(n < max_exact, n, large).astype(np.int32)


def _cast_specs(weights, n_steps, step_of):
    in_specs, out_specs, out_shapes = [], [], []
    for w in weights:
        rows, cols = w.shape
        period = next(p for p in range(1, n_steps + 1)
                      if n_steps % p == 0 and (rows * p) % (n_steps * BF16_SUBLANES) == 0)
        spec = pl.BlockSpec((rows * period // n_steps, cols),
                            functools.partial(lambda p, *ids: (step_of(*ids) // p, 0), period))
        in_specs.append(spec)
        out_specs.append(spec)
        out_shapes.append(jax.ShapeDtypeStruct(w.shape, BF16))
    return in_specs, out_specs, out_shapes


def _cast_slabs(src_refs, dst_refs):
    for src, dst in zip(src_refs, dst_refs):
        dst[...] = src[...].astype(BF16)


def _bias_kernel(bucket_ref, rel_ref, *refs):
    n_cast = (len(refs) - 1) // 2
    out_ref = refs[n_cast]
    _cast_slabs(refs[:n_cast], refs[n_cast + 1:])
    h = pl.program_id(0)
    L = MOBA_BLOCK
    key = lax.broadcasted_iota(jnp.int32, (L, L), 0)
    qry = lax.broadcasted_iota(jnp.int32, (L, L), 1)
    for delta in range(2):
        bucket = bucket_ref[delta]
        acc = jnp.zeros((L, L), F32)
        for b in range(NUM_BUCKETS):
            acc = jnp.where(bucket == b, rel_ref[b, h] * LOG2E, acc)
        if delta == 0:
            acc = jnp.where(key > qry, NEG, acc)
        out_ref[0, delta] = acc


def _bias_tiles(rel_bias, cast_weights):
    L = MOBA_BLOCK
    kk = np.arange(L)[:, None]
    qq = np.arange(L)[None, :]
    bucket = np.stack([_t5_bucket_np(d * L + qq - kk) for d in range(2)]).astype(np.int32)
    c_in, c_out, c_shapes = _cast_specs(cast_weights, MOBA_HEADS, lambda h: h)
    outs = pl.pallas_call(
        _bias_kernel,
        grid=(MOBA_HEADS,),
        in_specs=[
            pl.BlockSpec((2, L, L), lambda h: (0, 0, 0)),
            pl.BlockSpec(memory_space=pltpu.SMEM),
        ] + c_in,
        out_specs=[pl.BlockSpec((1, 2, L, L), lambda h: (h, 0, 0, 0))] + c_out,
        out_shape=[jax.ShapeDtypeStruct((MOBA_HEADS, 2, L, L), F32)] + c_shapes,
        compiler_params=pltpu.CompilerParams(
            dimension_semantics=("arbitrary",), vmem_limit_bytes=VMEM_LIMIT),
        name="bias_tiles",
    )(jnp.asarray(bucket), rel_bias, *cast_weights)
    return outs[0], outs[1:]


def _inproj_kernel(x_ref, g_ref, w_ref, lng_ref, lnb_ref, wsp_ref, bsp_ref, gn_ref,
                   qkv_ref, ya_ref, uz_scr):
    C = GMLP_CHUNK
    tm = x_ref.shape[0]
    n_qkv = 3 * D_MOBA
    x = x_ref[...]
    ms = jnp.mean(x * x, axis=-1, keepdims=True)
    h = (x * lax.rsqrt(ms + EPS) * g_ref[...]).astype(BF16)
    uz_scr[...] = jnp.dot(h, w_ref[:, n_qkv:], preferred_element_type=F32)

    t_ix = lax.broadcasted_iota(jnp.int32, (C, C), 0)
    s_ix = lax.broadcasted_iota(jnp.int32, (C, C), 1)
    w_sp = [jnp.where(t_ix >= s_ix, wsp_ref[g], 0.0).astype(BF16) for g in range(GMLP_GROUPS)]
    first = lax.broadcasted_iota(jnp.int32, (1, LANES), 1) < GMLP_GROUP_DIM

    def qkv_cols(c0, c1):
        acc = jnp.dot(h, w_ref[:, c0:c1], preferred_element_type=F32)
        if c1 <= D_MOBA:
            acc = acc * Q_SCALE
        qkv_ref[:, c0:c1] = acc.astype(BF16)

    def gmlp_rows(c):
        rows = slice(c * C, (c + 1) * C)
        z = jax.nn.gelu(uz_scr[rows, D_GMLP:])
        mu = jnp.mean(z, axis=-1, keepdims=True)
        zc = z - mu
        var = jnp.mean(zc * zc, axis=-1, keepdims=True)
        zn = (zc * lax.rsqrt(var + EPS) * lng_ref[...] + lnb_ref[...]).astype(BF16)
        cols = []
        for gp in range(D_GMLP // LANES):
            zp = zn[:, gp * LANES:(gp + 1) * LANES]
            r0 = jnp.dot(w_sp[2 * gp], zp, preferred_element_type=F32)
            r1 = jnp.dot(w_sp[2 * gp + 1], zp, preferred_element_type=F32)
            cols.append(jnp.where(first, r0, r1))
        s = jnp.concatenate(cols, axis=1) + bsp_ref[...]
        y = jax.nn.gelu(uz_scr[rows, :D_GMLP]) * s
        ms_y = jnp.mean(y * y, axis=-1, keepdims=True)
        ya_ref[rows, :] = (y * lax.rsqrt(ms_y + EPS) * gn_ref[...]).astype(BF16)

    col_blocks = [(c, c + 2 * LANES) for c in range(0, n_qkv, 2 * LANES)]
    n_chunks = tm // C
    for step in range(max(len(col_blocks), n_chunks)):
        if step < len(col_blocks):
            qkv_cols(*col_blocks[step])
        if step < n_chunks:
            gmlp_rows(step)


def _inproj(x2, g, w_bf, ln_g, ln_b, w_sp, b_sp_full, gn, tm):
    T, D = x2.shape
    n_all = w_bf.shape[1]
    n_qkv = 3 * D_MOBA
    C = GMLP_CHUNK
    vec = pl.BlockSpec((1, D_GMLP), lambda i: (0, 0))
    return pl.pallas_call(
        _inproj_kernel,
        grid=(T // tm,),
        in_specs=[
            pl.BlockSpec((tm, D), lambda i: (i, 0)),
            pl.BlockSpec((1, D), lambda i: (0, 0)),
            pl.BlockSpec((D, n_all), lambda i: (0, 0), pipeline_mode=pl.Buffered(1)),
            vec, vec,
            pl.BlockSpec((GMLP_GROUPS, C, C), lambda i: (0, 0, 0)),
            pl.BlockSpec((C, D_GMLP), lambda i: (0, 0)),
            vec,
        ],
        out_specs=[
            pl.BlockSpec((tm, n_qkv), lambda i: (i, 0)),
            pl.BlockSpec((tm, D_GMLP), lambda i: (i, 0)),
        ],
        out_shape=[
            jax.ShapeDtypeStruct((T, n_qkv), BF16),
            jax.ShapeDtypeStruct((T, D_GMLP), BF16),
        ],
        scratch_shapes=[pltpu.VMEM((tm, n_all - n_qkv), F32)],
        compiler_params=pltpu.CompilerParams(
            dimension_semantics=("arbitrary",), vmem_limit_bytes=VMEM_LIMIT),
        name="inproj_gmlp",
    )(x2, g, w_bf, ln_g, ln_b, w_sp, b_sp_full, gn)


def _dot_nt(a, b):
    return lax.dot_general(a, b, (((1,), (1,)), ((), ())), preferred_element_type=F32)


def _moba_kernel(q_ref, k_ref, v_ref, bias_ref, rel_ref, *refs, n_cast, far_bucket):
    cast_src, o_ref, cast_dst = refs[:n_cast], refs[n_cast], refs[n_cast + 1:2 * n_cast + 1]
    vt_scr, slots = refs[2 * n_cast + 1], refs[2 * n_cast + 2:]
    t_scr, p_scr = slots[:len(slots) // 2], slots[len(slots) // 2:]
    _cast_slabs(cast_src, cast_dst)
    L = MOBA_BLOCK
    S = q_ref.shape[1]
    NB = S // L
    hp = pl.program_id(1)
    lane = lax.broadcasted_iota(jnp.int32, (1, LANES), 1)
    head_lanes = [(lane >= hh * MOBA_HEAD_DIM) & (lane < (hh + 1) * MOBA_HEAD_DIM)
                  for hh in range(HEADS_PER_STEP)]
    row = lax.broadcasted_iota(jnp.int32, (NB, L), 0)
    bodies = [(i, hh) for i in range(NB) for hh in range(HEADS_PER_STEP)]
    nb = len(bodies)

    cfar, bmax, bmin = [], [], []
    for hh in range(HEADS_PER_STEP):
        vals = [rel_ref[b, hp * HEADS_PER_STEP + hh] * LOG2E for b in range(NUM_BUCKETS)]
        cfar.append(vals[far_bucket])
        bmax.append(functools.reduce(jnp.maximum, vals))
        bmin.append(functools.reduce(jnp.minimum, vals))

    def stage_operands():
        vt = v_ref[0].astype(F32).T.astype(BF16)
        for hh in range(HEADS_PER_STEP):
            vt_scr[hh * VT_ROWS:hh * VT_ROWS + MOBA_HEAD_DIM, :] = (
                vt[hh * MOBA_HEAD_DIM:(hh + 1) * MOBA_HEAD_DIM])
            vt_scr[hh * VT_ROWS + MOBA_HEAD_DIM:(hh + 1) * VT_ROWS, :] = jnp.ones((ONES_ROWS, S), BF16)
        km = jnp.concatenate(
            [jnp.sum(k_ref[0, n * L:(n + 1) * L, :].astype(F32), axis=0, keepdims=True) * (1.0 / L)
             for n in range(NB)], axis=0)
        kmm = jnp.concatenate([jnp.where(head_lanes[hh], km, 0.0) for hh in range(HEADS_PER_STEP)],
                              axis=0)
        km_hi = kmm.astype(BF16)
        return km_hi, (kmm - km_hi.astype(F32)).astype(BF16)

    sel_r = lax.broadcasted_iota(jnp.int32, (BF16_SUBLANES, LANES), 0)
    sel_l = lax.broadcasted_iota(jnp.int32, (BF16_SUBLANES, LANES), 1) // MOBA_HEAD_DIM
    head_rows = jnp.where(sel_r == sel_l, 1.0, 0.0).astype(BF16)
    k_all = k_ref[0]
    q_all = q_ref[0]
    kn = jnp.sqrt(jnp.max(_dot_nt(head_rows, k_all * k_all), axis=1, keepdims=True))
    qn = jnp.sqrt(_dot_nt(head_rows, q_all * q_all)) * BOUND_SLACK
    shift = [qn[hh:hh + 1, :] * kn[hh:hh + 1, :] + bmax[hh] for hh in range(HEADS_PER_STEP)]
    span = [2.0 * jnp.max(qn[hh:hh + 1, :], axis=1, keepdims=True) * kn[hh:hh + 1, :]
            + (bmax[hh] - bmin[hh]) for hh in range(HEADS_PER_STEP)]
    safe = jnp.max(functools.reduce(jnp.maximum, span)) <= SAFE_SPAN

    def slot_of(n):
        return n % len(p_scr)

    def block_select(k_means, gates, n):
        i, hh = bodies[n]
        q_i = q_ref[0, i * L:(i + 1) * L, :]
        if i not in gates:
            gates[i] = _dot_nt(k_means[0], q_i) + _dot_nt(k_means[1], q_i)
        g = gates[i][hh * NB:(hh + 1) * NB]
        cnt = jnp.zeros((NB, L), jnp.int32)
        for m in range(i):
            gm = g[m:m + 1, :]
            beats = (gm > g) | ((gm == g) & (m < row))
            cnt = cnt + jnp.where(beats, 1, 0)
        return cnt < MOBA_TOPK

    def masked_q(n):
        i, hh = bodies[n]
        q_i = q_ref[0, i * L:(i + 1) * L, :]
        return jnp.where(head_lanes[hh], q_i, jnp.zeros_like(q_i))

    def pv_stage(outs, n):
        i, hh = bodies[n]
        nk = (i + 1) * L
        acc = jnp.dot(vt_scr[hh * VT_ROWS:(hh + 1) * VT_ROWS, 0:nk], p_scr[slot_of(n)][0:nk, :],
                      preferred_element_type=F32)
        outs[n] = acc[:MOBA_HEAD_DIM] / acc[MOBA_HEAD_DIM:MOBA_HEAD_DIM + 1]
        if hh == HEADS_PER_STEP - 1:
            pair = [outs.pop(n - HEADS_PER_STEP + 1 + h) for h in range(HEADS_PER_STEP)]
            o_ref[0, i * L:(i + 1) * L, :] = jnp.concatenate(pair, axis=0).T

    def exact_path():
        gates, masks, qms, maxes, mrows, outs = {}, {}, {}, {}, {}, {}
        k_means = stage_operands()

        def begin_body(n):
            i, hh = bodies[n]
            qms[n] = masked_q(n)
            if i > MOBA_TOPK:
                sel = block_select(k_means, gates, n)
                masks[n] = (jnp.where(sel, cfar[hh], NEG),
                            jnp.where(sel, 0.0, NEG))

        def score_tile(n, j):
            i, hh = bodies[n]
            t = _dot_nt(k_ref[0, j * L:(j + 1) * L, :], qms[n])
            if j == i:
                t = t + bias_ref[hh, 0]
            elif j == i - 1:
                t = t + bias_ref[hh, 1]
                if i > MOBA_TOPK:
                    t = t + masks[n][1][j:j + 1, :]
            elif i > MOBA_TOPK:
                t = t + masks[n][0][j:j + 1, :]
            else:
                t = t + cfar[hh]
            t_scr[slot_of(n)][j * L:(j + 1) * L, :] = t
            tm = jnp.max(t.reshape(L // 8, 8, L), axis=0)
            maxes[n] = tm if j == 0 else jnp.maximum(maxes[n], tm)
            if j == i:
                mrows[n] = jnp.max(maxes.pop(n), axis=0, keepdims=True)
                qms.pop(n)
                masks.pop(n, None)

        def exp_tile(n, j):
            t_ref, p_ref = t_scr[slot_of(n)], p_scr[slot_of(n)]
            p_ref[j * L:(j + 1) * L, :] = jnp.exp2(t_ref[j * L:(j + 1) * L, :] - mrows[n]).astype(BF16)

        for n0 in range(min(LOOKAHEAD, nb)):
            begin_body(n0)
            for j in range(bodies[n0][0] + 1):
                score_tile(n0, j)
        for n in range(nb):
            n_exp = bodies[n][0] + 1
            ahead = n + LOOKAHEAD
            n_score = bodies[ahead][0] + 1 if ahead < nb else 0
            if n_score:
                begin_body(ahead)
            for j in range(max(n_exp, n_score)):
                if j < n_score:
                    score_tile(ahead, j)
                if j < n_exp:
                    exp_tile(n, j)
            mrows.pop(n)
            pv_stage(outs, n)

    def fast_path():
        gates, outs = {}, {}
        k_means = stage_operands()

        def prob_tiles(n):
            i, hh = bodies[n]
            qm = masked_q(n)
            s_row = shift[hh][:, i * L:(i + 1) * L]
            if i > MOBA_TOPK:
                sel = block_select(k_means, gates, n)
                m_far = jnp.where(sel, cfar[hh], NEG) - s_row
                m_prev = jnp.where(sel, 0.0, NEG) - s_row
            for j in range(i + 1):
                e = _dot_nt(k_ref[0, j * L:(j + 1) * L, :], qm)
                if j == i:
                    e = e + bias_ref[hh, 0] - s_row
                elif j == i - 1:
                    e = e + bias_ref[hh, 1] + (m_prev[j:j + 1, :] if i > MOBA_TOPK else -s_row)
                else:
                    e = e + (m_far[j:j + 1, :] if i > MOBA_TOPK else cfar[hh] - s_row)
                p_scr[slot_of(n)][j * L:(j + 1) * L, :] = jnp.exp2(e).astype(BF16)

        for n0 in range(min(FAST_LOOKAHEAD, nb)):
            prob_tiles(n0)
        for n in range(nb):
            if n + FAST_LOOKAHEAD < nb:
                prob_tiles(n + FAST_LOOKAHEAD)
            pv_stage(outs, n)

    pl.when(safe)(fast_path)
    pl.when(jnp.logical_not(safe))(exact_path)


def _moba(qkv3, bias_t, rel_bias, far_bucket, cast_weights):
    B, S, _ = qkv3.shape
    L = MOBA_BLOCK
    n_hp = MOBA_HEADS // HEADS_PER_STEP
    blk = lambda off: pl.BlockSpec((1, S, LANES), lambda b, hp: (b, 0, off + hp))
    c_in, c_out, c_shapes = _cast_specs(cast_weights, B * n_hp, lambda b, hp: b * n_hp + hp)
    outs = pl.pallas_call(
        functools.partial(_moba_kernel, n_cast=len(cast_weights), far_bucket=far_bucket),
        grid=(B, n_hp),
        in_specs=[
            blk(0), blk(n_hp), blk(2 * n_hp),
            pl.BlockSpec((HEADS_PER_STEP, 2, L, L), lambda b, hp: (hp, 0, 0, 0)),
            pl.BlockSpec(memory_space=pltpu.SMEM),
        ] + c_in,
        out_specs=[pl.BlockSpec((1, S, LANES), lambda b, hp: (b, 0, hp))] + c_out,
        out_shape=[jax.ShapeDtypeStruct((B, S, D_MOBA), F32)] + c_shapes,
        scratch_shapes=[
            pltpu.VMEM((HEADS_PER_STEP * VT_ROWS, S), BF16),
        ] + [pltpu.VMEM((S, L), F32)] * N_SLOTS + [pltpu.VMEM((S, L), BF16)] * N_SLOTS,
        compiler_params=pltpu.CompilerParams(
            dimension_semantics=("arbitrary", "arbitrary"), vmem_limit_bytes=VMEM_LIMIT),
        name="moba",
    )(qkv3, qkv3, qkv3, bias_t, rel_bias, *cast_weights)
    return outs[0], outs[1:]


def _ffn_kernel(x_ref, yb_ref, ya_ref, gb_ref, wo_ref, gf_ref, wg_ref, wu_ref, wd_ref, gl_ref,
                o_ref, *, final_norm):
    yb = yb_ref[...]
    ms = jnp.mean(yb * yb, axis=-1, keepdims=True)
    ybn = (yb * lax.rsqrt(ms + EPS) * gb_ref[...]).astype(BF16)
    x1 = (x_ref[...]
          + jnp.dot(ybn, wo_ref[:D_MOBA, :], preferred_element_type=F32)
          + jnp.dot(ya_ref[...], wo_ref[D_MOBA:, :], preferred_element_type=F32))
    ms = jnp.mean(x1 * x1, axis=-1, keepdims=True)
    h = (x1 * lax.rsqrt(ms + EPS) * gf_ref[...]).astype(BF16)
    g = jnp.dot(h, wg_ref[...], preferred_element_type=F32)
    u = jnp.dot(h, wu_ref[...], preferred_element_type=F32)
    a = (jax.nn.silu(g) * u).astype(BF16)
    x2 = x1 + jnp.dot(a, wd_ref[...], preferred_element_type=F32)
    if final_norm:
        ms = jnp.mean(x2 * x2, axis=-1, keepdims=True)
        x2 = x2 * lax.rsqrt(ms + EPS) * gl_ref[...]
    o_ref[...] = x2


def _ffn(x2, yb, ya, gb, wo, gf, wg, wu, wd, gl, tm, final_norm):
    T, D = x2.shape
    d_ff = wg.shape[1]
    const = lambda shape: pl.BlockSpec(shape, lambda i: (0, 0), pipeline_mode=pl.Buffered(1))
    return pl.pallas_call(
        functools.partial(_ffn_kernel, final_norm=final_norm),
        grid=(T // tm,),
        in_specs=[
            pl.BlockSpec((tm, D), lambda i: (i, 0)),
            pl.BlockSpec((tm, D_MOBA), lambda i: (i, 0)),
            pl.BlockSpec((tm, D_GMLP), lambda i: (i, 0)),
            pl.BlockSpec((1, D_MOBA), lambda i: (0, 0)),
            const((D_MOBA + D_GMLP, D)),
            pl.BlockSpec((1, D), lambda i: (0, 0)),
            const((D, d_ff)), const((D, d_ff)), const((d_ff, D)),
            pl.BlockSpec((1, D), lambda i: (0, 0)),
        ],
        out_specs=pl.BlockSpec((tm, D), lambda i: (i, 0)),
        out_shape=jax.ShapeDtypeStruct((T, D), F32),
        compiler_params=pltpu.CompilerParams(
            dimension_semantics=("arbitrary",), vmem_limit_bytes=VMEM_LIMIT),
        name="outproj_ffn",
    )(x2, yb, ya, gb, wo, gf, wg, wu, wd, gl)


def kernel(x, rel_bias, norm_mix, w_in, gmlp_ln_g, gmlp_ln_b, w_spatial, b_spatial, out_norm_b,
           out_norm_a, w_out, norm_ffn, w_gate, w_up, w_down, norm_final):
    B, S, D = x.shape
    depth = w_in.shape[0]
    assert S % MOBA_BLOCK == 0 and S // MOBA_BLOCK > MOBA_TOPK
    far = _t5_bucket_np(np.arange(MOBA_BLOCK + 1, S + MOBA_BLOCK))
    assert (far == far[0]).all()
    far_bucket = int(far[0])

    assert depth == 1, "weight casts ride along the single layer's kernels"
    tm = 512
    T = B * S
    x2 = x.reshape(T, D)
    row = lambda v: v.reshape(1, -1)
    bias_t, (w_in_bf,) = _bias_tiles(rel_bias, [w_in[0]])
    b_sp_full = jnp.repeat(b_spatial[0].T, GMLP_GROUP_DIM, axis=1)
    qkv, ya = _inproj(x2, row(norm_mix[0]), w_in_bf, row(gmlp_ln_g[0]), row(gmlp_ln_b[0]),
                      w_spatial[0], b_sp_full, row(out_norm_a[0]), tm)
    yb, (wo_bf, wg_bf, wu_bf, wd_bf) = _moba(qkv.reshape(B, S, -1), bias_t, rel_bias, far_bucket,
                                             [w_out[0], w_gate[0], w_up[0], w_down[0]])
    out = _ffn(x2, yb.reshape(T, D_MOBA), ya, row(out_norm_b[0]), wo_bf, row(norm_ffn[0]),
               wg_bf, wu_bf, wd_bf, row(norm_final), tm, True)
    return out.reshape(B, S, D)
```

```python
import functools
import math

import numpy as np
import jax
import jax.numpy as jnp
from jax import lax
from jax.experimental import pallas as pl
from jax.experimental.pallas import tpu as pltpu

F32 = jnp.float32
BF16 = jnp.bfloat16

MOBA_HEADS = 8
MOBA_HEAD_DIM = 64
D_MOBA = MOBA_HEADS * MOBA_HEAD_DIM
GMLP_GROUPS = 8
GMLP_GROUP_DIM = 64
D_GMLP = GMLP_GROUPS * GMLP_GROUP_DIM
MOBA_BLOCK = 256
MOBA_TOPK = 3
GMLP_CHUNK = 128
NUM_BUCKETS = 32
REL_MAX_DISTANCE = 128
EPS = 1e-6
NEG = -1e9

LANES = 128
HEADS_PER_STEP = LANES // MOBA_HEAD_DIM
BF16_SUBLANES = 16
ONES_ROWS = BF16_SUBLANES
VT_ROWS = MOBA_HEAD_DIM + ONES_ROWS
VMEM_LIMIT = 56 * 1024 * 1024
N_SLOTS = 4
LOOKAHEAD = 3
FAST_LOOKAHEAD = 2

LOG2E = math.log2(math.e)
Q_SCALE = MOBA_HEAD_DIM ** -0.5 * LOG2E

BOUND_SLACK = 1.0 + 2.0 ** -6
SAFE_SPAN = 100.0


def _t5_bucket_np(dist):
    n = np.maximum(dist, 0)
    max_exact = NUM_BUCKETS // 2
    nf = np.maximum(n, max_exact).astype(np.float32)
    ratio = np.log(nf / np.float32(max_exact)) / np.float32(math.log(REL_MAX_DISTANCE / max_exact))
    large = max_exact + (ratio * np.float32(NUM_BUCKETS - max_exact)).astype(np.int32)
    large = np.minimum(large, NUM_BUCKETS - 1)
    return np.where(n < max_exact, n, large).astype(np.int32)


def _cast_specs(weights, n_steps, step_of):
    in_specs, out_specs, out_shapes = [], [], []
    for w in weights:
        rows, cols = w.shape
        period = next(p for p in range(1, n_steps + 1)
                      if n_steps % p == 0 and (rows * p) % (n_steps * BF16_SUBLANES) == 0)
        spec = pl.BlockSpec((rows * period // n_steps, cols),
                            functools.partial(lambda p, *ids: (step_of(*ids) // p, 0), period))
        in_specs.append(spec)
        out_specs.append(spec)
        out_shapes.append(jax.ShapeDtypeStruct(w.shape, BF16))
    return in_specs, out_specs, out_shapes


def _cast_slabs(src_refs, dst_refs):
    for src, dst in zip(src_refs, dst_refs):
        dst[...] = src[...].astype(BF16)


def _bias_kernel(bucket_ref, rel_ref, *refs):
    n_cast = (len(refs) - 1) // 2
    out_ref = refs[n_cast]
    _cast_slabs(refs[:n_cast], refs[n_cast + 1:])
    h = pl.program_id(0)
    L = MOBA_BLOCK
    key = lax.broadcasted_iota(jnp.int32, (L, L), 0)
    qry = lax.broadcasted_iota(jnp.int32, (L, L), 1)
    for delta in range(2):
        bucket = bucket_ref[delta]
        acc = jnp.zeros((L, L), F32)
        for b in range(NUM_BUCKETS):
            acc = jnp.where(bucket == b, rel_ref[b, h] * LOG2E, acc)
        if delta == 0:
            acc = jnp.where(key > qry, NEG, acc)
        out_ref[0, delta] = acc


def _bias_tiles(rel_bias, cast_weights):
    L = MOBA_BLOCK
    kk = np.arange(L)[:, None]
    qq = np.arange(L)[None, :]
    bucket = np.stack([_t5_bucket_np(d * L + qq - kk) for d in range(2)]).astype(np.int32)
    c_in, c_out, c_shapes = _cast_specs(cast_weights, MOBA_HEADS, lambda h: h)
    outs = pl.pallas_call(
        _bias_kernel,
        grid=(MOBA_HEADS,),
        in_specs=[
            pl.BlockSpec((2, L, L), lambda h: (0, 0, 0)),
            pl.BlockSpec(memory_space=pltpu.SMEM),
        ] + c_in,
        out_specs=[pl.BlockSpec((1, 2, L, L), lambda h: (h, 0, 0, 0))] + c_out,
        out_shape=[jax.ShapeDtypeStruct((MOBA_HEADS, 2, L, L), F32)] + c_shapes,
        compiler_params=pltpu.CompilerParams(
            dimension_semantics=("arbitrary",), vmem_limit_bytes=VMEM_LIMIT),
        name="bias_tiles",
    )(jnp.asarray(bucket), rel_bias, *cast_weights)
    return outs[0], outs[1:]


def _inproj_kernel(x_ref, g_ref, w_ref, lng_ref, lnb_ref, wsp_ref, bsp_ref, gn_ref,
                   qkv_ref, ya_ref, norm_ref, uz_scr):
    C = GMLP_CHUNK
    tm = x_ref.shape[0]
    n_qkv = 3 * D_MOBA
    x = x_ref[...]
    ms = jnp.mean(x * x, axis=-1, keepdims=True)
    h = (x * lax.rsqrt(ms + EPS) * g_ref[...]).astype(BF16)
    uz_scr[...] = jnp.dot(h, w_ref[:, n_qkv:], preferred_element_type=F32)

    t_ix = lax.broadcasted_iota(jnp.int32, (C, C), 0)
    s_ix = lax.broadcasted_iota(jnp.int32, (C, C), 1)
    w_sp = [jnp.where(t_ix >= s_ix, wsp_ref[g], 0.0).astype(BF16) for g in range(GMLP_GROUPS)]
    first = lax.broadcasted_iota(jnp.int32, (1, LANES), 1) < GMLP_GROUP_DIM

    def qkv_cols(c0, c1):
        acc = jnp.dot(h, w_ref[:, c0:c1], preferred_element_type=F32)
        if c1 <= D_MOBA:
            acc = acc * Q_SCALE
        qkv_ref[:, c0:c1] = acc.astype(BF16)
        if c1 <= 2 * D_MOBA:
            sq = acc * acc
            for c in range(c0, c1, MOBA_HEAD_DIM):
                lanes = (c - c0) // LANES * LANES
                own = first if (c - c0) % LANES == 0 else jnp.logical_not(first)
                rows_sq = jnp.sum(jnp.where(own, sq[:, lanes:lanes + LANES], 0.0), axis=-1, keepdims=True)
                top = jnp.max(rows_sq, axis=0, keepdims=True)
                norm_ref[0, c // MOBA_HEAD_DIM:c // MOBA_HEAD_DIM + 1, :] = jnp.broadcast_to(top, (1, LANES))

    def gmlp_rows(c):
        rows = slice(c * C, (c + 1) * C)
        z = jax.nn.gelu(uz_scr[rows, D_GMLP:])
        mu = jnp.mean(z, axis=-1, keepdims=True)
        zc = z - mu
        var = jnp.mean(zc * zc, axis=-1, keepdims=True)
        zn = (zc * lax.rsqrt(var + EPS) * lng_ref[...] + lnb_ref[...]).astype(BF16)
        cols = []
        for gp in range(D_GMLP // LANES):
            zp = zn[:, gp * LANES:(gp + 1) * LANES]
            r0 = jnp.dot(w_sp[2 * gp], zp, preferred_element_type=F32)
            r1 = jnp.dot(w_sp[2 * gp + 1], zp, preferred_element_type=F32)
            cols.append(jnp.where(first, r0, r1))
        s = jnp.concatenate(cols, axis=1) + bsp_ref[...]
        y = jax.nn.gelu(uz_scr[rows, :D_GMLP]) * s
        ms_y = jnp.mean(y * y, axis=-1, keepdims=True)
        ya_ref[rows, :] = (y * lax.rsqrt(ms_y + EPS) * gn_ref[...]).astype(BF16)

    col_blocks = [(c, c + 2 * LANES) for c in range(0, n_qkv, 2 * LANES)]
    n_chunks = tm // C
    for step in range(max(len(col_blocks), n_chunks)):
        if step < len(col_blocks):
            qkv_cols(*col_blocks[step])
        if step < n_chunks:
            gmlp_rows(step)


def _inproj(x2, g, w_bf, ln_g, ln_b, w_sp, b_sp_full, gn, tm):
    T, D = x2.shape
    n_all = w_bf.shape[1]
    n_qkv = 3 * D_MOBA
    C = GMLP_CHUNK
    vec = pl.BlockSpec((1, D_GMLP), lambda i: (0, 0))
    return pl.pallas_call(
        _inproj_kernel,
        grid=(T // tm,),
        in_specs=[
            pl.BlockSpec((tm, D), lambda i: (i, 0)),
            pl.BlockSpec((1, D), lambda i: (0, 0)),
            pl.BlockSpec((D, n_all), lambda i: (0, 0), pipeline_mode=pl.Buffered(1)),
            vec, vec,
            pl.BlockSpec((GMLP_GROUPS, C, C), lambda i: (0, 0, 0)),
            pl.BlockSpec((C, D_GMLP), lambda i: (0, 0)),
            vec,
        ],
        out_specs=[
            pl.BlockSpec((tm, n_qkv), lambda i: (i, 0)),
            pl.BlockSpec((tm, D_GMLP), lambda i: (i, 0)),
            pl.BlockSpec((1, 2 * MOBA_HEADS, LANES), lambda i: (i, 0, 0)),
        ],
        out_shape=[
            jax.ShapeDtypeStruct((T, n_qkv), BF16),
            jax.ShapeDtypeStruct((T, D_GMLP), BF16),
            jax.ShapeDtypeStruct((T // tm, 2 * MOBA_HEADS, LANES), F32),
        ],
        scratch_shapes=[pltpu.VMEM((tm, n_all - n_qkv), F32)],
        compiler_params=pltpu.CompilerParams(
            dimension_semantics=("arbitrary",), vmem_limit_bytes=VMEM_LIMIT),
        name="inproj_gmlp",
    )(x2, g, w_bf, ln_g, ln_b, w_sp, b_sp_full, gn)


def _dot_nt(a, b):
    return lax.dot_general(a, b, (((1,), (1,)), ((), ())), preferred_element_type=F32)


def _moba_kernel(q_ref, k_ref, v_ref, norm_ref, bias_ref, rel_ref, *refs, n_cast, far_bucket):
    cast_src, o_ref, cast_dst = refs[:n_cast], refs[n_cast], refs[n_cast + 1:2 * n_cast + 1]
    vt_scr, slots = refs[2 * n_cast + 1], refs[2 * n_cast + 2:]
    t_scr, p_scr = slots[:len(slots) // 2], slots[len(slots) // 2:]
    _cast_slabs(cast_src, cast_dst)
    L = MOBA_BLOCK
    S = q_ref.shape[1]
    NB = S // L
    hp = pl.program_id(1)
    lane = lax.broadcasted_iota(jnp.int32, (1, LANES), 1)
    head_lanes = [(lane >= hh * MOBA_HEAD_DIM) & (lane < (hh + 1) * MOBA_HEAD_DIM)
                  for hh in range(HEADS_PER_STEP)]
    row = lax.broadcasted_iota(jnp.int32, (NB, L), 0)
    bodies = [(i, hh) for i in range(NB) for hh in range(HEADS_PER_STEP)]
    nb = len(bodies)

    cfar, bmax, bmin = [], [], []
    for hh in range(HEADS_PER_STEP):
        vals = [rel_ref[b, hp * HEADS_PER_STEP + hh] * LOG2E for b in range(NUM_BUCKETS)]
        cfar.append(vals[far_bucket])
        bmax.append(functools.reduce(jnp.maximum, vals))
        bmin.append(functools.reduce(jnp.minimum, vals))

    def stage_operands():
        vt = v_ref[0].astype(F32).T.astype(BF16)
        for hh in range(HEADS_PER_STEP):
            vt_scr[hh * VT_ROWS:hh * VT_ROWS + MOBA_HEAD_DIM, :] = (
                vt[hh * MOBA_HEAD_DIM:(hh + 1) * MOBA_HEAD_DIM])
            vt_scr[hh * VT_ROWS + MOBA_HEAD_DIM:(hh + 1) * VT_ROWS, :] = jnp.ones((ONES_ROWS, S), BF16)
        km = jnp.concatenate(
            [jnp.sum(k_ref[0, n * L:(n + 1) * L, :].astype(F32), axis=0, keepdims=True) * (1.0 / L)
             for n in range(NB)], axis=0)
        kmm = jnp.concatenate([jnp.where(head_lanes[hh], km, 0.0) for hh in range(HEADS_PER_STEP)],
                              axis=0)
        km_hi = kmm.astype(BF16)
        return km_hi, (kmm - km_hi.astype(F32)).astype(BF16)

    shift, span = [], []
    for hh in range(HEADS_PER_STEP):
        head = hp * HEADS_PER_STEP + hh
        q_sq = jnp.max(norm_ref[:, pl.ds(head, 1), :], axis=0)[:, :1]
        k_sq = jnp.max(norm_ref[:, pl.ds(MOBA_HEADS + head, 1), :], axis=0)[:, :1]
        reach = jnp.sqrt(q_sq * k_sq) * BOUND_SLACK
        shift.append(reach + bmax[hh])
        span.append(2.0 * reach + (bmax[hh] - bmin[hh]))
    safe = jnp.max(functools.reduce(jnp.maximum, span)) <= SAFE_SPAN

    def slot_of(n):
        return n % len(p_scr)

    def block_select(k_means, gates, n):
        i, hh = bodies[n]
        q_i = q_ref[0, i * L:(i + 1) * L, :]
        if i not in gates:
            gates[i] = _dot_nt(k_means[0], q_i) + _dot_nt(k_means[1], q_i)
        g = gates[i][hh * NB:(hh + 1) * NB]
        cnt = jnp.zeros((NB, L), jnp.int32)
        for m in range(i):
            gm = g[m:m + 1, :]
            beats = (gm > g) | ((gm == g) & (m < row))
            cnt = cnt + jnp.where(beats, 1, 0)
        return cnt < MOBA_TOPK

    def masked_q(n):
        i, hh = bodies[n]
        q_i = q_ref[0, i * L:(i + 1) * L, :]
        return jnp.where(head_lanes[hh], q_i, jnp.zeros_like(q_i))

    def pv_stage(outs, n):
        i, hh = bodies[n]
        nk = (i + 1) * L
        acc = jnp.dot(vt_scr[hh * VT_ROWS:(hh + 1) * VT_ROWS, 0:nk], p_scr[slot_of(n)][0:nk, :],
                      preferred_element_type=F32)
        outs[n] = acc[:MOBA_HEAD_DIM] / acc[MOBA_HEAD_DIM:MOBA_HEAD_DIM + 1]
        if hh == HEADS_PER_STEP - 1:
            pair = [outs.pop(n - HEADS_PER_STEP + 1 + h) for h in range(HEADS_PER_STEP)]
            o_ref[0, i * L:(i + 1) * L, :] = jnp.concatenate(pair, axis=0).T

    def exact_path():
        gates, masks, qms, maxes, mrows, outs = {}, {}, {}, {}, {}, {}
        k_means = stage_operands()

        def begin_body(n):
            i, hh = bodies[n]
            qms[n] = masked_q(n)
            if i > MOBA_TOPK:
                sel = block_select(k_means, gates, n)
                masks[n] = (jnp.where(sel, cfar[hh], NEG),
                            jnp.where(sel, 0.0, NEG))

        def score_tile(n, j):
            i, hh = bodies[n]
            t = _dot_nt(k_ref[0, j * L:(j + 1) * L, :], qms[n])
            if j == i:
                t = t + bias_ref[hh, 0]
            elif j == i - 1:
                t = t + bias_ref[hh, 1]
                if i > MOBA_TOPK:
                    t = t + masks[n][1][j:j + 1, :]
            elif i > MOBA_TOPK:
                t = t + masks[n][0][j:j + 1, :]
            else:
                t = t + cfar[hh]
            t_scr[slot_of(n)][j * L:(j + 1) * L, :] = t
            tm = jnp.max(t.reshape(L // 8, 8, L), axis=0)
            maxes[n] = tm if j == 0 else jnp.maximum(maxes[n], tm)
            if j == i:
                mrows[n] = jnp.max(maxes.pop(n), axis=0, keepdims=True)
                qms.pop(n)
                masks.pop(n, None)

        def exp_tile(n, j):
            t_ref, p_ref = t_scr[slot_of(n)], p_scr[slot_of(n)]
            p_ref[j * L:(j + 1) * L, :] = jnp.exp2(t_ref[j * L:(j + 1) * L, :] - mrows[n]).astype(BF16)

        for n0 in range(min(LOOKAHEAD, nb)):
            begin_body(n0)
            for j in range(bodies[n0][0] + 1):
                score_tile(n0, j)
        for n in range(nb):
            n_exp = bodies[n][0] + 1
            ahead = n + LOOKAHEAD
            n_score = bodies[ahead][0] + 1 if ahead < nb else 0
            if n_score:
                begin_body(ahead)
            for j in range(max(n_exp, n_score)):
                if j < n_score:
                    score_tile(ahead, j)
                if j < n_exp:
                    exp_tile(n, j)
            mrows.pop(n)
            pv_stage(outs, n)

    def fast_path():
        gates, outs = {}, {}
        k_means = stage_operands()

        def prob_tiles(n):
            i, hh = bodies[n]
            qm = masked_q(n)
            s_row = shift[hh]
            if i > MOBA_TOPK:
                sel = block_select(k_means, gates, n)
                m_far = jnp.where(sel, cfar[hh], NEG) - s_row
                m_prev = jnp.where(sel, 0.0, NEG) - s_row
            for j in range(i + 1):
                e = _dot_nt(k_ref[0, j * L:(j + 1) * L, :], qm)
                if j == i:
                    e = e + bias_ref[hh, 0] - s_row
                elif j == i - 1:
                    e = e + bias_ref[hh, 1] + (m_prev[j:j + 1, :] if i > MOBA_TOPK else -s_row)
                else:
                    e = e + (m_far[j:j + 1, :] if i > MOBA_TOPK else cfar[hh] - s_row)
                p_scr[slot_of(n)][j * L:(j + 1) * L, :] = jnp.exp2(e).astype(BF16)

        for n0 in range(min(FAST_LOOKAHEAD, nb)):
            prob_tiles(n0)
        for n in range(nb):
            if n + FAST_LOOKAHEAD < nb:
                prob_tiles(n + FAST_LOOKAHEAD)
            pv_stage(outs, n)

    pl.when(safe)(fast_path)
    pl.when(jnp.logical_not(safe))(exact_path)


def _moba(qkv3, norms, bias_t, rel_bias, far_bucket, cast_weights):
    B, S, _ = qkv3.shape
    L = MOBA_BLOCK
    n_hp = MOBA_HEADS // HEADS_PER_STEP
    blk = lambda off: pl.BlockSpec((1, S, LANES), lambda b, hp: (b, 0, off + hp))
    c_in, c_out, c_shapes = _cast_specs(cast_weights, B * n_hp, lambda b, hp: b * n_hp + hp)
    outs = pl.pallas_call(
        functools.partial(_moba_kernel, n_cast=len(cast_weights), far_bucket=far_bucket),
        grid=(B, n_hp),
        in_specs=[
            blk(0), blk(n_hp), blk(2 * n_hp),
            pl.BlockSpec((norms.shape[0] // B,) + norms.shape[1:], lambda b, hp: (b, 0, 0)),
            pl.BlockSpec((HEADS_PER_STEP, 2, L, L), lambda b, hp: (hp, 0, 0, 0)),
            pl.BlockSpec(memory_space=pltpu.SMEM),
        ] + c_in,
        out_specs=[pl.BlockSpec((1, S, LANES), lambda b, hp: (b, 0, hp))] + c_out,
        out_shape=[jax.ShapeDtypeStruct((B, S, D_MOBA), F32)] + c_shapes,
        scratch_shapes=[
            pltpu.VMEM((HEADS_PER_STEP * VT_ROWS, S), BF16),
        ] + [pltpu.VMEM((S, L), F32)] * N_SLOTS + [pltpu.VMEM((S, L), BF16)] * N_SLOTS,
        compiler_params=pltpu.CompilerParams(
            dimension_semantics=("arbitrary", "arbitrary"), vmem_limit_bytes=VMEM_LIMIT),
        name="moba",
    )(qkv3, qkv3, qkv3, norms, bias_t, rel_bias, *cast_weights)
    return outs[0], outs[1:]


def _ffn_kernel(x_ref, yb_ref, ya_ref, gb_ref, wo_ref, gf_ref, wg_ref, wu_ref, wd_ref, gl_ref,
                o_ref, *, final_norm):
    yb = yb_ref[...]
    ms = jnp.mean(yb * yb, axis=-1, keepdims=True)
    ybn = (yb * lax.rsqrt(ms + EPS) * gb_ref[...]).astype(BF16)
    x1 = (x_ref[...]
          + jnp.dot(ybn, wo_ref[:D_MOBA, :], preferred_element_type=F32)
          + jnp.dot(ya_ref[...], wo_ref[D_MOBA:, :], preferred_element_type=F32))
    ms = jnp.mean(x1 * x1, axis=-1, keepdims=True)
    h = (x1 * lax.rsqrt(ms + EPS) * gf_ref[...]).astype(BF16)
    g = jnp.dot(h, wg_ref[...], preferred_element_type=F32)
    u = jnp.dot(h, wu_ref[...], preferred_element_type=F32)
    a = (jax.nn.silu(g) * u).astype(BF16)
    x2 = x1 + jnp.dot(a, wd_ref[...], preferred_element_type=F32)
    if final_norm:
        ms = jnp.mean(x2 * x2, axis=-1, keepdims=True)
        x2 = x2 * lax.rsqrt(ms + EPS) * gl_ref[...]
    o_ref[...] = x2


def _ffn(x2, yb, ya, gb, wo, gf, wg, wu, wd, gl, tm, final_norm):
    T, D = x2.shape
    d_ff = wg.shape[1]
    const = lambda shape: pl.BlockSpec(shape, lambda i: (0, 0), pipeline_mode=pl.Buffered(1))
    return pl.pallas_call(
        functools.partial(_ffn_kernel, final_norm=final_norm),
        grid=(T // tm,),
        in_specs=[
            pl.BlockSpec((tm, D), lambda i: (i, 0)),
            pl.BlockSpec((tm, D_MOBA), lambda i: (i, 0)),
            pl.BlockSpec((tm, D_GMLP), lambda i: (i, 0)),
            pl.BlockSpec((1, D_MOBA), lambda i: (0, 0)),
            const((D_MOBA + D_GMLP, D)),
            pl.BlockSpec((1, D), lambda i: (0, 0)),
            const((D, d_ff)), const((D, d_ff)), const((d_ff, D)),
            pl.BlockSpec((1, D), lambda i: (0, 0)),
        ],
        out_specs=pl.BlockSpec((tm, D), lambda i: (i, 0)),
        out_shape=jax.ShapeDtypeStruct((T, D), F32),
        compiler_params=pltpu.CompilerParams(
            dimension_semantics=("arbitrary",), vmem_limit_bytes=VMEM_LIMIT),
        name="outproj_ffn",
    )(x2, yb, ya, gb, wo, gf, wg, wu, wd, gl)


def kernel(x, rel_bias, norm_mix, w_in, gmlp_ln_g, gmlp_ln_b, w_spatial, b_spatial, out_norm_b,
           out_norm_a, w_out, norm_ffn, w_gate, w_up, w_down, norm_final):
    B, S, D = x.shape
    depth = w_in.shape[0]
    assert S % MOBA_BLOCK == 0 and S // MOBA_BLOCK > MOBA_TOPK
    far = _t5_bucket_np(np.arange(MOBA_BLOCK + 1, S + MOBA_BLOCK))
    assert (far == far[0]).all()
    far_bucket = int(far[0])

    assert depth == 1, "weight casts ride along the single layer's kernels"
    tm = 512
    T = B * S
    x2 = x.reshape(T, D)
    row = lambda v: v.reshape(1, -1)
    bias_t, (w_in_bf,) = _bias_tiles(rel_bias, [w_in[0]])
    b_sp_full = jnp.repeat(b_spatial[0].T, GMLP_GROUP_DIM, axis=1)
    qkv, ya, norms = _inproj(x2, row(norm_mix[0]), w_in_bf, row(gmlp_ln_g[0]), row(gmlp_ln_b[0]),
                             w_spatial[0], b_sp_full, row(out_norm_a[0]), tm)
    yb, (wo_bf, wg_bf, wu_bf, wd_bf) = _moba(qkv.reshape(B, S, -1), norms, bias_t, rel_bias,
                                             far_bucket, [w_out[0], w_gate[0], w_up[0], w_down[0]])
    out = _ffn(x2, yb.reshape(T, D_MOBA), ya, row(out_norm_b[0]), wo_bf, row(norm_ffn[0]),
               wg_bf, wu_bf, wd_bf, row(norm_final), tm, True)
    return out.reshape(B, S, D)
```

```python
import functools
import math

import numpy as np
import jax
import jax.numpy as jnp
from jax import lax
from jax.experimental import pallas as pl
from jax.experimental.pallas import tpu as pltpu

F32 = jnp.float32
BF16 = jnp.bfloat16

MOBA_HEADS = 8
MOBA_HEAD_DIM = 64
D_MOBA = MOBA_HEADS * MOBA_HEAD_DIM
GMLP_GROUPS = 8
GMLP_GROUP_DIM = 64
D_GMLP = GMLP_GROUPS * GMLP_GROUP_DIM
MOBA_BLOCK = 256
MOBA_TOPK = 3
GMLP_CHUNK = 128
NUM_BUCKETS = 32
REL_MAX_DISTANCE = 128
EPS = 1e-6
NEG = -1e9

LANES = 128
HEADS_PER_STEP = LANES // MOBA_HEAD_DIM
BF16_SUBLANES = 16
ONES_ROWS = BF16_SUBLANES
VT_ROWS = MOBA_HEAD_DIM + ONES_ROWS
VMEM_LIMIT = 56 * 1024 * 1024
N_SLOTS = 4
LOOKAHEAD = 3
FAST_LOOKAHEAD = 2

LOG2E = math.log2(math.e)
Q_SCALE = MOBA_HEAD_DIM ** -0.5 * LOG2E

BOUND_SLACK = 1.0 + 2.0 ** -6
SAFE_SPAN = 100.0


def _t5_bucket_np(dist):
    n = np.maximum(dist, 0)
    max_exact = NUM_BUCKETS // 2
    nf = np.maximum(n, max_exact).astype(np.float32)
    ratio = np.log(nf / np.float32(max_exact)) / np.float32(math.log(REL_MAX_DISTANCE / max_exact))
    large = max_exact + (ratio * np.float32(NUM_BUCKETS - max_exact)).astype(np.int32)
    large = np.minimum(large, NUM_BUCKETS - 1)
    return np.where(n < max_exact, n, large).astype(np.int32)


def _cast_specs(weights, n_steps, step_of):
    in_specs, out_specs, out_shapes = [], [], []
    for w in weights:
        rows, cols = w.shape
        period = next(p for p in range(1, n_steps + 1)
                      if n_steps % p == 0 and (rows * p) % (n_steps * BF16_SUBLANES) == 0)
        spec = pl.BlockSpec((rows * period // n_steps, cols),
                            functools.partial(lambda p, *ids: (step_of(*ids) // p, 0), period))
        in_specs.append(spec)
        out_specs.append(spec)
        out_shapes.append(jax.ShapeDtypeStruct(w.shape, BF16))
    return in_specs, out_specs, out_shapes


def _cast_slabs(src_refs, dst_refs):
    for src, dst in zip(src_refs, dst_refs):
        dst[...] = src[...].astype(BF16)


def _bias_kernel(bucket_ref, rel_ref, *refs):
    n_cast = (len(refs) - 1) // 2
    out_ref = refs[n_cast]
    _cast_slabs(refs[:n_cast], refs[n_cast + 1:])
    h = pl.program_id(0)
    L = MOBA_BLOCK
    key = lax.broadcasted_iota(jnp.int32, (L, L), 0)
    qry = lax.broadcasted_iota(jnp.int32, (L, L), 1)
    for delta in range(2):
        bucket = bucket_ref[delta]
        acc = jnp.zeros((L, L), F32)
        for b in range(NUM_BUCKETS):
            acc = jnp.where(bucket == b, rel_ref[b, h] * LOG2E, acc)
        if delta == 0:
            acc = jnp.where(key > qry, NEG, acc)
        out_ref[0, delta] = acc


def _bias_tiles(rel_bias, cast_weights):
    L = MOBA_BLOCK
    kk = np.arange(L)[:, None]
    qq = np.arange(L)[None, :]
    bucket = np.stack([_t5_bucket_np(d * L + qq - kk) for d in range(2)]).astype(np.int32)
    c_in, c_out, c_shapes = _cast_specs(cast_weights, MOBA_HEADS, lambda h: h)
    outs = pl.pallas_call(
        _bias_kernel,
        grid=(MOBA_HEADS,),
        in_specs=[
            pl.BlockSpec((2, L, L), lambda h: (0, 0, 0)),
            pl.BlockSpec(memory_space=pltpu.SMEM),
        ] + c_in,
        out_specs=[pl.BlockSpec((1, 2, L, L), lambda h: (h, 0, 0, 0))] + c_out,
        out_shape=[jax.ShapeDtypeStruct((MOBA_HEADS, 2, L, L), F32)] + c_shapes,
        compiler_params=pltpu.CompilerParams(
            dimension_semantics=("arbitrary",), vmem_limit_bytes=VMEM_LIMIT),
        name="bias_tiles",
    )(jnp.asarray(bucket), rel_bias, *cast_weights)
    return outs[0], outs[1:]


def _inproj_kernel(x_ref, g_ref, w_ref, lng_ref, lnb_ref, wsp_ref, bsp_ref, gn_ref,
                   qkv_ref, ya_ref, norm_ref, uz_scr):
    C = GMLP_CHUNK
    tm = x_ref.shape[0]
    n_qkv = 3 * D_MOBA
    x = x_ref[...]
    ms = jnp.mean(x * x, axis=-1, keepdims=True)
    h = (x * lax.rsqrt(ms + EPS) * g_ref[...]).astype(BF16)
    uz_scr[...] = jnp.dot(h, w_ref[:, n_qkv:], preferred_element_type=F32)

    t_ix = lax.broadcasted_iota(jnp.int32, (C, C), 0)
    s_ix = lax.broadcasted_iota(jnp.int32, (C, C), 1)
    w_sp = [jnp.where(t_ix >= s_ix, wsp_ref[g], 0.0).astype(BF16) for g in range(GMLP_GROUPS)]
    first = lax.broadcasted_iota(jnp.int32, (1, LANES), 1) < GMLP_GROUP_DIM

    def qkv_cols(c0, c1):
        acc = jnp.dot(h, w_ref[:, c0:c1], preferred_element_type=F32)
        if c1 <= D_MOBA:
            acc = acc * Q_SCALE
        qkv_ref[:, c0:c1] = acc.astype(BF16)
        if c1 <= 2 * D_MOBA:
            sq = acc * acc
            for c in range(c0, c1, MOBA_HEAD_DIM):
                lanes = (c - c0) // LANES * LANES
                own = first if (c - c0) % LANES == 0 else jnp.logical_not(first)
                rows_sq = jnp.sum(jnp.where(own, sq[:, lanes:lanes + LANES], 0.0), axis=-1, keepdims=True)
                top = jnp.max(rows_sq, axis=0, keepdims=True)
                norm_ref[0, c // MOBA_HEAD_DIM:c // MOBA_HEAD_DIM + 1, :] = jnp.broadcast_to(top, (1, LANES))

    def gmlp_rows(c):
        rows = slice(c * C, (c + 1) * C)
        z = jax.nn.gelu(uz_scr[rows, D_GMLP:])
        mu = jnp.mean(z, axis=-1, keepdims=True)
        zc = z - mu
        var = jnp.mean(zc * zc, axis=-1, keepdims=True)
        zn = (zc * lax.rsqrt(var + EPS) * lng_ref[...] + lnb_ref[...]).astype(BF16)
        cols = []
        for gp in range(D_GMLP // LANES):
            zp = zn[:, gp * LANES:(gp + 1) * LANES]
            r0 = jnp.dot(w_sp[2 * gp], zp, preferred_element_type=F32)
            r1 = jnp.dot(w_sp[2 * gp + 1], zp, preferred_element_type=F32)
            cols.append(jnp.where(first, r0, r1))
        s = jnp.concatenate(cols, axis=1) + bsp_ref[...]
        y = jax.nn.gelu(uz_scr[rows, :D_GMLP]) * s
        ms_y = jnp.mean(y * y, axis=-1, keepdims=True)
        ya_ref[rows, :] = (y * lax.rsqrt(ms_y + EPS) * gn_ref[...]).astype(BF16)

    col_blocks = [(c, c + 2 * LANES) for c in range(0, n_qkv, 2 * LANES)]
    n_chunks = tm // C
    for step in range(max(len(col_blocks), n_chunks)):
        if step < len(col_blocks):
            qkv_cols(*col_blocks[step])
        if step < n_chunks:
            gmlp_rows(step)


def _inproj(x2, g, w_bf, ln_g, ln_b, w_sp, b_sp_full, gn, tm):
    T, D = x2.shape
    n_all = w_bf.shape[1]
    n_qkv = 3 * D_MOBA
    C = GMLP_CHUNK
    vec = pl.BlockSpec((1, D_GMLP), lambda i: (0, 0))
    return pl.pallas_call(
        _inproj_kernel,
        grid=(T // tm,),
        in_specs=[
            pl.BlockSpec((tm, D), lambda i: (i, 0)),
            pl.BlockSpec((1, D), lambda i: (0, 0)),
            pl.BlockSpec((D, n_all), lambda i: (0, 0), pipeline_mode=pl.Buffered(1)),
            vec, vec,
            pl.BlockSpec((GMLP_GROUPS, C, C), lambda i: (0, 0, 0)),
            pl.BlockSpec((C, D_GMLP), lambda i: (0, 0)),
            vec,
        ],
        out_specs=[
            pl.BlockSpec((tm, n_qkv), lambda i: (i, 0)),
            pl.BlockSpec((tm, D_GMLP), lambda i: (i, 0)),
            pl.BlockSpec((1, 2 * MOBA_HEADS, LANES), lambda i: (i, 0, 0)),
        ],
        out_shape=[
            jax.ShapeDtypeStruct((T, n_qkv), BF16),
            jax.ShapeDtypeStruct((T, D_GMLP), BF16),
            jax.ShapeDtypeStruct((T // tm, 2 * MOBA_HEADS, LANES), F32),
        ],
        scratch_shapes=[pltpu.VMEM((tm, n_all - n_qkv), F32)],
        compiler_params=pltpu.CompilerParams(
            dimension_semantics=("arbitrary",), vmem_limit_bytes=VMEM_LIMIT),
        name="inproj_gmlp",
    )(x2, g, w_bf, ln_g, ln_b, w_sp, b_sp_full, gn)


def _dot_nt(a, b):
    return lax.dot_general(a, b, (((1,), (1,)), ((), ())), preferred_element_type=F32)


def _moba_kernel(q_ref, k_ref, v_ref, norm_ref, bias_ref, rel_ref, *refs, n_cast, far_bucket):
    cast_src, o_ref, cast_dst = refs[:n_cast], refs[n_cast], refs[n_cast + 1:2 * n_cast + 1]
    vt_scr, slots = refs[2 * n_cast + 1], refs[2 * n_cast + 2:]
    t_scr, p_scr = slots[:len(slots) // 2], slots[len(slots) // 2:]
    _cast_slabs(cast_src, cast_dst)
    L = MOBA_BLOCK
    S = q_ref.shape[1]
    NB = S // L
    hp = pl.program_id(1)
    lane = lax.broadcasted_iota(jnp.int32, (1, LANES), 1)
    head_lanes = [(lane >= hh * MOBA_HEAD_DIM) & (lane < (hh + 1) * MOBA_HEAD_DIM)
                  for hh in range(HEADS_PER_STEP)]
    row = lax.broadcasted_iota(jnp.int32, (NB, L), 0)
    bodies = [(i, hh) for i in range(NB) for hh in range(HEADS_PER_STEP)]
    nb = len(bodies)

    cfar, bmax, bmin = [], [], []
    for hh in range(HEADS_PER_STEP):
        vals = [rel_ref[b, hp * HEADS_PER_STEP + hh] * LOG2E for b in range(NUM_BUCKETS)]
        cfar.append(vals[far_bucket])
        bmax.append(functools.reduce(jnp.maximum, vals))
        bmin.append(functools.reduce(jnp.minimum, vals))

    def stage_operands():
        vt = v_ref[0].astype(F32).T.astype(BF16)
        for hh in range(HEADS_PER_STEP):
            vt_scr[hh * VT_ROWS:hh * VT_ROWS + MOBA_HEAD_DIM, :] = (
                vt[hh * MOBA_HEAD_DIM:(hh + 1) * MOBA_HEAD_DIM])
            vt_scr[hh * VT_ROWS + MOBA_HEAD_DIM:(hh + 1) * VT_ROWS, :] = jnp.ones((ONES_ROWS, S), BF16)
        km = jnp.concatenate(
            [jnp.sum(k_ref[0, n * L:(n + 1) * L, :].astype(F32), axis=0, keepdims=True) * (1.0 / L)
             for n in range(NB)], axis=0)
        kmm = jnp.concatenate([jnp.where(head_lanes[hh], km, 0.0) for hh in range(HEADS_PER_STEP)],
                              axis=0)
        km_hi = kmm.astype(BF16)
        return km_hi, (kmm - km_hi.astype(F32)).astype(BF16)

    shift, span = [], []
    for hh in range(HEADS_PER_STEP):
        head = hp * HEADS_PER_STEP + hh
        q_sq = jnp.max(norm_ref[:, pl.ds(head, 1), :], axis=0)[:, :1]
        k_sq = jnp.max(norm_ref[:, pl.ds(MOBA_HEADS + head, 1), :], axis=0)[:, :1]
        reach = jnp.sqrt(q_sq * k_sq) * BOUND_SLACK
        shift.append(reach + bmax[hh])
        span.append(2.0 * reach + (bmax[hh] - bmin[hh]))
    safe = jnp.max(functools.reduce(jnp.maximum, span)) <= SAFE_SPAN

    def slot_of(n):
        return n % len(p_scr)

    def block_select(k_means, gates, n):
        i, hh = bodies[n]
        q_i = q_ref[0, i * L:(i + 1) * L, :]
        if i not in gates:
            gates[i] = _dot_nt(k_means[0], q_i) + _dot_nt(k_means[1], q_i)
        g = gates[i][hh * NB:(hh + 1) * NB]
        cnt = jnp.zeros((NB, L), jnp.int32)
        for m in range(i):
            gm = g[m:m + 1, :]
            beats = (gm > g) | ((gm == g) & (m < row))
            cnt = cnt + jnp.where(beats, 1, 0)
        return cnt < MOBA_TOPK

    def masked_q(n):
        i, hh = bodies[n]
        q_i = q_ref[0, i * L:(i + 1) * L, :]
        return jnp.where(head_lanes[hh], q_i, jnp.zeros_like(q_i))

    def pv_stage(outs, n):
        i, hh = bodies[n]
        nk = (i + 1) * L
        acc = jnp.dot(vt_scr[hh * VT_ROWS:(hh + 1) * VT_ROWS, 0:nk], p_scr[slot_of(n)][0:nk, :],
                      preferred_element_type=F32)
        outs[n] = acc[:MOBA_HEAD_DIM] / acc[MOBA_HEAD_DIM:MOBA_HEAD_DIM + 1]
        if hh == HEADS_PER_STEP - 1:
            pair = [outs.pop(n - HEADS_PER_STEP + 1 + h) for h in range(HEADS_PER_STEP)]
            o_ref[0, i * L:(i + 1) * L, :] = jnp.concatenate(pair, axis=0).T

    def exact_path():
        gates, masks, qms, maxes, mrows, outs = {}, {}, {}, {}, {}, {}
        k_means = stage_operands()

        def begin_body(n):
            i, hh = bodies[n]
            qms[n] = masked_q(n)
            if i > MOBA_TOPK:
                sel = block_select(k_means, gates, n)
                masks[n] = (jnp.where(sel, cfar[hh], NEG),
                            jnp.where(sel, 0.0, NEG))

        def score_tile(n, j):
            i, hh = bodies[n]
            t = _dot_nt(k_ref[0, j * L:(j + 1) * L, :], qms[n])
            if j == i:
                t = t + bias_ref[hh, 0]
            elif j == i - 1:
                t = t + bias_ref[hh, 1]
                if i > MOBA_TOPK:
                    t = t + masks[n][1][j:j + 1, :]
            elif i > MOBA_TOPK:
                t = t + masks[n][0][j:j + 1, :]
            else:
                t = t + cfar[hh]
            t_scr[slot_of(n)][j * L:(j + 1) * L, :] = t
            tm = jnp.max(t.reshape(L // 8, 8, L), axis=0)
            maxes[n] = tm if j == 0 else jnp.maximum(maxes[n], tm)
            if j == i:
                mrows[n] = jnp.max(maxes.pop(n), axis=0, keepdims=True)
                qms.pop(n)
                masks.pop(n, None)

        def exp_tile(n, j):
            t_ref, p_ref = t_scr[slot_of(n)], p_scr[slot_of(n)]
            p_ref[j * L:(j + 1) * L, :] = jnp.exp2(t_ref[j * L:(j + 1) * L, :] - mrows[n]).astype(BF16)

        for n0 in range(min(LOOKAHEAD, nb)):
            begin_body(n0)
            for j in range(bodies[n0][0] + 1):
                score_tile(n0, j)
        for n in range(nb):
            n_exp = bodies[n][0] + 1
            ahead = n + LOOKAHEAD
            n_score = bodies[ahead][0] + 1 if ahead < nb else 0
            if n_score:
                begin_body(ahead)
            for j in range(max(n_exp, n_score)):
                if j < n_score:
                    score_tile(ahead, j)
                if j < n_exp:
                    exp_tile(n, j)
            mrows.pop(n)
            pv_stage(outs, n)

    def fast_path():
        gates, outs = {}, {}
        k_means = stage_operands()

        def prob_tiles(n):
            i, hh = bodies[n]
            qm = masked_q(n)
            s_row = shift[hh]
            if i > MOBA_TOPK:
                sel = block_select(k_means, gates, n)
                m_far = jnp.where(sel, cfar[hh], NEG) - s_row
                m_prev = jnp.where(sel, 0.0, NEG) - s_row
            for j in range(i + 1):
                e = _dot_nt(k_ref[0, j * L:(j + 1) * L, :], qm)
                if j == i:
                    e = e + bias_ref[hh, 0] - s_row
                elif j == i - 1:
                    e = e + bias_ref[hh, 1] + (m_prev[j:j + 1, :] if i > MOBA_TOPK else -s_row)
                else:
                    e = e + (m_far[j:j + 1, :] if i > MOBA_TOPK else cfar[hh] - s_row)
                p_scr[slot_of(n)][j * L:(j + 1) * L, :] = jnp.exp2(e).astype(BF16)

        for n0 in range(min(FAST_LOOKAHEAD, nb)):
            prob_tiles(n0)
        for n in range(nb):
            if n + FAST_LOOKAHEAD < nb:
                prob_tiles(n + FAST_LOOKAHEAD)
            pv_stage(outs, n)

    pl.when(safe)(fast_path)
    pl.when(jnp.logical_not(safe))(exact_path)


def _moba(qkv3, norms, bias_t, rel_bias, far_bucket, cast_weights):
    B, S, _ = qkv3.shape
    L = MOBA_BLOCK
    n_hp = MOBA_HEADS // HEADS_PER_STEP
    blk = lambda off: pl.BlockSpec((1, S, LANES), lambda b, hp: (b, 0, off + hp))
    c_in, c_out, c_shapes = _cast_specs(cast_weights, B * n_hp, lambda b, hp: b * n_hp + hp)
    outs = pl.pallas_call(
        functools.partial(_moba_kernel, n_cast=len(cast_weights), far_bucket=far_bucket),
        grid=(B, n_hp),
        in_specs=[
            blk(0), blk(n_hp), blk(2 * n_hp),
            pl.BlockSpec((norms.shape[0] // B,) + norms.shape[1:], lambda b, hp: (b, 0, 0)),
            pl.BlockSpec((HEADS_PER_STEP, 2, L, L), lambda b, hp: (hp, 0, 0, 0)),
            pl.BlockSpec(memory_space=pltpu.SMEM),
        ] + c_in,
        out_specs=[pl.BlockSpec((1, S, LANES), lambda b, hp: (b, 0, hp))] + c_out,
        out_shape=[jax.ShapeDtypeStruct((B, S, D_MOBA), F32)] + c_shapes,
        scratch_shapes=[
            pltpu.VMEM((HEADS_PER_STEP * VT_ROWS, S), BF16),
        ] + [pltpu.VMEM((S, L), F32)] * N_SLOTS + [pltpu.VMEM((S, L), BF16)] * N_SLOTS,
        compiler_params=pltpu.CompilerParams(
            dimension_semantics=("arbitrary", "arbitrary"), vmem_limit_bytes=VMEM_LIMIT),
        name="moba",
    )(qkv3, qkv3, qkv3, norms, bias_t, rel_bias, *cast_weights)
    return outs[0], outs[1:]


def _ffn_kernel(x_ref, yb_ref, ya_ref, gb_ref, wo_ref, gf_ref, wg_ref, wu_ref, wd_ref, gl_ref,
                o_ref, *, final_norm):
    half = x_ref.shape[0] // 2
    st = {}

    def attn_norm(r):
        yb = yb_ref[r:r + half, :]
        ms = jnp.mean(yb * yb, axis=-1, keepdims=True)
        st["ybn", r] = (yb * lax.rsqrt(ms + EPS) * gb_ref[...]).astype(BF16)

    def out_proj(r):
        st["x1", r] = (x_ref[r:r + half, :]
                       + jnp.dot(st.pop(("ybn", r)), wo_ref[:D_MOBA, :], preferred_element_type=F32)
                       + jnp.dot(ya_ref[r:r + half, :], wo_ref[D_MOBA:, :],
                                 preferred_element_type=F32))

    def ffn_norm(r):
        x1 = st["x1", r]
        ms = jnp.mean(x1 * x1, axis=-1, keepdims=True)
        st["h", r] = (x1 * lax.rsqrt(ms + EPS) * gf_ref[...]).astype(BF16)

    def gate_up(r):
        h = st.pop(("h", r))
        g = jnp.dot(h, wg_ref[...], preferred_element_type=F32)
        u = jnp.dot(h, wu_ref[...], preferred_element_type=F32)
        st["a", r] = (jax.nn.silu(g) * u).astype(BF16)

    def down(r):
        x2 = st.pop(("x1", r)) + jnp.dot(st.pop(("a", r)), wd_ref[...], preferred_element_type=F32)
        if final_norm:
            ms = jnp.mean(x2 * x2, axis=-1, keepdims=True)
            x2 = x2 * lax.rsqrt(ms + EPS) * gl_ref[...]
        o_ref[r:r + half, :] = x2

    stages = [attn_norm, out_proj, ffn_norm, gate_up, down]
    for k in range(len(stages) + 1):
        if k < len(stages):
            stages[k](0)
        if k >= 1:
            stages[k - 1](half)


def _ffn(x2, yb, ya, gb, wo, gf, wg, wu, wd, gl, tm, final_norm):
    T, D = x2.shape
    d_ff = wg.shape[1]
    const = lambda shape: pl.BlockSpec(shape, lambda i: (0, 0), pipeline_mode=pl.Buffered(1))
    return pl.pallas_call(
        functools.partial(_ffn_kernel, final_norm=final_norm),
        grid=(T // tm,),
        in_specs=[
            pl.BlockSpec((tm, D), lambda i: (i, 0)),
            pl.BlockSpec((tm, D_MOBA), lambda i: (i, 0)),
            pl.BlockSpec((tm, D_GMLP), lambda i: (i, 0)),
            pl.BlockSpec((1, D_MOBA), lambda i: (0, 0)),
            const((D_MOBA + D_GMLP, D)),
            pl.BlockSpec((1, D), lambda i: (0, 0)),
            const((D, d_ff)), const((D, d_ff)), const((d_ff, D)),
            pl.BlockSpec((1, D), lambda i: (0, 0)),
        ],
        out_specs=pl.BlockSpec((tm, D), lambda i: (i, 0)),
        out_shape=jax.ShapeDtypeStruct((T, D), F32),
        compiler_params=pltpu.CompilerParams(
            dimension_semantics=("arbitrary",), vmem_limit_bytes=VMEM_LIMIT),
        name="outproj_ffn",
    )(x2, yb, ya, gb, wo, gf, wg, wu, wd, gl)


def kernel(x, rel_bias, norm_mix, w_in, gmlp_ln_g, gmlp_ln_b, w_spatial, b_spatial, out_norm_b,
           out_norm_a, w_out, norm_ffn, w_gate, w_up, w_down, norm_final):
    B, S, D = x.shape
    depth = w_in.shape[0]
    assert S % MOBA_BLOCK == 0 and S // MOBA_BLOCK > MOBA_TOPK
    far = _t5_bucket_np(np.arange(MOBA_BLOCK + 1, S + MOBA_BLOCK))
    assert (far == far[0]).all()
    far_bucket = int(far[0])

    assert depth == 1, "weight casts ride along the single layer's kernels"
    tm = 512
    T = B * S
    x2 = x.reshape(T, D)
    row = lambda v: v.reshape(1, -1)
    bias_t, (w_in_bf,) = _bias_tiles(rel_bias, [w_in[0]])
    b_sp_full = jnp.repeat(b_spatial[0].T, GMLP_GROUP_DIM, axis=1)
    qkv, ya, norms = _inproj(x2, row(norm_mix[0]), w_in_bf, row(gmlp_ln_g[0]), row(gmlp_ln_b[0]),
                             w_spatial[0], b_sp_full, row(out_norm_a[0]), tm)
    yb, (wo_bf, wg_bf, wu_bf, wd_bf) = _moba(qkv.reshape(B, S, -1), norms, bias_t, rel_bias,
                                             far_bucket, [w_out[0], w_gate[0], w_up[0], w_down[0]])
    out = _ffn(x2, yb.reshape(T, D_MOBA), ya, row(out_norm_b[0]), wo_bf, row(norm_ffn[0]),
               wg_bf, wu_bf, wd_bf, row(norm_final), tm, True)
    return out.reshape(B, S, D)
```

```python
import functools
import math

import numpy as np
import jax
import jax.numpy as jnp
from jax import lax
from jax.experimental import pallas as pl
from jax.experimental.pallas import tpu as pltpu

F32 = jnp.float32
BF16 = jnp.bfloat16

MOBA_HEADS = 8
MOBA_HEAD_DIM = 64
D_MOBA = MOBA_HEADS * MOBA_HEAD_DIM
GMLP_GROUPS = 8
GMLP_GROUP_DIM = 64
D_GMLP = GMLP_GROUPS * GMLP_GROUP_DIM
MOBA_BLOCK = 256
MOBA_TOPK = 3
GMLP_CHUNK = 128
NUM_BUCKETS = 32
REL_MAX_DISTANCE = 128
EPS = 1e-6
NEG = -1e9

LANES = 128
HEADS_PER_STEP = LANES // MOBA_HEAD_DIM
BF16_SUBLANES = 16
ONES_ROWS = BF16_SUBLANES
VT_ROWS = MOBA_HEAD_DIM + ONES_ROWS
VMEM_LIMIT = 56 * 1024 * 1024
N_SLOTS = 4
LOOKAHEAD = 3
FAST_LOOKAHEAD = 2

LOG2E = math.log2(math.e)
Q_SCALE = MOBA_HEAD_DIM ** -0.5 * LOG2E

BOUND_SLACK = 1.0 + 2.0 ** -6
SAFE_SPAN = 100.0


def _t5_bucket_np(dist):
    n = np.maximum(dist, 0)
    max_exact = NUM_BUCKETS // 2
    nf = np.maximum(n, max_exact).astype(np.float32)
    ratio = np.log(nf / np.float32(max_exact)) / np.float32(math.log(REL_MAX_DISTANCE / max_exact))
    large = max_exact + (ratio * np.float32(NUM_BUCKETS - max_exact)).astype(np.int32)
    large = np.minimum(large, NUM_BUCKETS - 1)
    return np.where(n < max_exact, n, large).astype(np.int32)


def _cast_specs(weights, n_steps, step_of):
    in_specs, out_specs, out_shapes = [], [], []
    for w in weights:
        rows, cols = w.shape
        period = next(p for p in range(1, n_steps + 1)
                      if n_steps % p == 0 and (rows * p) % (n_steps * BF16_SUBLANES) == 0)
        spec = pl.BlockSpec((rows * period // n_steps, cols),
                            functools.partial(lambda p, *ids: (step_of(*ids) // p, 0), period))
        in_specs.append(spec)
        out_specs.append(spec)
        out_shapes.append(jax.ShapeDtypeStruct(w.shape, BF16))
    return in_specs, out_specs, out_shapes


def _cast_slabs(src_refs, dst_refs):
    for src, dst in zip(src_refs, dst_refs):
        dst[...] = src[...].astype(BF16)


def _bias_kernel(bucket_ref, rel_ref, *refs):
    n_cast = (len(refs) - 1) // 2
    out_ref = refs[n_cast]
    _cast_slabs(refs[:n_cast], refs[n_cast + 1:])
    h = pl.program_id(0)
    L = MOBA_BLOCK
    key = lax.broadcasted_iota(jnp.int32, (L, L), 0)
    qry = lax.broadcasted_iota(jnp.int32, (L, L), 1)
    for delta in range(2):
        bucket = bucket_ref[delta]
        acc = jnp.zeros((L, L), F32)
        for b in range(NUM_BUCKETS):
            acc = jnp.where(bucket == b, rel_ref[b, h] * LOG2E, acc)
        if delta == 0:
            acc = jnp.where(key > qry, NEG, acc)
        out_ref[0, delta] = acc


def _bias_tiles(rel_bias, cast_weights):
    L = MOBA_BLOCK
    kk = np.arange(L)[:, None]
    qq = np.arange(L)[None, :]
    bucket = np.stack([_t5_bucket_np(d * L + qq - kk) for d in range(2)]).astype(np.int32)
    c_in, c_out, c_shapes = _cast_specs(cast_weights, MOBA_HEADS, lambda h: h)
    outs = pl.pallas_call(
        _bias_kernel,
        grid=(MOBA_HEADS,),
        in_specs=[
            pl.BlockSpec((2, L, L), lambda h: (0, 0, 0)),
            pl.BlockSpec(memory_space=pltpu.SMEM),
        ] + c_in,
        out_specs=[pl.BlockSpec((1, 2, L, L), lambda h: (h, 0, 0, 0))] + c_out,
        out_shape=[jax.ShapeDtypeStruct((MOBA_HEADS, 2, L, L), F32)] + c_shapes,
        compiler_params=pltpu.CompilerParams(
            dimension_semantics=("arbitrary",), vmem_limit_bytes=VMEM_LIMIT),
        name="bias_tiles",
    )(jnp.asarray(bucket), rel_bias, *cast_weights)
    return outs[0], outs[1:]


def _inproj_kernel(x_ref, g_ref, w_ref, lng_ref, lnb_ref, wsp_ref, bsp_ref, gn_ref,
                   qkv_ref, ya_ref, norm_ref, uz_scr):
    C = GMLP_CHUNK
    tm = x_ref.shape[0]
    n_qkv = 3 * D_MOBA
    x = x_ref[...]
    ms = jnp.mean(x * x, axis=-1, keepdims=True)
    h = (x * lax.rsqrt(ms + EPS) * g_ref[...]).astype(BF16)
    uz_scr[...] = jnp.dot(h, w_ref[:, n_qkv:], preferred_element_type=F32)

    t_ix = lax.broadcasted_iota(jnp.int32, (C, C), 0)
    s_ix = lax.broadcasted_iota(jnp.int32, (C, C), 1)
    w_sp = [jnp.where(t_ix >= s_ix, wsp_ref[g], 0.0).astype(BF16) for g in range(GMLP_GROUPS)]
    first = lax.broadcasted_iota(jnp.int32, (1, LANES), 1) < GMLP_GROUP_DIM

    def qkv_cols(c0, c1):
        acc = jnp.dot(h, w_ref[:, c0:c1], preferred_element_type=F32)
        if c1 <= D_MOBA:
            acc = acc * Q_SCALE
        qkv_ref[:, c0:c1] = acc.astype(BF16)
        if c1 <= 2 * D_MOBA:
            sq = acc * acc
            for c in range(c0, c1, MOBA_HEAD_DIM):
                lanes = (c - c0) // LANES * LANES
                own = first if (c - c0) % LANES == 0 else jnp.logical_not(first)
                rows_sq = jnp.sum(jnp.where(own, sq[:, lanes:lanes + LANES], 0.0), axis=-1, keepdims=True)
                top = jnp.max(rows_sq, axis=0, keepdims=True)
                norm_ref[0, c // MOBA_HEAD_DIM:c // MOBA_HEAD_DIM + 1, :] = jnp.broadcast_to(top, (1, LANES))

    def gmlp_rows(c):
        rows = slice(c * C, (c + 1) * C)
        z = jax.nn.gelu(uz_scr[rows, D_GMLP:])
        mu = jnp.mean(z, axis=-1, keepdims=True)
        zc = z - mu
        var = jnp.mean(zc * zc, axis=-1, keepdims=True)
        zn = (zc * lax.rsqrt(var + EPS) * lng_ref[...] + lnb_ref[...]).astype(BF16)
        cols = []
        for gp in range(D_GMLP // LANES):
            zp = zn[:, gp * LANES:(gp + 1) * LANES]
            r0 = jnp.dot(w_sp[2 * gp], zp, preferred_element_type=F32)
            r1 = jnp.dot(w_sp[2 * gp + 1], zp, preferred_element_type=F32)
            cols.append(jnp.where(first, r0, r1))
        s = jnp.concatenate(cols, axis=1) + bsp_ref[...]
        y = jax.nn.gelu(uz_scr[rows, :D_GMLP]) * s
        ms_y = jnp.mean(y * y, axis=-1, keepdims=True)
        ya_ref[rows, :] = (y * lax.rsqrt(ms_y + EPS) * gn_ref[...]).astype(BF16)

    col_blocks = [(c, c + 2 * LANES) for c in range(0, n_qkv, 2 * LANES)]
    n_chunks = tm // C
    for step in range(max(len(col_blocks), n_chunks)):
        if step < len(col_blocks):
            qkv_cols(*col_blocks[step])
        if step < n_chunks:
            gmlp_rows(step)


def _inproj(x2, g, w_bf, ln_g, ln_b, w_sp, b_sp_full, gn, tm):
    T, D = x2.shape
    n_all = w_bf.shape[1]
    n_qkv = 3 * D_MOBA
    C = GMLP_CHUNK
    vec = pl.BlockSpec((1, D_GMLP), lambda i: (0, 0))
    return pl.pallas_call(
        _inproj_kernel,
        grid=(T // tm,),
        in_specs=[
            pl.BlockSpec((tm, D), lambda i: (i, 0)),
            pl.BlockSpec((1, D), lambda i: (0, 0)),
            pl.BlockSpec((D, n_all), lambda i: (0, 0), pipeline_mode=pl.Buffered(1)),
            vec, vec,
            pl.BlockSpec((GMLP_GROUPS, C, C), lambda i: (0, 0, 0)),
            pl.BlockSpec((C, D_GMLP), lambda i: (0, 0)),
            vec,
        ],
        out_specs=[
            pl.BlockSpec((tm, n_qkv), lambda i: (i, 0)),
            pl.BlockSpec((tm, D_GMLP), lambda i: (i, 0)),
            pl.BlockSpec((1, 2 * MOBA_HEADS, LANES), lambda i: (i, 0, 0)),
        ],
        out_shape=[
            jax.ShapeDtypeStruct((T, n_qkv), BF16),
            jax.ShapeDtypeStruct((T, D_GMLP), BF16),
            jax.ShapeDtypeStruct((T // tm, 2 * MOBA_HEADS, LANES), F32),
        ],
        scratch_shapes=[pltpu.VMEM((tm, n_all - n_qkv), F32)],
        compiler_params=pltpu.CompilerParams(
            dimension_semantics=("arbitrary",), vmem_limit_bytes=VMEM_LIMIT),
        name="inproj_gmlp",
    )(x2, g, w_bf, ln_g, ln_b, w_sp, b_sp_full, gn)


def _dot_nt(a, b):
    return lax.dot_general(a, b, (((1,), (1,)), ((), ())), preferred_element_type=F32)


def _moba_kernel(q_ref, k_ref, v_ref, norm_ref, bias_ref, rel_ref, *refs, n_cast, far_bucket):
    cast_src, o_ref, cast_dst = refs[:n_cast], refs[n_cast], refs[n_cast + 1:2 * n_cast + 1]
    vt_scr, slots = refs[2 * n_cast + 1], refs[2 * n_cast + 2:]
    t_scr, p_scr = slots[:len(slots) // 2], slots[len(slots) // 2:]
    _cast_slabs(cast_src, cast_dst)
    L = MOBA_BLOCK
    S = q_ref.shape[1]
    NB = S // L
    hp = pl.program_id(1)
    lane = lax.broadcasted_iota(jnp.int32, (1, LANES), 1)
    head_lanes = [(lane >= hh * MOBA_HEAD_DIM) & (lane < (hh + 1) * MOBA_HEAD_DIM)
                  for hh in range(HEADS_PER_STEP)]
    row = lax.broadcasted_iota(jnp.int32, (NB, L), 0)
    bodies = [(i, hh) for i in range(NB) for hh in range(HEADS_PER_STEP)]
    nb = len(bodies)

    cfar, bmax, bmin = [], [], []
    for hh in range(HEADS_PER_STEP):
        vals = [rel_ref[b, hp * HEADS_PER_STEP + hh] * LOG2E for b in range(NUM_BUCKETS)]
        cfar.append(vals[far_bucket])
        bmax.append(functools.reduce(jnp.maximum, vals))
        bmin.append(functools.reduce(jnp.minimum, vals))

    def stage_operands():
        vt = v_ref[0].astype(F32).T.astype(BF16)
        for hh in range(HEADS_PER_STEP):
            vt_scr[hh * VT_ROWS:hh * VT_ROWS + MOBA_HEAD_DIM, :] = (
                vt[hh * MOBA_HEAD_DIM:(hh + 1) * MOBA_HEAD_DIM])
            vt_scr[hh * VT_ROWS + MOBA_HEAD_DIM:(hh + 1) * VT_ROWS, :] = jnp.ones((ONES_ROWS, S), BF16)
        km = jnp.concatenate(
            [jnp.sum(k_ref[0, n * L:(n + 1) * L, :].astype(F32), axis=0, keepdims=True) * (1.0 / L)
             for n in range(NB)], axis=0)
        kmm = jnp.concatenate([jnp.where(head_lanes[hh], km, 0.0) for hh in range(HEADS_PER_STEP)],
                              axis=0)
        km_hi = kmm.astype(BF16)
        return km_hi, (kmm - km_hi.astype(F32)).astype(BF16)

    shift, span = [], []
    for hh in range(HEADS_PER_STEP):
        head = hp * HEADS_PER_STEP + hh
        q_sq = jnp.max(norm_ref[:, pl.ds(head, 1), :], axis=0)[:, :1]
        k_sq = jnp.max(norm_ref[:, pl.ds(MOBA_HEADS + head, 1), :], axis=0)[:, :1]
        reach = jnp.sqrt(q_sq * k_sq) * BOUND_SLACK
        shift.append(reach + bmax[hh])
        span.append(2.0 * reach + (bmax[hh] - bmin[hh]))
    safe = jnp.max(functools.reduce(jnp.maximum, span)) <= SAFE_SPAN

    def slot_of(n):
        return n % len(p_scr)

    def block_select(k_means, gates, n):
        i, hh = bodies[n]
        q_i = q_ref[0, i * L:(i + 1) * L, :]
        if i not in gates:
            gates[i] = _dot_nt(k_means[0], q_i) + _dot_nt(k_means[1], q_i)
        g = gates[i][hh * NB:(hh + 1) * NB]
        cnt = jnp.zeros((NB, L), jnp.int32)
        for m in range(i):
            gm = g[m:m + 1, :]
            beats = (gm > g) | ((gm == g) & (m < row))
            cnt = cnt + jnp.where(beats, 1, 0)
        return cnt < MOBA_TOPK

    def masked_q(n):
        i, hh = bodies[n]
        q_i = q_ref[0, i * L:(i + 1) * L, :]
        return jnp.where(head_lanes[hh], q_i, jnp.zeros_like(q_i))

    def pv_stage(outs, n):
        i, hh = bodies[n]
        nk = (i + 1) * L
        acc = jnp.dot(vt_scr[hh * VT_ROWS:(hh + 1) * VT_ROWS, 0:nk], p_scr[slot_of(n)][0:nk, :],
                      preferred_element_type=F32)
        outs[n] = acc[:MOBA_HEAD_DIM] / acc[MOBA_HEAD_DIM:MOBA_HEAD_DIM + 1]
        if hh == HEADS_PER_STEP - 1:
            pair = [outs.pop(n - HEADS_PER_STEP + 1 + h) for h in range(HEADS_PER_STEP)]
            o_ref[0, i * L:(i + 1) * L, :] = jnp.concatenate(pair, axis=0).T

    def exact_path():
        gates, masks, qms, maxes, mrows, outs = {}, {}, {}, {}, {}, {}
        k_means = stage_operands()

        def begin_body(n):
            i, hh = bodies[n]
            qms[n] = masked_q(n)
            if i > MOBA_TOPK:
                sel = block_select(k_means, gates, n)
                masks[n] = (jnp.where(sel, cfar[hh], NEG),
                            jnp.where(sel, 0.0, NEG))

        def score_tile(n, j):
            i, hh = bodies[n]
            t = _dot_nt(k_ref[0, j * L:(j + 1) * L, :], qms[n])
            if j == i:
                t = t + bias_ref[hh, 0]
            elif j == i - 1:
                t = t + bias_ref[hh, 1]
                if i > MOBA_TOPK:
                    t = t + masks[n][1][j:j + 1, :]
            elif i > MOBA_TOPK:
                t = t + masks[n][0][j:j + 1, :]
            else:
                t = t + cfar[hh]
            t_scr[slot_of(n)][j * L:(j + 1) * L, :] = t
            tm = jnp.max(t.reshape(L // 8, 8, L), axis=0)
            maxes[n] = tm if j == 0 else jnp.maximum(maxes[n], tm)
            if j == i:
                mrows[n] = jnp.max(maxes.pop(n), axis=0, keepdims=True)
                qms.pop(n)
                masks.pop(n, None)

        def exp_tile(n, j):
            t_ref, p_ref = t_scr[slot_of(n)], p_scr[slot_of(n)]
            p_ref[j * L:(j + 1) * L, :] = jnp.exp2(t_ref[j * L:(j + 1) * L, :] - mrows[n]).astype(BF16)

        for n0 in range(min(LOOKAHEAD, nb)):
            begin_body(n0)
            for j in range(bodies[n0][0] + 1):
                score_tile(n0, j)
        for n in range(nb):
            n_exp = bodies[n][0] + 1
            ahead = n + LOOKAHEAD
            n_score = bodies[ahead][0] + 1 if ahead < nb else 0
            if n_score:
                begin_body(ahead)
            for j in range(max(n_exp, n_score)):
                if j < n_score:
                    score_tile(ahead, j)
                if j < n_exp:
                    exp_tile(n, j)
            mrows.pop(n)
            pv_stage(outs, n)

    def fast_path():
        gates, outs = {}, {}
        k_means = stage_operands()

        def prob_tiles(n):
            i, hh = bodies[n]
            qm = masked_q(n)
            s_row = shift[hh]
            if i > MOBA_TOPK:
                sel = block_select(k_means, gates, n)
                m_far = jnp.where(sel, cfar[hh], NEG) - s_row
                m_prev = jnp.where(sel, 0.0, NEG) - s_row
            for j in range(i + 1):
                e = _dot_nt(k_ref[0, j * L:(j + 1) * L, :], qm)
                if j == i:
                    e = e + bias_ref[hh, 0] - s_row
                elif j == i - 1:
                    e = e + bias_ref[hh, 1] + (m_prev[j:j + 1, :] if i > MOBA_TOPK else -s_row)
                else:
                    e = e + (m_far[j:j + 1, :] if i > MOBA_TOPK else cfar[hh] - s_row)
                p_scr[slot_of(n)][j * L:(j + 1) * L, :] = jnp.exp2(e).astype(BF16)

        for n0 in range(min(FAST_LOOKAHEAD, nb)):
            prob_tiles(n0)
        for n in range(nb):
            if n + FAST_LOOKAHEAD < nb:
                prob_tiles(n + FAST_LOOKAHEAD)
            pv_stage(outs, n)

    pl.when(safe)(fast_path)
    pl.when(jnp.logical_not(safe))(exact_path)


def _moba(qkv3, norms, bias_t, rel_bias, far_bucket, cast_weights):
    B, S, _ = qkv3.shape
    L = MOBA_BLOCK
    n_hp = MOBA_HEADS // HEADS_PER_STEP
    blk = lambda off: pl.BlockSpec((1, S, LANES), lambda b, hp: (b, 0, off + hp))
    c_in, c_out, c_shapes = _cast_specs(cast_weights, B * n_hp, lambda b, hp: b * n_hp + hp)
    outs = pl.pallas_call(
        functools.partial(_moba_kernel, n_cast=len(cast_weights), far_bucket=far_bucket),
        grid=(B, n_hp),
        in_specs=[
            blk(0), blk(n_hp), blk(2 * n_hp),
            pl.BlockSpec((norms.shape[0] // B,) + norms.shape[1:], lambda b, hp: (b, 0, 0)),
            pl.BlockSpec((HEADS_PER_STEP, 2, L, L), lambda b, hp: (hp, 0, 0, 0)),
            pl.BlockSpec(memory_space=pltpu.SMEM),
        ] + c_in,
        out_specs=[pl.BlockSpec((1, S, LANES), lambda b, hp: (b, 0, hp))] + c_out,
        out_shape=[jax.ShapeDtypeStruct((B, S, D_MOBA), F32)] + c_shapes,
        scratch_shapes=[
            pltpu.VMEM((HEADS_PER_STEP * VT_ROWS, S), BF16),
        ] + [pltpu.VMEM((S, L), F32)] * N_SLOTS + [pltpu.VMEM((S, L), BF16)] * N_SLOTS,
        compiler_params=pltpu.CompilerParams(
            dimension_semantics=("arbitrary", "arbitrary"), vmem_limit_bytes=VMEM_LIMIT),
        name="moba",
    )(qkv3, qkv3, qkv3, norms, bias_t, rel_bias, *cast_weights)
    return outs[0], outs[1:]


def _ffn_kernel(x_ref, yb_ref, ya_ref, gb_ref, wo_ref, gf_ref, wg_ref, wu_ref, wd_ref, gl_ref,
                o_ref, *, final_norm):
    half = x_ref.shape[0] // 2
    st = {}

    def attn_norm(r):
        yb = yb_ref[r:r + half, :]
        ms = jnp.mean(yb * yb, axis=-1, keepdims=True)
        st["ybn", r] = (yb * lax.rsqrt(ms + EPS) * gb_ref[...]).astype(BF16)

    def out_proj(r):
        st["x1", r] = (x_ref[r:r + half, :]
                       + jnp.dot(st.pop(("ybn", r)), wo_ref[:D_MOBA, :], preferred_element_type=F32)
                       + jnp.dot(ya_ref[r:r + half, :], wo_ref[D_MOBA:, :],
                                 preferred_element_type=F32))

    def ffn_norm(r):
        x1 = st["x1", r]
        ms = jnp.mean(x1 * x1, axis=-1, keepdims=True)
        st["h", r] = (x1 * lax.rsqrt(ms + EPS) * gf_ref[...]).astype(BF16)

    def gate_up(r):
        h = st.pop(("h", r))
        g = jnp.dot(h, wg_ref[...], preferred_element_type=F32)
        u = jnp.dot(h, wu_ref[...], preferred_element_type=F32)
        st["a", r] = (jax.nn.silu(g) * u).astype(BF16)

    def down(r):
        x2 = st.pop(("x1", r)) + jnp.dot(st.pop(("a", r)), wd_ref[...], preferred_element_type=F32)
        if final_norm:
            ms = jnp.mean(x2 * x2, axis=-1, keepdims=True)
            x2 = x2 * lax.rsqrt(ms + EPS) * gl_ref[...]
        o_ref[r:r + half, :] = x2

    stages = [attn_norm, out_proj, ffn_norm, gate_up, down]
    for k in range(len(stages) + 1):
        if k < len(stages):
            stages[k](0)
        if k >= 1:
            stages[k - 1](half)


def _ffn(x2, yb, ya, gb, wo, gf, wg, wu, wd, gl, tm, final_norm):
    T, D = x2.shape
    d_ff = wg.shape[1]
    const = lambda shape: pl.BlockSpec(shape, lambda i: (0, 0), pipeline_mode=pl.Buffered(1))
    return pl.pallas_call(
        functools.partial(_ffn_kernel, final_norm=final_norm),
        grid=(T // tm,),
        in_specs=[
            pl.BlockSpec((tm, D), lambda i: (i, 0)),
            pl.BlockSpec((tm, D_MOBA), lambda i: (i, 0)),
            pl.BlockSpec((tm, D_GMLP), lambda i: (i, 0)),
            pl.BlockSpec((1, D_MOBA), lambda i: (0, 0)),
            const((D_MOBA + D_GMLP, D)),
            pl.BlockSpec((1, D), lambda i: (0, 0)),
            const((D, d_ff)), const((D, d_ff)), const((d_ff, D)),
            pl.BlockSpec((1, D), lambda i: (0, 0)),
        ],
        out_specs=pl.BlockSpec((tm, D), lambda i: (i, 0)),
        out_shape=jax.ShapeDtypeStruct((T, D), F32),
        compiler_params=pltpu.CompilerParams(
            dimension_semantics=("arbitrary",), vmem_limit_bytes=VMEM_LIMIT),
        name="outproj_ffn",
    )(x2, yb, ya, gb, wo, gf, wg, wu, wd, gl)


def kernel(x, rel_bias, norm_mix, w_in, gmlp_ln_g, gmlp_ln_b, w_spatial, b_spatial, out_norm_b,
           out_norm_a, w_out, norm_ffn, w_gate, w_up, w_down, norm_final):
    B, S, D = x.shape
    depth = w_in.shape[0]
    assert S % MOBA_BLOCK == 0 and S // MOBA_BLOCK > MOBA_TOPK
    far = _t5_bucket_np(np.arange(MOBA_BLOCK + 1, S + MOBA_BLOCK))
    assert (far == far[0]).all()
    far_bucket = int(far[0])

    assert depth == 1, "weight casts ride along the single layer's kernels"
    tm = 512
    T = B * S
    x2 = x.reshape(T, D)
    row = lambda v: v.reshape(1, -1)
    bias_t, (w_in_bf,) = _bias_tiles(rel_bias, [w_in[0]])
    b_sp_full = jnp.repeat(b_spatial[0].T, GMLP_GROUP_DIM, axis=1)
    qkv, ya, norms = _inproj(x2, row(norm_mix[0]), w_in_bf, row(gmlp_ln_g[0]), row(gmlp_ln_b[0]),
                             w_spatial[0], b_sp_full, row(out_norm_a[0]), 2 * tm)
    yb, (wo_bf, wg_bf, wu_bf, wd_bf) = _moba(qkv.reshape(B, S, -1), norms, bias_t, rel_bias,
                                             far_bucket, [w_out[0], w_gate[0], w_up[0], w_down[0]])
    out = _ffn(x2, yb.reshape(T, D_MOBA), ya, row(out_norm_b[0]), wo_bf, row(norm_ffn[0]),
               wg_bf, wu_bf, wd_bf, row(norm_final), tm, True)
    return out.reshape(B, S, D)
```

```python
import functools
import math

import numpy as np
import jax
import jax.numpy as jnp
from jax import lax
from jax.experimental import pallas as pl
from jax.experimental.pallas import tpu as pltpu

F32 = jnp.float32
BF16 = jnp.bfloat16

MOBA_HEADS = 8
MOBA_HEAD_DIM = 64
D_MOBA = MOBA_HEADS * MOBA_HEAD_DIM
GMLP_GROUPS = 8
GMLP_GROUP_DIM = 64
D_GMLP = GMLP_GROUPS * GMLP_GROUP_DIM
MOBA_BLOCK = 256
MOBA_TOPK = 3
GMLP_CHUNK = 128
NUM_BUCKETS = 32
REL_MAX_DISTANCE = 128
EPS = 1e-6
NEG = -1e9

LANES = 128
HEADS_PER_STEP = LANES // MOBA_HEAD_DIM
BF16_SUBLANES = 16
ONES_ROWS = BF16_SUBLANES
VT_ROWS = MOBA_HEAD_DIM + ONES_ROWS
VMEM_LIMIT = 56 * 1024 * 1024
T_SLOTS = 4
P_SLOTS = 4
LOOKAHEAD = 3
FAST_LOOKAHEAD = 3

LOG2E = math.log2(math.e)
Q_SCALE = MOBA_HEAD_DIM ** -0.5 * LOG2E

BOUND_SLACK = 1.0 + 2.0 ** -6
SAFE_SPAN = 100.0


def _t5_bucket_np(dist):
    n = np.maximum(dist, 0)
    max_exact = NUM_BUCKETS // 2
    nf = np.maximum(n, max_exact).astype(np.float32)
    ratio = np.log(nf / np.float32(max_exact)) / np.float32(math.log(REL_MAX_DISTANCE / max_exact))
    large = max_exact + (ratio * np.float32(NUM_BUCKETS - max_exact)).astype(np.int32)
    large = np.minimum(large, NUM_BUCKETS - 1)
    return np.where(n < max_exact, n, large).astype(np.int32)


def _cast_specs(weights, n_steps, step_of):
    in_specs, out_specs, out_shapes = [], [], []
    for w in weights:
        rows, cols = w.shape
        period = next(p for p in range(1, n_steps + 1)
                      if n_steps % p == 0 and (rows * p) % (n_steps * BF16_SUBLANES) == 0)
        spec = pl.BlockSpec((rows * period // n_steps, cols),
                            functools.partial(lambda p, *ids: (step_of(*ids) // p, 0), period))
        in_specs.append(spec)
        out_specs.append(spec)
        out_shapes.append(jax.ShapeDtypeStruct(w.shape, BF16))
    return in_specs, out_specs, out_shapes


def _cast_slabs(src_refs, dst_refs):
    for src, dst in zip(src_refs, dst_refs):
        dst[...] = src[...].astype(BF16)


BIAS_FAR, BIAS_MAX, BIAS_MIN = 0, 1, 2
STAT_ROWS = 8


def _bias_kernel(bucket_ref, rel_ref, *refs, far_bucket):
    n_cast = (len(refs) - 2) // 2
    out_ref, stat_ref = refs[n_cast], refs[n_cast + 1]
    _cast_slabs(refs[:n_cast], refs[n_cast + 2:])
    h = pl.program_id(0)
    L = MOBA_BLOCK
    vals = [rel_ref[b, h] * LOG2E for b in range(NUM_BUCKETS)]
    key = lax.broadcasted_iota(jnp.int32, (L, L), 0)
    qry = lax.broadcasted_iota(jnp.int32, (L, L), 1)
    for delta in range(2):
        bucket = bucket_ref[delta]
        acc = jnp.zeros((L, L), F32)
        for b in range(NUM_BUCKETS):
            acc = jnp.where(bucket == b, vals[b], acc)
        if delta == 0:
            acc = jnp.where(key > qry, NEG, acc)
        out_ref[0, delta] = acc
    stats = {BIAS_FAR: vals[far_bucket], BIAS_MAX: functools.reduce(jnp.maximum, vals),
             BIAS_MIN: functools.reduce(jnp.minimum, vals)}
    for r in range(STAT_ROWS):
        stat_ref[0, r:r + 1, :] = jnp.full((1, LANES), stats.get(r, 0.0), F32)


def _bias_tiles(rel_bias, far_bucket, cast_weights):
    L = MOBA_BLOCK
    kk = np.arange(L)[:, None]
    qq = np.arange(L)[None, :]
    bucket = np.stack([_t5_bucket_np(d * L + qq - kk) for d in range(2)]).astype(np.int32)
    c_in, c_out, c_shapes = _cast_specs(cast_weights, MOBA_HEADS, lambda h: h)
    outs = pl.pallas_call(
        functools.partial(_bias_kernel, far_bucket=far_bucket),
        grid=(MOBA_HEADS,),
        in_specs=[
            pl.BlockSpec((2, L, L), lambda h: (0, 0, 0)),
            pl.BlockSpec(memory_space=pltpu.SMEM),
        ] + c_in,
        out_specs=[pl.BlockSpec((1, 2, L, L), lambda h: (h, 0, 0, 0)),
                   pl.BlockSpec((1, STAT_ROWS, LANES), lambda h: (h, 0, 0))] + c_out,
        out_shape=[jax.ShapeDtypeStruct((MOBA_HEADS, 2, L, L), F32),
                   jax.ShapeDtypeStruct((MOBA_HEADS, STAT_ROWS, LANES), F32)] + c_shapes,
        compiler_params=pltpu.CompilerParams(
            dimension_semantics=("arbitrary",), vmem_limit_bytes=VMEM_LIMIT),
        name="bias_tiles",
    )(jnp.asarray(bucket), rel_bias, *cast_weights)
    return outs[0], outs[1], outs[2:]


def _inproj_kernel(x_ref, g_ref, w_ref, lng_ref, lnb_ref, wsp_ref, bsp_ref, gn_ref,
                   qkv_ref, ya_ref, norm_ref, uz_scr):
    C = GMLP_CHUNK
    tm = x_ref.shape[0]
    n_qkv = 3 * D_MOBA
    x = x_ref[...]
    ms = jnp.mean(x * x, axis=-1, keepdims=True)
    h = (x * lax.rsqrt(ms + EPS) * g_ref[...]).astype(BF16)
    uz_scr[...] = jnp.dot(h, w_ref[:, n_qkv:], preferred_element_type=F32)

    t_ix = lax.broadcasted_iota(jnp.int32, (C, C), 0)
    s_ix = lax.broadcasted_iota(jnp.int32, (C, C), 1)
    w_sp = [jnp.where(t_ix >= s_ix, wsp_ref[g], 0.0).astype(BF16) for g in range(GMLP_GROUPS)]
    first = lax.broadcasted_iota(jnp.int32, (1, LANES), 1) < GMLP_GROUP_DIM

    def qkv_cols(c0, c1):
        acc = jnp.dot(h, w_ref[:, c0:c1], preferred_element_type=F32)
        if c1 <= D_MOBA:
            acc = acc * Q_SCALE
        qkv_ref[:, c0:c1] = acc.astype(BF16)
        if c1 <= 2 * D_MOBA:
            sq = acc * acc
            for c in range(c0, c1, MOBA_HEAD_DIM):
                lanes = (c - c0) // LANES * LANES
                own = first if (c - c0) % LANES == 0 else jnp.logical_not(first)
                rows_sq = jnp.sum(jnp.where(own, sq[:, lanes:lanes + LANES], 0.0), axis=-1, keepdims=True)
                top = jnp.max(rows_sq, axis=0, keepdims=True)
                norm_ref[0, c // MOBA_HEAD_DIM:c // MOBA_HEAD_DIM + 1, :] = jnp.broadcast_to(top, (1, LANES))

    def gmlp_rows(c):
        rows = slice(c * C, (c + 1) * C)
        z = jax.nn.gelu(uz_scr[rows, D_GMLP:])
        mu = jnp.mean(z, axis=-1, keepdims=True)
        zc = z - mu
        var = jnp.mean(zc * zc, axis=-1, keepdims=True)
        zn = (zc * lax.rsqrt(var + EPS) * lng_ref[...] + lnb_ref[...]).astype(BF16)
        cols = []
        for gp in range(D_GMLP // LANES):
            zp = zn[:, gp * LANES:(gp + 1) * LANES]
            r0 = jnp.dot(w_sp[2 * gp], zp, preferred_element_type=F32)
            r1 = jnp.dot(w_sp[2 * gp + 1], zp, preferred_element_type=F32)
            cols.append(jnp.where(first, r0, r1))
        s = jnp.concatenate(cols, axis=1) + bsp_ref[...]
        y = jax.nn.gelu(uz_scr[rows, :D_GMLP]) * s
        ms_y = jnp.mean(y * y, axis=-1, keepdims=True)
        ya_ref[rows, :] = (y * lax.rsqrt(ms_y + EPS) * gn_ref[...]).astype(BF16)

    col_blocks = [(c, c + 2 * LANES) for c in range(0, n_qkv, 2 * LANES)]
    n_chunks = tm // C
    for step in range(max(len(col_blocks), n_chunks)):
        if step < len(col_blocks):
            qkv_cols(*col_blocks[step])
        if step < n_chunks:
            gmlp_rows(step)


def _inproj(x2, g, w_bf, ln_g, ln_b, w_sp, b_sp_full, gn, tm):
    T, D = x2.shape
    n_all = w_bf.shape[1]
    n_qkv = 3 * D_MOBA
    C = GMLP_CHUNK
    vec = pl.BlockSpec((1, D_GMLP), lambda i: (0, 0))
    return pl.pallas_call(
        _inproj_kernel,
        grid=(T // tm,),
        in_specs=[
            pl.BlockSpec((tm, D), lambda i: (i, 0)),
            pl.BlockSpec((1, D), lambda i: (0, 0)),
            pl.BlockSpec((D, n_all), lambda i: (0, 0), pipeline_mode=pl.Buffered(1)),
            vec, vec,
            pl.BlockSpec((GMLP_GROUPS, C, C), lambda i: (0, 0, 0)),
            pl.BlockSpec((C, D_GMLP), lambda i: (0, 0)),
            vec,
        ],
        out_specs=[
            pl.BlockSpec((tm, n_qkv), lambda i: (i, 0)),
            pl.BlockSpec((tm, D_GMLP), lambda i: (i, 0)),
            pl.BlockSpec((1, 2 * MOBA_HEADS, LANES), lambda i: (i, 0, 0)),
        ],
        out_shape=[
            jax.ShapeDtypeStruct((T, n_qkv), BF16),
            jax.ShapeDtypeStruct((T, D_GMLP), BF16),
            jax.ShapeDtypeStruct((T // tm, 2 * MOBA_HEADS, LANES), F32),
        ],
        scratch_shapes=[pltpu.VMEM((tm, n_all - n_qkv), F32)],
        compiler_params=pltpu.CompilerParams(
            dimension_semantics=("arbitrary",), vmem_limit_bytes=VMEM_LIMIT),
        name="inproj_gmlp",
    )(x2, g, w_bf, ln_g, ln_b, w_sp, b_sp_full, gn)


def _dot_nt(a, b):
    return lax.dot_general(a, b, (((1,), (1,)), ((), ())), preferred_element_type=F32)


def _moba_kernel(q_ref, k_ref, v_ref, norm_ref, bias_ref, stat_ref, *refs, n_cast):
    cast_src, o_ref, cast_dst = refs[:n_cast], refs[n_cast], refs[n_cast + 1:2 * n_cast + 1]
    vt_scr, slots = refs[2 * n_cast + 1], refs[2 * n_cast + 2:]
    t_scr, p_scr = slots[:T_SLOTS], slots[T_SLOTS:]
    L = MOBA_BLOCK
    S = q_ref.shape[1]
    NB = S // L
    hp = pl.program_id(1)
    lane = lax.broadcasted_iota(jnp.int32, (1, LANES), 1)
    head_lanes = [(lane >= hh * MOBA_HEAD_DIM) & (lane < (hh + 1) * MOBA_HEAD_DIM)
                  for hh in range(HEADS_PER_STEP)]
    row = lax.broadcasted_iota(jnp.int32, (NB, L), 0)
    bodies = [(i, hh) for i in range(NB) for hh in range(HEADS_PER_STEP)]
    nb = len(bodies)

    cfar = [stat_ref[hh, BIAS_FAR:BIAS_FAR + 1, :1] for hh in range(HEADS_PER_STEP)]
    bmax = [stat_ref[hh, BIAS_MAX:BIAS_MAX + 1, :1] for hh in range(HEADS_PER_STEP)]
    bmin = [stat_ref[hh, BIAS_MIN:BIAS_MIN + 1, :1] for hh in range(HEADS_PER_STEP)]

    def stage_operands():
        vt = v_ref[0].astype(F32).T.astype(BF16)
        for hh in range(HEADS_PER_STEP):
            vt_scr[hh * VT_ROWS:hh * VT_ROWS + MOBA_HEAD_DIM, :] = (
                vt[hh * MOBA_HEAD_DIM:(hh + 1) * MOBA_HEAD_DIM])
            vt_scr[hh * VT_ROWS + MOBA_HEAD_DIM:(hh + 1) * VT_ROWS, :] = jnp.ones((ONES_ROWS, S), BF16)
        km = jnp.concatenate(
            [jnp.sum(k_ref[0, n * L:(n + 1) * L, :].astype(F32), axis=0, keepdims=True) * (1.0 / L)
             for n in range(NB)], axis=0)
        kmm = jnp.concatenate([jnp.where(head_lanes[hh], km, 0.0) for hh in range(HEADS_PER_STEP)],
                              axis=0)
        km_hi = kmm.astype(BF16)
        return km_hi, (kmm - km_hi.astype(F32)).astype(BF16)

    shift, span = [], []
    for hh in range(HEADS_PER_STEP):
        head = hp * HEADS_PER_STEP + hh
        q_sq = jnp.max(norm_ref[:, pl.ds(head, 1), :], axis=0)[:, :1]
        k_sq = jnp.max(norm_ref[:, pl.ds(MOBA_HEADS + head, 1), :], axis=0)[:, :1]
        reach = jnp.sqrt(q_sq * k_sq) * BOUND_SLACK
        shift.append(reach + bmax[hh])
        span.append(2.0 * reach + (bmax[hh] - bmin[hh]))
    safe = jnp.max(functools.reduce(jnp.maximum, span)) <= SAFE_SPAN

    def block_select(k_means, gates, n):
        i, hh = bodies[n]
        q_i = q_ref[0, i * L:(i + 1) * L, :]
        if i not in gates:
            gates[i] = _dot_nt(k_means[0], q_i) + _dot_nt(k_means[1], q_i)
        g = gates[i][hh * NB:(hh + 1) * NB]
        cnt = jnp.zeros((NB, L), jnp.int32)
        for m in range(i):
            gm = g[m:m + 1, :]
            beats = (gm > g) | ((gm == g) & (m < row))
            cnt = cnt + jnp.where(beats, 1, 0)
        return cnt < MOBA_TOPK

    def masked_q(n):
        i, hh = bodies[n]
        q_i = q_ref[0, i * L:(i + 1) * L, :]
        return jnp.where(head_lanes[hh], q_i, jnp.zeros_like(q_i))

    def pv_stage(outs, n, p_ref):
        i, hh = bodies[n]
        nk = (i + 1) * L
        acc = jnp.dot(vt_scr[hh * VT_ROWS:(hh + 1) * VT_ROWS, 0:nk], p_ref[0:nk, :],
                      preferred_element_type=F32)
        outs[n] = acc[:MOBA_HEAD_DIM] / acc[MOBA_HEAD_DIM:MOBA_HEAD_DIM + 1]
        if hh == HEADS_PER_STEP - 1:
            pair = [outs.pop(n - HEADS_PER_STEP + 1 + h) for h in range(HEADS_PER_STEP)]
            o_ref[0, i * L:(i + 1) * L, :] = jnp.concatenate(pair, axis=0).T

    def exact_path():
        gates, masks, qms, maxes, mrows, outs = {}, {}, {}, {}, {}, {}
        _cast_slabs(cast_src, cast_dst)
        k_means = stage_operands()

        def begin_body(n):
            i, hh = bodies[n]
            qms[n] = masked_q(n)
            if i > MOBA_TOPK:
                sel = block_select(k_means, gates, n)
                masks[n] = (jnp.where(sel, cfar[hh], NEG),
                            jnp.where(sel, 0.0, NEG))

        def score_tile(n, j):
            i, hh = bodies[n]
            t = _dot_nt(k_ref[0, j * L:(j + 1) * L, :], qms[n])
            if j == i:
                t = t + bias_ref[hh, 0]
            elif j == i - 1:
                t = t + bias_ref[hh, 1]
                if i > MOBA_TOPK:
                    t = t + masks[n][1][j:j + 1, :]
            elif i > MOBA_TOPK:
                t = t + masks[n][0][j:j + 1, :]
            else:
                t = t + cfar[hh]
            t_scr[n % T_SLOTS][j * L:(j + 1) * L, :] = t
            tm = jnp.max(t.reshape(L // 8, 8, L), axis=0)
            maxes[n] = tm if j == 0 else jnp.maximum(maxes[n], tm)
            if j == i:
                mrows[n] = jnp.max(maxes.pop(n), axis=0, keepdims=True)
                qms.pop(n)
                masks.pop(n, None)

        def exp_tile(n, j):
            t_ref, p_ref = t_scr[n % T_SLOTS], p_scr[n % T_SLOTS]
            p_ref[j * L:(j + 1) * L, :] = jnp.exp2(t_ref[j * L:(j + 1) * L, :] - mrows[n]).astype(BF16)

        for n0 in range(min(LOOKAHEAD, nb)):
            begin_body(n0)
            for j in range(bodies[n0][0] + 1):
                score_tile(n0, j)
        for n in range(nb):
            n_exp = bodies[n][0] + 1
            ahead = n + LOOKAHEAD
            n_score = bodies[ahead][0] + 1 if ahead < nb else 0
            if n_score:
                begin_body(ahead)
            for j in range(max(n_exp, n_score)):
                if j < n_score:
                    score_tile(ahead, j)
                if j < n_exp:
                    exp_tile(n, j)
            mrows.pop(n)
            pv_stage(outs, n, p_scr[n % T_SLOTS])

    def fast_path():
        gates, outs = {}, {}
        _cast_slabs(cast_src, cast_dst)
        k_means = stage_operands()
        shifted = {(hh, d): bias_ref[hh, d] - shift[hh]
                   for hh in range(HEADS_PER_STEP) for d in range(2)}

        def prob_tiles(n):
            i, hh = bodies[n]
            qm = masked_q(n)
            s_row = shift[hh]
            if i > MOBA_TOPK:
                sel = block_select(k_means, gates, n)
                m_far = jnp.where(sel, cfar[hh], NEG) - s_row
                m_prev = jnp.where(sel, 0.0, NEG)
            for j in range(i + 1):
                e = _dot_nt(k_ref[0, j * L:(j + 1) * L, :], qm)
                if j == i:
                    e = e + shifted[hh, 0]
                elif j == i - 1:
                    e = e + shifted[hh, 1]
                    if i > MOBA_TOPK:
                        e = e + m_prev[j:j + 1, :]
                else:
                    e = e + (m_far[j:j + 1, :] if i > MOBA_TOPK else cfar[hh] - s_row)
                p_scr[n % P_SLOTS][j * L:(j + 1) * L, :] = jnp.exp2(e).astype(BF16)

        for n0 in range(min(FAST_LOOKAHEAD, nb)):
            prob_tiles(n0)
        for n in range(nb):
            if n + FAST_LOOKAHEAD < nb:
                prob_tiles(n + FAST_LOOKAHEAD)
            pv_stage(outs, n, p_scr[n % P_SLOTS])

    pl.when(safe)(fast_path)
    pl.when(jnp.logical_not(safe))(exact_path)


def _moba(qkv3, norms, bias_t, bias_stats, cast_weights):
    B, S, _ = qkv3.shape
    L = MOBA_BLOCK
    n_hp = MOBA_HEADS // HEADS_PER_STEP
    blk = lambda off: pl.BlockSpec((1, S, LANES), lambda b, hp: (b, 0, off + hp))
    c_in, c_out, c_shapes = _cast_specs(cast_weights, B * n_hp, lambda b, hp: b * n_hp + hp)
    outs = pl.pallas_call(
        functools.partial(_moba_kernel, n_cast=len(cast_weights)),
        grid=(B, n_hp),
        in_specs=[
            blk(0), blk(n_hp), blk(2 * n_hp),
            pl.BlockSpec((norms.shape[0] // B,) + norms.shape[1:], lambda b, hp: (b, 0, 0)),
            pl.BlockSpec((HEADS_PER_STEP, 2, L, L), lambda b, hp: (hp, 0, 0, 0)),
            pl.BlockSpec((HEADS_PER_STEP, STAT_ROWS, LANES), lambda b, hp: (hp, 0, 0)),
        ] + c_in,
        out_specs=[pl.BlockSpec((1, S, LANES), lambda b, hp: (b, 0, hp))] + c_out,
        out_shape=[jax.ShapeDtypeStruct((B, S, D_MOBA), F32)] + c_shapes,
        scratch_shapes=[
            pltpu.VMEM((HEADS_PER_STEP * VT_ROWS, S), BF16),
        ] + [pltpu.VMEM((S, L), F32)] * T_SLOTS + [pltpu.VMEM((S, L), BF16)] * P_SLOTS,
        compiler_params=pltpu.CompilerParams(
            dimension_semantics=("arbitrary", "arbitrary"), vmem_limit_bytes=VMEM_LIMIT),
        name="moba",
    )(qkv3, qkv3, qkv3, norms, bias_t, bias_stats, *cast_weights)
    return outs[0], outs[1:]


def _ffn_kernel(x_ref, yb_ref, ya_ref, gb_ref, wo_ref, gf_ref, wg_ref, wu_ref, wd_ref, gl_ref,
                o_ref, *, final_norm):
    half = x_ref.shape[0] // 2
    st = {}

    def attn_norm(r):
        yb = yb_ref[r:r + half, :]
        ms = jnp.mean(yb * yb, axis=-1, keepdims=True)
        st["ybn", r] = (yb * lax.rsqrt(ms + EPS) * gb_ref[...]).astype(BF16)

    def out_proj(r):
        st["x1", r] = (x_ref[r:r + half, :]
                       + jnp.dot(st.pop(("ybn", r)), wo_ref[:D_MOBA, :], preferred_element_type=F32)
                       + jnp.dot(ya_ref[r:r + half, :], wo_ref[D_MOBA:, :],
                                 preferred_element_type=F32))

    def ffn_norm(r):
        x1 = st["x1", r]
        ms = jnp.mean(x1 * x1, axis=-1, keepdims=True)
        st["h", r] = (x1 * lax.rsqrt(ms + EPS) * gf_ref[...]).astype(BF16)

    def gate_up(r):
        h = st.pop(("h", r))
        g = jnp.dot(h, wg_ref[...], preferred_element_type=F32)
        u = jnp.dot(h, wu_ref[...], preferred_element_type=F32)
        st["a", r] = (jax.nn.silu(g) * u).astype(BF16)

    def down(r):
        x2 = st.pop(("x1", r)) + jnp.dot(st.pop(("a", r)), wd_ref[...], preferred_element_type=F32)
        if final_norm:
            ms = jnp.mean(x2 * x2, axis=-1, keepdims=True)
            x2 = x2 * lax.rsqrt(ms + EPS) * gl_ref[...]
        o_ref[r:r + half, :] = x2

    stages = [attn_norm, out_proj, ffn_norm, gate_up, down]
    for k in range(len(stages) + 1):
        if k < len(stages):
            stages[k](0)
        if k >= 1:
            stages[k - 1](half)


def _ffn(x2, yb, ya, gb, wo, gf, wg, wu, wd, gl, tm, final_norm):
    T, D = x2.shape
    d_ff = wg.shape[1]
    const = lambda shape: pl.BlockSpec(shape, lambda i: (0, 0), pipeline_mode=pl.Buffered(1))
    return pl.pallas_call(
        functools.partial(_ffn_kernel, final_norm=final_norm),
        grid=(T // tm,),
        in_specs=[
            pl.BlockSpec((tm, D), lambda i: (i, 0)),
            pl.BlockSpec((tm, D_MOBA), lambda i: (i, 0)),
            pl.BlockSpec((tm, D_GMLP), lambda i: (i, 0)),
            pl.BlockSpec((1, D_MOBA), lambda i: (0, 0)),
            const((D_MOBA + D_GMLP, D)),
            pl.BlockSpec((1, D), lambda i: (0, 0)),
            const((D, d_ff)), const((D, d_ff)), const((d_ff, D)),
            pl.BlockSpec((1, D), lambda i: (0, 0)),
        ],
        out_specs=pl.BlockSpec((tm, D), lambda i: (i, 0)),
        out_shape=jax.ShapeDtypeStruct((T, D), F32),
        compiler_params=pltpu.CompilerParams(
            dimension_semantics=("arbitrary",), vmem_limit_bytes=VMEM_LIMIT),
        name="outproj_ffn",
    )(x2, yb, ya, gb, wo, gf, wg, wu, wd, gl)


def kernel(x, rel_bias, norm_mix, w_in, gmlp_ln_g, gmlp_ln_b, w_spatial, b_spatial, out_norm_b,
           out_norm_a, w_out, norm_ffn, w_gate, w_up, w_down, norm_final):
    B, S, D = x.shape
    depth = w_in.shape[0]
    assert S % MOBA_BLOCK == 0 and S // MOBA_BLOCK > MOBA_TOPK
    far = _t5_bucket_np(np.arange(MOBA_BLOCK + 1, S + MOBA_BLOCK))
    assert (far == far[0]).all()
    far_bucket = int(far[0])

    assert depth == 1, "weight casts ride along the single layer's kernels"
    tm = 512
    T = B * S
    x2 = x.reshape(T, D)
    row = lambda v: v.reshape(1, -1)
    bias_t, bias_stats, (w_in_bf,) = _bias_tiles(rel_bias, far_bucket, [w_in[0]])
    b_sp_full = jnp.repeat(b_spatial[0].T, GMLP_GROUP_DIM, axis=1)
    qkv, ya, norms = _inproj(x2, row(norm_mix[0]), w_in_bf, row(gmlp_ln_g[0]), row(gmlp_ln_b[0]),
                             w_spatial[0], b_sp_full, row(out_norm_a[0]), 2 * tm)
    yb, (wo_bf, wg_bf, wu_bf, wd_bf) = _moba(qkv.reshape(B, S, -1), norms, bias_t, bias_stats,
                                             [w_out[0], w_gate[0], w_up[0], w_down[0]])
    out = _ffn(x2, yb.reshape(T, D_MOBA), ya, row(out_norm_b[0]), wo_bf, row(norm_ffn[0]),
               wg_bf, wu_bf, wd_bf, row(norm_final), tm, True)
    return out.reshape(B, S, D)
```

```python
import functools
import math

import numpy as np
import jax
import jax.numpy as jnp
from jax import lax
from jax.experimental import pallas as pl
from jax.experimental.pallas import tpu as pltpu

F32 = jnp.float32
BF16 = jnp.bfloat16

MOBA_HEADS = 8
MOBA_HEAD_DIM = 64
D_MOBA = MOBA_HEADS * MOBA_HEAD_DIM
GMLP_GROUPS = 8
GMLP_GROUP_DIM = 64
D_GMLP = GMLP_GROUPS * GMLP_GROUP_DIM
MOBA_BLOCK = 256
MOBA_TOPK = 3
GMLP_CHUNK = 128
NUM_BUCKETS = 32
REL_MAX_DISTANCE = 128
EPS = 1e-6
NEG = -1e9

LANES = 128
HEADS_PER_STEP = LANES // MOBA_HEAD_DIM
BF16_SUBLANES = 16
ONES_ROWS = BF16_SUBLANES
VT_ROWS = MOBA_HEAD_DIM + ONES_ROWS
VMEM_LIMIT = 56 * 1024 * 1024
T_SLOTS = 4
P_SLOTS = 4
LOOKAHEAD = 3
FAST_LOOKAHEAD = 3

LOG2E = math.log2(math.e)
Q_SCALE = MOBA_HEAD_DIM ** -0.5 * LOG2E

BOUND_SLACK = 1.0 + 2.0 ** -6
SAFE_SPAN = 100.0


def _t5_bucket_np(dist):
    n = np.maximum(dist, 0)
    max_exact = NUM_BUCKETS // 2
    nf = np.maximum(n, max_exact).astype(np.float32)
    ratio = np.log(nf / np.float32(max_exact)) / np.float32(math.log(REL_MAX_DISTANCE / max_exact))
    large = max_exact + (ratio * np.float32(NUM_BUCKETS - max_exact)).astype(np.int32)
    large = np.minimum(large, NUM_BUCKETS - 1)
    return np.where(n < max_exact, n, large).astype(np.int32)


def _cast_specs(weights, n_steps, step_of):
    in_specs, out_specs, out_shapes = [], [], []
    for w in weights:
        rows, cols = w.shape
        period = next(p for p in range(1, n_steps + 1)
                      if n_steps % p == 0 and (rows * p) % (n_steps * BF16_SUBLANES) == 0)
        spec = pl.BlockSpec((rows * period // n_steps, cols),
                            functools.partial(lambda p, *ids: (step_of(*ids) // p, 0), period))
        in_specs.append(spec)
        out_specs.append(spec)
        out_shapes.append(jax.ShapeDtypeStruct(w.shape, BF16))
    return in_specs, out_specs, out_shapes


def _cast_slabs(src_refs, dst_refs):
    for src, dst in zip(src_refs, dst_refs):
        dst[...] = src[...].astype(BF16)


BIAS_FAR, BIAS_MAX, BIAS_MIN = 0, 1, 2
STAT_ROWS = 8


def _bias_kernel(bucket_ref, rel_ref, *refs, far_bucket):
    n_cast = (len(refs) - 2) // 2
    out_ref, stat_ref = refs[n_cast], refs[n_cast + 1]
    _cast_slabs(refs[:n_cast], refs[n_cast + 2:])
    h = pl.program_id(0)
    L = MOBA_BLOCK
    vals = [rel_ref[b, h] * LOG2E for b in range(NUM_BUCKETS)]
    key = lax.broadcasted_iota(jnp.int32, (L, L), 0)
    qry = lax.broadcasted_iota(jnp.int32, (L, L), 1)
    for delta in range(2):
        bucket = bucket_ref[delta]
        acc = jnp.zeros((L, L), F32)
        for b in range(NUM_BUCKETS):
            acc = jnp.where(bucket == b, vals[b], acc)
        if delta == 0:
            acc = jnp.where(key > qry, NEG, acc)
        out_ref[0, delta] = acc
    stats = {BIAS_FAR: vals[far_bucket], BIAS_MAX: functools.reduce(jnp.maximum, vals),
             BIAS_MIN: functools.reduce(jnp.minimum, vals)}
    for r in range(STAT_ROWS):
        stat_ref[0, r:r + 1, :] = jnp.full((1, LANES), stats.get(r, 0.0), F32)


def _bias_tiles(rel_bias, far_bucket, cast_weights):
    L = MOBA_BLOCK
    kk = np.arange(L)[:, None]
    qq = np.arange(L)[None, :]
    bucket = np.stack([_t5_bucket_np(d * L + qq - kk) for d in range(2)]).astype(np.int32)
    c_in, c_out, c_shapes = _cast_specs(cast_weights, MOBA_HEADS, lambda h: h)
    outs = pl.pallas_call(
        functools.partial(_bias_kernel, far_bucket=far_bucket),
        grid=(MOBA_HEADS,),
        in_specs=[
            pl.BlockSpec((2, L, L), lambda h: (0, 0, 0)),
            pl.BlockSpec(memory_space=pltpu.SMEM),
        ] + c_in,
        out_specs=[pl.BlockSpec((1, 2, L, L), lambda h: (h, 0, 0, 0)),
                   pl.BlockSpec((1, STAT_ROWS, LANES), lambda h: (h, 0, 0))] + c_out,
        out_shape=[jax.ShapeDtypeStruct((MOBA_HEADS, 2, L, L), F32),
                   jax.ShapeDtypeStruct((MOBA_HEADS, STAT_ROWS, LANES), F32)] + c_shapes,
        compiler_params=pltpu.CompilerParams(
            dimension_semantics=("arbitrary",), vmem_limit_bytes=VMEM_LIMIT),
        name="bias_tiles",
    )(jnp.asarray(bucket), rel_bias, *cast_weights)
    return outs[0], outs[1], outs[2:]


def _inproj_kernel(x_ref, g_ref, w_ref, lng_ref, lnb_ref, wsp_ref, bsp_ref, gn_ref,
                   qkv_ref, ya_ref, norm_ref, uz_scr):
    C = GMLP_CHUNK
    tm = x_ref.shape[0]
    n_qkv = 3 * D_MOBA
    x = x_ref[...]
    ms = jnp.mean(x * x, axis=-1, keepdims=True)
    h = (x * lax.rsqrt(ms + EPS) * g_ref[...]).astype(BF16)
    uz_scr[...] = jnp.dot(h, w_ref[:, n_qkv:], preferred_element_type=F32)

    t_ix = lax.broadcasted_iota(jnp.int32, (C, C), 0)
    s_ix = lax.broadcasted_iota(jnp.int32, (C, C), 1)
    w_sp = [jnp.where(t_ix >= s_ix, wsp_ref[g], 0.0).astype(BF16) for g in range(GMLP_GROUPS)]
    first = lax.broadcasted_iota(jnp.int32, (1, LANES), 1) < GMLP_GROUP_DIM

    def qkv_cols(c0, c1):
        acc = jnp.dot(h, w_ref[:, c0:c1], preferred_element_type=F32)
        if c1 <= D_MOBA:
            acc = acc * Q_SCALE
        qkv_ref[:, c0:c1] = acc.astype(BF16)
        if c1 <= 2 * D_MOBA:
            sq = acc * acc
            for c in range(c0, c1, MOBA_HEAD_DIM):
                lanes = (c - c0) // LANES * LANES
                own = first if (c - c0) % LANES == 0 else jnp.logical_not(first)
                rows_sq = jnp.sum(jnp.where(own, sq[:, lanes:lanes + LANES], 0.0), axis=-1, keepdims=True)
                top = jnp.max(rows_sq, axis=0, keepdims=True)
                norm_ref[0, c // MOBA_HEAD_DIM:c // MOBA_HEAD_DIM + 1, :] = jnp.broadcast_to(top, (1, LANES))

    def gmlp_rows(c):
        rows = slice(c * C, (c + 1) * C)
        z = jax.nn.gelu(uz_scr[rows, D_GMLP:])
        mu = jnp.mean(z, axis=-1, keepdims=True)
        zc = z - mu
        var = jnp.mean(zc * zc, axis=-1, keepdims=True)
        zn = (zc * lax.rsqrt(var + EPS) * lng_ref[...] + lnb_ref[...]).astype(BF16)
        cols = []
        for gp in range(D_GMLP // LANES):
            zp = zn[:, gp * LANES:(gp + 1) * LANES]
            r0 = jnp.dot(w_sp[2 * gp], zp, preferred_element_type=F32)
            r1 = jnp.dot(w_sp[2 * gp + 1], zp, preferred_element_type=F32)
            cols.append(jnp.where(first, r0, r1))
        s = jnp.concatenate(cols, axis=1) + bsp_ref[...]
        y = jax.nn.gelu(uz_scr[rows, :D_GMLP]) * s
        ms_y = jnp.mean(y * y, axis=-1, keepdims=True)
        ya_ref[rows, :] = (y * lax.rsqrt(ms_y + EPS) * gn_ref[...]).astype(BF16)

    col_blocks = [(c, c + 2 * LANES) for c in range(0, n_qkv, 2 * LANES)]
    n_chunks = tm // C
    for step in range(max(len(col_blocks), n_chunks)):
        if step < len(col_blocks):
            qkv_cols(*col_blocks[step])
        if step < n_chunks:
            gmlp_rows(step)


def _inproj(x2, g, w_bf, ln_g, ln_b, w_sp, b_sp_full, gn, tm):
    T, D = x2.shape
    n_all = w_bf.shape[1]
    n_qkv = 3 * D_MOBA
    C = GMLP_CHUNK
    vec = pl.BlockSpec((1, D_GMLP), lambda i: (0, 0))
    return pl.pallas_call(
        _inproj_kernel,
        grid=(T // tm,),
        in_specs=[
            pl.BlockSpec((tm, D), lambda i: (i, 0)),
            pl.BlockSpec((1, D), lambda i: (0, 0)),
            pl.BlockSpec((D, n_all), lambda i: (0, 0), pipeline_mode=pl.Buffered(1)),
            vec, vec,
            pl.BlockSpec((GMLP_GROUPS, C, C), lambda i: (0, 0, 0)),
            pl.BlockSpec((C, D_GMLP), lambda i: (0, 0)),
            vec,
        ],
        out_specs=[
            pl.BlockSpec((tm, n_qkv), lambda i: (i, 0)),
            pl.BlockSpec((tm, D_GMLP), lambda i: (i, 0)),
            pl.BlockSpec((1, 2 * MOBA_HEADS, LANES), lambda i: (i, 0, 0)),
        ],
        out_shape=[
            jax.ShapeDtypeStruct((T, n_qkv), BF16),
            jax.ShapeDtypeStruct((T, D_GMLP), BF16),
            jax.ShapeDtypeStruct((T // tm, 2 * MOBA_HEADS, LANES), F32),
        ],
        scratch_shapes=[pltpu.VMEM((tm, n_all - n_qkv), F32)],
        compiler_params=pltpu.CompilerParams(
            dimension_semantics=("arbitrary",), vmem_limit_bytes=VMEM_LIMIT),
        name="inproj_gmlp",
    )(x2, g, w_bf, ln_g, ln_b, w_sp, b_sp_full, gn)


def _dot_nt(a, b):
    return lax.dot_general(a, b, (((1,), (1,)), ((), ())), preferred_element_type=F32)


def _moba_kernel(q_ref, k_ref, v_ref, norm_ref, bias_ref, stat_ref, *refs, n_cast):
    cast_src, o_ref, cast_dst = refs[:n_cast], refs[n_cast], refs[n_cast + 1:2 * n_cast + 1]
    vt_scr, slots = refs[2 * n_cast + 1], refs[2 * n_cast + 2:]
    t_scr, p_scr = slots[:T_SLOTS], slots[T_SLOTS:]
    L = MOBA_BLOCK
    S = q_ref.shape[1]
    NB = S // L
    hp = pl.program_id(0)
    lane = lax.broadcasted_iota(jnp.int32, (1, LANES), 1)
    head_lanes = [(lane >= hh * MOBA_HEAD_DIM) & (lane < (hh + 1) * MOBA_HEAD_DIM)
                  for hh in range(HEADS_PER_STEP)]
    row = lax.broadcasted_iota(jnp.int32, (NB, L), 0)
    bodies = [(i, hh) for i in range(NB) for hh in range(HEADS_PER_STEP)]
    nb = len(bodies)

    cfar = [stat_ref[hh, BIAS_FAR:BIAS_FAR + 1, :1] for hh in range(HEADS_PER_STEP)]
    bmax = [stat_ref[hh, BIAS_MAX:BIAS_MAX + 1, :1] for hh in range(HEADS_PER_STEP)]
    bmin = [stat_ref[hh, BIAS_MIN:BIAS_MIN + 1, :1] for hh in range(HEADS_PER_STEP)]

    def stage_operands():
        vt = v_ref[0].astype(F32).T.astype(BF16)
        for hh in range(HEADS_PER_STEP):
            vt_scr[hh * VT_ROWS:hh * VT_ROWS + MOBA_HEAD_DIM, :] = (
                vt[hh * MOBA_HEAD_DIM:(hh + 1) * MOBA_HEAD_DIM])
            vt_scr[hh * VT_ROWS + MOBA_HEAD_DIM:(hh + 1) * VT_ROWS, :] = jnp.ones((ONES_ROWS, S), BF16)
        km = jnp.concatenate(
            [jnp.sum(k_ref[0, n * L:(n + 1) * L, :].astype(F32), axis=0, keepdims=True) * (1.0 / L)
             for n in range(NB)], axis=0)
        kmm = jnp.concatenate([jnp.where(head_lanes[hh], km, 0.0) for hh in range(HEADS_PER_STEP)],
                              axis=0)
        km_hi = kmm.astype(BF16)
        return km_hi, (kmm - km_hi.astype(F32)).astype(BF16)

    shift, span = [], []
    for hh in range(HEADS_PER_STEP):
        head = hp * HEADS_PER_STEP + hh
        q_sq = jnp.max(norm_ref[:, pl.ds(head, 1), :], axis=0)[:, :1]
        k_sq = jnp.max(norm_ref[:, pl.ds(MOBA_HEADS + head, 1), :], axis=0)[:, :1]
        reach = jnp.sqrt(q_sq * k_sq) * BOUND_SLACK
        shift.append(reach + bmax[hh])
        span.append(2.0 * reach + (bmax[hh] - bmin[hh]))
    safe = jnp.max(functools.reduce(jnp.maximum, span)) <= SAFE_SPAN

    def block_select(k_means, gates, n):
        i, hh = bodies[n]
        q_i = q_ref[0, i * L:(i + 1) * L, :]
        if i not in gates:
            gates[i] = _dot_nt(k_means[0], q_i) + _dot_nt(k_means[1], q_i)
        g = gates[i][hh * NB:(hh + 1) * NB]
        cnt = jnp.zeros((NB, L), jnp.int32)
        for m in range(i):
            gm = g[m:m + 1, :]
            beats = (gm > g) | ((gm == g) & (m < row))
            cnt = cnt + jnp.where(beats, 1, 0)
        return cnt < MOBA_TOPK

    def masked_q(n):
        i, hh = bodies[n]
        q_i = q_ref[0, i * L:(i + 1) * L, :]
        return jnp.where(head_lanes[hh], q_i, jnp.zeros_like(q_i))

    def pv_stage(outs, n, p_ref):
        i, hh = bodies[n]
        nk = (i + 1) * L
        acc = jnp.dot(vt_scr[hh * VT_ROWS:(hh + 1) * VT_ROWS, 0:nk], p_ref[0:nk, :],
                      preferred_element_type=F32)
        outs[n] = acc[:MOBA_HEAD_DIM] / acc[MOBA_HEAD_DIM:MOBA_HEAD_DIM + 1]
        if hh == HEADS_PER_STEP - 1:
            pair = [outs.pop(n - HEADS_PER_STEP + 1 + h) for h in range(HEADS_PER_STEP)]
            o_ref[0, i * L:(i + 1) * L, :] = jnp.concatenate(pair, axis=0).T

    def exact_path():
        gates, masks, qms, maxes, mrows, outs = {}, {}, {}, {}, {}, {}
        _cast_slabs(cast_src, cast_dst)
        k_means = stage_operands()

        def begin_body(n):
            i, hh = bodies[n]
            qms[n] = masked_q(n)
            if i > MOBA_TOPK:
                sel = block_select(k_means, gates, n)
                masks[n] = (jnp.where(sel, cfar[hh], NEG),
                            jnp.where(sel, 0.0, NEG))

        def score_tile(n, j):
            i, hh = bodies[n]
            t = _dot_nt(k_ref[0, j * L:(j + 1) * L, :], qms[n])
            if j == i:
                t = t + bias_ref[hh, 0]
            elif j == i - 1:
                t = t + bias_ref[hh, 1]
                if i > MOBA_TOPK:
                    t = t + masks[n][1][j:j + 1, :]
            elif i > MOBA_TOPK:
                t = t + masks[n][0][j:j + 1, :]
            else:
                t = t + cfar[hh]
            t_scr[n % T_SLOTS][j * L:(j + 1) * L, :] = t
            tm = jnp.max(t.reshape(L // 8, 8, L), axis=0)
            maxes[n] = tm if j == 0 else jnp.maximum(maxes[n], tm)
            if j == i:
                mrows[n] = jnp.max(maxes.pop(n), axis=0, keepdims=True)
                qms.pop(n)
                masks.pop(n, None)

        def exp_tile(n, j):
            t_ref, p_ref = t_scr[n % T_SLOTS], p_scr[n % T_SLOTS]
            p_ref[j * L:(j + 1) * L, :] = jnp.exp2(t_ref[j * L:(j + 1) * L, :] - mrows[n]).astype(BF16)

        for n0 in range(min(LOOKAHEAD, nb)):
            begin_body(n0)
            for j in range(bodies[n0][0] + 1):
                score_tile(n0, j)
        for n in range(nb):
            n_exp = bodies[n][0] + 1
            ahead = n + LOOKAHEAD
            n_score = bodies[ahead][0] + 1 if ahead < nb else 0
            if n_score:
                begin_body(ahead)
            for j in range(max(n_exp, n_score)):
                if j < n_score:
                    score_tile(ahead, j)
                if j < n_exp:
                    exp_tile(n, j)
            mrows.pop(n)
            pv_stage(outs, n, p_scr[n % T_SLOTS])

    def fast_path():
        gates, outs = {}, {}
        _cast_slabs(cast_src, cast_dst)
        k_means = stage_operands()
        shifted = {(hh, d): bias_ref[hh, d] - shift[hh]
                   for hh in range(HEADS_PER_STEP) for d in range(2)}

        def prob_tiles(n):
            i, hh = bodies[n]
            qm = masked_q(n)
            s_row = shift[hh]
            if i > MOBA_TOPK:
                sel = block_select(k_means, gates, n)
                m_far = jnp.where(sel, cfar[hh], NEG) - s_row
                m_prev = jnp.where(sel, 0.0, NEG)
            for j in range(i + 1):
                e = _dot_nt(k_ref[0, j * L:(j + 1) * L, :], qm)
                if j == i:
                    e = e + shifted[hh, 0]
                elif j == i - 1:
                    e = e + shifted[hh, 1]
                    if i > MOBA_TOPK:
                        e = e + m_prev[j:j + 1, :]
                else:
                    e = e + (m_far[j:j + 1, :] if i > MOBA_TOPK else cfar[hh] - s_row)
                p_scr[n % P_SLOTS][j * L:(j + 1) * L, :] = jnp.exp2(e).astype(BF16)

        for n0 in range(min(FAST_LOOKAHEAD, nb)):
            prob_tiles(n0)
        for n in range(nb):
            if n + FAST_LOOKAHEAD < nb:
                prob_tiles(n + FAST_LOOKAHEAD)
            pv_stage(outs, n, p_scr[n % P_SLOTS])

    pl.when(safe)(fast_path)
    pl.when(jnp.logical_not(safe))(exact_path)


def _moba(qkv3, norms, bias_t, bias_stats, cast_weights):
    B, S, _ = qkv3.shape
    L = MOBA_BLOCK
    n_hp = MOBA_HEADS // HEADS_PER_STEP
    blk = lambda off: pl.BlockSpec((1, S, LANES), lambda hp, b: (b, 0, off + hp))
    c_in, c_out, c_shapes = _cast_specs(cast_weights, B * n_hp, lambda hp, b: hp * B + b)
    outs = pl.pallas_call(
        functools.partial(_moba_kernel, n_cast=len(cast_weights)),
        grid=(n_hp, B),
        in_specs=[
            blk(0), blk(n_hp), blk(2 * n_hp),
            pl.BlockSpec((norms.shape[0] // B,) + norms.shape[1:], lambda hp, b: (b, 0, 0)),
            pl.BlockSpec((HEADS_PER_STEP, 2, L, L), lambda hp, b: (hp, 0, 0, 0)),
            pl.BlockSpec((HEADS_PER_STEP, STAT_ROWS, LANES), lambda hp, b: (hp, 0, 0)),
        ] + c_in,
        out_specs=[pl.BlockSpec((1, S, LANES), lambda hp, b: (b, 0, hp))] + c_out,
        out_shape=[jax.ShapeDtypeStruct((B, S, D_MOBA), F32)] + c_shapes,
        scratch_shapes=[
            pltpu.VMEM((HEADS_PER_STEP * VT_ROWS, S), BF16),
        ] + [pltpu.VMEM((S, L), F32)] * T_SLOTS + [pltpu.VMEM((S, L), BF16)] * P_SLOTS,
        compiler_params=pltpu.CompilerParams(
            dimension_semantics=("arbitrary", "arbitrary"), vmem_limit_bytes=VMEM_LIMIT),
        name="moba",
    )(qkv3, qkv3, qkv3, norms, bias_t, bias_stats, *cast_weights)
    return outs[0], outs[1:]


def _ffn_kernel(x_ref, yb_ref, ya_ref, gb_ref, wo_ref, gf_ref, wg_ref, wu_ref, wd_ref, gl_ref,
                o_ref, *, final_norm):
    half = x_ref.shape[0] // 2
    st = {}

    def attn_norm(r):
        yb = yb_ref[r:r + half, :]
        ms = jnp.mean(yb * yb, axis=-1, keepdims=True)
        st["ybn", r] = (yb * lax.rsqrt(ms + EPS) * gb_ref[...]).astype(BF16)

    def out_proj(r):
        st["x1", r] = (x_ref[r:r + half, :]
                       + jnp.dot(st.pop(("ybn", r)), wo_ref[:D_MOBA, :], preferred_element_type=F32)
                       + jnp.dot(ya_ref[r:r + half, :], wo_ref[D_MOBA:, :],
                                 preferred_element_type=F32))

    def ffn_norm(r):
        x1 = st["x1", r]
        ms = jnp.mean(x1 * x1, axis=-1, keepdims=True)
        st["h", r] = (x1 * lax.rsqrt(ms + EPS) * gf_ref[...]).astype(BF16)

    def gate_up(r):
        h = st.pop(("h", r))
        g = jnp.dot(h, wg_ref[...], preferred_element_type=F32)
        u = jnp.dot(h, wu_ref[...], preferred_element_type=F32)
        st["a", r] = (jax.nn.silu(g) * u).astype(BF16)

    def down(r):
        x2 = st.pop(("x1", r)) + jnp.dot(st.pop(("a", r)), wd_ref[...], preferred_element_type=F32)
        if final_norm:
            ms = jnp.mean(x2 * x2, axis=-1, keepdims=True)
            x2 = x2 * lax.rsqrt(ms + EPS) * gl_ref[...]
        o_ref[r:r + half, :] = x2

    stages = [attn_norm, out_proj, ffn_norm, gate_up, down]
    for k in range(len(stages) + 1):
        if k < len(stages):
            stages[k](0)
        if k >= 1:
            stages[k - 1](half)


def _ffn(x2, yb, ya, gb, wo, gf, wg, wu, wd, gl, tm, final_norm):
    T, D = x2.shape
    d_ff = wg.shape[1]
    const = lambda shape: pl.BlockSpec(shape, lambda i: (0, 0), pipeline_mode=pl.Buffered(1))
    return pl.pallas_call(
        functools.partial(_ffn_kernel, final_norm=final_norm),
        grid=(T // tm,),
        in_specs=[
            pl.BlockSpec((tm, D), lambda i: (i, 0)),
            pl.BlockSpec((tm, D_MOBA), lambda i: (i, 0)),
            pl.BlockSpec((tm, D_GMLP), lambda i: (i, 0)),
            pl.BlockSpec((1, D_MOBA), lambda i: (0, 0)),
            const((D_MOBA + D_GMLP, D)),
            pl.BlockSpec((1, D), lambda i: (0, 0)),
            const((D, d_ff)), const((D, d_ff)), const((d_ff, D)),
            pl.BlockSpec((1, D), lambda i: (0, 0)),
        ],
        out_specs=pl.BlockSpec((tm, D), lambda i: (i, 0)),
        out_shape=jax.ShapeDtypeStruct((T, D), F32),
        compiler_params=pltpu.CompilerParams(
            dimension_semantics=("arbitrary",), vmem_limit_bytes=VMEM_LIMIT),
        name="outproj_ffn",
    )(x2, yb, ya, gb, wo, gf, wg, wu, wd, gl)


def kernel(x, rel_bias, norm_mix, w_in, gmlp_ln_g, gmlp_ln_b, w_spatial, b_spatial, out_norm_b,
           out_norm_a, w_out, norm_ffn, w_gate, w_up, w_down, norm_final):
    B, S, D = x.shape
    depth = w_in.shape[0]
    assert S % MOBA_BLOCK == 0 and S // MOBA_BLOCK > MOBA_TOPK
    far = _t5_bucket_np(np.arange(MOBA_BLOCK + 1, S + MOBA_BLOCK))
    assert (far == far[0]).all()
    far_bucket = int(far[0])

    assert depth == 1, "weight casts ride along the single layer's kernels"
    tm = 512
    T = B * S
    x2 = x.reshape(T, D)
    row = lambda v: v.reshape(1, -1)
    bias_t, bias_stats, (w_in_bf,) = _bias_tiles(rel_bias, far_bucket, [w_in[0]])
    b_sp_full = jnp.repeat(b_spatial[0].T, GMLP_GROUP_DIM, axis=1)
    qkv, ya, norms = _inproj(x2, row(norm_mix[0]), w_in_bf, row(gmlp_ln_g[0]), row(gmlp_ln_b[0]),
                             w_spatial[0], b_sp_full, row(out_norm_a[0]), 2 * tm)
    yb, (wo_bf, wg_bf, wu_bf, wd_bf) = _moba(qkv.reshape(B, S, -1), norms, bias_t, bias_stats,
                                             [w_out[0], w_gate[0], w_up[0], w_down[0]])
    out = _ffn(x2, yb.reshape(T, D_MOBA), ya, row(out_norm_b[0]), wo_bf, row(norm_ffn[0]),
               wg_bf, wu_bf, wd_bf, row(norm_final), tm, True)
    return out.reshape(B, S, D)
```

```python
import functools
import math

import numpy as np
import jax
import jax.numpy as jnp
from jax import lax
from jax.experimental import pallas as pl
from jax.experimental.pallas import tpu as pltpu

F32 = jnp.float32
BF16 = jnp.bfloat16

MOBA_HEADS = 8
MOBA_HEAD_DIM = 64
D_MOBA = MOBA_HEADS * MOBA_HEAD_DIM
GMLP_GROUPS = 8
GMLP_GROUP_DIM = 64
D_GMLP = GMLP_GROUPS * GMLP_GROUP_DIM
MOBA_BLOCK = 256
MOBA_TOPK = 3
GMLP_CHUNK = 128
NUM_BUCKETS = 32
REL_MAX_DISTANCE = 128
EPS = 1e-6
NEG = -1e9

LANES = 128
HEADS_PER_STEP = LANES // MOBA_HEAD_DIM
BF16_SUBLANES = 16
ONES_ROWS = BF16_SUBLANES
VT_ROWS = MOBA_HEAD_DIM + ONES_ROWS
VMEM_LIMIT = 56 * 1024 * 1024
T_SLOTS = 4
P_SLOTS = 4
LOOKAHEAD = 3
FAST_LOOKAHEAD = 3

LOG2E = math.log2(math.e)
Q_SCALE = MOBA_HEAD_DIM ** -0.5 * LOG2E

BOUND_SLACK = 1.0 + 2.0 ** -6
SAFE_SPAN = 100.0


def _t5_bucket_np(dist):
    n = np.maximum(dist, 0)
    max_exact = NUM_BUCKETS // 2
    nf = np.maximum(n, max_exact).astype(np.float32)
    ratio = np.log(nf / np.float32(max_exact)) / np.float32(math.log(REL_MAX_DISTANCE / max_exact))
    large = max_exact + (ratio * np.float32(NUM_BUCKETS - max_exact)).astype(np.int32)
    large = np.minimum(large, NUM_BUCKETS - 1)
    return np.where(n < max_exact, n, large).astype(np.int32)


def _cast_specs(weights, n_steps, step_of):
    in_specs, out_specs, out_shapes = [], [], []
    for w in weights:
        rows, cols = w.shape
        period = next(p for p in range(1, n_steps + 1)
                      if n_steps % p == 0 and (rows * p) % (n_steps * BF16_SUBLANES) == 0)
        spec = pl.BlockSpec((rows * period // n_steps, cols),
                            functools.partial(lambda p, *ids: (step_of(*ids) // p, 0), period))
        in_specs.append(spec)
        out_specs.append(spec)
        out_shapes.append(jax.ShapeDtypeStruct(w.shape, BF16))
    return in_specs, out_specs, out_shapes


def _cast_slabs(src_refs, dst_refs):
    for src, dst in zip(src_refs, dst_refs):
        dst[...] = src[...].astype(BF16)


BIAS_FAR, BIAS_MAX, BIAS_MIN = 0, 1, 2
STAT_ROWS = 8


def _bias_kernel(bucket_ref, rel_ref, *refs, far_bucket):
    n_cast = (len(refs) - 2) // 2
    out_ref, stat_ref = refs[n_cast], refs[n_cast + 1]
    _cast_slabs(refs[:n_cast], refs[n_cast + 2:])
    h = pl.program_id(0)
    L = MOBA_BLOCK
    vals = [rel_ref[b, h] * LOG2E for b in range(NUM_BUCKETS)]
    key = lax.broadcasted_iota(jnp.int32, (L, L), 0)
    qry = lax.broadcasted_iota(jnp.int32, (L, L), 1)
    for delta in range(2):
        bucket = bucket_ref[delta]
        acc = jnp.zeros((L, L), F32)
        for b in range(NUM_BUCKETS):
            acc = jnp.where(bucket == b, vals[b], acc)
        if delta == 0:
            acc = jnp.where(key > qry, NEG, acc)
        out_ref[0, delta] = acc
    stats = {BIAS_FAR: vals[far_bucket], BIAS_MAX: functools.reduce(jnp.maximum, vals),
             BIAS_MIN: functools.reduce(jnp.minimum, vals)}
    for r in range(STAT_ROWS):
        stat_ref[0, r:r + 1, :] = jnp.full((1, LANES), stats.get(r, 0.0), F32)


def _bias_tiles(rel_bias, far_bucket, cast_weights):
    L = MOBA_BLOCK
    kk = np.arange(L)[:, None]
    qq = np.arange(L)[None, :]
    bucket = np.stack([_t5_bucket_np(d * L + qq - kk) for d in range(2)]).astype(np.int32)
    c_in, c_out, c_shapes = _cast_specs(cast_weights, MOBA_HEADS, lambda h: h)
    outs = pl.pallas_call(
        functools.partial(_bias_kernel, far_bucket=far_bucket),
        grid=(MOBA_HEADS,),
        in_specs=[
            pl.BlockSpec((2, L, L), lambda h: (0, 0, 0)),
            pl.BlockSpec(memory_space=pltpu.SMEM),
        ] + c_in,
        out_specs=[pl.BlockSpec((1, 2, L, L), lambda h: (h, 0, 0, 0)),
                   pl.BlockSpec((1, STAT_ROWS, LANES), lambda h: (h, 0, 0))] + c_out,
        out_shape=[jax.ShapeDtypeStruct((MOBA_HEADS, 2, L, L), F32),
                   jax.ShapeDtypeStruct((MOBA_HEADS, STAT_ROWS, LANES), F32)] + c_shapes,
        compiler_params=pltpu.CompilerParams(
            dimension_semantics=("arbitrary",), vmem_limit_bytes=VMEM_LIMIT),
        name="bias_tiles",
    )(jnp.asarray(bucket), rel_bias, *cast_weights)
    return outs[0], outs[1], outs[2:]


def _inproj_kernel(x_ref, g_ref, w_ref, lng_ref, lnb_ref, wsp_ref, bsp_ref, gn_ref,
                   qkv_ref, ya_ref, norm_ref, uz_scr):
    C = GMLP_CHUNK
    tm = x_ref.shape[0]
    n_qkv = 3 * D_MOBA
    x = x_ref[...]
    ms = jnp.mean(x * x, axis=-1, keepdims=True)
    h = (x * lax.rsqrt(ms + EPS) * g_ref[...]).astype(BF16)
    uz_scr[...] = jnp.dot(h, w_ref[:, n_qkv:], preferred_element_type=F32)

    t_ix = lax.broadcasted_iota(jnp.int32, (C, C), 0)
    s_ix = lax.broadcasted_iota(jnp.int32, (C, C), 1)
    w_sp = [jnp.where(t_ix >= s_ix, wsp_ref[g], 0.0).astype(BF16) for g in range(GMLP_GROUPS)]
    first = lax.broadcasted_iota(jnp.int32, (1, LANES), 1) < GMLP_GROUP_DIM

    def qkv_cols(c0, c1):
        acc = jnp.dot(h, w_ref[:, c0:c1], preferred_element_type=F32)
        if c1 <= D_MOBA:
            acc = acc * Q_SCALE
        qkv_ref[:, c0:c1] = acc.astype(BF16)
        if c1 <= 2 * D_MOBA:
            sq = acc * acc
            for c in range(c0, c1, MOBA_HEAD_DIM):
                lanes = (c - c0) // LANES * LANES
                own = first if (c - c0) % LANES == 0 else jnp.logical_not(first)
                rows_sq = jnp.sum(jnp.where(own, sq[:, lanes:lanes + LANES], 0.0), axis=-1, keepdims=True)
                top = jnp.max(rows_sq, axis=0, keepdims=True)
                norm_ref[0, c // MOBA_HEAD_DIM:c // MOBA_HEAD_DIM + 1, :] = jnp.broadcast_to(top, (1, LANES))

    def gmlp_rows(c):
        rows = slice(c * C, (c + 1) * C)
        z = jax.nn.gelu(uz_scr[rows, D_GMLP:])
        mu = jnp.mean(z, axis=-1, keepdims=True)
        zc = z - mu
        var = jnp.mean(zc * zc, axis=-1, keepdims=True)
        zn = (zc * lax.rsqrt(var + EPS) * lng_ref[...] + lnb_ref[...]).astype(BF16)
        cols = []
        for gp in range(D_GMLP // LANES):
            zp = zn[:, gp * LANES:(gp + 1) * LANES]
            r0 = jnp.dot(w_sp[2 * gp], zp, preferred_element_type=F32)
            r1 = jnp.dot(w_sp[2 * gp + 1], zp, preferred_element_type=F32)
            cols.append(jnp.where(first, r0, r1))
        s = jnp.concatenate(cols, axis=1) + bsp_ref[...]
        y = jax.nn.gelu(uz_scr[rows, :D_GMLP]) * s
        ms_y = jnp.mean(y * y, axis=-1, keepdims=True)
        ya_ref[rows, :] = (y * lax.rsqrt(ms_y + EPS) * gn_ref[...]).astype(BF16)

    col_blocks = [(c, c + 2 * LANES) for c in range(0, n_qkv, 2 * LANES)]
    n_chunks = tm // C
    for step in range(max(len(col_blocks), n_chunks)):
        if step < len(col_blocks):
            qkv_cols(*col_blocks[step])
        if step < n_chunks:
            gmlp_rows(step)


def _inproj(x2, g, w_bf, ln_g, ln_b, w_sp, b_sp_full, gn, tm):
    T, D = x2.shape
    n_all = w_bf.shape[1]
    n_qkv = 3 * D_MOBA
    C = GMLP_CHUNK
    vec = pl.BlockSpec((1, D_GMLP), lambda i: (0, 0))
    return pl.pallas_call(
        _inproj_kernel,
        grid=(T // tm,),
        in_specs=[
            pl.BlockSpec((tm, D), lambda i: (i, 0)),
            pl.BlockSpec((1, D), lambda i: (0, 0)),
            pl.BlockSpec((D, n_all), lambda i: (0, 0), pipeline_mode=pl.Buffered(1)),
            vec, vec,
            pl.BlockSpec((GMLP_GROUPS, C, C), lambda i: (0, 0, 0)),
            pl.BlockSpec((C, D_GMLP), lambda i: (0, 0)),
            vec,
        ],
        out_specs=[
            pl.BlockSpec((tm, n_qkv), lambda i: (i, 0)),
            pl.BlockSpec((tm, D_GMLP), lambda i: (i, 0)),
            pl.BlockSpec((1, 2 * MOBA_HEADS, LANES), lambda i: (i, 0, 0)),
        ],
        out_shape=[
            jax.ShapeDtypeStruct((T, n_qkv), BF16),
            jax.ShapeDtypeStruct((T, D_GMLP), BF16),
            jax.ShapeDtypeStruct((T // tm, 2 * MOBA_HEADS, LANES), F32),
        ],
        scratch_shapes=[pltpu.VMEM((tm, n_all - n_qkv), F32)],
        compiler_params=pltpu.CompilerParams(
            dimension_semantics=("arbitrary",), vmem_limit_bytes=VMEM_LIMIT),
        name="inproj_gmlp",
    )(x2, g, w_bf, ln_g, ln_b, w_sp, b_sp_full, gn)


def _moba_kernel(q_ref, k_ref, v_ref, norm_ref, bias_ref, stat_ref, *refs, n_cast):
    cast_src, o_ref, cast_dst = refs[:n_cast], refs[n_cast], refs[n_cast + 1:2 * n_cast + 1]
    vt_scr, qt_scr, slots = refs[2 * n_cast + 1], refs[2 * n_cast + 2], refs[2 * n_cast + 3:]
    t_scr, p_scr = slots[:T_SLOTS], slots[T_SLOTS:]
    L = MOBA_BLOCK
    S = q_ref.shape[1]
    NB = S // L
    hp = pl.program_id(0)
    lane = lax.broadcasted_iota(jnp.int32, (1, LANES), 1)
    head_lanes = [(lane >= hh * MOBA_HEAD_DIM) & (lane < (hh + 1) * MOBA_HEAD_DIM)
                  for hh in range(HEADS_PER_STEP)]
    row = lax.broadcasted_iota(jnp.int32, (NB, L), 0)
    bodies = [(i, hh) for i in range(NB) for hh in range(HEADS_PER_STEP)]
    nb = len(bodies)

    cfar = [stat_ref[hh, BIAS_FAR:BIAS_FAR + 1, :1] for hh in range(HEADS_PER_STEP)]
    bmax = [stat_ref[hh, BIAS_MAX:BIAS_MAX + 1, :1] for hh in range(HEADS_PER_STEP)]
    bmin = [stat_ref[hh, BIAS_MIN:BIAS_MIN + 1, :1] for hh in range(HEADS_PER_STEP)]

    def stage_operands():
        vt = v_ref[0].astype(F32).T.astype(BF16)
        for hh in range(HEADS_PER_STEP):
            vt_scr[hh * VT_ROWS:hh * VT_ROWS + MOBA_HEAD_DIM, :] = (
                vt[hh * MOBA_HEAD_DIM:(hh + 1) * MOBA_HEAD_DIM])
            vt_scr[hh * VT_ROWS + MOBA_HEAD_DIM:(hh + 1) * VT_ROWS, :] = jnp.ones((ONES_ROWS, S), BF16)
        qt = q_ref[0].astype(F32).T.astype(BF16)
        dim_head = lax.broadcasted_iota(jnp.int32, (LANES, 1), 0) // MOBA_HEAD_DIM
        for hh in range(HEADS_PER_STEP):
            qt_scr[hh * LANES:(hh + 1) * LANES, :] = jnp.where(dim_head == hh, qt, jnp.zeros_like(qt))
        km = jnp.concatenate(
            [jnp.sum(k_ref[0, n * L:(n + 1) * L, :].astype(F32), axis=0, keepdims=True) * (1.0 / L)
             for n in range(NB)], axis=0)
        kmm = jnp.concatenate([jnp.where(head_lanes[hh], km, 0.0) for hh in range(HEADS_PER_STEP)],
                              axis=0)
        km_hi = kmm.astype(BF16)
        return km_hi, (kmm - km_hi.astype(F32)).astype(BF16)

    shift, span = [], []
    for hh in range(HEADS_PER_STEP):
        head = hp * HEADS_PER_STEP + hh
        q_sq = jnp.max(norm_ref[:, pl.ds(head, 1), :], axis=0)[:, :1]
        k_sq = jnp.max(norm_ref[:, pl.ds(MOBA_HEADS + head, 1), :], axis=0)[:, :1]
        reach = jnp.sqrt(q_sq * k_sq) * BOUND_SLACK
        shift.append(reach + bmax[hh])
        span.append(2.0 * reach + (bmax[hh] - bmin[hh]))
    safe = jnp.max(functools.reduce(jnp.maximum, span)) <= SAFE_SPAN

    def block_select(k_means, gates, n):
        i, hh = bodies[n]
        q_t = masked_q(n)
        gate = (jnp.dot(k_means[0], q_t, preferred_element_type=F32)
                + jnp.dot(k_means[1], q_t, preferred_element_type=F32))
        g = gate[hh * NB:(hh + 1) * NB]
        cnt = jnp.zeros((NB, L), jnp.int32)
        for m in range(i):
            gm = g[m:m + 1, :]
            beats = (gm > g) | ((gm == g) & (m < row))
            cnt = cnt + jnp.where(beats, 1, 0)
        return cnt < MOBA_TOPK

    def masked_q(n):
        i, hh = bodies[n]
        return qt_scr[hh * LANES:(hh + 1) * LANES, i * L:(i + 1) * L]

    def pv_stage(outs, n, p_ref):
        i, hh = bodies[n]
        nk = (i + 1) * L
        acc = jnp.dot(vt_scr[hh * VT_ROWS:(hh + 1) * VT_ROWS, 0:nk], p_ref[0:nk, :],
                      preferred_element_type=F32)
        outs[n] = acc[:MOBA_HEAD_DIM] / acc[MOBA_HEAD_DIM:MOBA_HEAD_DIM + 1]
        if hh == HEADS_PER_STEP - 1:
            pair = [outs.pop(n - HEADS_PER_STEP + 1 + h) for h in range(HEADS_PER_STEP)]
            o_ref[0, i * L:(i + 1) * L, :] = jnp.concatenate(pair, axis=0).T

    def exact_path():
        gates, masks, qms, maxes, mrows, outs = {}, {}, {}, {}, {}, {}
        _cast_slabs(cast_src, cast_dst)
        k_means = stage_operands()

        def begin_body(n):
            i, hh = bodies[n]
            qms[n] = masked_q(n)
            if i > MOBA_TOPK:
                sel = block_select(k_means, gates, n)
                masks[n] = (jnp.where(sel, cfar[hh], NEG),
                            jnp.where(sel, 0.0, NEG))

        def score_tile(n, j):
            i, hh = bodies[n]
            t = jnp.dot(k_ref[0, j * L:(j + 1) * L, :], qms[n],
                        preferred_element_type=F32)
            if j == i:
                t = t + bias_ref[hh, 0]
            elif j == i - 1:
                t = t + bias_ref[hh, 1]
                if i > MOBA_TOPK:
                    t = t + masks[n][1][j:j + 1, :]
            elif i > MOBA_TOPK:
                t = t + masks[n][0][j:j + 1, :]
            else:
                t = t + cfar[hh]
            t_scr[n % T_SLOTS][j * L:(j + 1) * L, :] = t
            tm = jnp.max(t.reshape(L // 8, 8, L), axis=0)
            maxes[n] = tm if j == 0 else jnp.maximum(maxes[n], tm)
            if j == i:
                mrows[n] = jnp.max(maxes.pop(n), axis=0, keepdims=True)
                qms.pop(n)
                masks.pop(n, None)

        def exp_tile(n, j):
            t_ref, p_ref = t_scr[n % T_SLOTS], p_scr[n % T_SLOTS]
            p_ref[j * L:(j + 1) * L, :] = jnp.exp2(t_ref[j * L:(j + 1) * L, :] - mrows[n]).astype(BF16)

        for n0 in range(min(LOOKAHEAD, nb)):
            begin_body(n0)
            for j in range(bodies[n0][0] + 1):
                score_tile(n0, j)
        for n in range(nb):
            n_exp = bodies[n][0] + 1
            ahead = n + LOOKAHEAD
            n_score = bodies[ahead][0] + 1 if ahead < nb else 0
            if n_score:
                begin_body(ahead)
            for j in range(max(n_exp, n_score)):
                if j < n_score:
                    score_tile(ahead, j)
                if j < n_exp:
                    exp_tile(n, j)
            mrows.pop(n)
            pv_stage(outs, n, p_scr[n % T_SLOTS])

    def fast_path():
        gates, outs = {}, {}
        _cast_slabs(cast_src, cast_dst)
        k_means = stage_operands()
        shifted = {(hh, d): bias_ref[hh, d] - shift[hh]
                   for hh in range(HEADS_PER_STEP) for d in range(2)}

        def prob_tiles(n):
            i, hh = bodies[n]
            qm = masked_q(n)
            s_row = shift[hh]
            if i > MOBA_TOPK:
                sel = block_select(k_means, gates, n)
                m_far = jnp.where(sel, cfar[hh], NEG) - s_row
                m_prev = jnp.where(sel, 0.0, NEG)
            for j in range(i + 1):
                e = jnp.dot(k_ref[0, j * L:(j + 1) * L, :], qm,
                            preferred_element_type=F32)
                if j == i:
                    e = e + shifted[hh, 0]
                elif j == i - 1:
                    e = e + shifted[hh, 1]
                    if i > MOBA_TOPK:
                        e = e + m_prev[j:j + 1, :]
                else:
                    e = e + (m_far[j:j + 1, :] if i > MOBA_TOPK else cfar[hh] - s_row)
                p_scr[n % P_SLOTS][j * L:(j + 1) * L, :] = jnp.exp2(e).astype(BF16)

        for n0 in range(min(FAST_LOOKAHEAD, nb)):
            prob_tiles(n0)
        for n in range(nb):
            if n + FAST_LOOKAHEAD < nb:
                prob_tiles(n + FAST_LOOKAHEAD)
            pv_stage(outs, n, p_scr[n % P_SLOTS])

    pl.when(safe)(fast_path)
    pl.when(jnp.logical_not(safe))(exact_path)


def _moba(qkv3, norms, bias_t, bias_stats, cast_weights):
    B, S, _ = qkv3.shape
    L = MOBA_BLOCK
    n_hp = MOBA_HEADS // HEADS_PER_STEP
    blk = lambda off: pl.BlockSpec((1, S, LANES), lambda hp, b: (b, 0, off + hp))
    c_in, c_out, c_shapes = _cast_specs(cast_weights, B * n_hp, lambda hp, b: hp * B + b)
    outs = pl.pallas_call(
        functools.partial(_moba_kernel, n_cast=len(cast_weights)),
        grid=(n_hp, B),
        in_specs=[
            blk(0), blk(n_hp), blk(2 * n_hp),
            pl.BlockSpec((norms.shape[0] // B,) + norms.shape[1:], lambda hp, b: (b, 0, 0)),
            pl.BlockSpec((HEADS_PER_STEP, 2, L, L), lambda hp, b: (hp, 0, 0, 0)),
            pl.BlockSpec((HEADS_PER_STEP, STAT_ROWS, LANES), lambda hp, b: (hp, 0, 0)),
        ] + c_in,
        out_specs=[pl.BlockSpec((1, S, LANES), lambda hp, b: (b, 0, hp))] + c_out,
        out_shape=[jax.ShapeDtypeStruct((B, S, D_MOBA), F32)] + c_shapes,
        scratch_shapes=[
            pltpu.VMEM((HEADS_PER_STEP * VT_ROWS, S), BF16),
            pltpu.VMEM((HEADS_PER_STEP * LANES, S), BF16),
        ] + [pltpu.VMEM((S, L), F32)] * T_SLOTS + [pltpu.VMEM((S, L), BF16)] * P_SLOTS,
        compiler_params=pltpu.CompilerParams(
            dimension_semantics=("arbitrary", "arbitrary"), vmem_limit_bytes=VMEM_LIMIT),
        name="moba",
    )(qkv3, qkv3, qkv3, norms, bias_t, bias_stats, *cast_weights)
    return outs[0], outs[1:]


def _ffn_kernel(x_ref, yb_ref, ya_ref, gb_ref, wo_ref, gf_ref, wg_ref, wu_ref, wd_ref, gl_ref,
                o_ref, *, final_norm):
    half = x_ref.shape[0] // 2
    st = {}

    def attn_norm(r):
        yb = yb_ref[r:r + half, :]
        ms = jnp.mean(yb * yb, axis=-1, keepdims=True)
        st["ybn", r] = (yb * lax.rsqrt(ms + EPS) * gb_ref[...]).astype(BF16)

    def out_proj(r):
        st["x1", r] = (x_ref[r:r + half, :]
                       + jnp.dot(st.pop(("ybn", r)), wo_ref[:D_MOBA, :], preferred_element_type=F32)
                       + jnp.dot(ya_ref[r:r + half, :], wo_ref[D_MOBA:, :],
                                 preferred_element_type=F32))

    def ffn_norm(r):
        x1 = st["x1", r]
        ms = jnp.mean(x1 * x1, axis=-1, keepdims=True)
        st["h", r] = (x1 * lax.rsqrt(ms + EPS) * gf_ref[...]).astype(BF16)

    def gate_up(r):
        h = st.pop(("h", r))
        g = jnp.dot(h, wg_ref[...], preferred_element_type=F32)
        u = jnp.dot(h, wu_ref[...], preferred_element_type=F32)
        st["a", r] = (jax.nn.silu(g) * u).astype(BF16)

    def down(r):
        x2 = st.pop(("x1", r)) + jnp.dot(st.pop(("a", r)), wd_ref[...], preferred_element_type=F32)
        if final_norm:
            ms = jnp.mean(x2 * x2, axis=-1, keepdims=True)
            x2 = x2 * lax.rsqrt(ms + EPS) * gl_ref[...]
        o_ref[r:r + half, :] = x2

    stages = [attn_norm, out_proj, ffn_norm, gate_up, down]
    for k in range(len(stages) + 1):
        if k < len(stages):
            stages[k](0)
        if k >= 1:
            stages[k - 1](half)


def _ffn(x2, yb, ya, gb, wo, gf, wg, wu, wd, gl, tm, final_norm):
    T, D = x2.shape
    d_ff = wg.shape[1]
    const = lambda shape: pl.BlockSpec(shape, lambda i: (0, 0), pipeline_mode=pl.Buffered(1))
    return pl.pallas_call(
        functools.partial(_ffn_kernel, final_norm=final_norm),
        grid=(T // tm,),
        in_specs=[
            pl.BlockSpec((tm, D), lambda i: (i, 0)),
            pl.BlockSpec((tm, D_MOBA), lambda i: (i, 0)),
            pl.BlockSpec((tm, D_GMLP), lambda i: (i, 0)),
            pl.BlockSpec((1, D_MOBA), lambda i: (0, 0)),
            const((D_MOBA + D_GMLP, D)),
            pl.BlockSpec((1, D), lambda i: (0, 0)),
            const((D, d_ff)), const((D, d_ff)), const((d_ff, D)),
            pl.BlockSpec((1, D), lambda i: (0, 0)),
        ],
        out_specs=pl.BlockSpec((tm, D), lambda i: (i, 0)),
        out_shape=jax.ShapeDtypeStruct((T, D), F32),
        compiler_params=pltpu.CompilerParams(
            dimension_semantics=("arbitrary",), vmem_limit_bytes=VMEM_LIMIT),
        name="outproj_ffn",
    )(x2, yb, ya, gb, wo, gf, wg, wu, wd, gl)


def kernel(x, rel_bias, norm_mix, w_in, gmlp_ln_g, gmlp_ln_b, w_spatial, b_spatial, out_norm_b,
           out_norm_a, w_out, norm_ffn, w_gate, w_up, w_down, norm_final):
    B, S, D = x.shape
    depth = w_in.shape[0]
    assert S % MOBA_BLOCK == 0 and S // MOBA_BLOCK > MOBA_TOPK
    far = _t5_bucket_np(np.arange(MOBA_BLOCK + 1, S + MOBA_BLOCK))
    assert (far == far[0]).all()
    far_bucket = int(far[0])

    assert depth == 1, "weight casts ride along the single layer's kernels"
    tm = 512
    T = B * S
    x2 = x.reshape(T, D)
    row = lambda v: v.reshape(1, -1)
    bias_t, bias_stats, (w_in_bf,) = _bias_tiles(rel_bias, far_bucket, [w_in[0]])
    b_sp_full = jnp.repeat(b_spatial[0].T, GMLP_GROUP_DIM, axis=1)
    qkv, ya, norms = _inproj(x2, row(norm_mix[0]), w_in_bf, row(gmlp_ln_g[0]), row(gmlp_ln_b[0]),
                             w_spatial[0], b_sp_full, row(out_norm_a[0]), 2 * tm)
    yb, (wo_bf, wg_bf, wu_bf, wd_bf) = _moba(qkv.reshape(B, S, -1), norms, bias_t, bias_stats,
                                             [w_out[0], w_gate[0], w_up[0], w_down[0]])
    out = _ffn(x2, yb.reshape(T, D_MOBA), ya, row(out_norm_b[0]), wo_bf, row(norm_ffn[0]),
               wg_bf, wu_bf, wd_bf, row(norm_final), tm, True)
    return out.reshape(B, S, D)
```

```python
import functools
import math

import numpy as np
import jax
import jax.numpy as jnp
from jax import lax
from jax.experimental import pallas as pl
from jax.experimental.pallas import tpu as pltpu

F32 = jnp.float32
BF16 = jnp.bfloat16

MOBA_HEADS = 8
MOBA_HEAD_DIM = 64
D_MOBA = MOBA_HEADS * MOBA_HEAD_DIM
GMLP_GROUPS = 8
GMLP_GROUP_DIM = 64
D_GMLP = GMLP_GROUPS * GMLP_GROUP_DIM
MOBA_BLOCK = 256
MOBA_TOPK = 3
GMLP_CHUNK = 128
NUM_BUCKETS = 32
REL_MAX_DISTANCE = 128
EPS = 1e-6
NEG = -1e9

LANES = 128
HEADS_PER_STEP = LANES // MOBA_HEAD_DIM
BF16_SUBLANES = 16
ONES_ROWS = BF16_SUBLANES
VT_ROWS = MOBA_HEAD_DIM + ONES_ROWS
VMEM_LIMIT = 56 * 1024 * 1024
T_SLOTS = 4
P_SLOTS = 4
LOOKAHEAD = 3
FAST_LOOKAHEAD = 3

LOG2E = math.log2(math.e)
Q_SCALE = MOBA_HEAD_DIM ** -0.5 * LOG2E

BOUND_SLACK = 1.0 + 2.0 ** -6
SAFE_SPAN = 100.0


def _t5_bucket_np(dist):
    n = np.maximum(dist, 0)
    max_exact = NUM_BUCKETS // 2
    nf = np.maximum(n, max_exact).astype(np.float32)
    ratio = np.log(nf / np.float32(max_exact)) / np.float32(math.log(REL_MAX_DISTANCE / max_exact))
    large = max_exact + (ratio * np.float32(NUM_BUCKETS - max_exact)).astype(np.int32)
    large = np.minimum(large, NUM_BUCKETS - 1)
    return np.where(n < max_exact, n, large).astype(np.int32)


def _cast_specs(weights, n_steps, step_of):
    in_specs, out_specs, out_shapes = [], [], []
    for w in weights:
        rows, cols = w.shape
        period = next(p for p in range(1, n_steps + 1)
                      if n_steps % p == 0 and (rows * p) % (n_steps * BF16_SUBLANES) == 0)
        spec = pl.BlockSpec((rows * period // n_steps, cols),
                            functools.partial(lambda p, *ids: (step_of(*ids) // p, 0), period))
        in_specs.append(spec)
        out_specs.append(spec)
        out_shapes.append(jax.ShapeDtypeStruct(w.shape, BF16))
    return in_specs, out_specs, out_shapes


def _cast_slabs(src_refs, dst_refs):
    for src, dst in zip(src_refs, dst_refs):
        dst[...] = src[...].astype(BF16)


BIAS_FAR, BIAS_MAX, BIAS_MIN = 0, 1, 2
STAT_ROWS = 8


def _bias_kernel(bucket_ref, rel_ref, *refs, far_bucket):
    n_cast = (len(refs) - 2) // 2
    out_ref, stat_ref = refs[n_cast], refs[n_cast + 1]
    _cast_slabs(refs[:n_cast], refs[n_cast + 2:])
    h = pl.program_id(0)
    L = MOBA_BLOCK
    vals = [rel_ref[b, h] * LOG2E for b in range(NUM_BUCKETS)]
    key = lax.broadcasted_iota(jnp.int32, (L, L), 0)
    qry = lax.broadcasted_iota(jnp.int32, (L, L), 1)
    for delta in range(2):
        bucket = bucket_ref[delta]
        acc = jnp.zeros((L, L), F32)
        for b in range(NUM_BUCKETS):
            acc = jnp.where(bucket == b, vals[b], acc)
        if delta == 0:
            acc = jnp.where(key > qry, NEG, acc)
        out_ref[0, delta] = acc
    stats = {BIAS_FAR: vals[far_bucket], BIAS_MAX: functools.reduce(jnp.maximum, vals),
             BIAS_MIN: functools.reduce(jnp.minimum, vals)}
    for r in range(STAT_ROWS):
        stat_ref[0, r:r + 1, :] = jnp.full((1, LANES), stats.get(r, 0.0), F32)


def _bias_tiles(rel_bias, far_bucket, cast_weights):
    L = MOBA_BLOCK
    kk = np.arange(L)[:, None]
    qq = np.arange(L)[None, :]
    bucket = np.stack([_t5_bucket_np(d * L + qq - kk) for d in range(2)]).astype(np.int32)
    c_in, c_out, c_shapes = _cast_specs(cast_weights, MOBA_HEADS, lambda h: h)
    outs = pl.pallas_call(
        functools.partial(_bias_kernel, far_bucket=far_bucket),
        grid=(MOBA_HEADS,),
        in_specs=[
            pl.BlockSpec((2, L, L), lambda h: (0, 0, 0)),
            pl.BlockSpec(memory_space=pltpu.SMEM),
        ] + c_in,
        out_specs=[pl.BlockSpec((1, 2, L, L), lambda h: (h, 0, 0, 0)),
                   pl.BlockSpec((1, STAT_ROWS, LANES), lambda h: (h, 0, 0))] + c_out,
        out_shape=[jax.ShapeDtypeStruct((MOBA_HEADS, 2, L, L), F32),
                   jax.ShapeDtypeStruct((MOBA_HEADS, STAT_ROWS, LANES), F32)] + c_shapes,
        compiler_params=pltpu.CompilerParams(
            dimension_semantics=("arbitrary",), vmem_limit_bytes=VMEM_LIMIT),
        name="bias_tiles",
    )(jnp.asarray(bucket), rel_bias, *cast_weights)
    return outs[0], outs[1], outs[2:]


def _inproj_kernel(x_ref, g_ref, w_ref, lng_ref, lnb_ref, wsp_ref, bsp_ref, gn_ref,
                   qkv_ref, ya_ref, norm_ref, uz_scr):
    C = GMLP_CHUNK
    tm = x_ref.shape[0]
    n_qkv = 3 * D_MOBA
    x = x_ref[...]
    ms = jnp.mean(x * x, axis=-1, keepdims=True)
    h = (x * lax.rsqrt(ms + EPS) * g_ref[...]).astype(BF16)
    uz_scr[...] = jnp.dot(h, w_ref[:, n_qkv:], preferred_element_type=F32)

    t_ix = lax.broadcasted_iota(jnp.int32, (C, C), 0)
    s_ix = lax.broadcasted_iota(jnp.int32, (C, C), 1)
    w_sp = [jnp.where(t_ix >= s_ix, wsp_ref[g], 0.0).astype(BF16) for g in range(GMLP_GROUPS)]
    first = lax.broadcasted_iota(jnp.int32, (1, LANES), 1) < GMLP_GROUP_DIM

    def qkv_cols(c0, c1):
        acc = jnp.dot(h, w_ref[:, c0:c1], preferred_element_type=F32)
        if c1 <= D_MOBA:
            acc = acc * Q_SCALE
        qkv_ref[:, c0:c1] = acc.astype(BF16)
        if c1 <= 2 * D_MOBA:
            sq = acc * acc
            for c in range(c0, c1, MOBA_HEAD_DIM):
                lanes = (c - c0) // LANES * LANES
                own = first if (c - c0) % LANES == 0 else jnp.logical_not(first)
                rows_sq = jnp.sum(jnp.where(own, sq[:, lanes:lanes + LANES], 0.0), axis=-1, keepdims=True)
                top = jnp.max(rows_sq, axis=0, keepdims=True)
                norm_ref[0, c // MOBA_HEAD_DIM:c // MOBA_HEAD_DIM + 1, :] = jnp.broadcast_to(top, (1, LANES))

    def gmlp_rows(c):
        rows = slice(c * C, (c + 1) * C)
        z = jax.nn.gelu(uz_scr[rows, D_GMLP:])
        mu = jnp.mean(z, axis=-1, keepdims=True)
        zc = z - mu
        var = jnp.mean(zc * zc, axis=-1, keepdims=True)
        zn = (zc * lax.rsqrt(var + EPS) * lng_ref[...] + lnb_ref[...]).astype(BF16)
        cols = []
        for gp in range(D_GMLP // LANES):
            zp = zn[:, gp * LANES:(gp + 1) * LANES]
            r0 = jnp.dot(w_sp[2 * gp], zp, preferred_element_type=F32)
            r1 = jnp.dot(w_sp[2 * gp + 1], zp, preferred_element_type=F32)
            cols.append(jnp.where(first, r0, r1))
        s = jnp.concatenate(cols, axis=1) + bsp_ref[...]
        y = jax.nn.gelu(uz_scr[rows, :D_GMLP]) * s
        ms_y = jnp.mean(y * y, axis=-1, keepdims=True)
        ya_ref[rows, :] = (y * lax.rsqrt(ms_y + EPS) * gn_ref[...]).astype(BF16)

    col_blocks = [(c, c + 2 * LANES) for c in range(0, n_qkv, 2 * LANES)]
    n_chunks = tm // C
    for step in range(max(len(col_blocks), n_chunks)):
        if step < len(col_blocks):
            qkv_cols(*col_blocks[step])
        if step < n_chunks:
            gmlp_rows(step)


def _inproj(x2, g, w_bf, ln_g, ln_b, w_sp, b_sp_full, gn, tm):
    T, D = x2.shape
    n_all = w_bf.shape[1]
    n_qkv = 3 * D_MOBA
    C = GMLP_CHUNK
    vec = pl.BlockSpec((1, D_GMLP), lambda i: (0, 0))
    return pl.pallas_call(
        _inproj_kernel,
        grid=(T // tm,),
        in_specs=[
            pl.BlockSpec((tm, D), lambda i: (i, 0)),
            pl.BlockSpec((1, D), lambda i: (0, 0)),
            pl.BlockSpec((D, n_all), lambda i: (0, 0), pipeline_mode=pl.Buffered(1)),
            vec, vec,
            pl.BlockSpec((GMLP_GROUPS, C, C), lambda i: (0, 0, 0)),
            pl.BlockSpec((C, D_GMLP), lambda i: (0, 0)),
            vec,
        ],
        out_specs=[
            pl.BlockSpec((tm, n_qkv), lambda i: (i, 0)),
            pl.BlockSpec((tm, D_GMLP), lambda i: (i, 0)),
            pl.BlockSpec((1, 2 * MOBA_HEADS, LANES), lambda i: (i, 0, 0)),
        ],
        out_shape=[
            jax.ShapeDtypeStruct((T, n_qkv), BF16),
            jax.ShapeDtypeStruct((T, D_GMLP), BF16),
            jax.ShapeDtypeStruct((T // tm, 2 * MOBA_HEADS, LANES), F32),
        ],
        scratch_shapes=[pltpu.VMEM((tm, n_all - n_qkv), F32)],
        compiler_params=pltpu.CompilerParams(
            dimension_semantics=("arbitrary",), vmem_limit_bytes=VMEM_LIMIT),
        name="inproj_gmlp",
    )(x2, g, w_bf, ln_g, ln_b, w_sp, b_sp_full, gn)


def _moba_kernel(q_ref, k_ref, v_ref, norm_ref, bias_ref, stat_ref, *refs, n_cast):
    cast_src, o_ref, cast_dst = refs[:n_cast], refs[n_cast], refs[n_cast + 1:2 * n_cast + 1]
    vt_scr, qt_scr, slots = refs[2 * n_cast + 1], refs[2 * n_cast + 2], refs[2 * n_cast + 3:]
    t_scr, p_scr = slots[:T_SLOTS], slots[T_SLOTS:]
    L = MOBA_BLOCK
    S = q_ref.shape[1]
    NB = S // L
    hp = pl.program_id(0)
    lane = lax.broadcasted_iota(jnp.int32, (1, LANES), 1)
    head_lanes = [(lane >= hh * MOBA_HEAD_DIM) & (lane < (hh + 1) * MOBA_HEAD_DIM)
                  for hh in range(HEADS_PER_STEP)]
    row = lax.broadcasted_iota(jnp.int32, (NB, L), 0)
    bodies = [(i, hh) for i in range(NB) for hh in range(HEADS_PER_STEP)]
    nb = len(bodies)

    cfar = [stat_ref[hh, BIAS_FAR:BIAS_FAR + 1, :1] for hh in range(HEADS_PER_STEP)]
    bmax = [stat_ref[hh, BIAS_MAX:BIAS_MAX + 1, :1] for hh in range(HEADS_PER_STEP)]
    bmin = [stat_ref[hh, BIAS_MIN:BIAS_MIN + 1, :1] for hh in range(HEADS_PER_STEP)]

    def stage_operands():
        vt = v_ref[0].astype(F32).T.astype(BF16)
        for hh in range(HEADS_PER_STEP):
            vt_scr[hh * VT_ROWS:hh * VT_ROWS + MOBA_HEAD_DIM, :] = (
                vt[hh * MOBA_HEAD_DIM:(hh + 1) * MOBA_HEAD_DIM])
            vt_scr[hh * VT_ROWS + MOBA_HEAD_DIM:(hh + 1) * VT_ROWS, :] = jnp.ones((ONES_ROWS, S), BF16)
        qt = q_ref[0].astype(F32).T.astype(BF16)
        dim_head = lax.broadcasted_iota(jnp.int32, (LANES, 1), 0) // MOBA_HEAD_DIM
        for hh in range(HEADS_PER_STEP):
            qt_scr[hh * LANES:(hh + 1) * LANES, :] = jnp.where(dim_head == hh, qt, jnp.zeros_like(qt))
        km = jnp.concatenate(
            [jnp.sum(k_ref[0, n * L:(n + 1) * L, :].astype(F32), axis=0, keepdims=True) * (1.0 / L)
             for n in range(NB)], axis=0)
        kmm = jnp.concatenate([jnp.where(head_lanes[hh], km, 0.0) for hh in range(HEADS_PER_STEP)],
                              axis=0)
        km_hi = kmm.astype(BF16)
        return km_hi, (kmm - km_hi.astype(F32)).astype(BF16)

    shift, span = [], []
    for hh in range(HEADS_PER_STEP):
        head = hp * HEADS_PER_STEP + hh
        q_sq = jnp.max(norm_ref[:, pl.ds(head, 1), :], axis=0)[:, :1]
        k_sq = jnp.max(norm_ref[:, pl.ds(MOBA_HEADS + head, 1), :], axis=0)[:, :1]
        reach = jnp.sqrt(q_sq * k_sq) * BOUND_SLACK
        shift.append(reach + bmax[hh])
        span.append(2.0 * reach + (bmax[hh] - bmin[hh]))
    safe = jnp.max(functools.reduce(jnp.maximum, span)) <= SAFE_SPAN

    def block_select(k_means, gates, n):
        i, hh = bodies[n]
        q_t = masked_q(n)
        gate = (jnp.dot(k_means[0], q_t, preferred_element_type=F32)
                + jnp.dot(k_means[1], q_t, preferred_element_type=F32))
        g = gate[hh * NB:(hh + 1) * NB]
        cnt = jnp.zeros((NB, L), jnp.int32)
        for m in range(i):
            gm = g[m:m + 1, :]
            beats = (gm > g) | ((gm == g) & (m < row))
            cnt = cnt + jnp.where(beats, 1, 0)
        return cnt < MOBA_TOPK

    def masked_q(n):
        i, hh = bodies[n]
        return qt_scr[hh * LANES:(hh + 1) * LANES, i * L:(i + 1) * L]

    def pv_stage(outs, n, p_ref):
        i, hh = bodies[n]
        nk = (i + 1) * L
        acc = jnp.dot(vt_scr[hh * VT_ROWS:(hh + 1) * VT_ROWS, 0:nk], p_ref[0:nk, :],
                      preferred_element_type=F32)
        outs[n] = acc[:MOBA_HEAD_DIM] / acc[MOBA_HEAD_DIM:MOBA_HEAD_DIM + 1]
        if hh == HEADS_PER_STEP - 1:
            pair = [outs.pop(n - HEADS_PER_STEP + 1 + h) for h in range(HEADS_PER_STEP)]
            o_ref[0, i * L:(i + 1) * L, :] = jnp.concatenate(pair, axis=0).T

    def exact_path():
        gates, masks, qms, maxes, mrows, outs = {}, {}, {}, {}, {}, {}
        _cast_slabs(cast_src, cast_dst)
        k_means = stage_operands()

        def begin_body(n):
            i, hh = bodies[n]
            qms[n] = masked_q(n)
            if i > MOBA_TOPK:
                sel = block_select(k_means, gates, n)
                masks[n] = (jnp.where(sel, cfar[hh], NEG),
                            jnp.where(sel, 0.0, NEG))

        def score_tile(n, j):
            i, hh = bodies[n]
            t = jnp.dot(k_ref[0, j * L:(j + 1) * L, :], qms[n],
                        preferred_element_type=F32)
            if j == i:
                t = t + bias_ref[hh, 0]
            elif j == i - 1:
                t = t + bias_ref[hh, 1]
                if i > MOBA_TOPK:
                    t = t + masks[n][1][j:j + 1, :]
            elif i > MOBA_TOPK:
                t = t + masks[n][0][j:j + 1, :]
            else:
                t = t + cfar[hh]
            t_scr[n % T_SLOTS][j * L:(j + 1) * L, :] = t
            tm = jnp.max(t.reshape(L // 8, 8, L), axis=0)
            maxes[n] = tm if j == 0 else jnp.maximum(maxes[n], tm)
            if j == i:
                mrows[n] = jnp.max(maxes.pop(n), axis=0, keepdims=True)
                qms.pop(n)
                masks.pop(n, None)

        def exp_tile(n, j):
            t_ref, p_ref = t_scr[n % T_SLOTS], p_scr[n % T_SLOTS]
            p_ref[j * L:(j + 1) * L, :] = jnp.exp2(t_ref[j * L:(j + 1) * L, :] - mrows[n]).astype(BF16)

        for n0 in range(min(LOOKAHEAD, nb)):
            begin_body(n0)
            for j in range(bodies[n0][0] + 1):
                score_tile(n0, j)
        for n in range(nb):
            n_exp = bodies[n][0] + 1
            ahead = n + LOOKAHEAD
            n_score = bodies[ahead][0] + 1 if ahead < nb else 0
            if n_score:
                begin_body(ahead)
            for j in range(max(n_exp, n_score)):
                if j < n_score:
                    score_tile(ahead, j)
                if j < n_exp:
                    exp_tile(n, j)
            mrows.pop(n)
            pv_stage(outs, n, p_scr[n % T_SLOTS])

    def fast_path():
        gates, outs = {}, {}
        _cast_slabs(cast_src, cast_dst)
        k_means = stage_operands()
        shifted = {(hh, d): bias_ref[hh, d] - shift[hh]
                   for hh in range(HEADS_PER_STEP) for d in range(2)}

        def prob_tiles(n):
            i, hh = bodies[n]
            qm = masked_q(n)
            s_row = shift[hh]
            if i > MOBA_TOPK:
                sel = block_select(k_means, gates, n)
                m_far = jnp.where(sel, cfar[hh], NEG) - s_row
                m_prev = jnp.where(sel, 0.0, NEG)
            scores = jnp.dot(k_ref[0, 0:(i + 1) * L, :], qm, preferred_element_type=F32)
            for j in range(i + 1):
                e = scores[j * L:(j + 1) * L]
                if j == i:
                    e = e + shifted[hh, 0]
                elif j == i - 1:
                    e = e + shifted[hh, 1]
                    if i > MOBA_TOPK:
                        e = e + m_prev[j:j + 1, :]
                else:
                    e = e + (m_far[j:j + 1, :] if i > MOBA_TOPK else cfar[hh] - s_row)
                p_scr[n % P_SLOTS][j * L:(j + 1) * L, :] = jnp.exp2(e).astype(BF16)

        for n0 in range(min(FAST_LOOKAHEAD, nb)):
            prob_tiles(n0)
        for n in range(nb):
            if n + FAST_LOOKAHEAD < nb:
                prob_tiles(n + FAST_LOOKAHEAD)
            pv_stage(outs, n, p_scr[n % P_SLOTS])

    pl.when(safe)(fast_path)
    pl.when(jnp.logical_not(safe))(exact_path)


def _moba(qkv3, norms, bias_t, bias_stats, cast_weights):
    B, S, _ = qkv3.shape
    L = MOBA_BLOCK
    n_hp = MOBA_HEADS // HEADS_PER_STEP
    blk = lambda off: pl.BlockSpec((1, S, LANES), lambda hp, b: (b, 0, off + hp))
    c_in, c_out, c_shapes = _cast_specs(cast_weights, B * n_hp, lambda hp, b: hp * B + b)
    outs = pl.pallas_call(
        functools.partial(_moba_kernel, n_cast=len(cast_weights)),
        grid=(n_hp, B),
        in_specs=[
            blk(0), blk(n_hp), blk(2 * n_hp),
            pl.BlockSpec((norms.shape[0] // B,) + norms.shape[1:], lambda hp, b: (b, 0, 0)),
            pl.BlockSpec((HEADS_PER_STEP, 2, L, L), lambda hp, b: (hp, 0, 0, 0)),
            pl.BlockSpec((HEADS_PER_STEP, STAT_ROWS, LANES), lambda hp, b: (hp, 0, 0)),
        ] + c_in,
        out_specs=[pl.BlockSpec((1, S, LANES), lambda hp, b: (b, 0, hp))] + c_out,
        out_shape=[jax.ShapeDtypeStruct((B, S, D_MOBA), F32)] + c_shapes,
        scratch_shapes=[
            pltpu.VMEM((HEADS_PER_STEP * VT_ROWS, S), BF16),
            pltpu.VMEM((HEADS_PER_STEP * LANES, S), BF16),
        ] + [pltpu.VMEM((S, L), F32)] * T_SLOTS + [pltpu.VMEM((S, L), BF16)] * P_SLOTS,
        compiler_params=pltpu.CompilerParams(
            dimension_semantics=("arbitrary", "arbitrary"), vmem_limit_bytes=VMEM_LIMIT),
        name="moba",
    )(qkv3, qkv3, qkv3, norms, bias_t, bias_stats, *cast_weights)
    return outs[0], outs[1:]


def _ffn_kernel(x_ref, yb_ref, ya_ref, gb_ref, wo_ref, gf_ref, wg_ref, wu_ref, wd_ref, gl_ref,
                o_ref, *, final_norm):
    half = x_ref.shape[0] // 2
    st = {}

    def attn_norm(r):
        yb = yb_ref[r:r + half, :]
        ms = jnp.mean(yb * yb, axis=-1, keepdims=True)
        st["ybn", r] = (yb * lax.rsqrt(ms + EPS) * gb_ref[...]).astype(BF16)

    def out_proj(r):
        st["x1", r] = (x_ref[r:r + half, :]
                       + jnp.dot(st.pop(("ybn", r)), wo_ref[:D_MOBA, :], preferred_element_type=F32)
                       + jnp.dot(ya_ref[r:r + half, :], wo_ref[D_MOBA:, :],
                                 preferred_element_type=F32))

    def ffn_norm(r):
        x1 = st["x1", r]
        ms = jnp.mean(x1 * x1, axis=-1, keepdims=True)
        st["h", r] = (x1 * lax.rsqrt(ms + EPS) * gf_ref[...]).astype(BF16)

    def gate_up(r):
        h = st.pop(("h", r))
        g = jnp.dot(h, wg_ref[...], preferred_element_type=F32)
        u = jnp.dot(h, wu_ref[...], preferred_element_type=F32)
        st["a", r] = (jax.nn.silu(g) * u).astype(BF16)

    def down(r):
        x2 = st.pop(("x1", r)) + jnp.dot(st.pop(("a", r)), wd_ref[...], preferred_element_type=F32)
        if final_norm:
            ms = jnp.mean(x2 * x2, axis=-1, keepdims=True)
            x2 = x2 * lax.rsqrt(ms + EPS) * gl_ref[...]
        o_ref[r:r + half, :] = x2

    stages = [attn_norm, out_proj, ffn_norm, gate_up, down]
    for k in range(len(stages) + 1):
        if k < len(stages):
            stages[k](0)
        if k >= 1:
            stages[k - 1](half)


def _ffn(x2, yb, ya, gb, wo, gf, wg, wu, wd, gl, tm, final_norm):
    T, D = x2.shape
    d_ff = wg.shape[1]
    const = lambda shape: pl.BlockSpec(shape, lambda i: (0, 0), pipeline_mode=pl.Buffered(1))
    return pl.pallas_call(
        functools.partial(_ffn_kernel, final_norm=final_norm),
        grid=(T // tm,),
        in_specs=[
            pl.BlockSpec((tm, D), lambda i: (i, 0)),
            pl.BlockSpec((tm, D_MOBA), lambda i: (i, 0)),
            pl.BlockSpec((tm, D_GMLP), lambda i: (i, 0)),
            pl.BlockSpec((1, D_MOBA), lambda i: (0, 0)),
            const((D_MOBA + D_GMLP, D)),
            pl.BlockSpec((1, D), lambda i: (0, 0)),
            const((D, d_ff)), const((D, d_ff)), const((d_ff, D)),
            pl.BlockSpec((1, D), lambda i: (0, 0)),
        ],
        out_specs=pl.BlockSpec((tm, D), lambda i: (i, 0)),
        out_shape=jax.ShapeDtypeStruct((T, D), F32),
        compiler_params=pltpu.CompilerParams(
            dimension_semantics=("arbitrary",), vmem_limit_bytes=VMEM_LIMIT),
        name="outproj_ffn",
    )(x2, yb, ya, gb, wo, gf, wg, wu, wd, gl)


def kernel(x, rel_bias, norm_mix, w_in, gmlp_ln_g, gmlp_ln_b, w_spatial, b_spatial, out_norm_b,
           out_norm_a, w_out, norm_ffn, w_gate, w_up, w_down, norm_final):
    B, S, D = x.shape
    depth = w_in.shape[0]
    assert S % MOBA_BLOCK == 0 and S // MOBA_BLOCK > MOBA_TOPK
    far = _t5_bucket_np(np.arange(MOBA_BLOCK + 1, S + MOBA_BLOCK))
    assert (far == far[0]).all()
    far_bucket = int(far[0])

    assert depth == 1, "weight casts ride along the single layer's kernels"
    tm = 512
    T = B * S
    x2 = x.reshape(T, D)
    row = lambda v: v.reshape(1, -1)
    bias_t, bias_stats, (w_in_bf,) = _bias_tiles(rel_bias, far_bucket, [w_in[0]])
    b_sp_full = jnp.repeat(b_spatial[0].T, GMLP_GROUP_DIM, axis=1)
    qkv, ya, norms = _inproj(x2, row(norm_mix[0]), w_in_bf, row(gmlp_ln_g[0]), row(gmlp_ln_b[0]),
                             w_spatial[0], b_sp_full, row(out_norm_a[0]), 2 * tm)
    yb, (wo_bf, wg_bf, wu_bf, wd_bf) = _moba(qkv.reshape(B, S, -1), norms, bias_t, bias_stats,
                                             [w_out[0], w_gate[0], w_up[0], w_down[0]])
    out = _ffn(x2, yb.reshape(T, D_MOBA), ya, row(out_norm_b[0]), wo_bf, row(norm_ffn[0]),
               wg_bf, wu_bf, wd_bf, row(norm_final), tm, True)
    return out.reshape(B, S, D)
```

```python
import functools
import math

import numpy as np
import jax
import jax.numpy as jnp
from jax import lax
from jax.experimental import pallas as pl
from jax.experimental.pallas import tpu as pltpu

F32 = jnp.float32
BF16 = jnp.bfloat16

MOBA_HEADS = 8
MOBA_HEAD_DIM = 64
D_MOBA = MOBA_HEADS * MOBA_HEAD_DIM
GMLP_GROUPS = 8
GMLP_GROUP_DIM = 64
D_GMLP = GMLP_GROUPS * GMLP_GROUP_DIM
MOBA_BLOCK = 256
MOBA_TOPK = 3
GMLP_CHUNK = 128
NUM_BUCKETS = 32
REL_MAX_DISTANCE = 128
EPS = 1e-6
NEG = -1e9

LANES = 128
HEADS_PER_STEP = LANES // MOBA_HEAD_DIM
BF16_SUBLANES = 16
ONES_ROWS = BF16_SUBLANES
VT_ROWS = MOBA_HEAD_DIM + ONES_ROWS
VMEM_LIMIT = 56 * 1024 * 1024
T_SLOTS = 4
P_SLOTS = 4
LOOKAHEAD = 3
FAST_LOOKAHEAD = 3

LOG2E = math.log2(math.e)
Q_SCALE = MOBA_HEAD_DIM ** -0.5 * LOG2E

BOUND_SLACK = 1.0 + 2.0 ** -6
SAFE_SPAN = 100.0


def _t5_bucket_np(dist):
    n = np.maximum(dist, 0)
    max_exact = NUM_BUCKETS // 2
    nf = np.maximum(n, max_exact).astype(np.float32)
    ratio = np.log(nf / np.float32(max_exact)) / np.float32(math.log(REL_MAX_DISTANCE / max_exact))
    large = max_exact + (ratio * np.float32(NUM_BUCKETS - max_exact)).astype(np.int32)
    large = np.minimum(large, NUM_BUCKETS - 1)
    return np.where(n < max_exact, n, large).astype(np.int32)


def _cast_specs(weights, n_steps, step_of):
    in_specs, out_specs, out_shapes = [], [], []
    for w in weights:
        rows, cols = w.shape
        period = next(p for p in range(1, n_steps + 1)
                      if n_steps % p == 0 and (rows * p) % (n_steps * BF16_SUBLANES) == 0)
        spec = pl.BlockSpec((rows * period // n_steps, cols),
                            functools.partial(lambda p, *ids: (step_of(*ids) // p, 0), period))
        in_specs.append(spec)
        out_specs.append(spec)
        out_shapes.append(jax.ShapeDtypeStruct(w.shape, BF16))
    return in_specs, out_specs, out_shapes


def _cast_slabs(src_refs, dst_refs):
    for src, dst in zip(src_refs, dst_refs):
        dst[...] = src[...].astype(BF16)


BIAS_FAR, BIAS_MAX, BIAS_MIN = 0, 1, 2
STAT_ROWS = 8
F32_SUBLANES = 8


def _bias_kernel(bucket_ref, rel_ref, *refs, far_bucket):
    n_cast = (len(refs) - 2) // 2
    out_ref, stat_ref = refs[n_cast], refs[n_cast + 1]
    _cast_slabs(refs[:n_cast], refs[n_cast + 2:])
    h = pl.program_id(0)
    L = out_ref.shape[2]
    span = bucket_ref.shape[1]
    vals = [rel_ref[b, h] * LOG2E for b in range(NUM_BUCKETS)]
    bucket = bucket_ref[...]
    prof = jnp.zeros((1, span), F32)
    for b in range(NUM_BUCKETS):
        prof = jnp.where(bucket == b, vals[b], prof)
    lane = lax.broadcasted_iota(jnp.int32, (1, span), 1)
    own = jnp.broadcast_to(jnp.where(lane < L, prof, NEG), (F32_SUBLANES, span))
    prev = jnp.broadcast_to(prof, (F32_SUBLANES, span))
    for g in range(L // F32_SUBLANES):
        k0 = g * F32_SUBLANES
        rows = slice(k0, k0 + F32_SUBLANES)
        out_ref[0, 0, rows, :] = pltpu.roll(own, k0, 1, stride=1, stride_axis=0)[:, :L]
        out_ref[0, 1, rows, :] = pltpu.roll(prev, (k0 + L) % span, 1, stride=1, stride_axis=0)[:, :L]
    stats = {BIAS_FAR: vals[far_bucket], BIAS_MAX: functools.reduce(jnp.maximum, vals),
             BIAS_MIN: functools.reduce(jnp.minimum, vals)}
    for r in range(STAT_ROWS):
        stat_ref[0, r:r + 1, :] = jnp.full((1, LANES), stats.get(r, 0.0), F32)


def _bias_tiles(rel_bias, far_bucket, cast_weights):
    L = MOBA_BLOCK
    bucket = _t5_bucket_np(np.arange(2 * L))[None, :]
    c_in, c_out, c_shapes = _cast_specs(cast_weights, MOBA_HEADS, lambda h: h)
    outs = pl.pallas_call(
        functools.partial(_bias_kernel, far_bucket=far_bucket),
        grid=(MOBA_HEADS,),
        in_specs=[
            pl.BlockSpec((1, 2 * L), lambda h: (0, 0)),
            pl.BlockSpec(memory_space=pltpu.SMEM),
        ] + c_in,
        out_specs=[pl.BlockSpec((1, 2, L, L), lambda h: (h, 0, 0, 0)),
                   pl.BlockSpec((1, STAT_ROWS, LANES), lambda h: (h, 0, 0))] + c_out,
        out_shape=[jax.ShapeDtypeStruct((MOBA_HEADS, 2, L, L), F32),
                   jax.ShapeDtypeStruct((MOBA_HEADS, STAT_ROWS, LANES), F32)] + c_shapes,
        compiler_params=pltpu.CompilerParams(
            dimension_semantics=("arbitrary",), vmem_limit_bytes=VMEM_LIMIT),
        name="bias_tiles",
    )(jnp.asarray(bucket), rel_bias, *cast_weights)
    return outs[0], outs[1], outs[2:]


def _inproj_kernel(x_ref, g_ref, w_ref, lng_ref, lnb_ref, wsp_ref, bsp_ref, gn_ref,
                   qkv_ref, ya_ref, norm_ref, uz_scr):
    C = GMLP_CHUNK
    tm = x_ref.shape[0]
    n_qkv = 3 * D_MOBA
    x = x_ref[...]
    ms = jnp.mean(x * x, axis=-1, keepdims=True)
    h = (x * lax.rsqrt(ms + EPS) * g_ref[...]).astype(BF16)
    uz_scr[...] = jnp.dot(h, w_ref[:, n_qkv:], preferred_element_type=F32)

    t_ix = lax.broadcasted_iota(jnp.int32, (C, C), 0)
    s_ix = lax.broadcasted_iota(jnp.int32, (C, C), 1)
    w_sp = [jnp.where(t_ix >= s_ix, wsp_ref[g], 0.0).astype(BF16) for g in range(GMLP_GROUPS)]
    first = lax.broadcasted_iota(jnp.int32, (1, LANES), 1) < GMLP_GROUP_DIM

    def qkv_cols(c0, c1):
        acc = jnp.dot(h, w_ref[:, c0:c1], preferred_element_type=F32)
        if c1 <= D_MOBA:
            acc = acc * Q_SCALE
        qkv_ref[:, c0:c1] = acc.astype(BF16)
        if c1 <= 2 * D_MOBA:
            sq = acc * acc
            for c in range(c0, c1, MOBA_HEAD_DIM):
                lanes = (c - c0) // LANES * LANES
                own = first if (c - c0) % LANES == 0 else jnp.logical_not(first)
                rows_sq = jnp.sum(jnp.where(own, sq[:, lanes:lanes + LANES], 0.0), axis=-1, keepdims=True)
                top = jnp.max(rows_sq, axis=0, keepdims=True)
                norm_ref[0, c // MOBA_HEAD_DIM:c // MOBA_HEAD_DIM + 1, :] = jnp.broadcast_to(top, (1, LANES))

    def gmlp_rows(c):
        rows = slice(c * C, (c + 1) * C)
        z = jax.nn.gelu(uz_scr[rows, D_GMLP:])
        mu = jnp.mean(z, axis=-1, keepdims=True)
        zc = z - mu
        var = jnp.mean(zc * zc, axis=-1, keepdims=True)
        zn = (zc * lax.rsqrt(var + EPS) * lng_ref[...] + lnb_ref[...]).astype(BF16)
        cols = []
        for gp in range(D_GMLP // LANES):
            zp = zn[:, gp * LANES:(gp + 1) * LANES]
            r0 = jnp.dot(w_sp[2 * gp], zp, preferred_element_type=F32)
            r1 = jnp.dot(w_sp[2 * gp + 1], zp, preferred_element_type=F32)
            cols.append(jnp.where(first, r0, r1))
        s = jnp.concatenate(cols, axis=1) + bsp_ref[...]
        y = jax.nn.gelu(uz_scr[rows, :D_GMLP]) * s
        ms_y = jnp.mean(y * y, axis=-1, keepdims=True)
        ya_ref[rows, :] = (y * lax.rsqrt(ms_y + EPS) * gn_ref[...]).astype(BF16)

    col_blocks = [(c, c + 2 * LANES) for c in range(0, n_qkv, 2 * LANES)]
    n_chunks = tm // C
    for step in range(max(len(col_blocks), n_chunks)):
        if step < len(col_blocks):
            qkv_cols(*col_blocks[step])
        if step < n_chunks:
            gmlp_rows(step)


def _inproj(x2, g, w_bf, ln_g, ln_b, w_sp, b_sp_full, gn, tm):
    T, D = x2.shape
    n_all = w_bf.shape[1]
    n_qkv = 3 * D_MOBA
    C = GMLP_CHUNK
    vec = pl.BlockSpec((1, D_GMLP), lambda i: (0, 0))
    return pl.pallas_call(
        _inproj_kernel,
        grid=(T // tm,),
        in_specs=[
            pl.BlockSpec((tm, D), lambda i: (i, 0)),
            pl.BlockSpec((1, D), lambda i: (0, 0)),
            pl.BlockSpec((D, n_all), lambda i: (0, 0), pipeline_mode=pl.Buffered(1)),
            vec, vec,
            pl.BlockSpec((GMLP_GROUPS, C, C), lambda i: (0, 0, 0)),
            pl.BlockSpec((C, D_GMLP), lambda i: (0, 0)),
            vec,
        ],
        out_specs=[
            pl.BlockSpec((tm, n_qkv), lambda i: (i, 0)),
            pl.BlockSpec((tm, D_GMLP), lambda i: (i, 0)),
            pl.BlockSpec((1, 2 * MOBA_HEADS, LANES), lambda i: (i, 0, 0)),
        ],
        out_shape=[
            jax.ShapeDtypeStruct((T, n_qkv), BF16),
            jax.ShapeDtypeStruct((T, D_GMLP), BF16),
            jax.ShapeDtypeStruct((T // tm, 2 * MOBA_HEADS, LANES), F32),
        ],
        scratch_shapes=[pltpu.VMEM((tm, n_all - n_qkv), F32)],
        compiler_params=pltpu.CompilerParams(
            dimension_semantics=("arbitrary",), vmem_limit_bytes=VMEM_LIMIT),
        name="inproj_gmlp",
    )(x2, g, w_bf, ln_g, ln_b, w_sp, b_sp_full, gn)


def _moba_kernel(q_ref, k_ref, v_ref, norm_ref, bias_ref, stat_ref, *refs, n_cast):
    cast_src, o_ref, cast_dst = refs[:n_cast], refs[n_cast], refs[n_cast + 1:2 * n_cast + 1]
    vt_scr, qt_scr, slots = refs[2 * n_cast + 1], refs[2 * n_cast + 2], refs[2 * n_cast + 3:]
    t_scr, p_scr = slots[:T_SLOTS], slots[T_SLOTS:]
    L = MOBA_BLOCK
    S = q_ref.shape[1]
    NB = S // L
    hp = pl.program_id(0)
    lane = lax.broadcasted_iota(jnp.int32, (1, LANES), 1)
    head_lanes = [(lane >= hh * MOBA_HEAD_DIM) & (lane < (hh + 1) * MOBA_HEAD_DIM)
                  for hh in range(HEADS_PER_STEP)]
    row = lax.broadcasted_iota(jnp.int32, (NB, L), 0)
    bodies = [(i, hh) for i in range(NB) for hh in range(HEADS_PER_STEP)]
    nb = len(bodies)

    cfar = [stat_ref[hh, BIAS_FAR:BIAS_FAR + 1, :1] for hh in range(HEADS_PER_STEP)]
    bmax = [stat_ref[hh, BIAS_MAX:BIAS_MAX + 1, :1] for hh in range(HEADS_PER_STEP)]
    bmin = [stat_ref[hh, BIAS_MIN:BIAS_MIN + 1, :1] for hh in range(HEADS_PER_STEP)]

    def stage_operands():
        vt = v_ref[0].astype(F32).T.astype(BF16)
        for hh in range(HEADS_PER_STEP):
            vt_scr[hh * VT_ROWS:hh * VT_ROWS + MOBA_HEAD_DIM, :] = (
                vt[hh * MOBA_HEAD_DIM:(hh + 1) * MOBA_HEAD_DIM])
            vt_scr[hh * VT_ROWS + MOBA_HEAD_DIM:(hh + 1) * VT_ROWS, :] = jnp.ones((ONES_ROWS, S), BF16)
        qt = q_ref[0].astype(F32).T.astype(BF16)
        dim_head = lax.broadcasted_iota(jnp.int32, (LANES, 1), 0) // MOBA_HEAD_DIM
        for hh in range(HEADS_PER_STEP):
            qt_scr[hh * LANES:(hh + 1) * LANES, :] = jnp.where(dim_head == hh, qt, jnp.zeros_like(qt))
        km = jnp.concatenate(
            [jnp.sum(k_ref[0, n * L:(n + 1) * L, :].astype(F32), axis=0, keepdims=True) * (1.0 / L)
             for n in range(NB)], axis=0)
        kmm = jnp.concatenate([jnp.where(head_lanes[hh], km, 0.0) for hh in range(HEADS_PER_STEP)],
                              axis=0)
        km_hi = kmm.astype(BF16)
        return km_hi, (kmm - km_hi.astype(F32)).astype(BF16)

    shift, span = [], []
    for hh in range(HEADS_PER_STEP):
        head = hp * HEADS_PER_STEP + hh
        q_sq = jnp.max(norm_ref[:, pl.ds(head, 1), :], axis=0)[:, :1]
        k_sq = jnp.max(norm_ref[:, pl.ds(MOBA_HEADS + head, 1), :], axis=0)[:, :1]
        reach = jnp.sqrt(q_sq * k_sq) * BOUND_SLACK
        shift.append(reach + bmax[hh])
        span.append(2.0 * reach + (bmax[hh] - bmin[hh]))
    safe = jnp.max(functools.reduce(jnp.maximum, span)) <= SAFE_SPAN

    def block_select(k_means, gates, n):
        i, hh = bodies[n]
        q_t = masked_q(n)
        gate = (jnp.dot(k_means[0], q_t, preferred_element_type=F32)
                + jnp.dot(k_means[1], q_t, preferred_element_type=F32))
        g = gate[hh * NB:(hh + 1) * NB]
        cnt = jnp.zeros((NB, L), jnp.int32)
        for m in range(i):
            gm = g[m:m + 1, :]
            beats = (gm > g) | ((gm == g) & (m < row))
            cnt = cnt + jnp.where(beats, 1, 0)
        return cnt < MOBA_TOPK

    def masked_q(n):
        i, hh = bodies[n]
        return qt_scr[hh * LANES:(hh + 1) * LANES, i * L:(i + 1) * L]

    def pv_stage(outs, n, p_ref):
        i, hh = bodies[n]
        nk = (i + 1) * L
        acc = jnp.dot(vt_scr[hh * VT_ROWS:(hh + 1) * VT_ROWS, 0:nk], p_ref[0:nk, :],
                      preferred_element_type=F32)
        outs[n] = acc[:MOBA_HEAD_DIM] / acc[MOBA_HEAD_DIM:MOBA_HEAD_DIM + 1]
        if hh == HEADS_PER_STEP - 1:
            pair = [outs.pop(n - HEADS_PER_STEP + 1 + h) for h in range(HEADS_PER_STEP)]
            o_ref[0, i * L:(i + 1) * L, :] = jnp.concatenate(pair, axis=0).T

    def exact_path():
        gates, masks, qms, maxes, mrows, outs = {}, {}, {}, {}, {}, {}
        _cast_slabs(cast_src, cast_dst)
        k_means = stage_operands()

        def begin_body(n):
            i, hh = bodies[n]
            qms[n] = masked_q(n)
            if i > MOBA_TOPK:
                sel = block_select(k_means, gates, n)
                masks[n] = (jnp.where(sel, cfar[hh], NEG),
                            jnp.where(sel, 0.0, NEG))

        def score_tile(n, j):
            i, hh = bodies[n]
            t = jnp.dot(k_ref[0, j * L:(j + 1) * L, :], qms[n],
                        preferred_element_type=F32)
            if j == i:
                t = t + bias_ref[hh, 0]
            elif j == i - 1:
                t = t + bias_ref[hh, 1]
                if i > MOBA_TOPK:
                    t = t + masks[n][1][j:j + 1, :]
            elif i > MOBA_TOPK:
                t = t + masks[n][0][j:j + 1, :]
            else:
                t = t + cfar[hh]
            t_scr[n % T_SLOTS][j * L:(j + 1) * L, :] = t
            tm = jnp.max(t.reshape(L // 8, 8, L), axis=0)
            maxes[n] = tm if j == 0 else jnp.maximum(maxes[n], tm)
            if j == i:
                mrows[n] = jnp.max(maxes.pop(n), axis=0, keepdims=True)
                qms.pop(n)
                masks.pop(n, None)

        def exp_tile(n, j):
            t_ref, p_ref = t_scr[n % T_SLOTS], p_scr[n % T_SLOTS]
            p_ref[j * L:(j + 1) * L, :] = jnp.exp2(t_ref[j * L:(j + 1) * L, :] - mrows[n]).astype(BF16)

        for n0 in range(min(LOOKAHEAD, nb)):
            begin_body(n0)
            for j in range(bodies[n0][0] + 1):
                score_tile(n0, j)
        for n in range(nb):
            n_exp = bodies[n][0] + 1
            ahead = n + LOOKAHEAD
            n_score = bodies[ahead][0] + 1 if ahead < nb else 0
            if n_score:
                begin_body(ahead)
            for j in range(max(n_exp, n_score)):
                if j < n_score:
                    score_tile(ahead, j)
                if j < n_exp:
                    exp_tile(n, j)
            mrows.pop(n)
            pv_stage(outs, n, p_scr[n % T_SLOTS])

    def fast_path():
        gates, outs = {}, {}
        _cast_slabs(cast_src, cast_dst)
        k_means = stage_operands()
        shifted = {(hh, d): bias_ref[hh, d] - shift[hh]
                   for hh in range(HEADS_PER_STEP) for d in range(2)}

        def prob_tiles(n):
            i, hh = bodies[n]
            qm = masked_q(n)
            s_row = shift[hh]
            if i > MOBA_TOPK:
                sel = block_select(k_means, gates, n)
                m_far = jnp.where(sel, cfar[hh], NEG) - s_row
                m_prev = jnp.where(sel, 0.0, NEG)
            for j in range(i + 1):
                e = jnp.dot(k_ref[0, j * L:(j + 1) * L, :], qm,
                            preferred_element_type=F32)
                if j == i:
                    e = e + shifted[hh, 0]
                elif j == i - 1:
                    e = e + shifted[hh, 1]
                    if i > MOBA_TOPK:
                        e = e + m_prev[j:j + 1, :]
                else:
                    e = e + (m_far[j:j + 1, :] if i > MOBA_TOPK else cfar[hh] - s_row)
                p_scr[n % P_SLOTS][j * L:(j + 1) * L, :] = jnp.exp2(e).astype(BF16)

        for n0 in range(min(FAST_LOOKAHEAD, nb)):
            prob_tiles(n0)
        for n in range(nb):
            if n + FAST_LOOKAHEAD < nb:
                prob_tiles(n + FAST_LOOKAHEAD)
            pv_stage(outs, n, p_scr[n % P_SLOTS])

    pl.when(safe)(fast_path)
    pl.when(jnp.logical_not(safe))(exact_path)


def _moba(qkv3, norms, bias_t, bias_stats, cast_weights):
    B, S, _ = qkv3.shape
    L = MOBA_BLOCK
    n_hp = MOBA_HEADS // HEADS_PER_STEP
    blk = lambda off: pl.BlockSpec((1, S, LANES), lambda hp, b: (b, 0, off + hp))
    c_in, c_out, c_shapes = _cast_specs(cast_weights, B * n_hp, lambda hp, b: hp * B + b)
    outs = pl.pallas_call(
        functools.partial(_moba_kernel, n_cast=len(cast_weights)),
        grid=(n_hp, B),
        in_specs=[
            blk(0), blk(n_hp), blk(2 * n_hp),
            pl.BlockSpec((norms.shape[0] // B,) + norms.shape[1:], lambda hp, b: (b, 0, 0)),
            pl.BlockSpec((HEADS_PER_STEP, 2, L, L), lambda hp, b: (hp, 0, 0, 0)),
            pl.BlockSpec((HEADS_PER_STEP, STAT_ROWS, LANES), lambda hp, b: (hp, 0, 0)),
        ] + c_in,
        out_specs=[pl.BlockSpec((1, S, LANES), lambda hp, b: (b, 0, hp))] + c_out,
        out_shape=[jax.ShapeDtypeStruct((B, S, D_MOBA), F32)] + c_shapes,
        scratch_shapes=[
            pltpu.VMEM((HEADS_PER_STEP * VT_ROWS, S), BF16),
            pltpu.VMEM((HEADS_PER_STEP * LANES, S), BF16),
        ] + [pltpu.VMEM((S, L), F32)] * T_SLOTS + [pltpu.VMEM((S, L), BF16)] * P_SLOTS,
        compiler_params=pltpu.CompilerParams(
            dimension_semantics=("arbitrary", "arbitrary"), vmem_limit_bytes=VMEM_LIMIT),
        name="moba",
    )(qkv3, qkv3, qkv3, norms, bias_t, bias_stats, *cast_weights)
    return outs[0], outs[1:]


def _ffn_kernel(x_ref, yb_ref, ya_ref, gb_ref, wo_ref, gf_ref, wg_ref, wu_ref, wd_ref, gl_ref,
                o_ref, *, final_norm):
    half = x_ref.shape[0] // 2
    st = {}

    def attn_norm(r):
        yb = yb_ref[r:r + half, :]
        ms = jnp.mean(yb * yb, axis=-1, keepdims=True)
        st["ybn", r] = (yb * lax.rsqrt(ms + EPS) * gb_ref[...]).astype(BF16)

    def out_proj(r):
        st["x1", r] = (x_ref[r:r + half, :]
                       + jnp.dot(st.pop(("ybn", r)), wo_ref[:D_MOBA, :], preferred_element_type=F32)
                       + jnp.dot(ya_ref[r:r + half, :], wo_ref[D_MOBA:, :],
                                 preferred_element_type=F32))

    def ffn_norm(r):
        x1 = st["x1", r]
        ms = jnp.mean(x1 * x1, axis=-1, keepdims=True)
        st["h", r] = (x1 * lax.rsqrt(ms + EPS) * gf_ref[...]).astype(BF16)

    def gate_up(r):
        h = st.pop(("h", r))
        g = jnp.dot(h, wg_ref[...], preferred_element_type=F32)
        u = jnp.dot(h, wu_ref[...], preferred_element_type=F32)
        st["a", r] = (jax.nn.silu(g) * u).astype(BF16)

    def down(r):
        x2 = st.pop(("x1", r)) + jnp.dot(st.pop(("a", r)), wd_ref[...], preferred_element_type=F32)
        if final_norm:
            ms = jnp.mean(x2 * x2, axis=-1, keepdims=True)
            x2 = x2 * lax.rsqrt(ms + EPS) * gl_ref[...]
        o_ref[r:r + half, :] = x2

    stages = [attn_norm, out_proj, ffn_norm, gate_up, down]
    for k in range(len(stages) + 1):
        if k < len(stages):
            stages[k](0)
        if k >= 1:
            stages[k - 1](half)


def _ffn(x2, yb, ya, gb, wo, gf, wg, wu, wd, gl, tm, final_norm):
    T, D = x2.shape
    d_ff = wg.shape[1]
    const = lambda shape: pl.BlockSpec(shape, lambda i: (0, 0), pipeline_mode=pl.Buffered(1))
    return pl.pallas_call(
        functools.partial(_ffn_kernel, final_norm=final_norm),
        grid=(T // tm,),
        in_specs=[
            pl.BlockSpec((tm, D), lambda i: (i, 0)),
            pl.BlockSpec((tm, D_MOBA), lambda i: (i, 0)),
            pl.BlockSpec((tm, D_GMLP), lambda i: (i, 0)),
            pl.BlockSpec((1, D_MOBA), lambda i: (0, 0)),
            const((D_MOBA + D_GMLP, D)),
            pl.BlockSpec((1, D), lambda i: (0, 0)),
            const((D, d_ff)), const((D, d_ff)), const((d_ff, D)),
            pl.BlockSpec((1, D), lambda i: (0, 0)),
        ],
        out_specs=pl.BlockSpec((tm, D), lambda i: (i, 0)),
        out_shape=jax.ShapeDtypeStruct((T, D), F32),
        compiler_params=pltpu.CompilerParams(
            dimension_semantics=("arbitrary",), vmem_limit_bytes=VMEM_LIMIT),
        name="outproj_ffn",
    )(x2, yb, ya, gb, wo, gf, wg, wu, wd, gl)


def kernel(x, rel_bias, norm_mix, w_in, gmlp_ln_g, gmlp_ln_b, w_spatial, b_spatial, out_norm_b,
           out_norm_a, w_out, norm_ffn, w_gate, w_up, w_down, norm_final):
    B, S, D = x.shape
    depth = w_in.shape[0]
    assert S % MOBA_BLOCK == 0 and S // MOBA_BLOCK > MOBA_TOPK
    far = _t5_bucket_np(np.arange(MOBA_BLOCK + 1, S + MOBA_BLOCK))
    assert (far == far[0]).all()
    far_bucket = int(far[0])

    assert depth == 1, "weight casts ride along the single layer's kernels"
    tm = 512
    T = B * S
    x2 = x.reshape(T, D)
    row = lambda v: v.reshape(1, -1)
    bias_t, bias_stats, (w_in_bf,) = _bias_tiles(rel_bias, far_bucket, [w_in[0]])
    b_sp_full = jnp.repeat(b_spatial[0].T, GMLP_GROUP_DIM, axis=1)
    qkv, ya, norms = _inproj(x2, row(norm_mix[0]), w_in_bf, row(gmlp_ln_g[0]), row(gmlp_ln_b[0]),
                             w_spatial[0], b_sp_full, row(out_norm_a[0]), 2 * tm)
    yb, (wo_bf, wg_bf, wu_bf, wd_bf) = _moba(qkv.reshape(B, S, -1), norms, bias_t, bias_stats,
                                             [w_out[0], w_gate[0], w_up[0], w_down[0]])
    out = _ffn(x2, yb.reshape(T, D_MOBA), ya, row(out_norm_b[0]), wo_bf, row(norm_ffn[0]),
               wg_bf, wu_bf, wd_bf, row(norm_final), tm, True)
    return out.reshape(B, S, D)
```

```python
import functools
import math

import numpy as np
import jax
import jax.numpy as jnp
from jax import lax
from jax.experimental import pallas as pl
from jax.experimental.pallas import tpu as pltpu

F32 = jnp.float32
BF16 = jnp.bfloat16

MOBA_HEADS = 8
MOBA_HEAD_DIM = 64
D_MOBA = MOBA_HEADS * MOBA_HEAD_DIM
GMLP_GROUPS = 8
GMLP_GROUP_DIM = 64
D_GMLP = GMLP_GROUPS * GMLP_GROUP_DIM
MOBA_BLOCK = 256
MOBA_TOPK = 3
GMLP_CHUNK = 128
NUM_BUCKETS = 32
REL_MAX_DISTANCE = 128
EPS = 1e-6
NEG = -1e9

LANES = 128
HEADS_PER_STEP = LANES // MOBA_HEAD_DIM
BF16_SUBLANES = 16
ONES_ROWS = BF16_SUBLANES
VT_ROWS = MOBA_HEAD_DIM + ONES_ROWS
VMEM_LIMIT = 56 * 1024 * 1024
T_SLOTS = 4
P_SLOTS = 4
LOOKAHEAD = 3
FAST_LOOKAHEAD = 3

LOG2E = math.log2(math.e)
Q_SCALE = MOBA_HEAD_DIM ** -0.5 * LOG2E

BOUND_SLACK = 1.0 + 2.0 ** -6
SAFE_SPAN = 100.0


def _t5_bucket_np(dist):
    n = np.maximum(dist, 0)
    max_exact = NUM_BUCKETS // 2
    nf = np.maximum(n, max_exact).astype(np.float32)
    ratio = np.log(nf / np.float32(max_exact)) / np.float32(math.log(REL_MAX_DISTANCE / max_exact))
    large = max_exact + (ratio * np.float32(NUM_BUCKETS - max_exact)).astype(np.int32)
    large = np.minimum(large, NUM_BUCKETS - 1)
    return np.where(n < max_exact, n, large).astype(np.int32)


def _cast_specs(weights, n_steps, step_of):
    in_specs, out_specs, out_shapes = [], [], []
    for w in weights:
        rows, cols = w.shape
        period = next(p for p in range(1, n_steps + 1)
                      if n_steps % p == 0 and (rows * p) % (n_steps * BF16_SUBLANES) == 0)
        spec = pl.BlockSpec((rows * period // n_steps, cols),
                            functools.partial(lambda p, *ids: (step_of(*ids) // p, 0), period))
        in_specs.append(spec)
        out_specs.append(spec)
        out_shapes.append(jax.ShapeDtypeStruct(w.shape, BF16))
    return in_specs, out_specs, out_shapes


def _cast_slabs(src_refs, dst_refs):
    for src, dst in zip(src_refs, dst_refs):
        dst[...] = src[...].astype(BF16)


BIAS_FAR, BIAS_MAX, BIAS_MIN = 0, 1, 2
STAT_ROWS = 8
F32_SUBLANES = 8
BIAS_HEADS_PER_STEP = 4


def _bias_kernel(bucket_ref, rel_ref, *refs, far_bucket):
    n_cast = (len(refs) - 2) // 2
    out_ref, stat_ref = refs[n_cast], refs[n_cast + 1]
    _cast_slabs(refs[:n_cast], refs[n_cast + 2:])
    heads_here, L = out_ref.shape[0], out_ref.shape[2]
    span = bucket_ref.shape[1]
    bucket = bucket_ref[...]
    lane = lax.broadcasted_iota(jnp.int32, (1, span), 1)
    for hh in range(heads_here):
        h = pl.program_id(0) * heads_here + hh
        vals = [rel_ref[b, h] * LOG2E for b in range(NUM_BUCKETS)]
        prof = jnp.zeros((1, span), F32)
        for b in range(NUM_BUCKETS):
            prof = jnp.where(bucket == b, vals[b], prof)
        own = jnp.broadcast_to(jnp.where(lane < L, prof, NEG), (F32_SUBLANES, span))
        prev = jnp.broadcast_to(prof, (F32_SUBLANES, span))
        for g in range(L // F32_SUBLANES):
            k0 = g * F32_SUBLANES
            rows = slice(k0, k0 + F32_SUBLANES)
            out_ref[hh, 0, rows, :] = pltpu.roll(own, k0, 1, stride=1, stride_axis=0)[:, :L]
            out_ref[hh, 1, rows, :] = pltpu.roll(prev, (k0 + L) % span, 1, stride=1,
                                                 stride_axis=0)[:, :L]
        stats = {BIAS_FAR: vals[far_bucket], BIAS_MAX: functools.reduce(jnp.maximum, vals),
                 BIAS_MIN: functools.reduce(jnp.minimum, vals)}
        for r in range(STAT_ROWS):
            stat_ref[hh, r:r + 1, :] = jnp.full((1, LANES), stats.get(r, 0.0), F32)


def _bias_tiles(rel_bias, far_bucket, cast_weights):
    L = MOBA_BLOCK
    bucket = _t5_bucket_np(np.arange(2 * L))[None, :]
    n_steps = MOBA_HEADS // BIAS_HEADS_PER_STEP
    c_in, c_out, c_shapes = _cast_specs(cast_weights, n_steps, lambda i: i)
    outs = pl.pallas_call(
        functools.partial(_bias_kernel, far_bucket=far_bucket),
        grid=(n_steps,),
        in_specs=[
            pl.BlockSpec((1, 2 * L), lambda i: (0, 0)),
            pl.BlockSpec(memory_space=pltpu.SMEM),
        ] + c_in,
        out_specs=[pl.BlockSpec((BIAS_HEADS_PER_STEP, 2, L, L), lambda i: (i, 0, 0, 0)),
                   pl.BlockSpec((BIAS_HEADS_PER_STEP, STAT_ROWS, LANES), lambda i: (i, 0, 0))] + c_out,
        out_shape=[jax.ShapeDtypeStruct((MOBA_HEADS, 2, L, L), F32),
                   jax.ShapeDtypeStruct((MOBA_HEADS, STAT_ROWS, LANES), F32)] + c_shapes,
        compiler_params=pltpu.CompilerParams(
            dimension_semantics=("arbitrary",), vmem_limit_bytes=VMEM_LIMIT),
        name="bias_tiles",
    )(jnp.asarray(bucket), rel_bias, *cast_weights)
    return outs[0], outs[1], outs[2:]


def _inproj_kernel(x_ref, g_ref, w_ref, lng_ref, lnb_ref, wsp_ref, bsp_ref, gn_ref,
                   qkv_ref, ya_ref, norm_ref, uz_scr):
    C = GMLP_CHUNK
    tm = x_ref.shape[0]
    n_qkv = 3 * D_MOBA
    x = x_ref[...]
    ms = jnp.mean(x * x, axis=-1, keepdims=True)
    h = (x * lax.rsqrt(ms + EPS) * g_ref[...]).astype(BF16)
    uz_scr[...] = jnp.dot(h, w_ref[:, n_qkv:], preferred_element_type=F32)

    t_ix = lax.broadcasted_iota(jnp.int32, (C, C), 0)
    s_ix = lax.broadcasted_iota(jnp.int32, (C, C), 1)
    w_sp = [jnp.where(t_ix >= s_ix, wsp_ref[g], 0.0).astype(BF16) for g in range(GMLP_GROUPS)]
    first = lax.broadcasted_iota(jnp.int32, (1, LANES), 1) < GMLP_GROUP_DIM

    def qkv_cols(c0, c1):
        acc = jnp.dot(h, w_ref[:, c0:c1], preferred_element_type=F32)
        if c1 <= D_MOBA:
            acc = acc * Q_SCALE
        qkv_ref[:, c0:c1] = acc.astype(BF16)
        if c1 <= 2 * D_MOBA:
            sq = acc * acc
            for c in range(c0, c1, MOBA_HEAD_DIM):
                lanes = (c - c0) // LANES * LANES
                own = first if (c - c0) % LANES == 0 else jnp.logical_not(first)
                rows_sq = jnp.sum(jnp.where(own, sq[:, lanes:lanes + LANES], 0.0), axis=-1, keepdims=True)
                top = jnp.max(rows_sq, axis=0, keepdims=True)
                norm_ref[0, c // MOBA_HEAD_DIM:c // MOBA_HEAD_DIM + 1, :] = jnp.broadcast_to(top, (1, LANES))

    def gmlp_rows(c):
        rows = slice(c * C, (c + 1) * C)
        z = jax.nn.gelu(uz_scr[rows, D_GMLP:])
        mu = jnp.mean(z, axis=-1, keepdims=True)
        zc = z - mu
        var = jnp.mean(zc * zc, axis=-1, keepdims=True)
        zn = (zc * lax.rsqrt(var + EPS) * lng_ref[...] + lnb_ref[...]).astype(BF16)
        cols = []
        for gp in range(D_GMLP // LANES):
            zp = zn[:, gp * LANES:(gp + 1) * LANES]
            r0 = jnp.dot(w_sp[2 * gp], zp, preferred_element_type=F32)
            r1 = jnp.dot(w_sp[2 * gp + 1], zp, preferred_element_type=F32)
            cols.append(jnp.where(first, r0, r1))
        s = jnp.concatenate(cols, axis=1) + bsp_ref[...]
        y = jax.nn.gelu(uz_scr[rows, :D_GMLP]) * s
        ms_y = jnp.mean(y * y, axis=-1, keepdims=True)
        ya_ref[rows, :] = (y * lax.rsqrt(ms_y + EPS) * gn_ref[...]).astype(BF16)

    col_blocks = [(c, c + 2 * LANES) for c in range(0, n_qkv, 2 * LANES)]
    n_chunks = tm // C
    for step in range(max(len(col_blocks), n_chunks)):
        if step < len(col_blocks):
            qkv_cols(*col_blocks[step])
        if step < n_chunks:
            gmlp_rows(step)


def _inproj(x2, g, w_bf, ln_g, ln_b, w_sp, b_sp_full, gn, tm):
    T, D = x2.shape
    n_all = w_bf.shape[1]
    n_qkv = 3 * D_MOBA
    C = GMLP_CHUNK
    vec = pl.BlockSpec((1, D_GMLP), lambda i: (0, 0))
    return pl.pallas_call(
        _inproj_kernel,
        grid=(T // tm,),
        in_specs=[
            pl.BlockSpec((tm, D), lambda i: (i, 0)),
            pl.BlockSpec((1, D), lambda i: (0, 0)),
            pl.BlockSpec((D, n_all), lambda i: (0, 0), pipeline_mode=pl.Buffered(1)),
            vec, vec,
            pl.BlockSpec((GMLP_GROUPS, C, C), lambda i: (0, 0, 0)),
            pl.BlockSpec((C, D_GMLP), lambda i: (0, 0)),
            vec,
        ],
        out_specs=[
            pl.BlockSpec((tm, n_qkv), lambda i: (i, 0)),
            pl.BlockSpec((tm, D_GMLP), lambda i: (i, 0)),
            pl.BlockSpec((1, 2 * MOBA_HEADS, LANES), lambda i: (i, 0, 0)),
        ],
        out_shape=[
            jax.ShapeDtypeStruct((T, n_qkv), BF16),
            jax.ShapeDtypeStruct((T, D_GMLP), BF16),
            jax.ShapeDtypeStruct((T // tm, 2 * MOBA_HEADS, LANES), F32),
        ],
        scratch_shapes=[pltpu.VMEM((tm, n_all - n_qkv), F32)],
        compiler_params=pltpu.CompilerParams(
            dimension_semantics=("arbitrary",), vmem_limit_bytes=VMEM_LIMIT),
        name="inproj_gmlp",
    )(x2, g, w_bf, ln_g, ln_b, w_sp, b_sp_full, gn)


def _moba_kernel(q_ref, k_ref, v_ref, norm_ref, bias_ref, stat_ref, *refs, n_cast):
    cast_src, o_ref, cast_dst = refs[:n_cast], refs[n_cast], refs[n_cast + 1:2 * n_cast + 1]
    vt_scr, qt_scr, slots = refs[2 * n_cast + 1], refs[2 * n_cast + 2], refs[2 * n_cast + 3:]
    t_scr, p_scr = slots[:T_SLOTS], slots[T_SLOTS:]
    L = MOBA_BLOCK
    S = q_ref.shape[1]
    NB = S // L
    hp = pl.program_id(0)
    lane = lax.broadcasted_iota(jnp.int32, (1, LANES), 1)
    head_lanes = [(lane >= hh * MOBA_HEAD_DIM) & (lane < (hh + 1) * MOBA_HEAD_DIM)
                  for hh in range(HEADS_PER_STEP)]
    row = lax.broadcasted_iota(jnp.int32, (NB, L), 0)
    bodies = [(i, hh) for i in range(NB) for hh in range(HEADS_PER_STEP)]
    nb = len(bodies)

    cfar = [stat_ref[hh, BIAS_FAR:BIAS_FAR + 1, :1] for hh in range(HEADS_PER_STEP)]
    bmax = [stat_ref[hh, BIAS_MAX:BIAS_MAX + 1, :1] for hh in range(HEADS_PER_STEP)]
    bmin = [stat_ref[hh, BIAS_MIN:BIAS_MIN + 1, :1] for hh in range(HEADS_PER_STEP)]

    def stage_operands():
        vt = v_ref[0].astype(F32).T.astype(BF16)
        for hh in range(HEADS_PER_STEP):
            vt_scr[hh * VT_ROWS:hh * VT_ROWS + MOBA_HEAD_DIM, :] = (
                vt[hh * MOBA_HEAD_DIM:(hh + 1) * MOBA_HEAD_DIM])
            vt_scr[hh * VT_ROWS + MOBA_HEAD_DIM:(hh + 1) * VT_ROWS, :] = jnp.ones((ONES_ROWS, S), BF16)
        qt = q_ref[0].astype(F32).T.astype(BF16)
        dim_head = lax.broadcasted_iota(jnp.int32, (LANES, 1), 0) // MOBA_HEAD_DIM
        for hh in range(HEADS_PER_STEP):
            qt_scr[hh * LANES:(hh + 1) * LANES, :] = jnp.where(dim_head == hh, qt, jnp.zeros_like(qt))
        km = jnp.concatenate(
            [jnp.sum(k_ref[0, n * L:(n + 1) * L, :].astype(F32), axis=0, keepdims=True) * (1.0 / L)
             for n in range(NB)], axis=0)
        kmm = jnp.concatenate([jnp.where(head_lanes[hh], km, 0.0) for hh in range(HEADS_PER_STEP)],
                              axis=0)
        km_hi = kmm.astype(BF16)
        return km_hi, (kmm - km_hi.astype(F32)).astype(BF16)

    shift, span = [], []
    for hh in range(HEADS_PER_STEP):
        head = hp * HEADS_PER_STEP + hh
        q_sq = jnp.max(norm_ref[:, pl.ds(head, 1), :], axis=0)[:, :1]
        k_sq = jnp.max(norm_ref[:, pl.ds(MOBA_HEADS + head, 1), :], axis=0)[:, :1]
        reach = jnp.sqrt(q_sq * k_sq) * BOUND_SLACK
        shift.append(reach + bmax[hh])
        span.append(2.0 * reach + (bmax[hh] - bmin[hh]))
    safe = jnp.max(functools.reduce(jnp.maximum, span)) <= SAFE_SPAN

    def block_select(k_means, gates, n):
        i, hh = bodies[n]
        q_t = masked_q(n)
        gate = (jnp.dot(k_means[0], q_t, preferred_element_type=F32)
                + jnp.dot(k_means[1], q_t, preferred_element_type=F32))
        g = gate[hh * NB:(hh + 1) * NB]
        cnt = jnp.zeros((NB, L), jnp.int32)
        for m in range(i):
            gm = g[m:m + 1, :]
            beats = (gm > g) | ((gm == g) & (m < row))
            cnt = cnt + jnp.where(beats, 1, 0)
        return cnt < MOBA_TOPK

    def masked_q(n):
        i, hh = bodies[n]
        return qt_scr[hh * LANES:(hh + 1) * LANES, i * L:(i + 1) * L]

    def pv_stage(outs, n, p_ref):
        i, hh = bodies[n]
        nk = (i + 1) * L
        acc = jnp.dot(vt_scr[hh * VT_ROWS:(hh + 1) * VT_ROWS, 0:nk], p_ref[0:nk, :],
                      preferred_element_type=F32)
        outs[n] = acc[:MOBA_HEAD_DIM] / acc[MOBA_HEAD_DIM:MOBA_HEAD_DIM + 1]
        if hh == HEADS_PER_STEP - 1:
            pair = [outs.pop(n - HEADS_PER_STEP + 1 + h) for h in range(HEADS_PER_STEP)]
            o_ref[0, i * L:(i + 1) * L, :] = jnp.concatenate(pair, axis=0).T

    def exact_path():
        gates, masks, qms, maxes, mrows, outs = {}, {}, {}, {}, {}, {}
        _cast_slabs(cast_src, cast_dst)
        k_means = stage_operands()

        def begin_body(n):
            i, hh = bodies[n]
            qms[n] = masked_q(n)
            if i > MOBA_TOPK:
                sel = block_select(k_means, gates, n)
                masks[n] = (jnp.where(sel, cfar[hh], NEG),
                            jnp.where(sel, 0.0, NEG))

        def score_tile(n, j):
            i, hh = bodies[n]
            t = jnp.dot(k_ref[0, j * L:(j + 1) * L, :], qms[n],
                        preferred_element_type=F32)
            if j == i:
                t = t + bias_ref[hh, 0]
            elif j == i - 1:
                t = t + bias_ref[hh, 1]
                if i > MOBA_TOPK:
                    t = t + masks[n][1][j:j + 1, :]
            elif i > MOBA_TOPK:
                t = t + masks[n][0][j:j + 1, :]
            else:
                t = t + cfar[hh]
            t_scr[n % T_SLOTS][j * L:(j + 1) * L, :] = t
            tm = jnp.max(t.reshape(L // 8, 8, L), axis=0)
            maxes[n] = tm if j == 0 else jnp.maximum(maxes[n], tm)
            if j == i:
                mrows[n] = jnp.max(maxes.pop(n), axis=0, keepdims=True)
                qms.pop(n)
                masks.pop(n, None)

        def exp_tile(n, j):
            t_ref, p_ref = t_scr[n % T_SLOTS], p_scr[n % T_SLOTS]
            p_ref[j * L:(j + 1) * L, :] = jnp.exp2(t_ref[j * L:(j + 1) * L, :] - mrows[n]).astype(BF16)

        for n0 in range(min(LOOKAHEAD, nb)):
            begin_body(n0)
            for j in range(bodies[n0][0] + 1):
                score_tile(n0, j)
        for n in range(nb):
            n_exp = bodies[n][0] + 1
            ahead = n + LOOKAHEAD
            n_score = bodies[ahead][0] + 1 if ahead < nb else 0
            if n_score:
                begin_body(ahead)
            for j in range(max(n_exp, n_score)):
                if j < n_score:
                    score_tile(ahead, j)
                if j < n_exp:
                    exp_tile(n, j)
            mrows.pop(n)
            pv_stage(outs, n, p_scr[n % T_SLOTS])

    def fast_path():
        gates, outs = {}, {}
        _cast_slabs(cast_src, cast_dst)
        k_means = stage_operands()
        shifted = {(hh, d): bias_ref[hh, d] - shift[hh]
                   for hh in range(HEADS_PER_STEP) for d in range(2)}

        def prob_tiles(n):
            i, hh = bodies[n]
            qm = masked_q(n)
            s_row = shift[hh]
            if i > MOBA_TOPK:
                sel = block_select(k_means, gates, n)
                m_far = jnp.where(sel, cfar[hh], NEG) - s_row
                m_prev = jnp.where(sel, 0.0, NEG)
            for j in range(i + 1):
                e = jnp.dot(k_ref[0, j * L:(j + 1) * L, :], qm,
                            preferred_element_type=F32)
                if j == i:
                    e = e + shifted[hh, 0]
                elif j == i - 1:
                    e = e + shifted[hh, 1]
                    if i > MOBA_TOPK:
                        e = e + m_prev[j:j + 1, :]
                else:
                    e = e + (m_far[j:j + 1, :] if i > MOBA_TOPK else cfar[hh] - s_row)
                p_scr[n % P_SLOTS][j * L:(j + 1) * L, :] = jnp.exp2(e).astype(BF16)

        for n0 in range(min(FAST_LOOKAHEAD, nb)):
            prob_tiles(n0)
        for n in range(nb):
            if n + FAST_LOOKAHEAD < nb:
                prob_tiles(n + FAST_LOOKAHEAD)
            pv_stage(outs, n, p_scr[n % P_SLOTS])

    pl.when(safe)(fast_path)
    pl.when(jnp.logical_not(safe))(exact_path)


def _moba(qkv3, norms, bias_t, bias_stats, cast_weights):
    B, S, _ = qkv3.shape
    L = MOBA_BLOCK
    n_hp = MOBA_HEADS // HEADS_PER_STEP
    blk = lambda off: pl.BlockSpec((1, S, LANES), lambda hp, b: (b, 0, off + hp))
    c_in, c_out, c_shapes = _cast_specs(cast_weights, B * n_hp, lambda hp, b: hp * B + b)
    outs = pl.pallas_call(
        functools.partial(_moba_kernel, n_cast=len(cast_weights)),
        grid=(n_hp, B),
        in_specs=[
            blk(0), blk(n_hp), blk(2 * n_hp),
            pl.BlockSpec((norms.shape[0] // B,) + norms.shape[1:], lambda hp, b: (b, 0, 0)),
            pl.BlockSpec((HEADS_PER_STEP, 2, L, L), lambda hp, b: (hp, 0, 0, 0)),
            pl.BlockSpec((HEADS_PER_STEP, STAT_ROWS, LANES), lambda hp, b: (hp, 0, 0)),
        ] + c_in,
        out_specs=[pl.BlockSpec((1, S, LANES), lambda hp, b: (b, 0, hp))] + c_out,
        out_shape=[jax.ShapeDtypeStruct((B, S, D_MOBA), F32)] + c_shapes,
        scratch_shapes=[
            pltpu.VMEM((HEADS_PER_STEP * VT_ROWS, S), BF16),
            pltpu.VMEM((HEADS_PER_STEP * LANES, S), BF16),
        ] + [pltpu.VMEM((S, L), F32)] * T_SLOTS + [pltpu.VMEM((S, L), BF16)] * P_SLOTS,
        compiler_params=pltpu.CompilerParams(
            dimension_semantics=("arbitrary", "arbitrary"), vmem_limit_bytes=VMEM_LIMIT),
        name="moba",
    )(qkv3, qkv3, qkv3, norms, bias_t, bias_stats, *cast_weights)
    return outs[0], outs[1:]


def _ffn_kernel(x_ref, yb_ref, ya_ref, gb_ref, wo_ref, gf_ref, wg_ref, wu_ref, wd_ref, gl_ref,
                o_ref, *, final_norm):
    half = x_ref.shape[0] // 2
    st = {}

    def attn_norm(r):
        yb = yb_ref[r:r + half, :]
        ms = jnp.mean(yb * yb, axis=-1, keepdims=True)
        st["ybn", r] = (yb * lax.rsqrt(ms + EPS) * gb_ref[...]).astype(BF16)

    def out_proj(r):
        st["x1", r] = (x_ref[r:r + half, :]
                       + jnp.dot(st.pop(("ybn", r)), wo_ref[:D_MOBA, :], preferred_element_type=F32)
                       + jnp.dot(ya_ref[r:r + half, :], wo_ref[D_MOBA:, :],
                                 preferred_element_type=F32))

    def ffn_norm(r):
        x1 = st["x1", r]
        ms = jnp.mean(x1 * x1, axis=-1, keepdims=True)
        st["h", r] = (x1 * lax.rsqrt(ms + EPS) * gf_ref[...]).astype(BF16)

    def gate_up(r):
        h = st.pop(("h", r))
        g = jnp.dot(h, wg_ref[...], preferred_element_type=F32)
        u = jnp.dot(h, wu_ref[...], preferred_element_type=F32)
        st["a", r] = (jax.nn.silu(g) * u).astype(BF16)

    def down(r):
        x2 = st.pop(("x1", r)) + jnp.dot(st.pop(("a", r)), wd_ref[...], preferred_element_type=F32)
        if final_norm:
            ms = jnp.mean(x2 * x2, axis=-1, keepdims=True)
            x2 = x2 * lax.rsqrt(ms + EPS) * gl_ref[...]
        o_ref[r:r + half, :] = x2

    stages = [attn_norm, out_proj, ffn_norm, gate_up, down]
    for k in range(len(stages) + 1):
        if k < len(stages):
            stages[k](0)
        if k >= 1:
            stages[k - 1](half)


def _ffn(x2, yb, ya, gb, wo, gf, wg, wu, wd, gl, tm, final_norm):
    T, D = x2.shape
    d_ff = wg.shape[1]
    const = lambda shape: pl.BlockSpec(shape, lambda i: (0, 0), pipeline_mode=pl.Buffered(1))
    return pl.pallas_call(
        functools.partial(_ffn_kernel, final_norm=final_norm),
        grid=(T // tm,),
        in_specs=[
            pl.BlockSpec((tm, D), lambda i: (i, 0)),
            pl.BlockSpec((tm, D_MOBA), lambda i: (i, 0)),
            pl.BlockSpec((tm, D_GMLP), lambda i: (i, 0)),
            pl.BlockSpec((1, D_MOBA), lambda i: (0, 0)),
            const((D_MOBA + D_GMLP, D)),
            pl.BlockSpec((1, D), lambda i: (0, 0)),
            const((D, d_ff)), const((D, d_ff)), const((d_ff, D)),
            pl.BlockSpec((1, D), lambda i: (0, 0)),
        ],
        out_specs=pl.BlockSpec((tm, D), lambda i: (i, 0)),
        out_shape=jax.ShapeDtypeStruct((T, D), F32),
        compiler_params=pltpu.CompilerParams(
            dimension_semantics=("arbitrary",), vmem_limit_bytes=VMEM_LIMIT),
        name="outproj_ffn",
    )(x2, yb, ya, gb, wo, gf, wg, wu, wd, gl)


def kernel(x, rel_bias, norm_mix, w_in, gmlp_ln_g, gmlp_ln_b, w_spatial, b_spatial, out_norm_b,
           out_norm_a, w_out, norm_ffn, w_gate, w_up, w_down, norm_final):
    B, S, D = x.shape
    depth = w_in.shape[0]
    assert S % MOBA_BLOCK == 0 and S // MOBA_BLOCK > MOBA_TOPK
    far = _t5_bucket_np(np.arange(MOBA_BLOCK + 1, S + MOBA_BLOCK))
    assert (far == far[0]).all()
    far_bucket = int(far[0])

    assert depth == 1, "weight casts ride along the single layer's kernels"
    tm = 512
    T = B * S
    x2 = x.reshape(T, D)
    row = lambda v: v.reshape(1, -1)
    bias_t, bias_stats, (w_in_bf,) = _bias_tiles(rel_bias, far_bucket, [w_in[0]])
    b_sp_full = jnp.repeat(b_spatial[0].T, GMLP_GROUP_DIM, axis=1)
    qkv, ya, norms = _inproj(x2, row(norm_mix[0]), w_in_bf, row(gmlp_ln_g[0]), row(gmlp_ln_b[0]),
                             w_spatial[0], b_sp_full, row(out_norm_a[0]), 2 * tm)
    yb, (wo_bf, wg_bf, wu_bf, wd_bf) = _moba(qkv.reshape(B, S, -1), norms, bias_t, bias_stats,
                                             [w_out[0], w_gate[0], w_up[0], w_down[0]])
    out = _ffn(x2, yb.reshape(T, D_MOBA), ya, row(out_norm_b[0]), wo_bf, row(norm_ffn[0]),
               wg_bf, wu_bf, wd_bf, row(norm_final), tm, True)
    return out.reshape(B, S, D)
```

```python
import functools
import math

import numpy as np
import jax
import jax.numpy as jnp
from jax import lax
from jax.experimental import pallas as pl
from jax.experimental.pallas import tpu as pltpu

F32 = jnp.float32
BF16 = jnp.bfloat16

MOBA_HEADS = 8
MOBA_HEAD_DIM = 64
D_MOBA = MOBA_HEADS * MOBA_HEAD_DIM
GMLP_GROUPS = 8
GMLP_GROUP_DIM = 64
D_GMLP = GMLP_GROUPS * GMLP_GROUP_DIM
MOBA_BLOCK = 256
MOBA_TOPK = 3
GMLP_CHUNK = 128
NUM_BUCKETS = 32
REL_MAX_DISTANCE = 128
EPS = 1e-6
NEG = -1e9

LANES = 128
HEADS_PER_STEP = LANES // MOBA_HEAD_DIM
BF16_SUBLANES = 16
ONES_ROWS = BF16_SUBLANES
VT_ROWS = MOBA_HEAD_DIM + ONES_ROWS
VMEM_LIMIT = 56 * 1024 * 1024
PROJ_ROWS = 1024
FFN_ROWS = 512
T_SLOTS = 4
P_SLOTS = 4
LOOKAHEAD = 3
FAST_LOOKAHEAD = 3

LOG2E = math.log2(math.e)
Q_SCALE = MOBA_HEAD_DIM ** -0.5 * LOG2E

BOUND_SLACK = 1.0 + 2.0 ** -6
SAFE_SPAN = 100.0


def _t5_bucket_np(dist):
    n = np.maximum(dist, 0)
    max_exact = NUM_BUCKETS // 2
    nf = np.maximum(n, max_exact).astype(np.float32)
    ratio = np.log(nf / np.float32(max_exact)) / np.float32(math.log(REL_MAX_DISTANCE / max_exact))
    large = max_exact + (ratio * np.float32(NUM_BUCKETS - max_exact)).astype(np.int32)
    large = np.minimum(large, NUM_BUCKETS - 1)
    return np.where(n < max_exact, n, large).astype(np.int32)


def _cast_specs(weights, n_steps, step_of):
    in_specs, out_specs, out_shapes = [], [], []
    for w in weights:
        rows, cols = w.shape
        period = next(p for p in range(1, n_steps + 1)
                      if n_steps % p == 0 and (rows * p) % (n_steps * BF16_SUBLANES) == 0)
        spec = pl.BlockSpec((rows * period // n_steps, cols),
                            functools.partial(lambda p, *ids: (step_of(*ids) // p, 0), period))
        in_specs.append(spec)
        out_specs.append(spec)
        out_shapes.append(jax.ShapeDtypeStruct(w.shape, BF16))
    return in_specs, out_specs, out_shapes


def _cast_slabs(src_refs, dst_refs):
    for src, dst in zip(src_refs, dst_refs):
        dst[...] = src[...].astype(BF16)


BIAS_FAR, BIAS_MAX, BIAS_MIN = 0, 1, 2
STAT_ROWS = 8
F32_SUBLANES = 8
BIAS_HEADS_PER_STEP = 4


def _bias_kernel(bucket_ref, rel_ref, *refs, far_bucket):
    n_cast = (len(refs) - 2) // 2
    out_ref, stat_ref = refs[n_cast], refs[n_cast + 1]
    _cast_slabs(refs[:n_cast], refs[n_cast + 2:])
    heads_here, L = out_ref.shape[0], out_ref.shape[2]
    span = bucket_ref.shape[1]
    bucket = bucket_ref[...]
    lane = lax.broadcasted_iota(jnp.int32, (1, span), 1)
    for hh in range(heads_here):
        h = pl.program_id(0) * heads_here + hh
        vals = [rel_ref[b, h] * LOG2E for b in range(NUM_BUCKETS)]
        prof = jnp.zeros((1, span), F32)
        for b in range(NUM_BUCKETS):
            prof = jnp.where(bucket == b, vals[b], prof)
        own = jnp.broadcast_to(jnp.where(lane < L, prof, NEG), (F32_SUBLANES, span))
        prev = jnp.broadcast_to(prof, (F32_SUBLANES, span))
        for g in range(L // F32_SUBLANES):
            k0 = g * F32_SUBLANES
            rows = slice(k0, k0 + F32_SUBLANES)
            out_ref[hh, 0, rows, :] = pltpu.roll(own, k0, 1, stride=1, stride_axis=0)[:, :L]
            out_ref[hh, 1, rows, :] = pltpu.roll(prev, (k0 + L) % span, 1, stride=1,
                                                 stride_axis=0)[:, :L]
        stats = {BIAS_FAR: vals[far_bucket], BIAS_MAX: functools.reduce(jnp.maximum, vals),
                 BIAS_MIN: functools.reduce(jnp.minimum, vals)}
        for r in range(STAT_ROWS):
            stat_ref[hh, r:r + 1, :] = jnp.full((1, LANES), stats.get(r, 0.0), F32)


def _bias_tiles(rel_bias, far_bucket, cast_weights):
    L = MOBA_BLOCK
    bucket = _t5_bucket_np(np.arange(2 * L))[None, :]
    n_steps = MOBA_HEADS // BIAS_HEADS_PER_STEP
    c_in, c_out, c_shapes = _cast_specs(cast_weights, n_steps, lambda i: i)
    outs = pl.pallas_call(
        functools.partial(_bias_kernel, far_bucket=far_bucket),
        grid=(n_steps,),
        in_specs=[
            pl.BlockSpec((1, 2 * L), lambda i: (0, 0)),
            pl.BlockSpec(memory_space=pltpu.SMEM),
        ] + c_in,
        out_specs=[pl.BlockSpec((BIAS_HEADS_PER_STEP, 2, L, L), lambda i: (i, 0, 0, 0)),
                   pl.BlockSpec((BIAS_HEADS_PER_STEP, STAT_ROWS, LANES), lambda i: (i, 0, 0))] + c_out,
        out_shape=[jax.ShapeDtypeStruct((MOBA_HEADS, 2, L, L), F32),
                   jax.ShapeDtypeStruct((MOBA_HEADS, STAT_ROWS, LANES), F32)] + c_shapes,
        compiler_params=pltpu.CompilerParams(
            dimension_semantics=("arbitrary",), vmem_limit_bytes=VMEM_LIMIT),
        name="bias_tiles",
    )(jnp.asarray(bucket), rel_bias, *cast_weights)
    return outs[0], outs[1], outs[2:]


def _inproj_kernel(x_ref, g_ref, w_ref, lng_ref, lnb_ref, wsp_ref, bsp_ref, gn_ref,
                   qkv_ref, ya_ref, norm_ref, uz_scr):
    C = GMLP_CHUNK
    tm = x_ref.shape[0]
    n_qkv = 3 * D_MOBA
    x = x_ref[...]
    ms = jnp.mean(x * x, axis=-1, keepdims=True)
    h = (x * lax.rsqrt(ms + EPS) * g_ref[...]).astype(BF16)
    uz_scr[...] = jnp.dot(h, w_ref[:, n_qkv:], preferred_element_type=F32)

    t_ix = lax.broadcasted_iota(jnp.int32, (C, C), 0)
    s_ix = lax.broadcasted_iota(jnp.int32, (C, C), 1)
    w_sp = [jnp.where(t_ix >= s_ix, wsp_ref[g], 0.0).astype(BF16) for g in range(GMLP_GROUPS)]
    first = lax.broadcasted_iota(jnp.int32, (1, LANES), 1) < GMLP_GROUP_DIM

    def qkv_cols(c0, c1):
        acc = jnp.dot(h, w_ref[:, c0:c1], preferred_element_type=F32)
        if c1 <= D_MOBA:
            acc = acc * Q_SCALE
        qkv_ref[:, c0:c1] = acc.astype(BF16)
        if c1 <= 2 * D_MOBA:
            sq = acc * acc
            for c in range(c0, c1, MOBA_HEAD_DIM):
                lanes = (c - c0) // LANES * LANES
                own = first if (c - c0) % LANES == 0 else jnp.logical_not(first)
                rows_sq = jnp.sum(jnp.where(own, sq[:, lanes:lanes + LANES], 0.0), axis=-1, keepdims=True)
                top = jnp.max(rows_sq, axis=0, keepdims=True)
                norm_ref[0, c // MOBA_HEAD_DIM:c // MOBA_HEAD_DIM + 1, :] = jnp.broadcast_to(top, (1, LANES))

    def gmlp_rows(c):
        rows = slice(c * C, (c + 1) * C)
        z = jax.nn.gelu(uz_scr[rows, D_GMLP:])
        mu = jnp.mean(z, axis=-1, keepdims=True)
        zc = z - mu
        var = jnp.mean(zc * zc, axis=-1, keepdims=True)
        zn = (zc * lax.rsqrt(var + EPS) * lng_ref[...] + lnb_ref[...]).astype(BF16)
        cols = []
        for gp in range(D_GMLP // LANES):
            zp = zn[:, gp * LANES:(gp + 1) * LANES]
            r0 = jnp.dot(w_sp[2 * gp], zp, preferred_element_type=F32)
            r1 = jnp.dot(w_sp[2 * gp + 1], zp, preferred_element_type=F32)
            cols.append(jnp.where(first, r0, r1))
        s = jnp.concatenate(cols, axis=1) + bsp_ref[...]
        y = jax.nn.gelu(uz_scr[rows, :D_GMLP]) * s
        ms_y = jnp.mean(y * y, axis=-1, keepdims=True)
        ya_ref[rows, :] = (y * lax.rsqrt(ms_y + EPS) * gn_ref[...]).astype(BF16)

    col_blocks = [(c, c + 2 * LANES) for c in range(0, n_qkv, 2 * LANES)]
    n_chunks = tm // C
    for step in range(max(len(col_blocks), n_chunks)):
        if step < len(col_blocks):
            qkv_cols(*col_blocks[step])
        if step < n_chunks:
            gmlp_rows(step)


def _inproj(x2, g, w_bf, ln_g, ln_b, w_sp, b_sp_full, gn, tm):
    T, D = x2.shape
    n_all = w_bf.shape[1]
    n_qkv = 3 * D_MOBA
    C = GMLP_CHUNK
    vec = pl.BlockSpec((1, D_GMLP), lambda i: (0, 0))
    return pl.pallas_call(
        _inproj_kernel,
        grid=(T // tm,),
        in_specs=[
            pl.BlockSpec((tm, D), lambda i: (i, 0)),
            pl.BlockSpec((1, D), lambda i: (0, 0)),
            pl.BlockSpec((D, n_all), lambda i: (0, 0), pipeline_mode=pl.Buffered(1)),
            vec, vec,
            pl.BlockSpec((GMLP_GROUPS, C, C), lambda i: (0, 0, 0)),
            pl.BlockSpec((C, D_GMLP), lambda i: (0, 0)),
            vec,
        ],
        out_specs=[
            pl.BlockSpec((tm, n_qkv), lambda i: (i, 0)),
            pl.BlockSpec((tm, D_GMLP), lambda i: (i, 0)),
            pl.BlockSpec((1, 2 * MOBA_HEADS, LANES), lambda i: (i, 0, 0)),
        ],
        out_shape=[
            jax.ShapeDtypeStruct((T, n_qkv), BF16),
            jax.ShapeDtypeStruct((T, D_GMLP), BF16),
            jax.ShapeDtypeStruct((T // tm, 2 * MOBA_HEADS, LANES), F32),
        ],
        scratch_shapes=[pltpu.VMEM((tm, n_all - n_qkv), F32)],
        compiler_params=pltpu.CompilerParams(
            dimension_semantics=("arbitrary",), vmem_limit_bytes=VMEM_LIMIT),
        name="inproj_gmlp",
    )(x2, g, w_bf, ln_g, ln_b, w_sp, b_sp_full, gn)


def _moba_kernel(q_ref, k_ref, v_ref, norm_ref, bias_ref, stat_ref, *refs, n_cast):
    cast_src, o_ref, cast_dst = refs[:n_cast], refs[n_cast], refs[n_cast + 1:2 * n_cast + 1]
    vt_scr, qt_scr, slots = refs[2 * n_cast + 1], refs[2 * n_cast + 2], refs[2 * n_cast + 3:]
    t_scr, p_scr = slots[:T_SLOTS], slots[T_SLOTS:]
    L = MOBA_BLOCK
    S = q_ref.shape[1]
    NB = S // L
    hp = pl.program_id(0)
    lane = lax.broadcasted_iota(jnp.int32, (1, LANES), 1)
    head_lanes = [(lane >= hh * MOBA_HEAD_DIM) & (lane < (hh + 1) * MOBA_HEAD_DIM)
                  for hh in range(HEADS_PER_STEP)]
    row = lax.broadcasted_iota(jnp.int32, (NB, L), 0)
    bodies = [(i, hh) for i in range(NB) for hh in range(HEADS_PER_STEP)]
    nb = len(bodies)

    cfar = [stat_ref[hh, BIAS_FAR:BIAS_FAR + 1, :1] for hh in range(HEADS_PER_STEP)]
    bmax = [stat_ref[hh, BIAS_MAX:BIAS_MAX + 1, :1] for hh in range(HEADS_PER_STEP)]
    bmin = [stat_ref[hh, BIAS_MIN:BIAS_MIN + 1, :1] for hh in range(HEADS_PER_STEP)]

    def stage_operands():
        vt = v_ref[0].astype(F32).T.astype(BF16)
        for hh in range(HEADS_PER_STEP):
            vt_scr[hh * VT_ROWS:hh * VT_ROWS + MOBA_HEAD_DIM, :] = (
                vt[hh * MOBA_HEAD_DIM:(hh + 1) * MOBA_HEAD_DIM])
            vt_scr[hh * VT_ROWS + MOBA_HEAD_DIM:(hh + 1) * VT_ROWS, :] = jnp.ones((ONES_ROWS, S), BF16)
        qt = q_ref[0].astype(F32).T.astype(BF16)
        dim_head = lax.broadcasted_iota(jnp.int32, (LANES, 1), 0) // MOBA_HEAD_DIM
        for hh in range(HEADS_PER_STEP):
            qt_scr[hh * LANES:(hh + 1) * LANES, :] = jnp.where(dim_head == hh, qt, jnp.zeros_like(qt))
        km = jnp.concatenate(
            [jnp.sum(k_ref[0, n * L:(n + 1) * L, :].astype(F32), axis=0, keepdims=True) * (1.0 / L)
             for n in range(NB)], axis=0)
        kmm = jnp.concatenate([jnp.where(head_lanes[hh], km, 0.0) for hh in range(HEADS_PER_STEP)],
                              axis=0)
        km_hi = kmm.astype(BF16)
        return km_hi, (kmm - km_hi.astype(F32)).astype(BF16)

    shift, span = [], []
    for hh in range(HEADS_PER_STEP):
        head = hp * HEADS_PER_STEP + hh
        q_sq = jnp.max(norm_ref[:, pl.ds(head, 1), :], axis=0)[:, :1]
        k_sq = jnp.max(norm_ref[:, pl.ds(MOBA_HEADS + head, 1), :], axis=0)[:, :1]
        reach = jnp.sqrt(q_sq * k_sq) * BOUND_SLACK
        shift.append(reach + bmax[hh])
        span.append(2.0 * reach + (bmax[hh] - bmin[hh]))
    safe = jnp.max(functools.reduce(jnp.maximum, span)) <= SAFE_SPAN

    def block_select(k_means, n):
        i, hh = bodies[n]
        q_t = masked_q(n)
        gate = (jnp.dot(k_means[0], q_t, preferred_element_type=F32)
                + jnp.dot(k_means[1], q_t, preferred_element_type=F32))
        g = gate[hh * NB:(hh + 1) * NB]
        cnt = jnp.zeros((NB, L), jnp.int32)
        for m in range(i):
            gm = g[m:m + 1, :]
            beats = (gm > g) | ((gm == g) & (m < row))
            cnt = cnt + jnp.where(beats, 1, 0)
        return cnt < MOBA_TOPK

    def masked_q(n):
        i, hh = bodies[n]
        return qt_scr[hh * LANES:(hh + 1) * LANES, i * L:(i + 1) * L]

    def pv_stage(outs, n, p_ref):
        i, hh = bodies[n]
        nk = (i + 1) * L
        acc = jnp.dot(vt_scr[hh * VT_ROWS:(hh + 1) * VT_ROWS, 0:nk], p_ref[0:nk, :],
                      preferred_element_type=F32)
        outs[n] = acc[:MOBA_HEAD_DIM] / acc[MOBA_HEAD_DIM:MOBA_HEAD_DIM + 1]
        if hh == HEADS_PER_STEP - 1:
            pair = [outs.pop(n - HEADS_PER_STEP + 1 + h) for h in range(HEADS_PER_STEP)]
            o_ref[0, i * L:(i + 1) * L, :] = jnp.concatenate(pair, axis=0).T

    def exact_path():
        masks, qms, maxes, mrows, outs = {}, {}, {}, {}, {}
        _cast_slabs(cast_src, cast_dst)
        k_means = stage_operands()

        def begin_body(n):
            i, hh = bodies[n]
            qms[n] = masked_q(n)
            if i > MOBA_TOPK:
                sel = block_select(k_means, n)
                masks[n] = (jnp.where(sel, cfar[hh], NEG),
                            jnp.where(sel, 0.0, NEG))

        def score_tile(n, j):
            i, hh = bodies[n]
            t = jnp.dot(k_ref[0, j * L:(j + 1) * L, :], qms[n],
                        preferred_element_type=F32)
            if j == i:
                t = t + bias_ref[hh, 0]
            elif j == i - 1:
                t = t + bias_ref[hh, 1]
                if i > MOBA_TOPK:
                    t = t + masks[n][1][j:j + 1, :]
            elif i > MOBA_TOPK:
                t = t + masks[n][0][j:j + 1, :]
            else:
                t = t + cfar[hh]
            t_scr[n % T_SLOTS][j * L:(j + 1) * L, :] = t
            tm = jnp.max(t.reshape(L // 8, 8, L), axis=0)
            maxes[n] = tm if j == 0 else jnp.maximum(maxes[n], tm)
            if j == i:
                mrows[n] = jnp.max(maxes.pop(n), axis=0, keepdims=True)
                qms.pop(n)
                masks.pop(n, None)

        def exp_tile(n, j):
            t_ref, p_ref = t_scr[n % T_SLOTS], p_scr[n % T_SLOTS]
            p_ref[j * L:(j + 1) * L, :] = jnp.exp2(t_ref[j * L:(j + 1) * L, :] - mrows[n]).astype(BF16)

        for n0 in range(min(LOOKAHEAD, nb)):
            begin_body(n0)
            for j in range(bodies[n0][0] + 1):
                score_tile(n0, j)
        for n in range(nb):
            n_exp = bodies[n][0] + 1
            ahead = n + LOOKAHEAD
            n_score = bodies[ahead][0] + 1 if ahead < nb else 0
            if n_score:
                begin_body(ahead)
            for j in range(max(n_exp, n_score)):
                if j < n_score:
                    score_tile(ahead, j)
                if j < n_exp:
                    exp_tile(n, j)
            mrows.pop(n)
            pv_stage(outs, n, p_scr[n % T_SLOTS])

    def fast_path():
        outs = {}
        _cast_slabs(cast_src, cast_dst)
        k_means = stage_operands()
        shifted = {(hh, d): bias_ref[hh, d] - shift[hh]
                   for hh in range(HEADS_PER_STEP) for d in range(2)}

        def prob_tiles(n):
            i, hh = bodies[n]
            qm = masked_q(n)
            s_row = shift[hh]
            if i > MOBA_TOPK:
                sel = block_select(k_means, n)
                m_far = jnp.where(sel, cfar[hh], NEG) - s_row
                m_prev = jnp.where(sel, 0.0, NEG)
            for j in range(i + 1):
                e = jnp.dot(k_ref[0, j * L:(j + 1) * L, :], qm,
                            preferred_element_type=F32)
                if j == i:
                    e = e + shifted[hh, 0]
                elif j == i - 1:
                    e = e + shifted[hh, 1]
                    if i > MOBA_TOPK:
                        e = e + m_prev[j:j + 1, :]
                else:
                    e = e + (m_far[j:j + 1, :] if i > MOBA_TOPK else cfar[hh] - s_row)
                p_scr[n % P_SLOTS][j * L:(j + 1) * L, :] = jnp.exp2(e).astype(BF16)

        for n0 in range(min(FAST_LOOKAHEAD, nb)):
            prob_tiles(n0)
        for n in range(nb):
            if n + FAST_LOOKAHEAD < nb:
                prob_tiles(n + FAST_LOOKAHEAD)
            pv_stage(outs, n, p_scr[n % P_SLOTS])

    pl.when(safe)(fast_path)
    pl.when(jnp.logical_not(safe))(exact_path)


def _moba(qkv3, norms, bias_t, bias_stats, cast_weights):
    B, S, _ = qkv3.shape
    L = MOBA_BLOCK
    n_hp = MOBA_HEADS // HEADS_PER_STEP
    blk = lambda off: pl.BlockSpec((1, S, LANES), lambda hp, b: (b, 0, off + hp))
    c_in, c_out, c_shapes = _cast_specs(cast_weights, B * n_hp, lambda hp, b: hp * B + b)
    outs = pl.pallas_call(
        functools.partial(_moba_kernel, n_cast=len(cast_weights)),
        grid=(n_hp, B),
        in_specs=[
            blk(0), blk(n_hp), blk(2 * n_hp),
            pl.BlockSpec((norms.shape[0] // B,) + norms.shape[1:], lambda hp, b: (b, 0, 0)),
            pl.BlockSpec((HEADS_PER_STEP, 2, L, L), lambda hp, b: (hp, 0, 0, 0)),
            pl.BlockSpec((HEADS_PER_STEP, STAT_ROWS, LANES), lambda hp, b: (hp, 0, 0)),
        ] + c_in,
        out_specs=[pl.BlockSpec((1, S, LANES), lambda hp, b: (b, 0, hp))] + c_out,
        out_shape=[jax.ShapeDtypeStruct((B, S, D_MOBA), F32)] + c_shapes,
        scratch_shapes=[
            pltpu.VMEM((HEADS_PER_STEP * VT_ROWS, S), BF16),
            pltpu.VMEM((HEADS_PER_STEP * LANES, S), BF16),
        ] + [pltpu.VMEM((S, L), F32)] * T_SLOTS + [pltpu.VMEM((S, L), BF16)] * P_SLOTS,
        compiler_params=pltpu.CompilerParams(
            dimension_semantics=("arbitrary", "arbitrary"), vmem_limit_bytes=VMEM_LIMIT),
        name="moba",
    )(qkv3, qkv3, qkv3, norms, bias_t, bias_stats, *cast_weights)
    return outs[0], outs[1:]


def _ffn_kernel(x_ref, yb_ref, ya_ref, gb_ref, wo_ref, gf_ref, wg_ref, wu_ref, wd_ref, gl_ref,
                o_ref, *, final_norm):
    half = x_ref.shape[0] // 2
    st = {}

    def attn_norm(r):
        yb = yb_ref[r:r + half, :]
        ms = jnp.mean(yb * yb, axis=-1, keepdims=True)
        st["ybn", r] = (yb * lax.rsqrt(ms + EPS) * gb_ref[...]).astype(BF16)

    def out_proj(r):
        st["x1", r] = (x_ref[r:r + half, :]
                       + jnp.dot(st.pop(("ybn", r)), wo_ref[:D_MOBA, :], preferred_element_type=F32)
                       + jnp.dot(ya_ref[r:r + half, :], wo_ref[D_MOBA:, :],
                                 preferred_element_type=F32))

    def ffn_norm(r):
        x1 = st["x1", r]
        ms = jnp.mean(x1 * x1, axis=-1, keepdims=True)
        st["h", r] = (x1 * lax.rsqrt(ms + EPS) * gf_ref[...]).astype(BF16)

    def gate_up(r):
        h = st.pop(("h", r))
        g = jnp.dot(h, wg_ref[...], preferred_element_type=F32)
        u = jnp.dot(h, wu_ref[...], preferred_element_type=F32)
        st["a", r] = (jax.nn.silu(g) * u).astype(BF16)

    def down(r):
        x2 = st.pop(("x1", r)) + jnp.dot(st.pop(("a", r)), wd_ref[...], preferred_element_type=F32)
        if final_norm:
            ms = jnp.mean(x2 * x2, axis=-1, keepdims=True)
            x2 = x2 * lax.rsqrt(ms + EPS) * gl_ref[...]
        o_ref[r:r + half, :] = x2

    stages = [attn_norm, out_proj, ffn_norm, gate_up, down]
    for k in range(len(stages) + 1):
        if k < len(stages):
            stages[k](0)
        if k >= 1:
            stages[k - 1](half)


def _ffn(x2, yb, ya, gb, wo, gf, wg, wu, wd, gl, tm, final_norm):
    T, D = x2.shape
    d_ff = wg.shape[1]
    const = lambda shape: pl.BlockSpec(shape, lambda i: (0, 0), pipeline_mode=pl.Buffered(1))
    return pl.pallas_call(
        functools.partial(_ffn_kernel, final_norm=final_norm),
        grid=(T // tm,),
        in_specs=[
            pl.BlockSpec((tm, D), lambda i: (i, 0)),
            pl.BlockSpec((tm, D_MOBA), lambda i: (i, 0)),
            pl.BlockSpec((tm, D_GMLP), lambda i: (i, 0)),
            pl.BlockSpec((1, D_MOBA), lambda i: (0, 0)),
            const((D_MOBA + D_GMLP, D)),
            pl.BlockSpec((1, D), lambda i: (0, 0)),
            const((D, d_ff)), const((D, d_ff)), const((d_ff, D)),
            pl.BlockSpec((1, D), lambda i: (0, 0)),
        ],
        out_specs=pl.BlockSpec((tm, D), lambda i: (i, 0)),
        out_shape=jax.ShapeDtypeStruct((T, D), F32),
        compiler_params=pltpu.CompilerParams(
            dimension_semantics=("arbitrary",), vmem_limit_bytes=VMEM_LIMIT),
        name="outproj_ffn",
    )(x2, yb, ya, gb, wo, gf, wg, wu, wd, gl)


def kernel(x, rel_bias, norm_mix, w_in, gmlp_ln_g, gmlp_ln_b, w_spatial, b_spatial, out_norm_b,
           out_norm_a, w_out, norm_ffn, w_gate, w_up, w_down, norm_final):
    B, S, D = x.shape
    depth = w_in.shape[0]
    assert S % MOBA_BLOCK == 0 and S // MOBA_BLOCK > MOBA_TOPK
    far = _t5_bucket_np(np.arange(MOBA_BLOCK + 1, S + MOBA_BLOCK))
    assert (far == far[0]).all()
    far_bucket = int(far[0])

    assert depth == 1, "weight casts ride along the single layer's kernels"
    T = B * S
    x2 = x.reshape(T, D)
    row = lambda v: v.reshape(1, -1)
    bias_t, bias_stats, (w_in_bf,) = _bias_tiles(rel_bias, far_bucket, [w_in[0]])
    b_sp_full = jnp.repeat(b_spatial[0].T, GMLP_GROUP_DIM, axis=1)
    qkv, ya, norms = _inproj(x2, row(norm_mix[0]), w_in_bf, row(gmlp_ln_g[0]), row(gmlp_ln_b[0]),
                             w_spatial[0], b_sp_full, row(out_norm_a[0]), PROJ_ROWS)
    yb, (wo_bf, wg_bf, wu_bf, wd_bf) = _moba(qkv.reshape(B, S, -1), norms, bias_t, bias_stats,
                                             [w_out[0], w_gate[0], w_up[0], w_down[0]])
    out = _ffn(x2, yb.reshape(T, D_MOBA), ya, row(out_norm_b[0]), wo_bf, row(norm_ffn[0]),
               wg_bf, wu_bf, wd_bf, row(norm_final), FFN_ROWS, True)
    return out.reshape(B, S, D)
```

```python
import functools
import math

import numpy as np
import jax
import jax.numpy as jnp
from jax import lax
from jax.experimental import pallas as pl
from jax.experimental.pallas import tpu as pltpu

F32 = jnp.float32
BF16 = jnp.bfloat16

MOBA_HEADS = 8
MOBA_HEAD_DIM = 64
D_MOBA = MOBA_HEADS * MOBA_HEAD_DIM
GMLP_GROUPS = 8
GMLP_GROUP_DIM = 64
D_GMLP = GMLP_GROUPS * GMLP_GROUP_DIM
MOBA_BLOCK = 256
MOBA_TOPK = 3
GMLP_CHUNK = 128
NUM_BUCKETS = 32
REL_MAX_DISTANCE = 128
EPS = 1e-6
NEG = -1e9

LANES = 128
HEADS_PER_STEP = LANES // MOBA_HEAD_DIM
BF16_SUBLANES = 16
ONES_ROWS = BF16_SUBLANES
VT_ROWS = MOBA_HEAD_DIM + ONES_ROWS
VMEM_LIMIT = 56 * 1024 * 1024
PROJ_ROWS = 1024
FFN_ROWS = 512
T_SLOTS = 4
P_SLOTS = 4
LOOKAHEAD = 3
FAST_LOOKAHEAD = 3
PAIRS_PER_STEP = 2

LOG2E = math.log2(math.e)
Q_SCALE = MOBA_HEAD_DIM ** -0.5 * LOG2E

BOUND_SLACK = 1.0 + 2.0 ** -6
SAFE_SPAN = 100.0


def _t5_bucket_np(dist):
    n = np.maximum(dist, 0)
    max_exact = NUM_BUCKETS // 2
    nf = np.maximum(n, max_exact).astype(np.float32)
    ratio = np.log(nf / np.float32(max_exact)) / np.float32(math.log(REL_MAX_DISTANCE / max_exact))
    large = max_exact + (ratio * np.float32(NUM_BUCKETS - max_exact)).astype(np.int32)
    large = np.minimum(large, NUM_BUCKETS - 1)
    return np.where(n < max_exact, n, large).astype(np.int32)


def _cast_specs(weights, n_steps, step_of):
    in_specs, out_specs, out_shapes = [], [], []
    for w in weights:
        rows, cols = w.shape
        period = next(p for p in range(1, n_steps + 1)
                      if n_steps % p == 0 and (rows * p) % (n_steps * BF16_SUBLANES) == 0)
        spec = pl.BlockSpec((rows * period // n_steps, cols),
                            functools.partial(lambda p, *ids: (step_of(*ids) // p, 0), period))
        in_specs.append(spec)
        out_specs.append(spec)
        out_shapes.append(jax.ShapeDtypeStruct(w.shape, BF16))
    return in_specs, out_specs, out_shapes


def _cast_slabs(src_refs, dst_refs):
    for src, dst in zip(src_refs, dst_refs):
        dst[...] = src[...].astype(BF16)


BIAS_FAR, BIAS_MAX, BIAS_MIN = 0, 1, 2
STAT_ROWS = 8
F32_SUBLANES = 8
BIAS_HEADS_PER_STEP = 4


def _bias_kernel(bucket_ref, rel_ref, *refs, far_bucket):
    n_cast = (len(refs) - 2) // 2
    out_ref, stat_ref = refs[n_cast], refs[n_cast + 1]
    _cast_slabs(refs[:n_cast], refs[n_cast + 2:])
    heads_here, L = out_ref.shape[0], out_ref.shape[2]
    span = bucket_ref.shape[1]
    bucket = bucket_ref[...]
    lane = lax.broadcasted_iota(jnp.int32, (1, span), 1)
    for hh in range(heads_here):
        h = pl.program_id(0) * heads_here + hh
        vals = [rel_ref[b, h] * LOG2E for b in range(NUM_BUCKETS)]
        prof = jnp.zeros((1, span), F32)
        for b in range(NUM_BUCKETS):
            prof = jnp.where(bucket == b, vals[b], prof)
        own = jnp.broadcast_to(jnp.where(lane < L, prof, NEG), (F32_SUBLANES, span))
        prev = jnp.broadcast_to(prof, (F32_SUBLANES, span))
        for g in range(L // F32_SUBLANES):
            k0 = g * F32_SUBLANES
            rows = slice(k0, k0 + F32_SUBLANES)
            out_ref[hh, 0, rows, :] = pltpu.roll(own, k0, 1, stride=1, stride_axis=0)[:, :L]
            out_ref[hh, 1, rows, :] = pltpu.roll(prev, (k0 + L) % span, 1, stride=1,
                                                 stride_axis=0)[:, :L]
        stats = {BIAS_FAR: vals[far_bucket], BIAS_MAX: functools.reduce(jnp.maximum, vals),
                 BIAS_MIN: functools.reduce(jnp.minimum, vals)}
        for r in range(STAT_ROWS):
            stat_ref[hh, r:r + 1, :] = jnp.full((1, LANES), stats.get(r, 0.0), F32)


def _bias_tiles(rel_bias, far_bucket, cast_weights):
    L = MOBA_BLOCK
    bucket = _t5_bucket_np(np.arange(2 * L))[None, :]
    n_steps = MOBA_HEADS // BIAS_HEADS_PER_STEP
    c_in, c_out, c_shapes = _cast_specs(cast_weights, n_steps, lambda i: i)
    outs = pl.pallas_call(
        functools.partial(_bias_kernel, far_bucket=far_bucket),
        grid=(n_steps,),
        in_specs=[
            pl.BlockSpec((1, 2 * L), lambda i: (0, 0)),
            pl.BlockSpec(memory_space=pltpu.SMEM),
        ] + c_in,
        out_specs=[pl.BlockSpec((BIAS_HEADS_PER_STEP, 2, L, L), lambda i: (i, 0, 0, 0)),
                   pl.BlockSpec((BIAS_HEADS_PER_STEP, STAT_ROWS, LANES), lambda i: (i, 0, 0))] + c_out,
        out_shape=[jax.ShapeDtypeStruct((MOBA_HEADS, 2, L, L), F32),
                   jax.ShapeDtypeStruct((MOBA_HEADS, STAT_ROWS, LANES), F32)] + c_shapes,
        compiler_params=pltpu.CompilerParams(
            dimension_semantics=("arbitrary",), vmem_limit_bytes=VMEM_LIMIT),
        name="bias_tiles",
    )(jnp.asarray(bucket), rel_bias, *cast_weights)
    return outs[0], outs[1], outs[2:]


def _inproj_kernel(x_ref, g_ref, w_ref, lng_ref, lnb_ref, wsp_ref, bsp_ref, gn_ref,
                   qkv_ref, ya_ref, norm_ref, uz_scr):
    C = GMLP_CHUNK
    tm = x_ref.shape[0]
    n_qkv = 3 * D_MOBA
    x = x_ref[...]
    ms = jnp.mean(x * x, axis=-1, keepdims=True)
    h = (x * lax.rsqrt(ms + EPS) * g_ref[...]).astype(BF16)
    uz_scr[...] = jnp.dot(h, w_ref[:, n_qkv:], preferred_element_type=F32)

    t_ix = lax.broadcasted_iota(jnp.int32, (C, C), 0)
    s_ix = lax.broadcasted_iota(jnp.int32, (C, C), 1)
    w_sp = [jnp.where(t_ix >= s_ix, wsp_ref[g], 0.0).astype(BF16) for g in range(GMLP_GROUPS)]
    first = lax.broadcasted_iota(jnp.int32, (1, LANES), 1) < GMLP_GROUP_DIM

    def qkv_cols(c0, c1):
        acc = jnp.dot(h, w_ref[:, c0:c1], preferred_element_type=F32)
        if c1 <= D_MOBA:
            acc = acc * Q_SCALE
        qkv_ref[:, c0:c1] = acc.astype(BF16)
        if c1 <= 2 * D_MOBA:
            sq = acc * acc
            for c in range(c0, c1, MOBA_HEAD_DIM):
                lanes = (c - c0) // LANES * LANES
                own = first if (c - c0) % LANES == 0 else jnp.logical_not(first)
                rows_sq = jnp.sum(jnp.where(own, sq[:, lanes:lanes + LANES], 0.0), axis=-1, keepdims=True)
                top = jnp.max(rows_sq, axis=0, keepdims=True)
                norm_ref[0, c // MOBA_HEAD_DIM:c // MOBA_HEAD_DIM + 1, :] = jnp.broadcast_to(top, (1, LANES))

    def gmlp_rows(c):
        rows = slice(c * C, (c + 1) * C)
        z = jax.nn.gelu(uz_scr[rows, D_GMLP:])
        mu = jnp.mean(z, axis=-1, keepdims=True)
        zc = z - mu
        var = jnp.mean(zc * zc, axis=-1, keepdims=True)
        zn = (zc * lax.rsqrt(var + EPS) * lng_ref[...] + lnb_ref[...]).astype(BF16)
        cols = []
        for gp in range(D_GMLP // LANES):
            zp = zn[:, gp * LANES:(gp + 1) * LANES]
            r0 = jnp.dot(w_sp[2 * gp], zp, preferred_element_type=F32)
            r1 = jnp.dot(w_sp[2 * gp + 1], zp, preferred_element_type=F32)
            cols.append(jnp.where(first, r0, r1))
        s = jnp.concatenate(cols, axis=1) + bsp_ref[...]
        y = jax.nn.gelu(uz_scr[rows, :D_GMLP]) * s
        ms_y = jnp.mean(y * y, axis=-1, keepdims=True)
        ya_ref[rows, :] = (y * lax.rsqrt(ms_y + EPS) * gn_ref[...]).astype(BF16)

    col_blocks = [(c, c + 2 * LANES) for c in range(0, n_qkv, 2 * LANES)]
    n_chunks = tm // C
    for step in range(max(len(col_blocks), n_chunks)):
        if step < len(col_blocks):
            qkv_cols(*col_blocks[step])
        if step < n_chunks:
            gmlp_rows(step)


def _inproj(x2, g, w_bf, ln_g, ln_b, w_sp, b_sp_full, gn, tm):
    T, D = x2.shape
    n_all = w_bf.shape[1]
    n_qkv = 3 * D_MOBA
    C = GMLP_CHUNK
    vec = pl.BlockSpec((1, D_GMLP), lambda i: (0, 0))
    return pl.pallas_call(
        _inproj_kernel,
        grid=(T // tm,),
        in_specs=[
            pl.BlockSpec((tm, D), lambda i: (i, 0)),
            pl.BlockSpec((1, D), lambda i: (0, 0)),
            pl.BlockSpec((D, n_all), lambda i: (0, 0), pipeline_mode=pl.Buffered(1)),
            vec, vec,
            pl.BlockSpec((GMLP_GROUPS, C, C), lambda i: (0, 0, 0)),
            pl.BlockSpec((C, D_GMLP), lambda i: (0, 0)),
            vec,
        ],
        out_specs=[
            pl.BlockSpec((tm, n_qkv), lambda i: (i, 0)),
            pl.BlockSpec((tm, D_GMLP), lambda i: (i, 0)),
            pl.BlockSpec((1, 2 * MOBA_HEADS, LANES), lambda i: (i, 0, 0)),
        ],
        out_shape=[
            jax.ShapeDtypeStruct((T, n_qkv), BF16),
            jax.ShapeDtypeStruct((T, D_GMLP), BF16),
            jax.ShapeDtypeStruct((T // tm, 2 * MOBA_HEADS, LANES), F32),
        ],
        scratch_shapes=[pltpu.VMEM((tm, n_all - n_qkv), F32)],
        compiler_params=pltpu.CompilerParams(
            dimension_semantics=("arbitrary",), vmem_limit_bytes=VMEM_LIMIT),
        name="inproj_gmlp",
    )(x2, g, w_bf, ln_g, ln_b, w_sp, b_sp_full, gn)


def _moba_kernel(q_ref, k_ref, v_ref, norm_ref, bias_ref, stat_ref, *refs, n_cast):
    cast_src, o_ref, cast_dst = refs[:n_cast], refs[n_cast], refs[n_cast + 1:2 * n_cast + 1]
    vt_scr, qt_scr, slots = refs[2 * n_cast + 1], refs[2 * n_cast + 2], refs[2 * n_cast + 3:]
    for pair in range(PAIRS_PER_STEP):
        lanes = pl.ds(pair * LANES, LANES)
        heads = pl.ds(pair * HEADS_PER_STEP, HEADS_PER_STEP)
        casts = (cast_src, cast_dst) if pair == 0 else ((), ())
        _moba_pair(q_ref.at[:, :, lanes], k_ref.at[:, :, lanes], v_ref.at[:, :, lanes], norm_ref,
                   bias_ref.at[heads], stat_ref.at[heads], o_ref.at[:, :, lanes], *casts,
                   vt_scr, qt_scr, slots[:T_SLOTS], slots[T_SLOTS:],
                   pl.program_id(0) * PAIRS_PER_STEP + pair)


def _moba_pair(q_ref, k_ref, v_ref, norm_ref, bias_ref, stat_ref, o_ref, cast_src, cast_dst,
               vt_scr, qt_scr, t_scr, p_scr, hp):
    L = MOBA_BLOCK
    S = q_ref.shape[1]
    NB = S // L
    lane = lax.broadcasted_iota(jnp.int32, (1, LANES), 1)
    head_lanes = [(lane >= hh * MOBA_HEAD_DIM) & (lane < (hh + 1) * MOBA_HEAD_DIM)
                  for hh in range(HEADS_PER_STEP)]
    row = lax.broadcasted_iota(jnp.int32, (NB, L), 0)
    bodies = [(i, hh) for i in range(NB) for hh in range(HEADS_PER_STEP)]
    nb = len(bodies)

    cfar = [stat_ref[hh, BIAS_FAR:BIAS_FAR + 1, :1] for hh in range(HEADS_PER_STEP)]
    bmax = [stat_ref[hh, BIAS_MAX:BIAS_MAX + 1, :1] for hh in range(HEADS_PER_STEP)]
    bmin = [stat_ref[hh, BIAS_MIN:BIAS_MIN + 1, :1] for hh in range(HEADS_PER_STEP)]

    def stage_operands():
        vt = v_ref[0].astype(F32).T.astype(BF16)
        for hh in range(HEADS_PER_STEP):
            vt_scr[hh * VT_ROWS:hh * VT_ROWS + MOBA_HEAD_DIM, :] = (
                vt[hh * MOBA_HEAD_DIM:(hh + 1) * MOBA_HEAD_DIM])
            vt_scr[hh * VT_ROWS + MOBA_HEAD_DIM:(hh + 1) * VT_ROWS, :] = jnp.ones((ONES_ROWS, S), BF16)
        qt = q_ref[0].astype(F32).T.astype(BF16)
        dim_head = lax.broadcasted_iota(jnp.int32, (LANES, 1), 0) // MOBA_HEAD_DIM
        for hh in range(HEADS_PER_STEP):
            qt_scr[hh * LANES:(hh + 1) * LANES, :] = jnp.where(dim_head == hh, qt, jnp.zeros_like(qt))
        km = jnp.concatenate(
            [jnp.sum(k_ref[0, n * L:(n + 1) * L, :].astype(F32), axis=0, keepdims=True) * (1.0 / L)
             for n in range(NB)], axis=0)
        kmm = jnp.concatenate([jnp.where(head_lanes[hh], km, 0.0) for hh in range(HEADS_PER_STEP)],
                              axis=0)
        km_hi = kmm.astype(BF16)
        return km_hi, (kmm - km_hi.astype(F32)).astype(BF16)

    shift, span = [], []
    for hh in range(HEADS_PER_STEP):
        head = hp * HEADS_PER_STEP + hh
        q_sq = jnp.max(norm_ref[:, pl.ds(head, 1), :], axis=0)[:, :1]
        k_sq = jnp.max(norm_ref[:, pl.ds(MOBA_HEADS + head, 1), :], axis=0)[:, :1]
        reach = jnp.sqrt(q_sq * k_sq) * BOUND_SLACK
        shift.append(reach + bmax[hh])
        span.append(2.0 * reach + (bmax[hh] - bmin[hh]))
    safe = jnp.max(functools.reduce(jnp.maximum, span)) <= SAFE_SPAN

    def block_select(k_means, n):
        i, hh = bodies[n]
        q_t = masked_q(n)
        gate = (jnp.dot(k_means[0], q_t, preferred_element_type=F32)
                + jnp.dot(k_means[1], q_t, preferred_element_type=F32))
        g = gate[hh * NB:(hh + 1) * NB]
        cnt = jnp.zeros((NB, L), jnp.int32)
        for m in range(i):
            gm = g[m:m + 1, :]
            beats = (gm > g) | ((gm == g) & (m < row))
            cnt = cnt + jnp.where(beats, 1, 0)
        return cnt < MOBA_TOPK

    def masked_q(n):
        i, hh = bodies[n]
        return qt_scr[hh * LANES:(hh + 1) * LANES, i * L:(i + 1) * L]

    def pv_stage(outs, n, p_ref):
        i, hh = bodies[n]
        nk = (i + 1) * L
        acc = jnp.dot(vt_scr[hh * VT_ROWS:(hh + 1) * VT_ROWS, 0:nk], p_ref[0:nk, :],
                      preferred_element_type=F32)
        outs[n] = acc[:MOBA_HEAD_DIM] / acc[MOBA_HEAD_DIM:MOBA_HEAD_DIM + 1]
        if hh == HEADS_PER_STEP - 1:
            pair = [outs.pop(n - HEADS_PER_STEP + 1 + h) for h in range(HEADS_PER_STEP)]
            o_ref[0, i * L:(i + 1) * L, :] = jnp.concatenate(pair, axis=0).T

    def exact_path():
        masks, qms, maxes, mrows, outs = {}, {}, {}, {}, {}
        _cast_slabs(cast_src, cast_dst)
        k_means = stage_operands()

        def begin_body(n):
            i, hh = bodies[n]
            qms[n] = masked_q(n)
            if i > MOBA_TOPK:
                sel = block_select(k_means, n)
                masks[n] = (jnp.where(sel, cfar[hh], NEG),
                            jnp.where(sel, 0.0, NEG))

        def score_tile(n, j):
            i, hh = bodies[n]
            t = jnp.dot(k_ref[0, j * L:(j + 1) * L, :], qms[n],
                        preferred_element_type=F32)
            if j == i:
                t = t + bias_ref[hh, 0]
            elif j == i - 1:
                t = t + bias_ref[hh, 1]
                if i > MOBA_TOPK:
                    t = t + masks[n][1][j:j + 1, :]
            elif i > MOBA_TOPK:
                t = t + masks[n][0][j:j + 1, :]
            else:
                t = t + cfar[hh]
            t_scr[n % T_SLOTS][j * L:(j + 1) * L, :] = t
            tm = jnp.max(t.reshape(L // 8, 8, L), axis=0)
            maxes[n] = tm if j == 0 else jnp.maximum(maxes[n], tm)
            if j == i:
                mrows[n] = jnp.max(maxes.pop(n), axis=0, keepdims=True)
                qms.pop(n)
                masks.pop(n, None)

        def exp_tile(n, j):
            t_ref, p_ref = t_scr[n % T_SLOTS], p_scr[n % T_SLOTS]
            p_ref[j * L:(j + 1) * L, :] = jnp.exp2(t_ref[j * L:(j + 1) * L, :] - mrows[n]).astype(BF16)

        for n0 in range(min(LOOKAHEAD, nb)):
            begin_body(n0)
            for j in range(bodies[n0][0] + 1):
                score_tile(n0, j)
        for n in range(nb):
            n_exp = bodies[n][0] + 1
            ahead = n + LOOKAHEAD
            n_score = bodies[ahead][0] + 1 if ahead < nb else 0
            if n_score:
                begin_body(ahead)
            for j in range(max(n_exp, n_score)):
                if j < n_score:
                    score_tile(ahead, j)
                if j < n_exp:
                    exp_tile(n, j)
            mrows.pop(n)
            pv_stage(outs, n, p_scr[n % T_SLOTS])

    def fast_path():
        outs = {}
        _cast_slabs(cast_src, cast_dst)
        k_means = stage_operands()
        shifted = {(hh, d): bias_ref[hh, d] - shift[hh]
                   for hh in range(HEADS_PER_STEP) for d in range(2)}

        def prob_tiles(n):
            i, hh = bodies[n]
            qm = masked_q(n)
            s_row = shift[hh]
            if i > MOBA_TOPK:
                sel = block_select(k_means, n)
                m_far = jnp.where(sel, cfar[hh], NEG) - s_row
                m_prev = jnp.where(sel, 0.0, NEG)
            for j in range(i + 1):
                e = jnp.dot(k_ref[0, j * L:(j + 1) * L, :], qm,
                            preferred_element_type=F32)
                if j == i:
                    e = e + shifted[hh, 0]
                elif j == i - 1:
                    e = e + shifted[hh, 1]
                    if i > MOBA_TOPK:
                        e = e + m_prev[j:j + 1, :]
                else:
                    e = e + (m_far[j:j + 1, :] if i > MOBA_TOPK else cfar[hh] - s_row)
                p_scr[n % P_SLOTS][j * L:(j + 1) * L, :] = jnp.exp2(e).astype(BF16)

        for n0 in range(min(FAST_LOOKAHEAD, nb)):
            prob_tiles(n0)
        for n in range(nb):
            if n + FAST_LOOKAHEAD < nb:
                prob_tiles(n + FAST_LOOKAHEAD)
            pv_stage(outs, n, p_scr[n % P_SLOTS])

    pl.when(safe)(fast_path)
    pl.when(jnp.logical_not(safe))(exact_path)


def _moba(qkv3, norms, bias_t, bias_stats, cast_weights):
    B, S, _ = qkv3.shape
    L = MOBA_BLOCK
    n_hp = MOBA_HEADS // (HEADS_PER_STEP * PAIRS_PER_STEP)
    width = PAIRS_PER_STEP * LANES
    step_heads = PAIRS_PER_STEP * HEADS_PER_STEP
    blk = lambda off: pl.BlockSpec((1, S, width), lambda hp, b: (b, 0, off + hp))
    c_in, c_out, c_shapes = _cast_specs(cast_weights, B * n_hp, lambda hp, b: hp * B + b)
    outs = pl.pallas_call(
        functools.partial(_moba_kernel, n_cast=len(cast_weights)),
        grid=(n_hp, B),
        in_specs=[
            blk(0), blk(n_hp), blk(2 * n_hp),
            pl.BlockSpec((norms.shape[0] // B,) + norms.shape[1:], lambda hp, b: (b, 0, 0)),
            pl.BlockSpec((step_heads, 2, L, L), lambda hp, b: (hp, 0, 0, 0)),
            pl.BlockSpec((step_heads, STAT_ROWS, LANES), lambda hp, b: (hp, 0, 0)),
        ] + c_in,
        out_specs=[pl.BlockSpec((1, S, width), lambda hp, b: (b, 0, hp))] + c_out,
        out_shape=[jax.ShapeDtypeStruct((B, S, D_MOBA), F32)] + c_shapes,
        scratch_shapes=[
            pltpu.VMEM((HEADS_PER_STEP * VT_ROWS, S), BF16),
            pltpu.VMEM((HEADS_PER_STEP * LANES, S), BF16),
        ] + [pltpu.VMEM((S, L), F32)] * T_SLOTS + [pltpu.VMEM((S, L), BF16)] * P_SLOTS,
        compiler_params=pltpu.CompilerParams(
            dimension_semantics=("arbitrary", "arbitrary"), vmem_limit_bytes=VMEM_LIMIT),
        name="moba",
    )(qkv3, qkv3, qkv3, norms, bias_t, bias_stats, *cast_weights)
    return outs[0], outs[1:]


def _ffn_kernel(x_ref, yb_ref, ya_ref, gb_ref, wo_ref, gf_ref, wg_ref, wu_ref, wd_ref, gl_ref,
                o_ref, *, final_norm):
    half = x_ref.shape[0] // 2
    st = {}

    def attn_norm(r):
        yb = yb_ref[r:r + half, :]
        ms = jnp.mean(yb * yb, axis=-1, keepdims=True)
        st["ybn", r] = (yb * lax.rsqrt(ms + EPS) * gb_ref[...]).astype(BF16)

    def out_proj(r):
        st["x1", r] = (x_ref[r:r + half, :]
                       + jnp.dot(st.pop(("ybn", r)), wo_ref[:D_MOBA, :], preferred_element_type=F32)
                       + jnp.dot(ya_ref[r:r + half, :], wo_ref[D_MOBA:, :],
                                 preferred_element_type=F32))

    def ffn_norm(r):
        x1 = st["x1", r]
        ms = jnp.mean(x1 * x1, axis=-1, keepdims=True)
        st["h", r] = (x1 * lax.rsqrt(ms + EPS) * gf_ref[...]).astype(BF16)

    def gate_up(r):
        h = st.pop(("h", r))
        g = jnp.dot(h, wg_ref[...], preferred_element_type=F32)
        u = jnp.dot(h, wu_ref[...], preferred_element_type=F32)
        st["a", r] = (jax.nn.silu(g) * u).astype(BF16)

    def down(r):
        x2 = st.pop(("x1", r)) + jnp.dot(st.pop(("a", r)), wd_ref[...], preferred_element_type=F32)
        if final_norm:
            ms = jnp.mean(x2 * x2, axis=-1, keepdims=True)
            x2 = x2 * lax.rsqrt(ms + EPS) * gl_ref[...]
        o_ref[r:r + half, :] = x2

    stages = [attn_norm, out_proj, ffn_norm, gate_up, down]
    for k in range(len(stages) + 1):
        if k < len(stages):
            stages[k](0)
        if k >= 1:
            stages[k - 1](half)


def _ffn(x2, yb, ya, gb, wo, gf, wg, wu, wd, gl, tm, final_norm):
    T, D = x2.shape
    d_ff = wg.shape[1]
    const = lambda shape: pl.BlockSpec(shape, lambda i: (0, 0), pipeline_mode=pl.Buffered(1))
    return pl.pallas_call(
        functools.partial(_ffn_kernel, final_norm=final_norm),
        grid=(T // tm,),
        in_specs=[
            pl.BlockSpec((tm, D), lambda i: (i, 0)),
            pl.BlockSpec((tm, D_MOBA), lambda i: (i, 0)),
            pl.BlockSpec((tm, D_GMLP), lambda i: (i, 0)),
            pl.BlockSpec((1, D_MOBA), lambda i: (0, 0)),
            const((D_MOBA + D_GMLP, D)),
            pl.BlockSpec((1, D), lambda i: (0, 0)),
            const((D, d_ff)), const((D, d_ff)), const((d_ff, D)),
            pl.BlockSpec((1, D), lambda i: (0, 0)),
        ],
        out_specs=pl.BlockSpec((tm, D), lambda i: (i, 0)),
        out_shape=jax.ShapeDtypeStruct((T, D), F32),
        compiler_params=pltpu.CompilerParams(
            dimension_semantics=("arbitrary",), vmem_limit_bytes=VMEM_LIMIT),
        name="outproj_ffn",
    )(x2, yb, ya, gb, wo, gf, wg, wu, wd, gl)


def kernel(x, rel_bias, norm_mix, w_in, gmlp_ln_g, gmlp_ln_b, w_spatial, b_spatial, out_norm_b,
           out_norm_a, w_out, norm_ffn, w_gate, w_up, w_down, norm_final):
    B, S, D = x.shape
    depth = w_in.shape[0]
    assert S % MOBA_BLOCK == 0 and S // MOBA_BLOCK > MOBA_TOPK
    far = _t5_bucket_np(np.arange(MOBA_BLOCK + 1, S + MOBA_BLOCK))
    assert (far == far[0]).all()
    far_bucket = int(far[0])

    assert depth == 1, "weight casts ride along the single layer's kernels"
    T = B * S
    x2 = x.reshape(T, D)
    row = lambda v: v.reshape(1, -1)
    bias_t, bias_stats, (w_in_bf,) = _bias_tiles(rel_bias, far_bucket, [w_in[0]])
    b_sp_full = jnp.repeat(b_spatial[0].T, GMLP_GROUP_DIM, axis=1)
    qkv, ya, norms = _inproj(x2, row(norm_mix[0]), w_in_bf, row(gmlp_ln_g[0]), row(gmlp_ln_b[0]),
                             w_spatial[0], b_sp_full, row(out_norm_a[0]), PROJ_ROWS)
    yb, (wo_bf, wg_bf, wu_bf, wd_bf) = _moba(qkv.reshape(B, S, -1), norms, bias_t, bias_stats,
                                             [w_out[0], w_gate[0], w_up[0], w_down[0]])
    out = _ffn(x2, yb.reshape(T, D_MOBA), ya, row(out_norm_b[0]), wo_bf, row(norm_ffn[0]),
               wg_bf, wu_bf, wd_bf, row(norm_final), FFN_ROWS, True)
    return out.reshape(B, S, D)
```

```python
import functools
import math

import numpy as np
import jax
import jax.numpy as jnp
from jax import lax
from jax.experimental import pallas as pl
from jax.experimental.pallas import tpu as pltpu

F32 = jnp.float32
BF16 = jnp.bfloat16

MOBA_HEADS = 8
MOBA_HEAD_DIM = 64
D_MOBA = MOBA_HEADS * MOBA_HEAD_DIM
GMLP_GROUPS = 8
GMLP_GROUP_DIM = 64
D_GMLP = GMLP_GROUPS * GMLP_GROUP_DIM
MOBA_BLOCK = 256
MOBA_TOPK = 3
GMLP_CHUNK = 128
NUM_BUCKETS = 32
REL_MAX_DISTANCE = 128
EPS = 1e-6
NEG = -1e9

LANES = 128
HEADS_PER_STEP = LANES // MOBA_HEAD_DIM
BF16_SUBLANES = 16
ONES_ROWS = BF16_SUBLANES
VT_ROWS = MOBA_HEAD_DIM + ONES_ROWS
VMEM_LIMIT = 56 * 1024 * 1024
PROJ_ROWS = 1024
FFN_ROWS = 512
T_SLOTS = 4
P_SLOTS = 4
LOOKAHEAD = 3
FAST_LOOKAHEAD = 3

LOG2E = math.log2(math.e)
Q_SCALE = MOBA_HEAD_DIM ** -0.5 * LOG2E

BOUND_SLACK = 1.0 + 2.0 ** -6
SAFE_SPAN = 100.0


def _t5_bucket_np(dist):
    n = np.maximum(dist, 0)
    max_exact = NUM_BUCKETS // 2
    nf = np.maximum(n, max_exact).astype(np.float32)
    ratio = np.log(nf / np.float32(max_exact)) / np.float32(math.log(REL_MAX_DISTANCE / max_exact))
    large = max_exact + (ratio * np.float32(NUM_BUCKETS - max_exact)).astype(np.int32)
    large = np.minimum(large, NUM_BUCKETS - 1)
    return np.where(n < max_exact, n, large).astype(np.int32)


def _cast_specs(weights, n_steps, step_of):
    in_specs, out_specs, out_shapes = [], [], []
    for w in weights:
        rows, cols = w.shape
        period = next(p for p in range(1, n_steps + 1)
                      if n_steps % p == 0 and (rows * p) % (n_steps * BF16_SUBLANES) == 0)
        spec = pl.BlockSpec((rows * period // n_steps, cols),
                            functools.partial(lambda p, *ids: (step_of(*ids) // p, 0), period))
        in_specs.append(spec)
        out_specs.append(spec)
        out_shapes.append(jax.ShapeDtypeStruct(w.shape, BF16))
    return in_specs, out_specs, out_shapes


def _cast_slabs(src_refs, dst_refs):
    for src, dst in zip(src_refs, dst_refs):
        dst[...] = src[...].astype(BF16)


BIAS_FAR, BIAS_MAX, BIAS_MIN = 0, 1, 2
STAT_ROWS = 8
F32_SUBLANES = 8
BIAS_HEADS_PER_STEP = 4


def _bias_kernel(bucket_ref, rel_ref, *refs, far_bucket):
    n_cast = (len(refs) - 2) // 2
    out_ref, stat_ref = refs[n_cast], refs[n_cast + 1]
    _cast_slabs(refs[:n_cast], refs[n_cast + 2:])
    heads_here, L = out_ref.shape[0], out_ref.shape[2]
    span = bucket_ref.shape[1]
    bucket = bucket_ref[...]
    lane = lax.broadcasted_iota(jnp.int32, (1, span), 1)
    for hh in range(heads_here):
        h = pl.program_id(0) * heads_here + hh
        vals = [rel_ref[b, h] * LOG2E for b in range(NUM_BUCKETS)]
        prof = jnp.zeros((1, span), F32)
        for b in range(NUM_BUCKETS):
            prof = jnp.where(bucket == b, vals[b], prof)
        own = jnp.broadcast_to(jnp.where(lane < L, prof, NEG), (F32_SUBLANES, span))
        prev = jnp.broadcast_to(prof, (F32_SUBLANES, span))
        for g in range(L // F32_SUBLANES):
            k0 = g * F32_SUBLANES
            rows = slice(k0, k0 + F32_SUBLANES)
            out_ref[hh, 0, rows, :] = pltpu.roll(own, k0, 1, stride=1, stride_axis=0)[:, :L]
            out_ref[hh, 1, rows, :] = pltpu.roll(prev, (k0 + L) % span, 1, stride=1,
                                                 stride_axis=0)[:, :L]
        stats = {BIAS_FAR: vals[far_bucket], BIAS_MAX: functools.reduce(jnp.maximum, vals),
                 BIAS_MIN: functools.reduce(jnp.minimum, vals)}
        for r in range(STAT_ROWS):
            stat_ref[hh, r:r + 1, :] = jnp.full((1, LANES), stats.get(r, 0.0), F32)


def _bias_tiles(rel_bias, far_bucket, cast_weights):
    L = MOBA_BLOCK
    bucket = _t5_bucket_np(np.arange(2 * L))[None, :]
    n_steps = MOBA_HEADS // BIAS_HEADS_PER_STEP
    c_in, c_out, c_shapes = _cast_specs(cast_weights, n_steps, lambda i: i)
    outs = pl.pallas_call(
        functools.partial(_bias_kernel, far_bucket=far_bucket),
        grid=(n_steps,),
        in_specs=[
            pl.BlockSpec((1, 2 * L), lambda i: (0, 0)),
            pl.BlockSpec(memory_space=pltpu.SMEM),
        ] + c_in,
        out_specs=[pl.BlockSpec((BIAS_HEADS_PER_STEP, 2, L, L), lambda i: (i, 0, 0, 0)),
                   pl.BlockSpec((BIAS_HEADS_PER_STEP, STAT_ROWS, LANES), lambda i: (i, 0, 0))] + c_out,
        out_shape=[jax.ShapeDtypeStruct((MOBA_HEADS, 2, L, L), F32),
                   jax.ShapeDtypeStruct((MOBA_HEADS, STAT_ROWS, LANES), F32)] + c_shapes,
        compiler_params=pltpu.CompilerParams(
            dimension_semantics=("arbitrary",), vmem_limit_bytes=VMEM_LIMIT),
        name="bias_tiles",
    )(jnp.asarray(bucket), rel_bias, *cast_weights)
    return outs[0], outs[1], outs[2:]


def _inproj_kernel(x_ref, g_ref, w_ref, lng_ref, lnb_ref, wsp_ref, bsp_ref, gn_ref,
                   qkv_ref, ya_ref, norm_ref, uz_scr):
    C = GMLP_CHUNK
    tm = x_ref.shape[0]
    n_qkv = 3 * D_MOBA
    x = x_ref[...]
    ms = jnp.mean(x * x, axis=-1, keepdims=True)
    h = (x * lax.rsqrt(ms + EPS) * g_ref[...]).astype(BF16)
    uz_scr[...] = jnp.dot(h, w_ref[:, n_qkv:], preferred_element_type=F32)

    t_ix = lax.broadcasted_iota(jnp.int32, (C, C), 0)
    s_ix = lax.broadcasted_iota(jnp.int32, (C, C), 1)
    w_sp = [jnp.where(t_ix >= s_ix, wsp_ref[g], 0.0).astype(BF16) for g in range(GMLP_GROUPS)]
    first = lax.broadcasted_iota(jnp.int32, (1, LANES), 1) < GMLP_GROUP_DIM

    def qkv_cols(c0, c1):
        acc = jnp.dot(h, w_ref[:, c0:c1], preferred_element_type=F32)
        if c1 <= D_MOBA:
            acc = acc * Q_SCALE
        qkv_ref[:, c0:c1] = acc.astype(BF16)
        if c1 <= 2 * D_MOBA:
            sq = acc * acc
            for c in range(c0, c1, MOBA_HEAD_DIM):
                lanes = (c - c0) // LANES * LANES
                own = first if (c - c0) % LANES == 0 else jnp.logical_not(first)
                rows_sq = jnp.sum(jnp.where(own, sq[:, lanes:lanes + LANES], 0.0), axis=-1, keepdims=True)
                top = jnp.max(rows_sq, axis=0, keepdims=True)
                norm_ref[0, c // MOBA_HEAD_DIM:c // MOBA_HEAD_DIM + 1, :] = jnp.broadcast_to(top, (1, LANES))

    def gmlp_rows(c):
        rows = slice(c * C, (c + 1) * C)
        z = jax.nn.gelu(uz_scr[rows, D_GMLP:])
        mu = jnp.mean(z, axis=-1, keepdims=True)
        zc = z - mu
        var = jnp.mean(zc * zc, axis=-1, keepdims=True)
        zn = (zc * lax.rsqrt(var + EPS) * lng_ref[...] + lnb_ref[...]).astype(BF16)
        cols = []
        for gp in range(D_GMLP // LANES):
            zp = zn[:, gp * LANES:(gp + 1) * LANES]
            r0 = jnp.dot(w_sp[2 * gp], zp, preferred_element_type=F32)
            r1 = jnp.dot(w_sp[2 * gp + 1], zp, preferred_element_type=F32)
            cols.append(jnp.where(first, r0, r1))
        s = jnp.concatenate(cols, axis=1) + bsp_ref[...]
        y = jax.nn.gelu(uz_scr[rows, :D_GMLP]) * s
        ms_y = jnp.mean(y * y, axis=-1, keepdims=True)
        ya_ref[rows, :] = (y * lax.rsqrt(ms_y + EPS) * gn_ref[...]).astype(BF16)

    col_blocks = [(c, c + 2 * LANES) for c in range(0, n_qkv, 2 * LANES)]
    n_chunks = tm // C
    for step in range(max(len(col_blocks), n_chunks)):
        if step < len(col_blocks):
            qkv_cols(*col_blocks[step])
        if step < n_chunks:
            gmlp_rows(step)


def _inproj(x2, g, w_bf, ln_g, ln_b, w_sp, b_sp_full, gn, tm):
    T, D = x2.shape
    n_all = w_bf.shape[1]
    n_qkv = 3 * D_MOBA
    C = GMLP_CHUNK
    vec = pl.BlockSpec((1, D_GMLP), lambda i: (0, 0))
    return pl.pallas_call(
        _inproj_kernel,
        grid=(T // tm,),
        in_specs=[
            pl.BlockSpec((tm, D), lambda i: (i, 0)),
            pl.BlockSpec((1, D), lambda i: (0, 0)),
            pl.BlockSpec((D, n_all), lambda i: (0, 0), pipeline_mode=pl.Buffered(1)),
            vec, vec,
            pl.BlockSpec((GMLP_GROUPS, C, C), lambda i: (0, 0, 0)),
            pl.BlockSpec((C, D_GMLP), lambda i: (0, 0)),
            vec,
        ],
        out_specs=[
            pl.BlockSpec((tm, n_qkv), lambda i: (i, 0)),
            pl.BlockSpec((tm, D_GMLP), lambda i: (i, 0)),
            pl.BlockSpec((1, 2 * MOBA_HEADS, LANES), lambda i: (i, 0, 0)),
        ],
        out_shape=[
            jax.ShapeDtypeStruct((T, n_qkv), BF16),
            jax.ShapeDtypeStruct((T, D_GMLP), BF16),
            jax.ShapeDtypeStruct((T // tm, 2 * MOBA_HEADS, LANES), F32),
        ],
        scratch_shapes=[pltpu.VMEM((tm, n_all - n_qkv), F32)],
        compiler_params=pltpu.CompilerParams(
            dimension_semantics=("arbitrary",), vmem_limit_bytes=VMEM_LIMIT),
        name="inproj_gmlp",
    )(x2, g, w_bf, ln_g, ln_b, w_sp, b_sp_full, gn)


def _moba_kernel(q_ref, k_ref, v_ref, norm_ref, bias_ref, stat_ref, *refs, n_cast):
    cast_src, o_ref, cast_dst = refs[:n_cast], refs[n_cast], refs[n_cast + 1:2 * n_cast + 1]
    vt_scr, qt_scr, slots = refs[2 * n_cast + 1], refs[2 * n_cast + 2], refs[2 * n_cast + 3:]
    t_scr, p_scr = slots[:T_SLOTS], slots[T_SLOTS:]
    L = MOBA_BLOCK
    S = q_ref.shape[1]
    NB = S // L
    hp = pl.program_id(0)
    lane = lax.broadcasted_iota(jnp.int32, (1, LANES), 1)
    head_lanes = [(lane >= hh * MOBA_HEAD_DIM) & (lane < (hh + 1) * MOBA_HEAD_DIM)
                  for hh in range(HEADS_PER_STEP)]
    row = lax.broadcasted_iota(jnp.int32, (NB, L), 0)
    bodies = [(i, hh) for i in range(NB) for hh in range(HEADS_PER_STEP)]
    nb = len(bodies)

    cfar = [stat_ref[hh, BIAS_FAR:BIAS_FAR + 1, :1] for hh in range(HEADS_PER_STEP)]
    bmax = [stat_ref[hh, BIAS_MAX:BIAS_MAX + 1, :1] for hh in range(HEADS_PER_STEP)]
    bmin = [stat_ref[hh, BIAS_MIN:BIAS_MIN + 1, :1] for hh in range(HEADS_PER_STEP)]

    def stage_operands():
        vt = v_ref[0].astype(F32).T.astype(BF16)
        for j in range(NB):
            for hh in range(HEADS_PER_STEP):
                vt_scr[j, hh, :MOBA_HEAD_DIM, :] = vt[hh * MOBA_HEAD_DIM:(hh + 1) * MOBA_HEAD_DIM,
                                                      j * L:(j + 1) * L]
                vt_scr[j, hh, MOBA_HEAD_DIM:, :] = jnp.ones((ONES_ROWS, L), BF16)
        qt = q_ref[0].astype(F32).T.astype(BF16)
        dim_head = lax.broadcasted_iota(jnp.int32, (LANES, 1), 0) // MOBA_HEAD_DIM
        for i in range(NB):
            q_blk = qt[:, i * L:(i + 1) * L]
            for hh in range(HEADS_PER_STEP):
                qt_scr[i, hh] = jnp.where(dim_head == hh, q_blk, jnp.zeros_like(q_blk))
        km = jnp.concatenate(
            [jnp.sum(k_ref[0, n * L:(n + 1) * L, :].astype(F32), axis=0, keepdims=True) * (1.0 / L)
             for n in range(NB)], axis=0)
        kmm = jnp.concatenate([jnp.where(head_lanes[hh], km, 0.0) for hh in range(HEADS_PER_STEP)],
                              axis=0)
        km_hi = kmm.astype(BF16)
        return km_hi, (kmm - km_hi.astype(F32)).astype(BF16)

    shift, span = [], []
    for hh in range(HEADS_PER_STEP):
        head = hp * HEADS_PER_STEP + hh
        q_sq = jnp.max(norm_ref[:, pl.ds(head, 1), :], axis=0)[:, :1]
        k_sq = jnp.max(norm_ref[:, pl.ds(MOBA_HEADS + head, 1), :], axis=0)[:, :1]
        reach = jnp.sqrt(q_sq * k_sq) * BOUND_SLACK
        shift.append(reach + bmax[hh])
        span.append(2.0 * reach + (bmax[hh] - bmin[hh]))
    safe = jnp.max(functools.reduce(jnp.maximum, span)) <= SAFE_SPAN

    def block_select(k_means, n):
        i, hh = bodies[n]
        q_t = masked_q(n)
        gate = (jnp.dot(k_means[0], q_t, preferred_element_type=F32)
                + jnp.dot(k_means[1], q_t, preferred_element_type=F32))
        g = gate[hh * NB:(hh + 1) * NB]
        cnt = jnp.zeros((NB, L), jnp.int32)
        for m in range(i):
            gm = g[m:m + 1, :]
            beats = (gm > g) | ((gm == g) & (m < row))
            cnt = cnt + jnp.where(beats, 1, 0)
        return cnt < MOBA_TOPK

    def masked_q(n):
        i, hh = bodies[n]
        return qt_scr[i, hh]

    def pv_stage(outs, n, p_ref):
        i, hh = bodies[n]
        acc = sum(jnp.dot(vt_scr[j, hh], p_ref[j * L:(j + 1) * L, :], preferred_element_type=F32)
                  for j in range(i + 1))
        outs[n] = acc[:MOBA_HEAD_DIM] / acc[MOBA_HEAD_DIM:MOBA_HEAD_DIM + 1]
        if hh == HEADS_PER_STEP - 1:
            pair = [outs.pop(n - HEADS_PER_STEP + 1 + h) for h in range(HEADS_PER_STEP)]
            o_ref[0, i * L:(i + 1) * L, :] = jnp.concatenate(pair, axis=0).T

    def exact_path():
        masks, qms, maxes, mrows, outs = {}, {}, {}, {}, {}
        _cast_slabs(cast_src, cast_dst)
        k_means = stage_operands()

        def begin_body(n):
            i, hh = bodies[n]
            qms[n] = masked_q(n)
            if i > MOBA_TOPK:
                sel = block_select(k_means, n)
                masks[n] = (jnp.where(sel, cfar[hh], NEG),
                            jnp.where(sel, 0.0, NEG))

        def score_tile(n, j):
            i, hh = bodies[n]
            t = jnp.dot(k_ref[0, j * L:(j + 1) * L, :], qms[n],
                        preferred_element_type=F32)
            if j == i:
                t = t + bias_ref[hh, 0]
            elif j == i - 1:
                t = t + bias_ref[hh, 1]
                if i > MOBA_TOPK:
                    t = t + masks[n][1][j:j + 1, :]
            elif i > MOBA_TOPK:
                t = t + masks[n][0][j:j + 1, :]
            else:
                t = t + cfar[hh]
            t_scr[n % T_SLOTS][j * L:(j + 1) * L, :] = t
            tm = jnp.max(t.reshape(L // 8, 8, L), axis=0)
            maxes[n] = tm if j == 0 else jnp.maximum(maxes[n], tm)
            if j == i:
                mrows[n] = jnp.max(maxes.pop(n), axis=0, keepdims=True)
                qms.pop(n)
                masks.pop(n, None)

        def exp_tile(n, j):
            t_ref, p_ref = t_scr[n % T_SLOTS], p_scr[n % T_SLOTS]
            p_ref[j * L:(j + 1) * L, :] = jnp.exp2(t_ref[j * L:(j + 1) * L, :] - mrows[n]).astype(BF16)

        for n0 in range(min(LOOKAHEAD, nb)):
            begin_body(n0)
            for j in range(bodies[n0][0] + 1):
                score_tile(n0, j)
        for n in range(nb):
            n_exp = bodies[n][0] + 1
            ahead = n + LOOKAHEAD
            n_score = bodies[ahead][0] + 1 if ahead < nb else 0
            if n_score:
                begin_body(ahead)
            for j in range(max(n_exp, n_score)):
                if j < n_score:
                    score_tile(ahead, j)
                if j < n_exp:
                    exp_tile(n, j)
            mrows.pop(n)
            pv_stage(outs, n, p_scr[n % T_SLOTS])

    def fast_path():
        outs = {}
        _cast_slabs(cast_src, cast_dst)
        k_means = stage_operands()
        shifted = {(hh, d): bias_ref[hh, d] - shift[hh]
                   for hh in range(HEADS_PER_STEP) for d in range(2)}

        def prob_tiles(n):
            i, hh = bodies[n]
            qm = masked_q(n)
            s_row = shift[hh]
            if i > MOBA_TOPK:
                sel = block_select(k_means, n)
                m_far = jnp.where(sel, cfar[hh], NEG) - s_row
                m_prev = jnp.where(sel, 0.0, NEG)
            for j in range(i + 1):
                e = jnp.dot(k_ref[0, j * L:(j + 1) * L, :], qm,
                            preferred_element_type=F32)
                if j == i:
                    e = e + shifted[hh, 0]
                elif j == i - 1:
                    e = e + shifted[hh, 1]
                    if i > MOBA_TOPK:
                        e = e + m_prev[j:j + 1, :]
                else:
                    e = e + (m_far[j:j + 1, :] if i > MOBA_TOPK else cfar[hh] - s_row)
                p_scr[n % P_SLOTS][j * L:(j + 1) * L, :] = jnp.exp2(e).astype(BF16)

        for n0 in range(min(FAST_LOOKAHEAD, nb)):
            prob_tiles(n0)
        for n in range(nb):
            if n + FAST_LOOKAHEAD < nb:
                prob_tiles(n + FAST_LOOKAHEAD)
            pv_stage(outs, n, p_scr[n % P_SLOTS])

    pl.when(safe)(fast_path)
    pl.when(jnp.logical_not(safe))(exact_path)


def _moba(qkv3, norms, bias_t, bias_stats, cast_weights):
    B, S, _ = qkv3.shape
    L = MOBA_BLOCK
    n_hp = MOBA_HEADS // HEADS_PER_STEP
    blk = lambda off: pl.BlockSpec((1, S, LANES), lambda hp, b: (b, 0, off + hp))
    c_in, c_out, c_shapes = _cast_specs(cast_weights, B * n_hp, lambda hp, b: hp * B + b)
    outs = pl.pallas_call(
        functools.partial(_moba_kernel, n_cast=len(cast_weights)),
        grid=(n_hp, B),
        in_specs=[
            blk(0), blk(n_hp), blk(2 * n_hp),
            pl.BlockSpec((norms.shape[0] // B,) + norms.shape[1:], lambda hp, b: (b, 0, 0)),
            pl.BlockSpec((HEADS_PER_STEP, 2, L, L), lambda hp, b: (hp, 0, 0, 0)),
            pl.BlockSpec((HEADS_PER_STEP, STAT_ROWS, LANES), lambda hp, b: (hp, 0, 0)),
        ] + c_in,
        out_specs=[pl.BlockSpec((1, S, LANES), lambda hp, b: (b, 0, hp))] + c_out,
        out_shape=[jax.ShapeDtypeStruct((B, S, D_MOBA), F32)] + c_shapes,
        scratch_shapes=[
            pltpu.VMEM((S // L, HEADS_PER_STEP, VT_ROWS, L), BF16),
            pltpu.VMEM((S // L, HEADS_PER_STEP, LANES, L), BF16),
        ] + [pltpu.VMEM((S, L), F32)] * T_SLOTS + [pltpu.VMEM((S, L), BF16)] * P_SLOTS,
        compiler_params=pltpu.CompilerParams(
            dimension_semantics=("arbitrary", "arbitrary"), vmem_limit_bytes=VMEM_LIMIT),
        name="moba",
    )(qkv3, qkv3, qkv3, norms, bias_t, bias_stats, *cast_weights)
    return outs[0], outs[1:]


def _ffn_kernel(x_ref, yb_ref, ya_ref, gb_ref, wo_ref, gf_ref, wg_ref, wu_ref, wd_ref, gl_ref,
                o_ref, *, final_norm):
    half = x_ref.shape[0] // 2
    st = {}

    def attn_norm(r):
        yb = yb_ref[r:r + half, :]
        ms = jnp.mean(yb * yb, axis=-1, keepdims=True)
        st["ybn", r] = (yb * lax.rsqrt(ms + EPS) * gb_ref[...]).astype(BF16)

    def out_proj(r):
        st["x1", r] = (x_ref[r:r + half, :]
                       + jnp.dot(st.pop(("ybn", r)), wo_ref[:D_MOBA, :], preferred_element_type=F32)
                       + jnp.dot(ya_ref[r:r + half, :], wo_ref[D_MOBA:, :],
                                 preferred_element_type=F32))

    def ffn_norm(r):
        x1 = st["x1", r]
        ms = jnp.mean(x1 * x1, axis=-1, keepdims=True)
        st["h", r] = (x1 * lax.rsqrt(ms + EPS) * gf_ref[...]).astype(BF16)

    def gate_up(r):
        h = st.pop(("h", r))
        g = jnp.dot(h, wg_ref[...], preferred_element_type=F32)
        u = jnp.dot(h, wu_ref[...], preferred_element_type=F32)
        st["a", r] = (jax.nn.silu(g) * u).astype(BF16)

    def down(r):
        x2 = st.pop(("x1", r)) + jnp.dot(st.pop(("a", r)), wd_ref[...], preferred_element_type=F32)
        if final_norm:
            ms = jnp.mean(x2 * x2, axis=-1, keepdims=True)
            x2 = x2 * lax.rsqrt(ms + EPS) * gl_ref[...]
        o_ref[r:r + half, :] = x2

    stages = [attn_norm, out_proj, ffn_norm, gate_up, down]
    for k in range(len(stages) + 1):
        if k < len(stages):
            stages[k](0)
        if k >= 1:
            stages[k - 1](half)


def _ffn(x2, yb, ya, gb, wo, gf, wg, wu, wd, gl, tm, final_norm):
    T, D = x2.shape
    d_ff = wg.shape[1]
    const = lambda shape: pl.BlockSpec(shape, lambda i: (0, 0), pipeline_mode=pl.Buffered(1))
    return pl.pallas_call(
        functools.partial(_ffn_kernel, final_norm=final_norm),
        grid=(T // tm,),
        in_specs=[
            pl.BlockSpec((tm, D), lambda i: (i, 0)),
            pl.BlockSpec((tm, D_MOBA), lambda i: (i, 0)),
            pl.BlockSpec((tm, D_GMLP), lambda i: (i, 0)),
            pl.BlockSpec((1, D_MOBA), lambda i: (0, 0)),
            const((D_MOBA + D_GMLP, D)),
            pl.BlockSpec((1, D), lambda i: (0, 0)),
            const((D, d_ff)), const((D, d_ff)), const((d_ff, D)),
            pl.BlockSpec((1, D), lambda i: (0, 0)),
        ],
        out_specs=pl.BlockSpec((tm, D), lambda i: (i, 0)),
        out_shape=jax.ShapeDtypeStruct((T, D), F32),
        compiler_params=pltpu.CompilerParams(
            dimension_semantics=("arbitrary",), vmem_limit_bytes=VMEM_LIMIT),
        name="outproj_ffn",
    )(x2, yb, ya, gb, wo, gf, wg, wu, wd, gl)


def kernel(x, rel_bias, norm_mix, w_in, gmlp_ln_g, gmlp_ln_b, w_spatial, b_spatial, out_norm_b,
           out_norm_a, w_out, norm_ffn, w_gate, w_up, w_down, norm_final):
    B, S, D = x.shape
    depth = w_in.shape[0]
    assert S % MOBA_BLOCK == 0 and S // MOBA_BLOCK > MOBA_TOPK
    far = _t5_bucket_np(np.arange(MOBA_BLOCK + 1, S + MOBA_BLOCK))
    assert (far == far[0]).all()
    far_bucket = int(far[0])

    assert depth == 1, "weight casts ride along the single layer's kernels"
    T = B * S
    x2 = x.reshape(T, D)
    row = lambda v: v.reshape(1, -1)
    bias_t, bias_stats, (w_in_bf,) = _bias_tiles(rel_bias, far_bucket, [w_in[0]])
    b_sp_full = jnp.repeat(b_spatial[0].T, GMLP_GROUP_DIM, axis=1)
    qkv, ya, norms = _inproj(x2, row(norm_mix[0]), w_in_bf, row(gmlp_ln_g[0]), row(gmlp_ln_b[0]),
                             w_spatial[0], b_sp_full, row(out_norm_a[0]), PROJ_ROWS)
    yb, (wo_bf, wg_bf, wu_bf, wd_bf) = _moba(qkv.reshape(B, S, -1), norms, bias_t, bias_stats,
                                             [w_out[0], w_gate[0], w_up[0], w_down[0]])
    out = _ffn(x2, yb.reshape(T, D_MOBA), ya, row(out_norm_b[0]), wo_bf, row(norm_ffn[0]),
               wg_bf, wu_bf, wd_bf, row(norm_final), FFN_ROWS, True)
    return out.reshape(B, S, D)
```

```python
import functools
import math

import numpy as np
import jax
import jax.numpy as jnp
from jax import lax
from jax.experimental import pallas as pl
from jax.experimental.pallas import tpu as pltpu

F32 = jnp.float32
BF16 = jnp.bfloat16

MOBA_HEADS = 8
MOBA_HEAD_DIM = 64
D_MOBA = MOBA_HEADS * MOBA_HEAD_DIM
GMLP_GROUPS = 8
GMLP_GROUP_DIM = 64
D_GMLP = GMLP_GROUPS * GMLP_GROUP_DIM
MOBA_BLOCK = 256
MOBA_TOPK = 3
GMLP_CHUNK = 128
NUM_BUCKETS = 32
REL_MAX_DISTANCE = 128
EPS = 1e-6
NEG = -1e9

LANES = 128
HEADS_PER_STEP = LANES // MOBA_HEAD_DIM
BF16_SUBLANES = 16
ONES_ROWS = BF16_SUBLANES
VT_ROWS = MOBA_HEAD_DIM + ONES_ROWS
VMEM_LIMIT = 56 * 1024 * 1024
PROJ_ROWS = 1024
FFN_ROWS = 512
T_SLOTS = 4
P_SLOTS = 4
LOOKAHEAD = 3
FAST_LOOKAHEAD = 3

LOG2E = math.log2(math.e)
Q_SCALE = MOBA_HEAD_DIM ** -0.5 * LOG2E

BOUND_SLACK = 1.0 + 2.0 ** -6
SAFE_SPAN = 100.0


def _t5_bucket_np(dist):
    n = np.maximum(dist, 0)
    max_exact = NUM_BUCKETS // 2
    nf = np.maximum(n, max_exact).astype(np.float32)
    ratio = np.log(nf / np.float32(max_exact)) / np.float32(math.log(REL_MAX_DISTANCE / max_exact))
    large = max_exact + (ratio * np.float32(NUM_BUCKETS - max_exact)).astype(np.int32)
    large = np.minimum(large, NUM_BUCKETS - 1)
    return np.where(n < max_exact, n, large).astype(np.int32)


def _cast_specs(weights, n_steps, step_of):
    in_specs, out_specs, out_shapes = [], [], []
    for w in weights:
        rows, cols = w.shape
        period = next(p for p in range(1, n_steps + 1)
                      if n_steps % p == 0 and (rows * p) % (n_steps * BF16_SUBLANES) == 0)
        spec = pl.BlockSpec((rows * period // n_steps, cols),
                            functools.partial(lambda p, *ids: (step_of(*ids) // p, 0), period))
        in_specs.append(spec)
        out_specs.append(spec)
        out_shapes.append(jax.ShapeDtypeStruct(w.shape, BF16))
    return in_specs, out_specs, out_shapes


def _cast_slabs(src_refs, dst_refs):
    for src, dst in zip(src_refs, dst_refs):
        dst[...] = src[...].astype(BF16)


BIAS_FAR, BIAS_MAX, BIAS_MIN = 0, 1, 2
STAT_ROWS = 8
F32_SUBLANES = 8
BIAS_HEADS_PER_STEP = 4


def _bias_kernel(bucket_ref, rel_ref, *refs, far_bucket):
    n_cast = (len(refs) - 2) // 2
    out_ref, stat_ref = refs[n_cast], refs[n_cast + 1]
    _cast_slabs(refs[:n_cast], refs[n_cast + 2:])
    heads_here, L = out_ref.shape[0], out_ref.shape[2]
    span = bucket_ref.shape[1]
    bucket = bucket_ref[...]
    lane = lax.broadcasted_iota(jnp.int32, (1, span), 1)
    for hh in range(heads_here):
        h = pl.program_id(0) * heads_here + hh
        vals = [rel_ref[b, h] * LOG2E for b in range(NUM_BUCKETS)]
        prof = jnp.zeros((1, span), F32)
        for b in range(NUM_BUCKETS):
            prof = jnp.where(bucket == b, vals[b], prof)
        own = jnp.broadcast_to(jnp.where(lane < L, prof, NEG), (F32_SUBLANES, span))
        prev = jnp.broadcast_to(prof, (F32_SUBLANES, span))
        for g in range(L // F32_SUBLANES):
            k0 = g * F32_SUBLANES
            rows = slice(k0, k0 + F32_SUBLANES)
            out_ref[hh, 0, rows, :] = pltpu.roll(own, k0, 1, stride=1, stride_axis=0)[:, :L]
            out_ref[hh, 1, rows, :] = pltpu.roll(prev, (k0 + L) % span, 1, stride=1,
                                                 stride_axis=0)[:, :L]
        stats = {BIAS_FAR: vals[far_bucket], BIAS_MAX: functools.reduce(jnp.maximum, vals),
                 BIAS_MIN: functools.reduce(jnp.minimum, vals)}
        for r in range(STAT_ROWS):
            stat_ref[hh, r:r + 1, :] = jnp.full((1, LANES), stats.get(r, 0.0), F32)


def _bias_tiles(rel_bias, far_bucket, cast_weights):
    L = MOBA_BLOCK
    bucket = _t5_bucket_np(np.arange(2 * L))[None, :]
    n_steps = MOBA_HEADS // BIAS_HEADS_PER_STEP
    c_in, c_out, c_shapes = _cast_specs(cast_weights, n_steps, lambda i: i)
    outs = pl.pallas_call(
        functools.partial(_bias_kernel, far_bucket=far_bucket),
        grid=(n_steps,),
        in_specs=[
            pl.BlockSpec((1, 2 * L), lambda i: (0, 0)),
            pl.BlockSpec(memory_space=pltpu.SMEM),
        ] + c_in,
        out_specs=[pl.BlockSpec((BIAS_HEADS_PER_STEP, 2, L, L), lambda i: (i, 0, 0, 0)),
                   pl.BlockSpec((BIAS_HEADS_PER_STEP, STAT_ROWS, LANES), lambda i: (i, 0, 0))] + c_out,
        out_shape=[jax.ShapeDtypeStruct((MOBA_HEADS, 2, L, L), F32),
                   jax.ShapeDtypeStruct((MOBA_HEADS, STAT_ROWS, LANES), F32)] + c_shapes,
        compiler_params=pltpu.CompilerParams(
            dimension_semantics=("arbitrary",), vmem_limit_bytes=VMEM_LIMIT),
        name="bias_tiles",
    )(jnp.asarray(bucket), rel_bias, *cast_weights)
    return outs[0], outs[1], outs[2:]


def _inproj_kernel(x_ref, g_ref, w_ref, lng_ref, lnb_ref, wsp_ref, bsp_ref, gn_ref,
                   qkv_ref, ya_ref, norm_ref, uz_scr):
    C = GMLP_CHUNK
    tm = x_ref.shape[0]
    n_qkv = 3 * D_MOBA
    x = x_ref[...]
    ms = jnp.mean(x * x, axis=-1, keepdims=True)
    h = (x * lax.rsqrt(ms + EPS) * g_ref[...]).astype(BF16)
    uz_scr[...] = jnp.dot(h, w_ref[:, n_qkv:], preferred_element_type=F32)

    t_ix = lax.broadcasted_iota(jnp.int32, (C, C), 0)
    s_ix = lax.broadcasted_iota(jnp.int32, (C, C), 1)
    w_sp = [jnp.where(t_ix >= s_ix, wsp_ref[g], 0.0).astype(BF16) for g in range(GMLP_GROUPS)]
    first = lax.broadcasted_iota(jnp.int32, (1, LANES), 1) < GMLP_GROUP_DIM
    b_t = bsp_ref[...].T
    lane_group = lax.broadcasted_iota(jnp.int32, (1, D_GMLP), 1) // GMLP_GROUP_DIM
    b_sp = jnp.zeros((C, D_GMLP), F32)
    for g in range(GMLP_GROUPS):
        b_sp = jnp.where(lane_group == g, b_t[:, g:g + 1], b_sp)

    def qkv_cols(c0, c1):
        acc = jnp.dot(h, w_ref[:, c0:c1], preferred_element_type=F32)
        if c1 <= D_MOBA:
            acc = acc * Q_SCALE
        qkv_ref[:, c0:c1] = acc.astype(BF16)
        if c1 <= 2 * D_MOBA:
            sq = acc * acc
            for c in range(c0, c1, MOBA_HEAD_DIM):
                lanes = (c - c0) // LANES * LANES
                own = first if (c - c0) % LANES == 0 else jnp.logical_not(first)
                rows_sq = jnp.sum(jnp.where(own, sq[:, lanes:lanes + LANES], 0.0), axis=-1, keepdims=True)
                top = jnp.max(rows_sq, axis=0, keepdims=True)
                norm_ref[0, c // MOBA_HEAD_DIM:c // MOBA_HEAD_DIM + 1, :] = jnp.broadcast_to(top, (1, LANES))

    def gmlp_rows(c):
        rows = slice(c * C, (c + 1) * C)
        z = jax.nn.gelu(uz_scr[rows, D_GMLP:])
        mu = jnp.mean(z, axis=-1, keepdims=True)
        zc = z - mu
        var = jnp.mean(zc * zc, axis=-1, keepdims=True)
        zn = (zc * lax.rsqrt(var + EPS) * lng_ref[...] + lnb_ref[...]).astype(BF16)
        cols = []
        for gp in range(D_GMLP // LANES):
            zp = zn[:, gp * LANES:(gp + 1) * LANES]
            r0 = jnp.dot(w_sp[2 * gp], zp, preferred_element_type=F32)
            r1 = jnp.dot(w_sp[2 * gp + 1], zp, preferred_element_type=F32)
            cols.append(jnp.where(first, r0, r1))
        s = jnp.concatenate(cols, axis=1) + b_sp
        y = jax.nn.gelu(uz_scr[rows, :D_GMLP]) * s
        ms_y = jnp.mean(y * y, axis=-1, keepdims=True)
        ya_ref[rows, :] = (y * lax.rsqrt(ms_y + EPS) * gn_ref[...]).astype(BF16)

    col_blocks = [(c, c + 2 * LANES) for c in range(0, n_qkv, 2 * LANES)]
    n_chunks = tm // C
    for step in range(max(len(col_blocks), n_chunks)):
        if step < len(col_blocks):
            qkv_cols(*col_blocks[step])
        if step < n_chunks:
            gmlp_rows(step)


def _inproj(x2, g, w_bf, ln_g, ln_b, w_sp, b_sp, gn, tm):
    T, D = x2.shape
    n_all = w_bf.shape[1]
    n_qkv = 3 * D_MOBA
    C = GMLP_CHUNK
    vec = pl.BlockSpec((1, D_GMLP), lambda i: (0, 0))
    return pl.pallas_call(
        _inproj_kernel,
        grid=(T // tm,),
        in_specs=[
            pl.BlockSpec((tm, D), lambda i: (i, 0)),
            pl.BlockSpec((1, D), lambda i: (0, 0)),
            pl.BlockSpec((D, n_all), lambda i: (0, 0), pipeline_mode=pl.Buffered(1)),
            vec, vec,
            pl.BlockSpec((GMLP_GROUPS, C, C), lambda i: (0, 0, 0)),
            pl.BlockSpec((GMLP_GROUPS, C), lambda i: (0, 0)),
            vec,
        ],
        out_specs=[
            pl.BlockSpec((tm, n_qkv), lambda i: (i, 0)),
            pl.BlockSpec((tm, D_GMLP), lambda i: (i, 0)),
            pl.BlockSpec((1, 2 * MOBA_HEADS, LANES), lambda i: (i, 0, 0)),
        ],
        out_shape=[
            jax.ShapeDtypeStruct((T, n_qkv), BF16),
            jax.ShapeDtypeStruct((T, D_GMLP), BF16),
            jax.ShapeDtypeStruct((T // tm, 2 * MOBA_HEADS, LANES), F32),
        ],
        scratch_shapes=[pltpu.VMEM((tm, n_all - n_qkv), F32)],
        compiler_params=pltpu.CompilerParams(
            dimension_semantics=("arbitrary",), vmem_limit_bytes=VMEM_LIMIT),
        name="inproj_gmlp",
    )(x2, g, w_bf, ln_g, ln_b, w_sp, b_sp, gn)


def _moba_kernel(q_ref, k_ref, v_ref, norm_ref, bias_ref, stat_ref, *refs, n_cast):
    cast_src, o_ref, cast_dst = refs[:n_cast], refs[n_cast], refs[n_cast + 1:2 * n_cast + 1]
    vt_scr, qt_scr, slots = refs[2 * n_cast + 1], refs[2 * n_cast + 2], refs[2 * n_cast + 3:]
    t_scr, p_scr = slots[:T_SLOTS], slots[T_SLOTS:]
    L = MOBA_BLOCK
    S = q_ref.shape[1]
    NB = S // L
    hp = pl.program_id(0)
    lane = lax.broadcasted_iota(jnp.int32, (1, LANES), 1)
    head_lanes = [(lane >= hh * MOBA_HEAD_DIM) & (lane < (hh + 1) * MOBA_HEAD_DIM)
                  for hh in range(HEADS_PER_STEP)]
    row = lax.broadcasted_iota(jnp.int32, (NB, L), 0)
    bodies = [(i, hh) for i in range(NB) for hh in range(HEADS_PER_STEP)]
    nb = len(bodies)

    cfar = [stat_ref[hh, BIAS_FAR:BIAS_FAR + 1, :1] for hh in range(HEADS_PER_STEP)]
    bmax = [stat_ref[hh, BIAS_MAX:BIAS_MAX + 1, :1] for hh in range(HEADS_PER_STEP)]
    bmin = [stat_ref[hh, BIAS_MIN:BIAS_MIN + 1, :1] for hh in range(HEADS_PER_STEP)]

    def stage_operands():
        vt = v_ref[0].astype(F32).T.astype(BF16)
        for hh in range(HEADS_PER_STEP):
            vt_scr[hh * VT_ROWS:hh * VT_ROWS + MOBA_HEAD_DIM, :] = (
                vt[hh * MOBA_HEAD_DIM:(hh + 1) * MOBA_HEAD_DIM])
            vt_scr[hh * VT_ROWS + MOBA_HEAD_DIM:(hh + 1) * VT_ROWS, :] = jnp.ones((ONES_ROWS, S), BF16)
        qt = q_ref[0].astype(F32).T.astype(BF16)
        dim_head = lax.broadcasted_iota(jnp.int32, (LANES, 1), 0) // MOBA_HEAD_DIM
        for hh in range(HEADS_PER_STEP):
            qt_scr[hh * LANES:(hh + 1) * LANES, :] = jnp.where(dim_head == hh, qt, jnp.zeros_like(qt))
        km = jnp.concatenate(
            [jnp.sum(k_ref[0, n * L:(n + 1) * L, :].astype(F32), axis=0, keepdims=True) * (1.0 / L)
             for n in range(NB)], axis=0)
        kmm = jnp.concatenate([jnp.where(head_lanes[hh], km, 0.0) for hh in range(HEADS_PER_STEP)],
                              axis=0)
        km_hi = kmm.astype(BF16)
        return km_hi, (kmm - km_hi.astype(F32)).astype(BF16)

    shift, span = [], []
    for hh in range(HEADS_PER_STEP):
        head = hp * HEADS_PER_STEP + hh
        q_sq = jnp.max(norm_ref[:, pl.ds(head, 1), :], axis=0)[:, :1]
        k_sq = jnp.max(norm_ref[:, pl.ds(MOBA_HEADS + head, 1), :], axis=0)[:, :1]
        reach = jnp.sqrt(q_sq * k_sq) * BOUND_SLACK
        shift.append(reach + bmax[hh])
        span.append(2.0 * reach + (bmax[hh] - bmin[hh]))
    safe = jnp.max(functools.reduce(jnp.maximum, span)) <= SAFE_SPAN

    def block_select(k_means, n):
        i, hh = bodies[n]
        q_t = masked_q(n)
        gate = (jnp.dot(k_means[0], q_t, preferred_element_type=F32)
                + jnp.dot(k_means[1], q_t, preferred_element_type=F32))
        g = gate[hh * NB:(hh + 1) * NB]
        cnt = jnp.zeros((NB, L), jnp.int32)
        for m in range(i):
            gm = g[m:m + 1, :]
            beats = (gm > g) | ((gm == g) & (m < row))
            cnt = cnt + jnp.where(beats, 1, 0)
        return cnt < MOBA_TOPK

    def masked_q(n):
        i, hh = bodies[n]
        return qt_scr[hh * LANES:(hh + 1) * LANES, i * L:(i + 1) * L]

    def pv_stage(outs, n, p_ref):
        i, hh = bodies[n]
        nk = (i + 1) * L
        acc = jnp.dot(vt_scr[hh * VT_ROWS:(hh + 1) * VT_ROWS, 0:nk], p_ref[0:nk, :],
                      preferred_element_type=F32)
        outs[n] = acc[:MOBA_HEAD_DIM] / acc[MOBA_HEAD_DIM:MOBA_HEAD_DIM + 1]
        if hh == HEADS_PER_STEP - 1:
            pair = [outs.pop(n - HEADS_PER_STEP + 1 + h) for h in range(HEADS_PER_STEP)]
            o_ref[0, i * L:(i + 1) * L, :] = jnp.concatenate(pair, axis=0).T

    def exact_path():
        masks, qms, maxes, mrows, outs = {}, {}, {}, {}, {}
        _cast_slabs(cast_src, cast_dst)
        k_means = stage_operands()

        def begin_body(n):
            i, hh = bodies[n]
            qms[n] = masked_q(n)
            if i > MOBA_TOPK:
                sel = block_select(k_means, n)
                masks[n] = (jnp.where(sel, cfar[hh], NEG),
                            jnp.where(sel, 0.0, NEG))

        def score_tile(n, j):
            i, hh = bodies[n]
            t = jnp.dot(k_ref[0, j * L:(j + 1) * L, :], qms[n],
                        preferred_element_type=F32)
            if j == i:
                t = t + bias_ref[hh, 0]
            elif j == i - 1:
                t = t + bias_ref[hh, 1]
                if i > MOBA_TOPK:
                    t = t + masks[n][1][j:j + 1, :]
            elif i > MOBA_TOPK:
                t = t + masks[n][0][j:j + 1, :]
            else:
                t = t + cfar[hh]
            t_scr[n % T_SLOTS][j * L:(j + 1) * L, :] = t
            tm = jnp.max(t.reshape(L // 8, 8, L), axis=0)
            maxes[n] = tm if j == 0 else jnp.maximum(maxes[n], tm)
            if j == i:
                mrows[n] = jnp.max(maxes.pop(n), axis=0, keepdims=True)
                qms.pop(n)
                masks.pop(n, None)

        def exp_tile(n, j):
            t_ref, p_ref = t_scr[n % T_SLOTS], p_scr[n % T_SLOTS]
            p_ref[j * L:(j + 1) * L, :] = jnp.exp2(t_ref[j * L:(j + 1) * L, :] - mrows[n]).astype(BF16)

        for n0 in range(min(LOOKAHEAD, nb)):
            begin_body(n0)
            for j in range(bodies[n0][0] + 1):
                score_tile(n0, j)
        for n in range(nb):
            n_exp = bodies[n][0] + 1
            ahead = n + LOOKAHEAD
            n_score = bodies[ahead][0] + 1 if ahead < nb else 0
            if n_score:
                begin_body(ahead)
            for j in range(max(n_exp, n_score)):
                if j < n_score:
                    score_tile(ahead, j)
                if j < n_exp:
                    exp_tile(n, j)
            mrows.pop(n)
            pv_stage(outs, n, p_scr[n % T_SLOTS])

    def fast_path():
        outs = {}
        _cast_slabs(cast_src, cast_dst)
        k_means = stage_operands()
        shifted = {(hh, d): bias_ref[hh, d] - shift[hh]
                   for hh in range(HEADS_PER_STEP) for d in range(2)}

        def prob_tiles(n):
            i, hh = bodies[n]
            qm = masked_q(n)
            s_row = shift[hh]
            if i > MOBA_TOPK:
                sel = block_select(k_means, n)
                m_far = jnp.where(sel, cfar[hh], NEG) - s_row
                m_prev = jnp.where(sel, 0.0, NEG)
            for j in range(i + 1):
                e = jnp.dot(k_ref[0, j * L:(j + 1) * L, :], qm,
                            preferred_element_type=F32)
                if j == i:
                    e = e + shifted[hh, 0]
                elif j == i - 1:
                    e = e + shifted[hh, 1]
                    if i > MOBA_TOPK:
                        e = e + m_prev[j:j + 1, :]
                else:
                    e = e + (m_far[j:j + 1, :] if i > MOBA_TOPK else cfar[hh] - s_row)
                p_scr[n % P_SLOTS][j * L:(j + 1) * L, :] = jnp.exp2(e).astype(BF16)

        for n0 in range(min(FAST_LOOKAHEAD, nb)):
            prob_tiles(n0)
        for n in range(nb):
            if n + FAST_LOOKAHEAD < nb:
                prob_tiles(n + FAST_LOOKAHEAD)
            pv_stage(outs, n, p_scr[n % P_SLOTS])

    pl.when(safe)(fast_path)
    pl.when(jnp.logical_not(safe))(exact_path)


def _moba(qkv3, norms, bias_t, bias_stats, cast_weights):
    B, S, _ = qkv3.shape
    L = MOBA_BLOCK
    n_hp = MOBA_HEADS // HEADS_PER_STEP
    blk = lambda off: pl.BlockSpec((1, S, LANES), lambda hp, b: (b, 0, off + hp))
    c_in, c_out, c_shapes = _cast_specs(cast_weights, B * n_hp, lambda hp, b: hp * B + b)
    outs = pl.pallas_call(
        functools.partial(_moba_kernel, n_cast=len(cast_weights)),
        grid=(n_hp, B),
        in_specs=[
            blk(0), blk(n_hp), blk(2 * n_hp),
            pl.BlockSpec((norms.shape[0] // B,) + norms.shape[1:], lambda hp, b: (b, 0, 0)),
            pl.BlockSpec((HEADS_PER_STEP, 2, L, L), lambda hp, b: (hp, 0, 0, 0)),
            pl.BlockSpec((HEADS_PER_STEP, STAT_ROWS, LANES), lambda hp, b: (hp, 0, 0)),
        ] + c_in,
        out_specs=[pl.BlockSpec((1, S, LANES), lambda hp, b: (b, 0, hp))] + c_out,
        out_shape=[jax.ShapeDtypeStruct((B, S, D_MOBA), F32)] + c_shapes,
        scratch_shapes=[
            pltpu.VMEM((HEADS_PER_STEP * VT_ROWS, S), BF16),
            pltpu.VMEM((HEADS_PER_STEP * LANES, S), BF16),
        ] + [pltpu.VMEM((S, L), F32)] * T_SLOTS + [pltpu.VMEM((S, L), BF16)] * P_SLOTS,
        compiler_params=pltpu.CompilerParams(
            dimension_semantics=("arbitrary", "arbitrary"), vmem_limit_bytes=VMEM_LIMIT),
        name="moba",
    )(qkv3, qkv3, qkv3, norms, bias_t, bias_stats, *cast_weights)
    return outs[0], outs[1:]


def _ffn_kernel(x_ref, yb_ref, ya_ref, gb_ref, wo_ref, gf_ref, wg_ref, wu_ref, wd_ref, gl_ref,
                o_ref, *, final_norm):
    half = x_ref.shape[0] // 2
    st = {}

    def attn_norm(r):
        yb = yb_ref[r:r + half, :]
        ms = jnp.mean(yb * yb, axis=-1, keepdims=True)
        st["ybn", r] = (yb * lax.rsqrt(ms + EPS) * gb_ref[...]).astype(BF16)

    def out_proj(r):
        st["x1", r] = (x_ref[r:r + half, :]
                       + jnp.dot(st.pop(("ybn", r)), wo_ref[:D_MOBA, :], preferred_element_type=F32)
                       + jnp.dot(ya_ref[r:r + half, :], wo_ref[D_MOBA:, :],
                                 preferred_element_type=F32))

    def ffn_norm(r):
        x1 = st["x1", r]
        ms = jnp.mean(x1 * x1, axis=-1, keepdims=True)
        st["h", r] = (x1 * lax.rsqrt(ms + EPS) * gf_ref[...]).astype(BF16)

    def gate_up(r):
        h = st.pop(("h", r))
        g = jnp.dot(h, wg_ref[...], preferred_element_type=F32)
        u = jnp.dot(h, wu_ref[...], preferred_element_type=F32)
        st["a", r] = (jax.nn.silu(g) * u).astype(BF16)

    def down(r):
        x2 = st.pop(("x1", r)) + jnp.dot(st.pop(("a", r)), wd_ref[...], preferred_element_type=F32)
        if final_norm:
            ms = jnp.mean(x2 * x2, axis=-1, keepdims=True)
            x2 = x2 * lax.rsqrt(ms + EPS) * gl_ref[...]
        o_ref[r:r + half, :] = x2

    stages = [attn_norm, out_proj, ffn_norm, gate_up, down]
    for k in range(len(stages) + 1):
        if k < len(stages):
            stages[k](0)
        if k >= 1:
            stages[k - 1](half)


def _ffn(x2, yb, ya, gb, wo, gf, wg, wu, wd, gl, tm, final_norm):
    T, D = x2.shape
    d_ff = wg.shape[1]
    const = lambda shape: pl.BlockSpec(shape, lambda i: (0, 0), pipeline_mode=pl.Buffered(1))
    return pl.pallas_call(
        functools.partial(_ffn_kernel, final_norm=final_norm),
        grid=(T // tm,),
        in_specs=[
            pl.BlockSpec((tm, D), lambda i: (i, 0)),
            pl.BlockSpec((tm, D_MOBA), lambda i: (i, 0)),
            pl.BlockSpec((tm, D_GMLP), lambda i: (i, 0)),
            pl.BlockSpec((1, D_MOBA), lambda i: (0, 0)),
            const((D_MOBA + D_GMLP, D)),
            pl.BlockSpec((1, D), lambda i: (0, 0)),
            const((D, d_ff)), const((D, d_ff)), const((d_ff, D)),
            pl.BlockSpec((1, D), lambda i: (0, 0)),
        ],
        out_specs=pl.BlockSpec((tm, D), lambda i: (i, 0)),
        out_shape=jax.ShapeDtypeStruct((T, D), F32),
        compiler_params=pltpu.CompilerParams(
            dimension_semantics=("arbitrary",), vmem_limit_bytes=VMEM_LIMIT),
        name="outproj_ffn",
    )(x2, yb, ya, gb, wo, gf, wg, wu, wd, gl)


def kernel(x, rel_bias, norm_mix, w_in, gmlp_ln_g, gmlp_ln_b, w_spatial, b_spatial, out_norm_b,
           out_norm_a, w_out, norm_ffn, w_gate, w_up, w_down, norm_final):
    B, S, D = x.shape
    depth = w_in.shape[0]
    assert S % MOBA_BLOCK == 0 and S // MOBA_BLOCK > MOBA_TOPK
    far = _t5_bucket_np(np.arange(MOBA_BLOCK + 1, S + MOBA_BLOCK))
    assert (far == far[0]).all()
    far_bucket = int(far[0])

    assert depth == 1, "weight casts ride along the single layer's kernels"
    T = B * S
    x2 = x.reshape(T, D)
    row = lambda v: v.reshape(1, -1)
    bias_t, bias_stats, (w_in_bf,) = _bias_tiles(rel_bias, far_bucket, [w_in[0]])
    qkv, ya, norms = _inproj(x2, row(norm_mix[0]), w_in_bf, row(gmlp_ln_g[0]), row(gmlp_ln_b[0]),
                             w_spatial[0], b_spatial[0], row(out_norm_a[0]), PROJ_ROWS)
    yb, (wo_bf, wg_bf, wu_bf, wd_bf) = _moba(qkv.reshape(B, S, -1), norms, bias_t, bias_stats,
                                             [w_out[0], w_gate[0], w_up[0], w_down[0]])
    out = _ffn(x2, yb.reshape(T, D_MOBA), ya, row(out_norm_b[0]), wo_bf, row(norm_ffn[0]),
               wg_bf, wu_bf, wd_bf, row(norm_final), FFN_ROWS, True)
    return out.reshape(B, S, D)
```

```python
import functools
import math

import numpy as np
import jax
import jax.numpy as jnp
from jax import lax
from jax.experimental import pallas as pl
from jax.experimental.pallas import tpu as pltpu

F32 = jnp.float32
BF16 = jnp.bfloat16

MOBA_HEADS = 8
MOBA_HEAD_DIM = 64
D_MOBA = MOBA_HEADS * MOBA_HEAD_DIM
GMLP_GROUPS = 8
GMLP_GROUP_DIM = 64
D_GMLP = GMLP_GROUPS * GMLP_GROUP_DIM
MOBA_BLOCK = 256
MOBA_TOPK = 3
GMLP_CHUNK = 128
NUM_BUCKETS = 32
REL_MAX_DISTANCE = 128
EPS = 1e-6
NEG = -1e9

LANES = 128
HEADS_PER_STEP = LANES // MOBA_HEAD_DIM
BF16_SUBLANES = 16
ONES_ROWS = BF16_SUBLANES
VT_ROWS = MOBA_HEAD_DIM + ONES_ROWS
VMEM_LIMIT = 56 * 1024 * 1024
PROJ_ROWS = 1024
FFN_ROWS = 512
T_SLOTS = 4
P_SLOTS = 4
LOOKAHEAD = 3
FAST_LOOKAHEAD = 1

LOG2E = math.log2(math.e)
Q_SCALE = MOBA_HEAD_DIM ** -0.5 * LOG2E

BOUND_SLACK = 1.0 + 2.0 ** -6
SAFE_SPAN = 100.0


def _t5_bucket_np(dist):
    n = np.maximum(dist, 0)
    max_exact = NUM_BUCKETS // 2
    nf = np.maximum(n, max_exact).astype(np.float32)
    ratio = np.log(nf / np.float32(max_exact)) / np.float32(math.log(REL_MAX_DISTANCE / max_exact))
    large = max_exact + (ratio * np.float32(NUM_BUCKETS - max_exact)).astype(np.int32)
    large = np.minimum(large, NUM_BUCKETS - 1)
    return np.where(n < max_exact, n, large).astype(np.int32)


def _cast_specs(weights, n_steps, step_of):
    in_specs, out_specs, out_shapes = [], [], []
    for w in weights:
        rows, cols = w.shape
        period = next(p for p in range(1, n_steps + 1)
                      if n_steps % p == 0 and (rows * p) % (n_steps * BF16_SUBLANES) == 0)
        spec = pl.BlockSpec((rows * period // n_steps, cols),
                            functools.partial(lambda p, *ids: (step_of(*ids) // p, 0), period))
        in_specs.append(spec)
        out_specs.append(spec)
        out_shapes.append(jax.ShapeDtypeStruct(w.shape, BF16))
    return in_specs, out_specs, out_shapes


def _cast_slabs(src_refs, dst_refs):
    for src, dst in zip(src_refs, dst_refs):
        dst[...] = src[...].astype(BF16)


BIAS_FAR, BIAS_MAX, BIAS_MIN = 0, 1, 2
STAT_ROWS = 8
F32_SUBLANES = 8
BIAS_HEADS_PER_STEP = 4


def _bias_kernel(bucket_ref, rel_ref, *refs, far_bucket):
    n_cast = (len(refs) - 2) // 2
    out_ref, stat_ref = refs[n_cast], refs[n_cast + 1]
    _cast_slabs(refs[:n_cast], refs[n_cast + 2:])
    heads_here, L = out_ref.shape[0], out_ref.shape[2]
    span = bucket_ref.shape[1]
    bucket = bucket_ref[...]
    lane = lax.broadcasted_iota(jnp.int32, (1, span), 1)
    for hh in range(heads_here):
        h = pl.program_id(0) * heads_here + hh
        vals = [rel_ref[b, h] * LOG2E for b in range(NUM_BUCKETS)]
        prof = jnp.zeros((1, span), F32)
        for b in range(NUM_BUCKETS):
            prof = jnp.where(bucket == b, vals[b], prof)
        own = jnp.broadcast_to(jnp.where(lane < L, prof, NEG), (F32_SUBLANES, span))
        prev = jnp.broadcast_to(prof, (F32_SUBLANES, span))
        for g in range(L // F32_SUBLANES):
            k0 = g * F32_SUBLANES
            rows = slice(k0, k0 + F32_SUBLANES)
            out_ref[hh, 0, rows, :] = pltpu.roll(own, k0, 1, stride=1, stride_axis=0)[:, :L]
            out_ref[hh, 1, rows, :] = pltpu.roll(prev, (k0 + L) % span, 1, stride=1,
                                                 stride_axis=0)[:, :L]
        stats = {BIAS_FAR: vals[far_bucket], BIAS_MAX: functools.reduce(jnp.maximum, vals),
                 BIAS_MIN: functools.reduce(jnp.minimum, vals)}
        for r in range(STAT_ROWS):
            stat_ref[hh, r:r + 1, :] = jnp.full((1, LANES), stats.get(r, 0.0), F32)


def _bias_tiles(rel_bias, far_bucket, cast_weights):
    L = MOBA_BLOCK
    bucket = _t5_bucket_np(np.arange(2 * L))[None, :]
    n_steps = MOBA_HEADS // BIAS_HEADS_PER_STEP
    c_in, c_out, c_shapes = _cast_specs(cast_weights, n_steps, lambda i: i)
    outs = pl.pallas_call(
        functools.partial(_bias_kernel, far_bucket=far_bucket),
        grid=(n_steps,),
        in_specs=[
            pl.BlockSpec((1, 2 * L), lambda i: (0, 0)),
            pl.BlockSpec(memory_space=pltpu.SMEM),
        ] + c_in,
        out_specs=[pl.BlockSpec((BIAS_HEADS_PER_STEP, 2, L, L), lambda i: (i, 0, 0, 0)),
                   pl.BlockSpec((BIAS_HEADS_PER_STEP, STAT_ROWS, LANES), lambda i: (i, 0, 0))] + c_out,
        out_shape=[jax.ShapeDtypeStruct((MOBA_HEADS, 2, L, L), F32),
                   jax.ShapeDtypeStruct((MOBA_HEADS, STAT_ROWS, LANES), F32)] + c_shapes,
        compiler_params=pltpu.CompilerParams(
            dimension_semantics=("arbitrary",), vmem_limit_bytes=VMEM_LIMIT),
        name="bias_tiles",
    )(jnp.asarray(bucket), rel_bias, *cast_weights)
    return outs[0], outs[1], outs[2:]


def _inproj_kernel(x_ref, g_ref, w_ref, lng_ref, lnb_ref, wsp_ref, bsp_ref, gn_ref,
                   qkv_ref, ya_ref, norm_ref, uz_scr):
    C = GMLP_CHUNK
    tm = x_ref.shape[0]
    n_qkv = 3 * D_MOBA
    x = x_ref[...]
    ms = jnp.mean(x * x, axis=-1, keepdims=True)
    h = (x * lax.rsqrt(ms + EPS) * g_ref[...]).astype(BF16)
    uz_scr[...] = jnp.dot(h, w_ref[:, n_qkv:], preferred_element_type=F32)

    t_ix = lax.broadcasted_iota(jnp.int32, (C, C), 0)
    s_ix = lax.broadcasted_iota(jnp.int32, (C, C), 1)
    w_sp = [jnp.where(t_ix >= s_ix, wsp_ref[g], 0.0).astype(BF16) for g in range(GMLP_GROUPS)]
    first = lax.broadcasted_iota(jnp.int32, (1, LANES), 1) < GMLP_GROUP_DIM
    b_t = bsp_ref[...].T
    lane_group = lax.broadcasted_iota(jnp.int32, (1, D_GMLP), 1) // GMLP_GROUP_DIM
    b_sp = jnp.zeros((C, D_GMLP), F32)
    for g in range(GMLP_GROUPS):
        b_sp = jnp.where(lane_group == g, b_t[:, g:g + 1], b_sp)

    def qkv_cols(c0, c1):
        acc = jnp.dot(h, w_ref[:, c0:c1], preferred_element_type=F32)
        if c1 <= D_MOBA:
            acc = acc * Q_SCALE
        qkv_ref[:, c0:c1] = acc.astype(BF16)
        if c1 <= 2 * D_MOBA:
            sq = acc * acc
            for c in range(c0, c1, MOBA_HEAD_DIM):
                lanes = (c - c0) // LANES * LANES
                own = first if (c - c0) % LANES == 0 else jnp.logical_not(first)
                rows_sq = jnp.sum(jnp.where(own, sq[:, lanes:lanes + LANES], 0.0), axis=-1, keepdims=True)
                top = jnp.max(rows_sq, axis=0, keepdims=True)
                norm_ref[0, c // MOBA_HEAD_DIM:c // MOBA_HEAD_DIM + 1, :] = jnp.broadcast_to(top, (1, LANES))

    def gmlp_rows(c):
        rows = slice(c * C, (c + 1) * C)
        z = jax.nn.gelu(uz_scr[rows, D_GMLP:])
        mu = jnp.mean(z, axis=-1, keepdims=True)
        zc = z - mu
        var = jnp.mean(zc * zc, axis=-1, keepdims=True)
        zn = (zc * lax.rsqrt(var + EPS) * lng_ref[...] + lnb_ref[...]).astype(BF16)
        cols = []
        for gp in range(D_GMLP // LANES):
            zp = zn[:, gp * LANES:(gp + 1) * LANES]
            r0 = jnp.dot(w_sp[2 * gp], zp, preferred_element_type=F32)
            r1 = jnp.dot(w_sp[2 * gp + 1], zp, preferred_element_type=F32)
            cols.append(jnp.where(first, r0, r1))
        s = jnp.concatenate(cols, axis=1) + b_sp
        y = jax.nn.gelu(uz_scr[rows, :D_GMLP]) * s
        ms_y = jnp.mean(y * y, axis=-1, keepdims=True)
        ya_ref[rows, :] = (y * lax.rsqrt(ms_y + EPS) * gn_ref[...]).astype(BF16)

    col_blocks = [(c, c + 2 * LANES) for c in range(0, n_qkv, 2 * LANES)]
    n_chunks = tm // C
    for step in range(max(len(col_blocks), n_chunks)):
        if step < len(col_blocks):
            qkv_cols(*col_blocks[step])
        if step < n_chunks:
            gmlp_rows(step)


def _inproj(x2, g, w_bf, ln_g, ln_b, w_sp, b_sp, gn, tm):
    T, D = x2.shape
    n_all = w_bf.shape[1]
    n_qkv = 3 * D_MOBA
    C = GMLP_CHUNK
    vec = pl.BlockSpec((1, D_GMLP), lambda i: (0, 0))
    return pl.pallas_call(
        _inproj_kernel,
        grid=(T // tm,),
        in_specs=[
            pl.BlockSpec((tm, D), lambda i: (i, 0)),
            pl.BlockSpec((1, D), lambda i: (0, 0)),
            pl.BlockSpec((D, n_all), lambda i: (0, 0), pipeline_mode=pl.Buffered(1)),
            vec, vec,
            pl.BlockSpec((GMLP_GROUPS, C, C), lambda i: (0, 0, 0)),
            pl.BlockSpec((GMLP_GROUPS, C), lambda i: (0, 0)),
            vec,
        ],
        out_specs=[
            pl.BlockSpec((tm, n_qkv), lambda i: (i, 0)),
            pl.BlockSpec((tm, D_GMLP), lambda i: (i, 0)),
            pl.BlockSpec((1, 2 * MOBA_HEADS, LANES), lambda i: (i, 0, 0)),
        ],
        out_shape=[
            jax.ShapeDtypeStruct((T, n_qkv), BF16),
            jax.ShapeDtypeStruct((T, D_GMLP), BF16),
            jax.ShapeDtypeStruct((T // tm, 2 * MOBA_HEADS, LANES), F32),
        ],
        scratch_shapes=[pltpu.VMEM((tm, n_all - n_qkv), F32)],
        compiler_params=pltpu.CompilerParams(
            dimension_semantics=("arbitrary",), vmem_limit_bytes=VMEM_LIMIT),
        name="inproj_gmlp",
    )(x2, g, w_bf, ln_g, ln_b, w_sp, b_sp, gn)


def _moba_kernel(q_ref, k_ref, v_ref, norm_ref, bias_ref, stat_ref, *refs, n_cast):
    cast_src, o_ref, cast_dst = refs[:n_cast], refs[n_cast], refs[n_cast + 1:2 * n_cast + 1]
    vt_scr, qt_scr, slots = refs[2 * n_cast + 1], refs[2 * n_cast + 2], refs[2 * n_cast + 3:]
    t_scr, p_scr = slots[:T_SLOTS], slots[T_SLOTS:]
    L = MOBA_BLOCK
    S = q_ref.shape[1]
    NB = S // L
    hp = pl.program_id(0)
    lane = lax.broadcasted_iota(jnp.int32, (1, LANES), 1)
    head_lanes = [(lane >= hh * MOBA_HEAD_DIM) & (lane < (hh + 1) * MOBA_HEAD_DIM)
                  for hh in range(HEADS_PER_STEP)]
    row = lax.broadcasted_iota(jnp.int32, (NB, L), 0)
    bodies = [(i, hh) for i in range(NB) for hh in range(HEADS_PER_STEP)]
    nb = len(bodies)

    cfar = [stat_ref[hh, BIAS_FAR:BIAS_FAR + 1, :1] for hh in range(HEADS_PER_STEP)]
    bmax = [stat_ref[hh, BIAS_MAX:BIAS_MAX + 1, :1] for hh in range(HEADS_PER_STEP)]
    bmin = [stat_ref[hh, BIAS_MIN:BIAS_MIN + 1, :1] for hh in range(HEADS_PER_STEP)]

    def stage_operands():
        vt = v_ref[0].astype(F32).T.astype(BF16)
        for hh in range(HEADS_PER_STEP):
            vt_scr[hh * VT_ROWS:hh * VT_ROWS + MOBA_HEAD_DIM, :] = (
                vt[hh * MOBA_HEAD_DIM:(hh + 1) * MOBA_HEAD_DIM])
            vt_scr[hh * VT_ROWS + MOBA_HEAD_DIM:(hh + 1) * VT_ROWS, :] = jnp.ones((ONES_ROWS, S), BF16)
        qt = q_ref[0].astype(F32).T.astype(BF16)
        dim_head = lax.broadcasted_iota(jnp.int32, (LANES, 1), 0) // MOBA_HEAD_DIM
        for hh in range(HEADS_PER_STEP):
            qt_scr[hh * LANES:(hh + 1) * LANES, :] = jnp.where(dim_head == hh, qt, jnp.zeros_like(qt))
        km = jnp.concatenate(
            [jnp.sum(k_ref[0, n * L:(n + 1) * L, :].astype(F32), axis=0, keepdims=True) * (1.0 / L)
             for n in range(NB)], axis=0)
        kmm = jnp.concatenate([jnp.where(head_lanes[hh], km, 0.0) for hh in range(HEADS_PER_STEP)],
                              axis=0)
        km_hi = kmm.astype(BF16)
        return km_hi, (kmm - km_hi.astype(F32)).astype(BF16)

    shift, span = [], []
    for hh in range(HEADS_PER_STEP):
        head = hp * HEADS_PER_STEP + hh
        q_sq = jnp.max(norm_ref[:, pl.ds(head, 1), :], axis=0)[:, :1]
        k_sq = jnp.max(norm_ref[:, pl.ds(MOBA_HEADS + head, 1), :], axis=0)[:, :1]
        reach = jnp.sqrt(q_sq * k_sq) * BOUND_SLACK
        shift.append(reach + bmax[hh])
        span.append(2.0 * reach + (bmax[hh] - bmin[hh]))
    safe = jnp.max(functools.reduce(jnp.maximum, span)) <= SAFE_SPAN

    def block_select(k_means, n):
        i, hh = bodies[n]
        q_t = masked_q(n)
        gate = (jnp.dot(k_means[0], q_t, preferred_element_type=F32)
                + jnp.dot(k_means[1], q_t, preferred_element_type=F32))
        g = gate[hh * NB:(hh + 1) * NB]
        cnt = jnp.zeros((NB, L), jnp.int32)
        for m in range(i):
            gm = g[m:m + 1, :]
            beats = (gm > g) | ((gm == g) & (m < row))
            cnt = cnt + jnp.where(beats, 1, 0)
        return cnt < MOBA_TOPK

    def masked_q(n):
        i, hh = bodies[n]
        return qt_scr[hh * LANES:(hh + 1) * LANES, i * L:(i + 1) * L]

    def pv_stage(outs, n, p_ref):
        i, hh = bodies[n]
        nk = (i + 1) * L
        acc = jnp.dot(vt_scr[hh * VT_ROWS:(hh + 1) * VT_ROWS, 0:nk], p_ref[0:nk, :],
                      preferred_element_type=F32)
        outs[n] = acc[:MOBA_HEAD_DIM] / acc[MOBA_HEAD_DIM:MOBA_HEAD_DIM + 1]
        if hh == HEADS_PER_STEP - 1:
            pair = [outs.pop(n - HEADS_PER_STEP + 1 + h) for h in range(HEADS_PER_STEP)]
            o_ref[0, i * L:(i + 1) * L, :] = jnp.concatenate(pair, axis=0).T

    def exact_path():
        masks, qms, maxes, mrows, outs = {}, {}, {}, {}, {}
        _cast_slabs(cast_src, cast_dst)
        k_means = stage_operands()

        def begin_body(n):
            i, hh = bodies[n]
            qms[n] = masked_q(n)
            if i > MOBA_TOPK:
                sel = block_select(k_means, n)
                masks[n] = (jnp.where(sel, cfar[hh], NEG),
                            jnp.where(sel, 0.0, NEG))

        def score_tile(n, j):
            i, hh = bodies[n]
            t = jnp.dot(k_ref[0, j * L:(j + 1) * L, :], qms[n],
                        preferred_element_type=F32)
            if j == i:
                t = t + bias_ref[hh, 0]
            elif j == i - 1:
                t = t + bias_ref[hh, 1]
                if i > MOBA_TOPK:
                    t = t + masks[n][1][j:j + 1, :]
            elif i > MOBA_TOPK:
                t = t + masks[n][0][j:j + 1, :]
            else:
                t = t + cfar[hh]
            t_scr[n % T_SLOTS][j * L:(j + 1) * L, :] = t
            tm = jnp.max(t.reshape(L // 8, 8, L), axis=0)
            maxes[n] = tm if j == 0 else jnp.maximum(maxes[n], tm)
            if j == i:
                mrows[n] = jnp.max(maxes.pop(n), axis=0, keepdims=True)
                qms.pop(n)
                masks.pop(n, None)

        def exp_tile(n, j):
            t_ref, p_ref = t_scr[n % T_SLOTS], p_scr[n % T_SLOTS]
            p_ref[j * L:(j + 1) * L, :] = jnp.exp2(t_ref[j * L:(j + 1) * L, :] - mrows[n]).astype(BF16)

        for n0 in range(min(LOOKAHEAD, nb)):
            begin_body(n0)
            for j in range(bodies[n0][0] + 1):
                score_tile(n0, j)
        for n in range(nb):
            n_exp = bodies[n][0] + 1
            ahead = n + LOOKAHEAD
            n_score = bodies[ahead][0] + 1 if ahead < nb else 0
            if n_score:
                begin_body(ahead)
            for j in range(max(n_exp, n_score)):
                if j < n_score:
                    score_tile(ahead, j)
                if j < n_exp:
                    exp_tile(n, j)
            mrows.pop(n)
            pv_stage(outs, n, p_scr[n % T_SLOTS])

    def fast_path():
        outs = {}
        _cast_slabs(cast_src, cast_dst)
        k_means = stage_operands()
        shifted = {(hh, d): bias_ref[hh, d] - shift[hh]
                   for hh in range(HEADS_PER_STEP) for d in range(2)}

        def prob_tiles(n):
            i, hh = bodies[n]
            qm = masked_q(n)
            s_row = shift[hh]
            if i > MOBA_TOPK:
                sel = block_select(k_means, n)
                m_far = jnp.where(sel, cfar[hh], NEG) - s_row
                m_prev = jnp.where(sel, 0.0, NEG)
            for j in range(i + 1):
                e = jnp.dot(k_ref[0, j * L:(j + 1) * L, :], qm,
                            preferred_element_type=F32)
                if j == i:
                    e = e + shifted[hh, 0]
                elif j == i - 1:
                    e = e + shifted[hh, 1]
                    if i > MOBA_TOPK:
                        e = e + m_prev[j:j + 1, :]
                else:
                    e = e + (m_far[j:j + 1, :] if i > MOBA_TOPK else cfar[hh] - s_row)
                p_scr[n % P_SLOTS][j * L:(j + 1) * L, :] = jnp.exp2(e).astype(BF16)

        for n0 in range(min(FAST_LOOKAHEAD, nb)):
            prob_tiles(n0)
        for n in range(nb):
            if n + FAST_LOOKAHEAD < nb:
                prob_tiles(n + FAST_LOOKAHEAD)
            pv_stage(outs, n, p_scr[n % P_SLOTS])

    pl.when(safe)(fast_path)
    pl.when(jnp.logical_not(safe))(exact_path)


def _moba(qkv3, norms, bias_t, bias_stats, cast_weights):
    B, S, _ = qkv3.shape
    L = MOBA_BLOCK
    n_hp = MOBA_HEADS // HEADS_PER_STEP
    blk = lambda off: pl.BlockSpec((1, S, LANES), lambda hp, b: (b, 0, off + hp))
    c_in, c_out, c_shapes = _cast_specs(cast_weights, B * n_hp, lambda hp, b: hp * B + b)
    outs = pl.pallas_call(
        functools.partial(_moba_kernel, n_cast=len(cast_weights)),
        grid=(n_hp, B),
        in_specs=[
            blk(0), blk(n_hp), blk(2 * n_hp),
            pl.BlockSpec((norms.shape[0] // B,) + norms.shape[1:], lambda hp, b: (b, 0, 0)),
            pl.BlockSpec((HEADS_PER_STEP, 2, L, L), lambda hp, b: (hp, 0, 0, 0)),
            pl.BlockSpec((HEADS_PER_STEP, STAT_ROWS, LANES), lambda hp, b: (hp, 0, 0)),
        ] + c_in,
        out_specs=[pl.BlockSpec((1, S, LANES), lambda hp, b: (b, 0, hp))] + c_out,
        out_shape=[jax.ShapeDtypeStruct((B, S, D_MOBA), F32)] + c_shapes,
        scratch_shapes=[
            pltpu.VMEM((HEADS_PER_STEP * VT_ROWS, S), BF16),
            pltpu.VMEM((HEADS_PER_STEP * LANES, S), BF16),
        ] + [pltpu.VMEM((S, L), F32)] * T_SLOTS + [pltpu.VMEM((S, L), BF16)] * P_SLOTS,
        compiler_params=pltpu.CompilerParams(
            dimension_semantics=("arbitrary", "arbitrary"), vmem_limit_bytes=VMEM_LIMIT),
        name="moba",
    )(qkv3, qkv3, qkv3, norms, bias_t, bias_stats, *cast_weights)
    return outs[0], outs[1:]


def _ffn_kernel(x_ref, yb_ref, ya_ref, gb_ref, wo_ref, gf_ref, wg_ref, wu_ref, wd_ref, gl_ref,
                o_ref, *, final_norm):
    half = x_ref.shape[0] // 2
    st = {}

    def attn_norm(r):
        yb = yb_ref[r:r + half, :]
        ms = jnp.mean(yb * yb, axis=-1, keepdims=True)
        st["ybn", r] = (yb * lax.rsqrt(ms + EPS) * gb_ref[...]).astype(BF16)

    def out_proj(r):
        st["x1", r] = (x_ref[r:r + half, :]
                       + jnp.dot(st.pop(("ybn", r)), wo_ref[:D_MOBA, :], preferred_element_type=F32)
                       + jnp.dot(ya_ref[r:r + half, :], wo_ref[D_MOBA:, :],
                                 preferred_element_type=F32))

    def ffn_norm(r):
        x1 = st["x1", r]
        ms = jnp.mean(x1 * x1, axis=-1, keepdims=True)
        st["h", r] = (x1 * lax.rsqrt(ms + EPS) * gf_ref[...]).astype(BF16)

    def gate_up(r):
        h = st.pop(("h", r))
        g = jnp.dot(h, wg_ref[...], preferred_element_type=F32)
        u = jnp.dot(h, wu_ref[...], preferred_element_type=F32)
        st["a", r] = (jax.nn.silu(g) * u).astype(BF16)

    def down(r):
        x2 = st.pop(("x1", r)) + jnp.dot(st.pop(("a", r)), wd_ref[...], preferred_element_type=F32)
        if final_norm:
            ms = jnp.mean(x2 * x2, axis=-1, keepdims=True)
            x2 = x2 * lax.rsqrt(ms + EPS) * gl_ref[...]
        o_ref[r:r + half, :] = x2

    stages = [attn_norm, out_proj, ffn_norm, gate_up, down]
    for k in range(len(stages) + 1):
        if k < len(stages):
            stages[k](0)
        if k >= 1:
            stages[k - 1](half)


def _ffn(x2, yb, ya, gb, wo, gf, wg, wu, wd, gl, tm, final_norm):
    T, D = x2.shape
    d_ff = wg.shape[1]
    const = lambda shape: pl.BlockSpec(shape, lambda i: (0, 0), pipeline_mode=pl.Buffered(1))
    return pl.pallas_call(
        functools.partial(_ffn_kernel, final_norm=final_norm),
        grid=(T // tm,),
        in_specs=[
            pl.BlockSpec((tm, D), lambda i: (i, 0)),
            pl.BlockSpec((tm, D_MOBA), lambda i: (i, 0)),
            pl.BlockSpec((tm, D_GMLP), lambda i: (i, 0)),
            pl.BlockSpec((1, D_MOBA), lambda i: (0, 0)),
            const((D_MOBA + D_GMLP, D)),
            pl.BlockSpec((1, D), lambda i: (0, 0)),
            const((D, d_ff)), const((D, d_ff)), const((d_ff, D)),
            pl.BlockSpec((1, D), lambda i: (0, 0)),
        ],
        out_specs=pl.BlockSpec((tm, D), lambda i: (i, 0)),
        out_shape=jax.ShapeDtypeStruct((T, D), F32),
        compiler_params=pltpu.CompilerParams(
            dimension_semantics=("arbitrary",), vmem_limit_bytes=VMEM_LIMIT),
        name="outproj_ffn",
    )(x2, yb, ya, gb, wo, gf, wg, wu, wd, gl)


def kernel(x, rel_bias, norm_mix, w_in, gmlp_ln_g, gmlp_ln_b, w_spatial, b_spatial, out_norm_b,
           out_norm_a, w_out, norm_ffn, w_gate, w_up, w_down, norm_final):
    B, S, D = x.shape
    depth = w_in.shape[0]
    assert S % MOBA_BLOCK == 0 and S // MOBA_BLOCK > MOBA_TOPK
    far = _t5_bucket_np(np.arange(MOBA_BLOCK + 1, S + MOBA_BLOCK))
    assert (far == far[0]).all()
    far_bucket = int(far[0])

    assert depth == 1, "weight casts ride along the single layer's kernels"
    T = B * S
    x2 = x.reshape(T, D)
    row = lambda v: v.reshape(1, -1)
    bias_t, bias_stats, (w_in_bf,) = _bias_tiles(rel_bias, far_bucket, [w_in[0]])
    qkv, ya, norms = _inproj(x2, row(norm_mix[0]), w_in_bf, row(gmlp_ln_g[0]), row(gmlp_ln_b[0]),
                             w_spatial[0], b_spatial[0], row(out_norm_a[0]), PROJ_ROWS)
    yb, (wo_bf, wg_bf, wu_bf, wd_bf) = _moba(qkv.reshape(B, S, -1), norms, bias_t, bias_stats,
                                             [w_out[0], w_gate[0], w_up[0], w_down[0]])
    out = _ffn(x2, yb.reshape(T, D_MOBA), ya, row(out_norm_b[0]), wo_bf, row(norm_ffn[0]),
               wg_bf, wu_bf, wd_bf, row(norm_final), FFN_ROWS, True)
    return out.reshape(B, S, D)
```

```python
import functools
import math

import numpy as np
import jax
import jax.numpy as jnp
from jax import lax
from jax.experimental import pallas as pl
from jax.experimental.pallas import tpu as pltpu

F32 = jnp.float32
BF16 = jnp.bfloat16

MOBA_HEADS = 8
MOBA_HEAD_DIM = 64
D_MOBA = MOBA_HEADS * MOBA_HEAD_DIM
GMLP_GROUPS = 8
GMLP_GROUP_DIM = 64
D_GMLP = GMLP_GROUPS * GMLP_GROUP_DIM
MOBA_BLOCK = 256
MOBA_TOPK = 3
GMLP_CHUNK = 128
NUM_BUCKETS = 32
REL_MAX_DISTANCE = 128
EPS = 1e-6
NEG = -1e9

LANES = 128
HEADS_PER_STEP = LANES // MOBA_HEAD_DIM
BF16_SUBLANES = 16
ONES_ROWS = BF16_SUBLANES
VT_ROWS = MOBA_HEAD_DIM + ONES_ROWS
VMEM_LIMIT = 56 * 1024 * 1024
PROJ_ROWS = 1024
FFN_ROWS = 1024
FFN_PARTS = 4
T_SLOTS = 4
P_SLOTS = 4
LOOKAHEAD = 3
FAST_LOOKAHEAD = 3

LOG2E = math.log2(math.e)
Q_SCALE = MOBA_HEAD_DIM ** -0.5 * LOG2E

BOUND_SLACK = 1.0 + 2.0 ** -6
SAFE_SPAN = 100.0


def _t5_bucket_np(dist):
    n = np.maximum(dist, 0)
    max_exact = NUM_BUCKETS // 2
    nf = np.maximum(n, max_exact).astype(np.float32)
    ratio = np.log(nf / np.float32(max_exact)) / np.float32(math.log(REL_MAX_DISTANCE / max_exact))
    large = max_exact + (ratio * np.float32(NUM_BUCKETS - max_exact)).astype(np.int32)
    large = np.minimum(large, NUM_BUCKETS - 1)
    return np.where(n < max_exact, n, large).astype(np.int32)


def _cast_specs(weights, n_steps, step_of):
    in_specs, out_specs, out_shapes = [], [], []
    for w in weights:
        rows, cols = w.shape
        period = next(p for p in range(1, n_steps + 1)
                      if n_steps % p == 0 and (rows * p) % (n_steps * BF16_SUBLANES) == 0)
        spec = pl.BlockSpec((rows * period // n_steps, cols),
                            functools.partial(lambda p, *ids: (step_of(*ids) // p, 0), period))
        in_specs.append(spec)
        out_specs.append(spec)
        out_shapes.append(jax.ShapeDtypeStruct(w.shape, BF16))
    return in_specs, out_specs, out_shapes


def _cast_slabs(src_refs, dst_refs):
    for src, dst in zip(src_refs, dst_refs):
        dst[...] = src[...].astype(BF16)


BIAS_FAR, BIAS_MAX, BIAS_MIN = 0, 1, 2
STAT_ROWS = 8
F32_SUBLANES = 8
BIAS_HEADS_PER_STEP = 4


def _bias_kernel(bucket_ref, rel_ref, *refs, far_bucket):
    n_cast = (len(refs) - 2) // 2
    out_ref, stat_ref = refs[n_cast], refs[n_cast + 1]
    _cast_slabs(refs[:n_cast], refs[n_cast + 2:])
    heads_here, L = out_ref.shape[0], out_ref.shape[2]
    span = bucket_ref.shape[1]
    bucket = bucket_ref[...]
    lane = lax.broadcasted_iota(jnp.int32, (1, span), 1)
    for hh in range(heads_here):
        h = pl.program_id(0) * heads_here + hh
        vals = [rel_ref[b, h] * LOG2E for b in range(NUM_BUCKETS)]
        prof = jnp.zeros((1, span), F32)
        for b in range(NUM_BUCKETS):
            prof = jnp.where(bucket == b, vals[b], prof)
        own = jnp.broadcast_to(jnp.where(lane < L, prof, NEG), (F32_SUBLANES, span))
        prev = jnp.broadcast_to(prof, (F32_SUBLANES, span))
        for g in range(L // F32_SUBLANES):
            k0 = g * F32_SUBLANES
            rows = slice(k0, k0 + F32_SUBLANES)
            out_ref[hh, 0, rows, :] = pltpu.roll(own, k0, 1, stride=1, stride_axis=0)[:, :L]
            out_ref[hh, 1, rows, :] = pltpu.roll(prev, (k0 + L) % span, 1, stride=1,
                                                 stride_axis=0)[:, :L]
        stats = {BIAS_FAR: vals[far_bucket], BIAS_MAX: functools.reduce(jnp.maximum, vals),
                 BIAS_MIN: functools.reduce(jnp.minimum, vals)}
        for r in range(STAT_ROWS):
            stat_ref[hh, r:r + 1, :] = jnp.full((1, LANES), stats.get(r, 0.0), F32)


def _bias_tiles(rel_bias, far_bucket, cast_weights):
    L = MOBA_BLOCK
    bucket = _t5_bucket_np(np.arange(2 * L))[None, :]
    n_steps = MOBA_HEADS // BIAS_HEADS_PER_STEP
    c_in, c_out, c_shapes = _cast_specs(cast_weights, n_steps, lambda i: i)
    outs = pl.pallas_call(
        functools.partial(_bias_kernel, far_bucket=far_bucket),
        grid=(n_steps,),
        in_specs=[
            pl.BlockSpec((1, 2 * L), lambda i: (0, 0)),
            pl.BlockSpec(memory_space=pltpu.SMEM),
        ] + c_in,
        out_specs=[pl.BlockSpec((BIAS_HEADS_PER_STEP, 2, L, L), lambda i: (i, 0, 0, 0)),
                   pl.BlockSpec((BIAS_HEADS_PER_STEP, STAT_ROWS, LANES), lambda i: (i, 0, 0))] + c_out,
        out_shape=[jax.ShapeDtypeStruct((MOBA_HEADS, 2, L, L), F32),
                   jax.ShapeDtypeStruct((MOBA_HEADS, STAT_ROWS, LANES), F32)] + c_shapes,
        compiler_params=pltpu.CompilerParams(
            dimension_semantics=("arbitrary",), vmem_limit_bytes=VMEM_LIMIT),
        name="bias_tiles",
    )(jnp.asarray(bucket), rel_bias, *cast_weights)
    return outs[0], outs[1], outs[2:]


def _inproj_kernel(x_ref, g_ref, w_ref, lng_ref, lnb_ref, wsp_ref, bsp_ref, gn_ref,
                   qkv_ref, ya_ref, norm_ref, uz_scr):
    C = GMLP_CHUNK
    tm = x_ref.shape[0]
    n_qkv = 3 * D_MOBA
    x = x_ref[...]
    ms = jnp.mean(x * x, axis=-1, keepdims=True)
    h = (x * lax.rsqrt(ms + EPS) * g_ref[...]).astype(BF16)
    uz_scr[...] = jnp.dot(h, w_ref[:, n_qkv:], preferred_element_type=F32)

    t_ix = lax.broadcasted_iota(jnp.int32, (C, C), 0)
    s_ix = lax.broadcasted_iota(jnp.int32, (C, C), 1)
    w_sp = [jnp.where(t_ix >= s_ix, wsp_ref[g], 0.0).astype(BF16) for g in range(GMLP_GROUPS)]
    first = lax.broadcasted_iota(jnp.int32, (1, LANES), 1) < GMLP_GROUP_DIM
    b_t = bsp_ref[...].T
    lane_group = lax.broadcasted_iota(jnp.int32, (1, D_GMLP), 1) // GMLP_GROUP_DIM
    b_sp = jnp.zeros((C, D_GMLP), F32)
    for g in range(GMLP_GROUPS):
        b_sp = jnp.where(lane_group == g, b_t[:, g:g + 1], b_sp)

    def qkv_cols(c0, c1):
        acc = jnp.dot(h, w_ref[:, c0:c1], preferred_element_type=F32)
        if c1 <= D_MOBA:
            acc = acc * Q_SCALE
        qkv_ref[:, c0:c1] = acc.astype(BF16)
        if c1 <= 2 * D_MOBA:
            sq = acc * acc
            for c in range(c0, c1, MOBA_HEAD_DIM):
                lanes = (c - c0) // LANES * LANES
                own = first if (c - c0) % LANES == 0 else jnp.logical_not(first)
                rows_sq = jnp.sum(jnp.where(own, sq[:, lanes:lanes + LANES], 0.0), axis=-1, keepdims=True)
                top = jnp.max(rows_sq, axis=0, keepdims=True)
                norm_ref[0, c // MOBA_HEAD_DIM:c // MOBA_HEAD_DIM + 1, :] = jnp.broadcast_to(top, (1, LANES))

    def gmlp_rows(c):
        rows = slice(c * C, (c + 1) * C)
        z = jax.nn.gelu(uz_scr[rows, D_GMLP:])
        mu = jnp.mean(z, axis=-1, keepdims=True)
        zc = z - mu
        var = jnp.mean(zc * zc, axis=-1, keepdims=True)
        zn = (zc * lax.rsqrt(var + EPS) * lng_ref[...] + lnb_ref[...]).astype(BF16)
        cols = []
        for gp in range(D_GMLP // LANES):
            zp = zn[:, gp * LANES:(gp + 1) * LANES]
            r0 = jnp.dot(w_sp[2 * gp], zp, preferred_element_type=F32)
            r1 = jnp.dot(w_sp[2 * gp + 1], zp, preferred_element_type=F32)
            cols.append(jnp.where(first, r0, r1))
        s = jnp.concatenate(cols, axis=1) + b_sp
        y = jax.nn.gelu(uz_scr[rows, :D_GMLP]) * s
        ms_y = jnp.mean(y * y, axis=-1, keepdims=True)
        ya_ref[rows, :] = (y * lax.rsqrt(ms_y + EPS) * gn_ref[...]).astype(BF16)

    col_blocks = [(c, c + 2 * LANES) for c in range(0, n_qkv, 2 * LANES)]
    n_chunks = tm // C
    for step in range(max(len(col_blocks), n_chunks)):
        if step < len(col_blocks):
            qkv_cols(*col_blocks[step])
        if step < n_chunks:
            gmlp_rows(step)


def _inproj(x2, g, w_bf, ln_g, ln_b, w_sp, b_sp, gn, tm):
    T, D = x2.shape
    n_all = w_bf.shape[1]
    n_qkv = 3 * D_MOBA
    C = GMLP_CHUNK
    vec = pl.BlockSpec((1, D_GMLP), lambda i: (0, 0))
    return pl.pallas_call(
        _inproj_kernel,
        grid=(T // tm,),
        in_specs=[
            pl.BlockSpec((tm, D), lambda i: (i, 0)),
            pl.BlockSpec((1, D), lambda i: (0, 0)),
            pl.BlockSpec((D, n_all), lambda i: (0, 0), pipeline_mode=pl.Buffered(1)),
            vec, vec,
            pl.BlockSpec((GMLP_GROUPS, C, C), lambda i: (0, 0, 0)),
            pl.BlockSpec((GMLP_GROUPS, C), lambda i: (0, 0)),
            vec,
        ],
        out_specs=[
            pl.BlockSpec((tm, n_qkv), lambda i: (i, 0)),
            pl.BlockSpec((tm, D_GMLP), lambda i: (i, 0)),
            pl.BlockSpec((1, 2 * MOBA_HEADS, LANES), lambda i: (i, 0, 0)),
        ],
        out_shape=[
            jax.ShapeDtypeStruct((T, n_qkv), BF16),
            jax.ShapeDtypeStruct((T, D_GMLP), BF16),
            jax.ShapeDtypeStruct((T // tm, 2 * MOBA_HEADS, LANES), F32),
        ],
        scratch_shapes=[pltpu.VMEM((tm, n_all - n_qkv), F32)],
        compiler_params=pltpu.CompilerParams(
            dimension_semantics=("arbitrary",), vmem_limit_bytes=VMEM_LIMIT),
        name="inproj_gmlp",
    )(x2, g, w_bf, ln_g, ln_b, w_sp, b_sp, gn)


def _moba_kernel(q_ref, k_ref, v_ref, norm_ref, bias_ref, stat_ref, *refs, n_cast):
    cast_src, o_ref, cast_dst = refs[:n_cast], refs[n_cast], refs[n_cast + 1:2 * n_cast + 1]
    vt_scr, qt_scr, slots = refs[2 * n_cast + 1], refs[2 * n_cast + 2], refs[2 * n_cast + 3:]
    t_scr, p_scr = slots[:T_SLOTS], slots[T_SLOTS:]
    L = MOBA_BLOCK
    S = q_ref.shape[1]
    NB = S // L
    hp = pl.program_id(0)
    lane = lax.broadcasted_iota(jnp.int32, (1, LANES), 1)
    head_lanes = [(lane >= hh * MOBA_HEAD_DIM) & (lane < (hh + 1) * MOBA_HEAD_DIM)
                  for hh in range(HEADS_PER_STEP)]
    row = lax.broadcasted_iota(jnp.int32, (NB, L), 0)
    bodies = [(i, hh) for i in range(NB) for hh in range(HEADS_PER_STEP)]
    nb = len(bodies)

    cfar = [stat_ref[hh, BIAS_FAR:BIAS_FAR + 1, :1] for hh in range(HEADS_PER_STEP)]
    bmax = [stat_ref[hh, BIAS_MAX:BIAS_MAX + 1, :1] for hh in range(HEADS_PER_STEP)]
    bmin = [stat_ref[hh, BIAS_MIN:BIAS_MIN + 1, :1] for hh in range(HEADS_PER_STEP)]

    def stage_operands():
        vt = v_ref[0].astype(F32).T.astype(BF16)
        for hh in range(HEADS_PER_STEP):
            vt_scr[hh * VT_ROWS:hh * VT_ROWS + MOBA_HEAD_DIM, :] = (
                vt[hh * MOBA_HEAD_DIM:(hh + 1) * MOBA_HEAD_DIM])
            vt_scr[hh * VT_ROWS + MOBA_HEAD_DIM:(hh + 1) * VT_ROWS, :] = jnp.ones((ONES_ROWS, S), BF16)
        qt = q_ref[0].astype(F32).T.astype(BF16)
        dim_head = lax.broadcasted_iota(jnp.int32, (LANES, 1), 0) // MOBA_HEAD_DIM
        for hh in range(HEADS_PER_STEP):
            qt_scr[hh * LANES:(hh + 1) * LANES, :] = jnp.where(dim_head == hh, qt, jnp.zeros_like(qt))
        km = jnp.concatenate(
            [jnp.sum(k_ref[0, n * L:(n + 1) * L, :].astype(F32), axis=0, keepdims=True) * (1.0 / L)
             for n in range(NB)], axis=0)
        kmm = jnp.concatenate([jnp.where(head_lanes[hh], km, 0.0) for hh in range(HEADS_PER_STEP)],
                              axis=0)
        km_hi = kmm.astype(BF16)
        return km_hi, (kmm - km_hi.astype(F32)).astype(BF16)

    shift, span = [], []
    for hh in range(HEADS_PER_STEP):
        head = hp * HEADS_PER_STEP + hh
        q_sq = jnp.max(norm_ref[:, pl.ds(head, 1), :], axis=0)[:, :1]
        k_sq = jnp.max(norm_ref[:, pl.ds(MOBA_HEADS + head, 1), :], axis=0)[:, :1]
        reach = jnp.sqrt(q_sq * k_sq) * BOUND_SLACK
        shift.append(reach + bmax[hh])
        span.append(2.0 * reach + (bmax[hh] - bmin[hh]))
    safe = jnp.max(functools.reduce(jnp.maximum, span)) <= SAFE_SPAN

    def block_select(k_means, n):
        i, hh = bodies[n]
        q_t = masked_q(n)
        gate = (jnp.dot(k_means[0], q_t, preferred_element_type=F32)
                + jnp.dot(k_means[1], q_t, preferred_element_type=F32))
        g = gate[hh * NB:(hh + 1) * NB]
        cnt = jnp.zeros((NB, L), jnp.int32)
        for m in range(i):
            gm = g[m:m + 1, :]
            beats = (gm > g) | ((gm == g) & (m < row))
            cnt = cnt + jnp.where(beats, 1, 0)
        return cnt < MOBA_TOPK

    def masked_q(n):
        i, hh = bodies[n]
        return qt_scr[hh * LANES:(hh + 1) * LANES, i * L:(i + 1) * L]

    def pv_stage(outs, n, p_ref):
        i, hh = bodies[n]
        nk = (i + 1) * L
        acc = jnp.dot(vt_scr[hh * VT_ROWS:(hh + 1) * VT_ROWS, 0:nk], p_ref[0:nk, :],
                      preferred_element_type=F32)
        outs[n] = acc[:MOBA_HEAD_DIM] / acc[MOBA_HEAD_DIM:MOBA_HEAD_DIM + 1]
        if hh == HEADS_PER_STEP - 1:
            pair = [outs.pop(n - HEADS_PER_STEP + 1 + h) for h in range(HEADS_PER_STEP)]
            o_ref[0, i * L:(i + 1) * L, :] = jnp.concatenate(pair, axis=0).T

    def exact_path():
        masks, qms, maxes, mrows, outs = {}, {}, {}, {}, {}
        _cast_slabs(cast_src, cast_dst)
        k_means = stage_operands()

        def begin_body(n):
            i, hh = bodies[n]
            qms[n] = masked_q(n)
            if i > MOBA_TOPK:
                sel = block_select(k_means, n)
                masks[n] = (jnp.where(sel, cfar[hh], NEG),
                            jnp.where(sel, 0.0, NEG))

        def score_tile(n, j):
            i, hh = bodies[n]
            t = jnp.dot(k_ref[0, j * L:(j + 1) * L, :], qms[n],
                        preferred_element_type=F32)
            if j == i:
                t = t + bias_ref[hh, 0]
            elif j == i - 1:
                t = t + bias_ref[hh, 1]
                if i > MOBA_TOPK:
                    t = t + masks[n][1][j:j + 1, :]
            elif i > MOBA_TOPK:
                t = t + masks[n][0][j:j + 1, :]
            else:
                t = t + cfar[hh]
            t_scr[n % T_SLOTS][j * L:(j + 1) * L, :] = t
            tm = jnp.max(t.reshape(L // 8, 8, L), axis=0)
            maxes[n] = tm if j == 0 else jnp.maximum(maxes[n], tm)
            if j == i:
                mrows[n] = jnp.max(maxes.pop(n), axis=0, keepdims=True)
                qms.pop(n)
                masks.pop(n, None)

        def exp_tile(n, j):
            t_ref, p_ref = t_scr[n % T_SLOTS], p_scr[n % T_SLOTS]
            p_ref[j * L:(j + 1) * L, :] = jnp.exp2(t_ref[j * L:(j + 1) * L, :] - mrows[n]).astype(BF16)

        for n0 in range(min(LOOKAHEAD, nb)):
            begin_body(n0)
            for j in range(bodies[n0][0] + 1):
                score_tile(n0, j)
        for n in range(nb):
            n_exp = bodies[n][0] + 1
            ahead = n + LOOKAHEAD
            n_score = bodies[ahead][0] + 1 if ahead < nb else 0
            if n_score:
                begin_body(ahead)
            for j in range(max(n_exp, n_score)):
                if j < n_score:
                    score_tile(ahead, j)
                if j < n_exp:
                    exp_tile(n, j)
            mrows.pop(n)
            pv_stage(outs, n, p_scr[n % T_SLOTS])

    def fast_path():
        outs = {}
        _cast_slabs(cast_src, cast_dst)
        k_means = stage_operands()
        shifted = {(hh, d): bias_ref[hh, d] - shift[hh]
                   for hh in range(HEADS_PER_STEP) for d in range(2)}

        def prob_tiles(n):
            i, hh = bodies[n]
            qm = masked_q(n)
            s_row = shift[hh]
            if i > MOBA_TOPK:
                sel = block_select(k_means, n)
                m_far = jnp.where(sel, cfar[hh], NEG) - s_row
                m_prev = jnp.where(sel, 0.0, NEG)
            for j in range(i + 1):
                e = jnp.dot(k_ref[0, j * L:(j + 1) * L, :], qm,
                            preferred_element_type=F32)
                if j == i:
                    e = e + shifted[hh, 0]
                elif j == i - 1:
                    e = e + shifted[hh, 1]
                    if i > MOBA_TOPK:
                        e = e + m_prev[j:j + 1, :]
                else:
                    e = e + (m_far[j:j + 1, :] if i > MOBA_TOPK else cfar[hh] - s_row)
                p_scr[n % P_SLOTS][j * L:(j + 1) * L, :] = jnp.exp2(e).astype(BF16)

        for n0 in range(min(FAST_LOOKAHEAD, nb)):
            prob_tiles(n0)
        for n in range(nb):
            if n + FAST_LOOKAHEAD < nb:
                prob_tiles(n + FAST_LOOKAHEAD)
            pv_stage(outs, n, p_scr[n % P_SLOTS])

    pl.when(safe)(fast_path)
    pl.when(jnp.logical_not(safe))(exact_path)


def _moba(qkv3, norms, bias_t, bias_stats, cast_weights):
    B, S, _ = qkv3.shape
    L = MOBA_BLOCK
    n_hp = MOBA_HEADS // HEADS_PER_STEP
    blk = lambda off: pl.BlockSpec((1, S, LANES), lambda hp, b: (b, 0, off + hp))
    c_in, c_out, c_shapes = _cast_specs(cast_weights, B * n_hp, lambda hp, b: hp * B + b)
    outs = pl.pallas_call(
        functools.partial(_moba_kernel, n_cast=len(cast_weights)),
        grid=(n_hp, B),
        in_specs=[
            blk(0), blk(n_hp), blk(2 * n_hp),
            pl.BlockSpec((norms.shape[0] // B,) + norms.shape[1:], lambda hp, b: (b, 0, 0)),
            pl.BlockSpec((HEADS_PER_STEP, 2, L, L), lambda hp, b: (hp, 0, 0, 0)),
            pl.BlockSpec((HEADS_PER_STEP, STAT_ROWS, LANES), lambda hp, b: (hp, 0, 0)),
        ] + c_in,
        out_specs=[pl.BlockSpec((1, S, LANES), lambda hp, b: (b, 0, hp))] + c_out,
        out_shape=[jax.ShapeDtypeStruct((B, S, D_MOBA), F32)] + c_shapes,
        scratch_shapes=[
            pltpu.VMEM((HEADS_PER_STEP * VT_ROWS, S), BF16),
            pltpu.VMEM((HEADS_PER_STEP * LANES, S), BF16),
        ] + [pltpu.VMEM((S, L), F32)] * T_SLOTS + [pltpu.VMEM((S, L), BF16)] * P_SLOTS,
        compiler_params=pltpu.CompilerParams(
            dimension_semantics=("arbitrary", "arbitrary"), vmem_limit_bytes=VMEM_LIMIT),
        name="moba",
    )(qkv3, qkv3, qkv3, norms, bias_t, bias_stats, *cast_weights)
    return outs[0], outs[1:]


def _ffn_kernel(x_ref, yb_ref, ya_ref, gb_ref, wo_ref, gf_ref, wg_ref, wu_ref, wd_ref, gl_ref,
                o_ref, *, final_norm):
    half = x_ref.shape[0] // FFN_PARTS
    st = {}

    def attn_norm(r):
        yb = yb_ref[r:r + half, :]
        ms = jnp.mean(yb * yb, axis=-1, keepdims=True)
        st["ybn", r] = (yb * lax.rsqrt(ms + EPS) * gb_ref[...]).astype(BF16)

    def out_proj(r):
        st["x1", r] = (x_ref[r:r + half, :]
                       + jnp.dot(st.pop(("ybn", r)), wo_ref[:D_MOBA, :], preferred_element_type=F32)
                       + jnp.dot(ya_ref[r:r + half, :], wo_ref[D_MOBA:, :],
                                 preferred_element_type=F32))

    def ffn_norm(r):
        x1 = st["x1", r]
        ms = jnp.mean(x1 * x1, axis=-1, keepdims=True)
        st["h", r] = (x1 * lax.rsqrt(ms + EPS) * gf_ref[...]).astype(BF16)

    def gate_up(r):
        h = st.pop(("h", r))
        g = jnp.dot(h, wg_ref[...], preferred_element_type=F32)
        u = jnp.dot(h, wu_ref[...], preferred_element_type=F32)
        st["a", r] = (jax.nn.silu(g) * u).astype(BF16)

    def down(r):
        x2 = st.pop(("x1", r)) + jnp.dot(st.pop(("a", r)), wd_ref[...], preferred_element_type=F32)
        if final_norm:
            ms = jnp.mean(x2 * x2, axis=-1, keepdims=True)
            x2 = x2 * lax.rsqrt(ms + EPS) * gl_ref[...]
        o_ref[r:r + half, :] = x2

    stages = [attn_norm, out_proj, ffn_norm, gate_up, down]
    for k in range(len(stages) + FFN_PARTS - 1):
        for part in range(FFN_PARTS):
            if 0 <= k - part < len(stages):
                stages[k - part](part * half)


def _ffn(x2, yb, ya, gb, wo, gf, wg, wu, wd, gl, tm, final_norm):
    T, D = x2.shape
    d_ff = wg.shape[1]
    const = lambda shape: pl.BlockSpec(shape, lambda i: (0, 0), pipeline_mode=pl.Buffered(1))
    return pl.pallas_call(
        functools.partial(_ffn_kernel, final_norm=final_norm),
        grid=(T // tm,),
        in_specs=[
            pl.BlockSpec((tm, D), lambda i: (i, 0)),
            pl.BlockSpec((tm, D_MOBA), lambda i: (i, 0)),
            pl.BlockSpec((tm, D_GMLP), lambda i: (i, 0)),
            pl.BlockSpec((1, D_MOBA), lambda i: (0, 0)),
            const((D_MOBA + D_GMLP, D)),
            pl.BlockSpec((1, D), lambda i: (0, 0)),
            const((D, d_ff)), const((D, d_ff)), const((d_ff, D)),
            pl.BlockSpec((1, D), lambda i: (0, 0)),
        ],
        out_specs=pl.BlockSpec((tm, D), lambda i: (i, 0)),
        out_shape=jax.ShapeDtypeStruct((T, D), F32),
        compiler_params=pltpu.CompilerParams(
            dimension_semantics=("arbitrary",), vmem_limit_bytes=VMEM_LIMIT),
        name="outproj_ffn",
    )(x2, yb, ya, gb, wo, gf, wg, wu, wd, gl)


def kernel(x, rel_bias, norm_mix, w_in, gmlp_ln_g, gmlp_ln_b, w_spatial, b_spatial, out_norm_b,
           out_norm_a, w_out, norm_ffn, w_gate, w_up, w_down, norm_final):
    B, S, D = x.shape
    depth = w_in.shape[0]
    assert S % MOBA_BLOCK == 0 and S // MOBA_BLOCK > MOBA_TOPK
    far = _t5_bucket_np(np.arange(MOBA_BLOCK + 1, S + MOBA_BLOCK))
    assert (far == far[0]).all()
    far_bucket = int(far[0])

    assert depth == 1, "weight casts ride along the single layer's kernels"
    T = B * S
    x2 = x.reshape(T, D)
    row = lambda v: v.reshape(1, -1)
    bias_t, bias_stats, (w_in_bf,) = _bias_tiles(rel_bias, far_bucket, [w_in[0]])
    qkv, ya, norms = _inproj(x2, row(norm_mix[0]), w_in_bf, row(gmlp_ln_g[0]), row(gmlp_ln_b[0]),
                             w_spatial[0], b_spatial[0], row(out_norm_a[0]), PROJ_ROWS)
    yb, (wo_bf, wg_bf, wu_bf, wd_bf) = _moba(qkv.reshape(B, S, -1), norms, bias_t, bias_stats,
                                             [w_out[0], w_gate[0], w_up[0], w_down[0]])
    out = _ffn(x2, yb.reshape(T, D_MOBA), ya, row(out_norm_b[0]), wo_bf, row(norm_ffn[0]),
               wg_bf, wu_bf, wd_bf, row(norm_final), FFN_ROWS, True)
    return out.reshape(B, S, D)
```

```python
import functools
import math

import numpy as np
import jax
import jax.numpy as jnp
from jax import lax
from jax.experimental import pallas as pl
from jax.experimental.pallas import tpu as pltpu

F32 = jnp.float32
BF16 = jnp.bfloat16

MOBA_HEADS = 8
MOBA_HEAD_DIM = 64
D_MOBA = MOBA_HEADS * MOBA_HEAD_DIM
GMLP_GROUPS = 8
GMLP_GROUP_DIM = 64
D_GMLP = GMLP_GROUPS * GMLP_GROUP_DIM
MOBA_BLOCK = 256
MOBA_TOPK = 3
GMLP_CHUNK = 128
NUM_BUCKETS = 32
REL_MAX_DISTANCE = 128
EPS = 1e-6
NEG = -1e9

LANES = 128
HEADS_PER_STEP = LANES // MOBA_HEAD_DIM
BF16_SUBLANES = 16
ONES_ROWS = BF16_SUBLANES
VT_ROWS = MOBA_HEAD_DIM + ONES_ROWS
VMEM_LIMIT = 56 * 1024 * 1024
PROJ_ROWS = 1024
FFN_ROWS = 512
T_SLOTS = 4
P_SLOTS = 4
LOOKAHEAD = 3
FAST_LOOKAHEAD = 3

LOG2E = math.log2(math.e)
Q_SCALE = MOBA_HEAD_DIM ** -0.5 * LOG2E

BOUND_SLACK = 1.0 + 2.0 ** -6
SAFE_SPAN = 100.0


def _t5_bucket_np(dist):
    n = np.maximum(dist, 0)
    max_exact = NUM_BUCKETS // 2
    nf = np.maximum(n, max_exact).astype(np.float32)
    ratio = np.log(nf / np.float32(max_exact)) / np.float32(math.log(REL_MAX_DISTANCE / max_exact))
    large = max_exact + (ratio * np.float32(NUM_BUCKETS - max_exact)).astype(np.int32)
    large = np.minimum(large, NUM_BUCKETS - 1)
    return np.where(n < max_exact, n, large).astype(np.int32)


def _cast_specs(weights, n_steps, step_of):
    in_specs, out_specs, out_shapes = [], [], []
    for w in weights:
        rows, cols = w.shape
        period = next(p for p in range(1, n_steps + 1)
                      if n_steps % p == 0 and (rows * p) % (n_steps * BF16_SUBLANES) == 0)
        spec = pl.BlockSpec((rows * period // n_steps, cols),
                            functools.partial(lambda p, *ids: (step_of(*ids) // p, 0), period))
        in_specs.append(spec)
        out_specs.append(spec)
        out_shapes.append(jax.ShapeDtypeStruct(w.shape, BF16))
    return in_specs, out_specs, out_shapes


def _cast_slabs(src_refs, dst_refs):
    for src, dst in zip(src_refs, dst_refs):
        dst[...] = src[...].astype(BF16)


BIAS_FAR, BIAS_MAX, BIAS_MIN = 0, 1, 2
STAT_ROWS = 8
F32_SUBLANES = 8
BIAS_HEADS_PER_STEP = 4


def _bias_kernel(bucket_ref, rel_ref, *refs, far_bucket):
    n_cast = (len(refs) - 2) // 2
    out_ref, stat_ref = refs[n_cast], refs[n_cast + 1]
    _cast_slabs(refs[:n_cast], refs[n_cast + 2:])
    heads_here, L = out_ref.shape[0], out_ref.shape[2]
    span = bucket_ref.shape[1]
    bucket = bucket_ref[...]
    lane = lax.broadcasted_iota(jnp.int32, (1, span), 1)
    for hh in range(heads_here):
        h = pl.program_id(0) * heads_here + hh
        vals = [rel_ref[b, h] * LOG2E for b in range(NUM_BUCKETS)]
        prof = jnp.zeros((1, span), F32)
        for b in range(NUM_BUCKETS):
            prof = jnp.where(bucket == b, vals[b], prof)
        own = jnp.broadcast_to(jnp.where(lane < L, prof, NEG), (F32_SUBLANES, span))
        prev = jnp.broadcast_to(prof, (F32_SUBLANES, span))
        for g in range(L // F32_SUBLANES):
            k0 = g * F32_SUBLANES
            rows = slice(k0, k0 + F32_SUBLANES)
            out_ref[hh, 0, rows, :] = pltpu.roll(own, k0, 1, stride=1, stride_axis=0)[:, :L]
            out_ref[hh, 1, rows, :] = pltpu.roll(prev, (k0 + L) % span, 1, stride=1,
                                                 stride_axis=0)[:, :L]
        stats = {BIAS_FAR: vals[far_bucket], BIAS_MAX: functools.reduce(jnp.maximum, vals),
                 BIAS_MIN: functools.reduce(jnp.minimum, vals)}
        for r in range(STAT_ROWS):
            stat_ref[hh, r:r + 1, :] = jnp.full((1, LANES), stats.get(r, 0.0), F32)


def _bias_tiles(rel_bias, far_bucket, cast_weights):
    L = MOBA_BLOCK
    bucket = _t5_bucket_np(np.arange(2 * L))[None, :]
    n_steps = MOBA_HEADS // BIAS_HEADS_PER_STEP
    c_in, c_out, c_shapes = _cast_specs(cast_weights, n_steps, lambda i: i)
    outs = pl.pallas_call(
        functools.partial(_bias_kernel, far_bucket=far_bucket),
        grid=(n_steps,),
        in_specs=[
            pl.BlockSpec((1, 2 * L), lambda i: (0, 0)),
            pl.BlockSpec(memory_space=pltpu.SMEM),
        ] + c_in,
        out_specs=[pl.BlockSpec((BIAS_HEADS_PER_STEP, 2, L, L), lambda i: (i, 0, 0, 0)),
                   pl.BlockSpec((BIAS_HEADS_PER_STEP, STAT_ROWS, LANES), lambda i: (i, 0, 0))] + c_out,
        out_shape=[jax.ShapeDtypeStruct((MOBA_HEADS, 2, L, L), F32),
                   jax.ShapeDtypeStruct((MOBA_HEADS, STAT_ROWS, LANES), F32)] + c_shapes,
        compiler_params=pltpu.CompilerParams(
            dimension_semantics=("arbitrary",), vmem_limit_bytes=VMEM_LIMIT),
        name="bias_tiles",
    )(jnp.asarray(bucket), rel_bias, *cast_weights)
    return outs[0], outs[1], outs[2:]


def _inproj_kernel(x_ref, g_ref, w_ref, lng_ref, lnb_ref, wsp_ref, bsp_ref, gn_ref,
                   qkv_ref, ya_ref, norm_ref, uz_scr):
    C = GMLP_CHUNK
    tm = x_ref.shape[0]
    n_qkv = 3 * D_MOBA
    x = x_ref[...]
    ms = jnp.mean(x * x, axis=-1, keepdims=True)
    h = (x * lax.rsqrt(ms + EPS) * g_ref[...]).astype(BF16)
    uz_scr[...] = jnp.dot(h, w_ref[:, n_qkv:], preferred_element_type=F32)

    t_ix = lax.broadcasted_iota(jnp.int32, (C, C), 0)
    s_ix = lax.broadcasted_iota(jnp.int32, (C, C), 1)
    w_sp = [jnp.where(t_ix >= s_ix, wsp_ref[g], 0.0).astype(BF16) for g in range(GMLP_GROUPS)]
    first = lax.broadcasted_iota(jnp.int32, (1, LANES), 1) < GMLP_GROUP_DIM
    b_t = bsp_ref[...].T
    lane_group = lax.broadcasted_iota(jnp.int32, (1, D_GMLP), 1) // GMLP_GROUP_DIM
    b_sp = jnp.zeros((C, D_GMLP), F32)
    for g in range(GMLP_GROUPS):
        b_sp = jnp.where(lane_group == g, b_t[:, g:g + 1], b_sp)

    def qkv_cols(c0, c1):
        acc = jnp.dot(h, w_ref[:, c0:c1], preferred_element_type=F32)
        if c1 <= D_MOBA:
            acc = acc * Q_SCALE
        qkv_ref[:, c0:c1] = acc.astype(BF16)
        if c1 <= 2 * D_MOBA:
            sq = acc * acc
            for c in range(c0, c1, MOBA_HEAD_DIM):
                lanes = (c - c0) // LANES * LANES
                own = first if (c - c0) % LANES == 0 else jnp.logical_not(first)
                rows_sq = jnp.sum(jnp.where(own, sq[:, lanes:lanes + LANES], 0.0), axis=-1, keepdims=True)
                top = jnp.max(rows_sq, axis=0, keepdims=True)
                norm_ref[0, c // MOBA_HEAD_DIM:c // MOBA_HEAD_DIM + 1, :] = jnp.broadcast_to(top, (1, LANES))

    def gmlp_rows(c):
        rows = slice(c * C, (c + 1) * C)
        z = jax.nn.gelu(uz_scr[rows, D_GMLP:])
        mu = jnp.mean(z, axis=-1, keepdims=True)
        zc = z - mu
        var = jnp.mean(zc * zc, axis=-1, keepdims=True)
        zn = (zc * lax.rsqrt(var + EPS) * lng_ref[...] + lnb_ref[...]).astype(BF16)
        cols = []
        for gp in range(D_GMLP // LANES):
            zp = zn[:, gp * LANES:(gp + 1) * LANES]
            r0 = jnp.dot(w_sp[2 * gp], zp, preferred_element_type=F32)
            r1 = jnp.dot(w_sp[2 * gp + 1], zp, preferred_element_type=F32)
            cols.append(jnp.where(first, r0, r1))
        s = jnp.concatenate(cols, axis=1) + b_sp
        y = jax.nn.gelu(uz_scr[rows, :D_GMLP]) * s
        ms_y = jnp.mean(y * y, axis=-1, keepdims=True)
        ya_ref[rows, :] = (y * lax.rsqrt(ms_y + EPS) * gn_ref[...]).astype(BF16)

    col_blocks = [(c, c + 2 * LANES) for c in range(0, n_qkv, 2 * LANES)]
    n_chunks = tm // C
    for step in range(max(len(col_blocks), n_chunks)):
        if step < len(col_blocks):
            qkv_cols(*col_blocks[step])
        if step < n_chunks:
            gmlp_rows(step)


def _inproj(x2, g, w_bf, ln_g, ln_b, w_sp, b_sp, gn, tm):
    T, D = x2.shape
    n_all = w_bf.shape[1]
    n_qkv = 3 * D_MOBA
    C = GMLP_CHUNK
    vec = pl.BlockSpec((1, D_GMLP), lambda i: (0, 0))
    return pl.pallas_call(
        _inproj_kernel,
        grid=(T // tm,),
        in_specs=[
            pl.BlockSpec((tm, D), lambda i: (i, 0)),
            pl.BlockSpec((1, D), lambda i: (0, 0)),
            pl.BlockSpec((D, n_all), lambda i: (0, 0), pipeline_mode=pl.Buffered(1)),
            vec, vec,
            pl.BlockSpec((GMLP_GROUPS, C, C), lambda i: (0, 0, 0)),
            pl.BlockSpec((GMLP_GROUPS, C), lambda i: (0, 0)),
            vec,
        ],
        out_specs=[
            pl.BlockSpec((tm, n_qkv), lambda i: (i, 0)),
            pl.BlockSpec((tm, D_GMLP), lambda i: (i, 0)),
            pl.BlockSpec((1, 2 * MOBA_HEADS, LANES), lambda i: (i, 0, 0)),
        ],
        out_shape=[
            jax.ShapeDtypeStruct((T, n_qkv), BF16),
            jax.ShapeDtypeStruct((T, D_GMLP), BF16),
            jax.ShapeDtypeStruct((T // tm, 2 * MOBA_HEADS, LANES), F32),
        ],
        scratch_shapes=[pltpu.VMEM((tm, n_all - n_qkv), F32)],
        compiler_params=pltpu.CompilerParams(
            dimension_semantics=("arbitrary",), vmem_limit_bytes=VMEM_LIMIT),
        name="inproj_gmlp",
    )(x2, g, w_bf, ln_g, ln_b, w_sp, b_sp, gn)


def _moba_kernel(q_ref, k_ref, v_ref, norm_ref, bias_ref, stat_ref, *refs, n_cast):
    cast_src, o_ref, cast_dst = refs[:n_cast], refs[n_cast], refs[n_cast + 1:2 * n_cast + 1]
    vt_scr, qt_scr, slots = refs[2 * n_cast + 1], refs[2 * n_cast + 2], refs[2 * n_cast + 3:]
    t_scr, p_scr = slots[:T_SLOTS], slots[T_SLOTS:]
    L = MOBA_BLOCK
    S = q_ref.shape[1]
    NB = S // L
    hp = pl.program_id(0)
    lane = lax.broadcasted_iota(jnp.int32, (1, LANES), 1)
    head_lanes = [(lane >= hh * MOBA_HEAD_DIM) & (lane < (hh + 1) * MOBA_HEAD_DIM)
                  for hh in range(HEADS_PER_STEP)]
    row = lax.broadcasted_iota(jnp.int32, (NB, L), 0)
    bodies = [(i, hh) for i in range(NB) for hh in range(HEADS_PER_STEP)]
    nb = len(bodies)

    cfar = [stat_ref[hh, BIAS_FAR:BIAS_FAR + 1, :1] for hh in range(HEADS_PER_STEP)]
    bmax = [stat_ref[hh, BIAS_MAX:BIAS_MAX + 1, :1] for hh in range(HEADS_PER_STEP)]
    bmin = [stat_ref[hh, BIAS_MIN:BIAS_MIN + 1, :1] for hh in range(HEADS_PER_STEP)]

    def stage_operands():
        vt = v_ref[0].T
        for hh in range(HEADS_PER_STEP):
            vt_scr[hh * VT_ROWS:hh * VT_ROWS + MOBA_HEAD_DIM, :] = (
                vt[hh * MOBA_HEAD_DIM:(hh + 1) * MOBA_HEAD_DIM])
            vt_scr[hh * VT_ROWS + MOBA_HEAD_DIM:(hh + 1) * VT_ROWS, :] = jnp.ones((ONES_ROWS, S), BF16)
        qt = q_ref[0].T
        dim_head = lax.broadcasted_iota(jnp.int32, (LANES, 1), 0) // MOBA_HEAD_DIM
        for hh in range(HEADS_PER_STEP):
            qt_scr[hh * LANES:(hh + 1) * LANES, :] = jnp.where(dim_head == hh, qt, jnp.zeros_like(qt))
        km = jnp.concatenate(
            [jnp.sum(k_ref[0, n * L:(n + 1) * L, :].astype(F32), axis=0, keepdims=True) * (1.0 / L)
             for n in range(NB)], axis=0)
        kmm = jnp.concatenate([jnp.where(head_lanes[hh], km, 0.0) for hh in range(HEADS_PER_STEP)],
                              axis=0)
        km_hi = kmm.astype(BF16)
        return km_hi, (kmm - km_hi.astype(F32)).astype(BF16)

    shift, span = [], []
    for hh in range(HEADS_PER_STEP):
        head = hp * HEADS_PER_STEP + hh
        q_sq = jnp.max(norm_ref[:, pl.ds(head, 1), :], axis=0)[:, :1]
        k_sq = jnp.max(norm_ref[:, pl.ds(MOBA_HEADS + head, 1), :], axis=0)[:, :1]
        reach = jnp.sqrt(q_sq * k_sq) * BOUND_SLACK
        shift.append(reach + bmax[hh])
        span.append(2.0 * reach + (bmax[hh] - bmin[hh]))
    safe = jnp.max(functools.reduce(jnp.maximum, span)) <= SAFE_SPAN

    def block_select(k_means, n):
        i, hh = bodies[n]
        q_t = masked_q(n)
        gate = (jnp.dot(k_means[0], q_t, preferred_element_type=F32)
                + jnp.dot(k_means[1], q_t, preferred_element_type=F32))
        g = gate[hh * NB:(hh + 1) * NB]
        cnt = jnp.zeros((NB, L), jnp.int32)
        for m in range(i):
            gm = g[m:m + 1, :]
            beats = (gm > g) | ((gm == g) & (m < row))
            cnt = cnt + jnp.where(beats, 1, 0)
        return cnt < MOBA_TOPK

    def masked_q(n):
        i, hh = bodies[n]
        return qt_scr[hh * LANES:(hh + 1) * LANES, i * L:(i + 1) * L]

    def pv_stage(outs, n, p_ref):
        i, hh = bodies[n]
        nk = (i + 1) * L
        acc = jnp.dot(vt_scr[hh * VT_ROWS:(hh + 1) * VT_ROWS, 0:nk], p_ref[0:nk, :],
                      preferred_element_type=F32)
        outs[n] = acc[:MOBA_HEAD_DIM] / acc[MOBA_HEAD_DIM:MOBA_HEAD_DIM + 1]
        if hh == HEADS_PER_STEP - 1:
            pair = [outs.pop(n - HEADS_PER_STEP + 1 + h) for h in range(HEADS_PER_STEP)]
            o_ref[0, i * L:(i + 1) * L, :] = jnp.concatenate(pair, axis=0).T

    def exact_path():
        masks, qms, maxes, mrows, outs = {}, {}, {}, {}, {}
        _cast_slabs(cast_src, cast_dst)
        k_means = stage_operands()

        def begin_body(n):
            i, hh = bodies[n]
            qms[n] = masked_q(n)
            if i > MOBA_TOPK:
                sel = block_select(k_means, n)
                masks[n] = (jnp.where(sel, cfar[hh], NEG),
                            jnp.where(sel, 0.0, NEG))

        def score_tile(n, j):
            i, hh = bodies[n]
            t = jnp.dot(k_ref[0, j * L:(j + 1) * L, :], qms[n],
                        preferred_element_type=F32)
            if j == i:
                t = t + bias_ref[hh, 0]
            elif j == i - 1:
                t = t + bias_ref[hh, 1]
                if i > MOBA_TOPK:
                    t = t + masks[n][1][j:j + 1, :]
            elif i > MOBA_TOPK:
                t = t + masks[n][0][j:j + 1, :]
            else:
                t = t + cfar[hh]
            t_scr[n % T_SLOTS][j * L:(j + 1) * L, :] = t
            tm = jnp.max(t.reshape(L // 8, 8, L), axis=0)
            maxes[n] = tm if j == 0 else jnp.maximum(maxes[n], tm)
            if j == i:
                mrows[n] = jnp.max(maxes.pop(n), axis=0, keepdims=True)
                qms.pop(n)
                masks.pop(n, None)

        def exp_tile(n, j):
            t_ref, p_ref = t_scr[n % T_SLOTS], p_scr[n % T_SLOTS]
            p_ref[j * L:(j + 1) * L, :] = jnp.exp2(t_ref[j * L:(j + 1) * L, :] - mrows[n]).astype(BF16)

        for n0 in range(min(LOOKAHEAD, nb)):
            begin_body(n0)
            for j in range(bodies[n0][0] + 1):
                score_tile(n0, j)
        for n in range(nb):
            n_exp = bodies[n][0] + 1
            ahead = n + LOOKAHEAD
            n_score = bodies[ahead][0] + 1 if ahead < nb else 0
            if n_score:
                begin_body(ahead)
            for j in range(max(n_exp, n_score)):
                if j < n_score:
                    score_tile(ahead, j)
                if j < n_exp:
                    exp_tile(n, j)
            mrows.pop(n)
            pv_stage(outs, n, p_scr[n % T_SLOTS])

    def fast_path():
        outs = {}
        _cast_slabs(cast_src, cast_dst)
        k_means = stage_operands()
        shifted = {(hh, d): bias_ref[hh, d] - shift[hh]
                   for hh in range(HEADS_PER_STEP) for d in range(2)}

        def prob_tiles(n):
            i, hh = bodies[n]
            qm = masked_q(n)
            s_row = shift[hh]
            if i > MOBA_TOPK:
                sel = block_select(k_means, n)
                m_far = jnp.where(sel, cfar[hh], NEG) - s_row
                m_prev = jnp.where(sel, 0.0, NEG)
            for j in range(i + 1):
                e = jnp.dot(k_ref[0, j * L:(j + 1) * L, :], qm,
                            preferred_element_type=F32)
                if j == i:
                    e = e + shifted[hh, 0]
                elif j == i - 1:
                    e = e + shifted[hh, 1]
                    if i > MOBA_TOPK:
                        e = e + m_prev[j:j + 1, :]
                else:
                    e = e + (m_far[j:j + 1, :] if i > MOBA_TOPK else cfar[hh] - s_row)
                p_scr[n % P_SLOTS][j * L:(j + 1) * L, :] = jnp.exp2(e).astype(BF16)

        for n0 in range(min(FAST_LOOKAHEAD, nb)):
            prob_tiles(n0)
        for n in range(nb):
            if n + FAST_LOOKAHEAD < nb:
                prob_tiles(n + FAST_LOOKAHEAD)
            pv_stage(outs, n, p_scr[n % P_SLOTS])

    pl.when(safe)(fast_path)
    pl.when(jnp.logical_not(safe))(exact_path)


def _moba(qkv3, norms, bias_t, bias_stats, cast_weights):
    B, S, _ = qkv3.shape
    L = MOBA_BLOCK
    n_hp = MOBA_HEADS // HEADS_PER_STEP
    blk = lambda off: pl.BlockSpec((1, S, LANES), lambda hp, b: (b, 0, off + hp))
    c_in, c_out, c_shapes = _cast_specs(cast_weights, B * n_hp, lambda hp, b: hp * B + b)
    outs = pl.pallas_call(
        functools.partial(_moba_kernel, n_cast=len(cast_weights)),
        grid=(n_hp, B),
        in_specs=[
            blk(0), blk(n_hp), blk(2 * n_hp),
            pl.BlockSpec((norms.shape[0] // B,) + norms.shape[1:], lambda hp, b: (b, 0, 0)),
            pl.BlockSpec((HEADS_PER_STEP, 2, L, L), lambda hp, b: (hp, 0, 0, 0)),
            pl.BlockSpec((HEADS_PER_STEP, STAT_ROWS, LANES), lambda hp, b: (hp, 0, 0)),
        ] + c_in,
        out_specs=[pl.BlockSpec((1, S, LANES), lambda hp, b: (b, 0, hp))] + c_out,
        out_shape=[jax.ShapeDtypeStruct((B, S, D_MOBA), F32)] + c_shapes,
        scratch_shapes=[
            pltpu.VMEM((HEADS_PER_STEP * VT_ROWS, S), BF16),
            pltpu.VMEM((HEADS_PER_STEP * LANES, S), BF16),
        ] + [pltpu.VMEM((S, L), F32)] * T_SLOTS + [pltpu.VMEM((S, L), BF16)] * P_SLOTS,
        compiler_params=pltpu.CompilerParams(
            dimension_semantics=("arbitrary", "arbitrary"), vmem_limit_bytes=VMEM_LIMIT),
        name="moba",
    )(qkv3, qkv3, qkv3, norms, bias_t, bias_stats, *cast_weights)
    return outs[0], outs[1:]


def _ffn_kernel(x_ref, yb_ref, ya_ref, gb_ref, wo_ref, gf_ref, wg_ref, wu_ref, wd_ref, gl_ref,
                o_ref, *, final_norm):
    half = x_ref.shape[0] // 2
    st = {}

    def attn_norm(r):
        yb = yb_ref[r:r + half, :]
        ms = jnp.mean(yb * yb, axis=-1, keepdims=True)
        st["ybn", r] = (yb * lax.rsqrt(ms + EPS) * gb_ref[...]).astype(BF16)

    def out_proj(r):
        st["x1", r] = (x_ref[r:r + half, :]
                       + jnp.dot(st.pop(("ybn", r)), wo_ref[:D_MOBA, :], preferred_element_type=F32)
                       + jnp.dot(ya_ref[r:r + half, :], wo_ref[D_MOBA:, :],
                                 preferred_element_type=F32))

    def ffn_norm(r):
        x1 = st["x1", r]
        ms = jnp.mean(x1 * x1, axis=-1, keepdims=True)
        st["h", r] = (x1 * lax.rsqrt(ms + EPS) * gf_ref[...]).astype(BF16)

    def gate_up(r):
        h = st.pop(("h", r))
        g = jnp.dot(h, wg_ref[...], preferred_element_type=F32)
        u = jnp.dot(h, wu_ref[...], preferred_element_type=F32)
        st["a", r] = (jax.nn.silu(g) * u).astype(BF16)

    def down(r):
        x2 = st.pop(("x1", r)) + jnp.dot(st.pop(("a", r)), wd_ref[...], preferred_element_type=F32)
        if final_norm:
            ms = jnp.mean(x2 * x2, axis=-1, keepdims=True)
            x2 = x2 * lax.rsqrt(ms + EPS) * gl_ref[...]
        o_ref[r:r + half, :] = x2

    stages = [attn_norm, out_proj, ffn_norm, gate_up, down]
    for k in range(len(stages) + 1):
        if k < len(stages):
            stages[k](0)
        if k >= 1:
            stages[k - 1](half)


def _ffn(x2, yb, ya, gb, wo, gf, wg, wu, wd, gl, tm, final_norm):
    T, D = x2.shape
    d_ff = wg.shape[1]
    const = lambda shape: pl.BlockSpec(shape, lambda i: (0, 0), pipeline_mode=pl.Buffered(1))
    return pl.pallas_call(
        functools.partial(_ffn_kernel, final_norm=final_norm),
        grid=(T // tm,),
        in_specs=[
            pl.BlockSpec((tm, D), lambda i: (i, 0)),
            pl.BlockSpec((tm, D_MOBA), lambda i: (i, 0)),
            pl.BlockSpec((tm, D_GMLP), lambda i: (i, 0)),
            pl.BlockSpec((1, D_MOBA), lambda i: (0, 0)),
            const((D_MOBA + D_GMLP, D)),
            pl.BlockSpec((1, D), lambda i: (0, 0)),
            const((D, d_ff)), const((D, d_ff)), const((d_ff, D)),
            pl.BlockSpec((1, D), lambda i: (0, 0)),
        ],
        out_specs=pl.BlockSpec((tm, D), lambda i: (i, 0)),
        out_shape=jax.ShapeDtypeStruct((T, D), F32),
        compiler_params=pltpu.CompilerParams(
            dimension_semantics=("arbitrary",), vmem_limit_bytes=VMEM_LIMIT),
        name="outproj_ffn",
    )(x2, yb, ya, gb, wo, gf, wg, wu, wd, gl)


def kernel(x, rel_bias, norm_mix, w_in, gmlp_ln_g, gmlp_ln_b, w_spatial, b_spatial, out_norm_b,
           out_norm_a, w_out, norm_ffn, w_gate, w_up, w_down, norm_final):
    B, S, D = x.shape
    depth = w_in.shape[0]
    assert S % MOBA_BLOCK == 0 and S // MOBA_BLOCK > MOBA_TOPK
    far = _t5_bucket_np(np.arange(MOBA_BLOCK + 1, S + MOBA_BLOCK))
    assert (far == far[0]).all()
    far_bucket = int(far[0])

    assert depth == 1, "weight casts ride along the single layer's kernels"
    T = B * S
    x2 = x.reshape(T, D)
    row = lambda v: v.reshape(1, -1)
    bias_t, bias_stats, (w_in_bf,) = _bias_tiles(rel_bias, far_bucket, [w_in[0]])
    qkv, ya, norms = _inproj(x2, row(norm_mix[0]), w_in_bf, row(gmlp_ln_g[0]), row(gmlp_ln_b[0]),
                             w_spatial[0], b_spatial[0], row(out_norm_a[0]), PROJ_ROWS)
    yb, (wo_bf, wg_bf, wu_bf, wd_bf) = _moba(qkv.reshape(B, S, -1), norms, bias_t, bias_stats,
                                             [w_out[0], w_gate[0], w_up[0], w_down[0]])
    out = _ffn(x2, yb.reshape(T, D_MOBA), ya, row(out_norm_b[0]), wo_bf, row(norm_ffn[0]),
               wg_bf, wu_bf, wd_bf, row(norm_final), FFN_ROWS, True)
    return out.reshape(B, S, D)
```

```python
import functools
import math

import numpy as np
import jax
import jax.numpy as jnp
from jax import lax
from jax.experimental import pallas as pl
from jax.experimental.pallas import tpu as pltpu

F32 = jnp.float32
BF16 = jnp.bfloat16

MOBA_HEADS = 8
MOBA_HEAD_DIM = 64
D_MOBA = MOBA_HEADS * MOBA_HEAD_DIM
GMLP_GROUPS = 8
GMLP_GROUP_DIM = 64
D_GMLP = GMLP_GROUPS * GMLP_GROUP_DIM
MOBA_BLOCK = 256
MOBA_TOPK = 3
GMLP_CHUNK = 128
NUM_BUCKETS = 32
REL_MAX_DISTANCE = 128
EPS = 1e-6
NEG = -1e9

LANES = 128
HEADS_PER_STEP = LANES // MOBA_HEAD_DIM
BF16_SUBLANES = 16
ONES_ROWS = BF16_SUBLANES
VT_ROWS = MOBA_HEAD_DIM + ONES_ROWS
VMEM_LIMIT = 56 * 1024 * 1024
PROJ_ROWS = 1024
FFN_ROWS = 512
FFN_COLS = 768
T_SLOTS = 4
P_SLOTS = 4
LOOKAHEAD = 3
FAST_LOOKAHEAD = 3

LOG2E = math.log2(math.e)
Q_SCALE = MOBA_HEAD_DIM ** -0.5 * LOG2E

BOUND_SLACK = 1.0 + 2.0 ** -6
SAFE_SPAN = 100.0


def _t5_bucket_np(dist):
    n = np.maximum(dist, 0)
    max_exact = NUM_BUCKETS // 2
    nf = np.maximum(n, max_exact).astype(np.float32)
    ratio = np.log(nf / np.float32(max_exact)) / np.float32(math.log(REL_MAX_DISTANCE / max_exact))
    large = max_exact + (ratio * np.float32(NUM_BUCKETS - max_exact)).astype(np.int32)
    large = np.minimum(large, NUM_BUCKETS - 1)
    return np.where(n < max_exact, n, large).astype(np.int32)


def _cast_specs(weights, n_steps, step_of):
    in_specs, out_specs, out_shapes = [], [], []
    for w in weights:
        rows, cols = w.shape
        period = next(p for p in range(1, n_steps + 1)
                      if n_steps % p == 0 and (rows * p) % (n_steps * BF16_SUBLANES) == 0)
        spec = pl.BlockSpec((rows * period // n_steps, cols),
                            functools.partial(lambda p, *ids: (step_of(*ids) // p, 0), period))
        in_specs.append(spec)
        out_specs.append(spec)
        out_shapes.append(jax.ShapeDtypeStruct(w.shape, BF16))
    return in_specs, out_specs, out_shapes


def _cast_slabs(src_refs, dst_refs):
    for src, dst in zip(src_refs, dst_refs):
        dst[...] = src[...].astype(BF16)


BIAS_FAR, BIAS_MAX, BIAS_MIN = 0, 1, 2
STAT_ROWS = 8
F32_SUBLANES = 8
BIAS_HEADS_PER_STEP = 4


def _bias_kernel(bucket_ref, rel_ref, *refs, far_bucket):
    n_cast = (len(refs) - 2) // 2
    out_ref, stat_ref = refs[n_cast], refs[n_cast + 1]
    _cast_slabs(refs[:n_cast], refs[n_cast + 2:])
    heads_here, L = out_ref.shape[0], out_ref.shape[2]
    span = bucket_ref.shape[1]
    bucket = bucket_ref[...]
    lane = lax.broadcasted_iota(jnp.int32, (1, span), 1)
    for hh in range(heads_here):
        h = pl.program_id(0) * heads_here + hh
        vals = [rel_ref[b, h] * LOG2E for b in range(NUM_BUCKETS)]
        prof = jnp.zeros((1, span), F32)
        for b in range(NUM_BUCKETS):
            prof = jnp.where(bucket == b, vals[b], prof)
        own = jnp.broadcast_to(jnp.where(lane < L, prof, NEG), (F32_SUBLANES, span))
        prev = jnp.broadcast_to(prof, (F32_SUBLANES, span))
        for g in range(L // F32_SUBLANES):
            k0 = g * F32_SUBLANES
            rows = slice(k0, k0 + F32_SUBLANES)
            out_ref[hh, 0, rows, :] = pltpu.roll(own, k0, 1, stride=1, stride_axis=0)[:, :L]
            out_ref[hh, 1, rows, :] = pltpu.roll(prev, (k0 + L) % span, 1, stride=1,
                                                 stride_axis=0)[:, :L]
        stats = {BIAS_FAR: vals[far_bucket], BIAS_MAX: functools.reduce(jnp.maximum, vals),
                 BIAS_MIN: functools.reduce(jnp.minimum, vals)}
        for r in range(STAT_ROWS):
            stat_ref[hh, r:r + 1, :] = jnp.full((1, LANES), stats.get(r, 0.0), F32)


def _bias_tiles(rel_bias, far_bucket, cast_weights):
    L = MOBA_BLOCK
    bucket = _t5_bucket_np(np.arange(2 * L))[None, :]
    n_steps = MOBA_HEADS // BIAS_HEADS_PER_STEP
    c_in, c_out, c_shapes = _cast_specs(cast_weights, n_steps, lambda i: i)
    outs = pl.pallas_call(
        functools.partial(_bias_kernel, far_bucket=far_bucket),
        grid=(n_steps,),
        in_specs=[
            pl.BlockSpec((1, 2 * L), lambda i: (0, 0)),
            pl.BlockSpec(memory_space=pltpu.SMEM),
        ] + c_in,
        out_specs=[pl.BlockSpec((BIAS_HEADS_PER_STEP, 2, L, L), lambda i: (i, 0, 0, 0)),
                   pl.BlockSpec((BIAS_HEADS_PER_STEP, STAT_ROWS, LANES), lambda i: (i, 0, 0))] + c_out,
        out_shape=[jax.ShapeDtypeStruct((MOBA_HEADS, 2, L, L), F32),
                   jax.ShapeDtypeStruct((MOBA_HEADS, STAT_ROWS, LANES), F32)] + c_shapes,
        compiler_params=pltpu.CompilerParams(
            dimension_semantics=("arbitrary",), vmem_limit_bytes=VMEM_LIMIT),
        name="bias_tiles",
    )(jnp.asarray(bucket), rel_bias, *cast_weights)
    return outs[0], outs[1], outs[2:]


def _inproj_kernel(x_ref, g_ref, w_ref, lng_ref, lnb_ref, wsp_ref, bsp_ref, gn_ref,
                   qkv_ref, ya_ref, norm_ref, uz_scr):
    C = GMLP_CHUNK
    tm = x_ref.shape[0]
    n_qkv = 3 * D_MOBA
    x = x_ref[...]
    ms = jnp.mean(x * x, axis=-1, keepdims=True)
    h = (x * lax.rsqrt(ms + EPS) * g_ref[...]).astype(BF16)
    uz_scr[...] = jnp.dot(h, w_ref[:, n_qkv:], preferred_element_type=F32)

    t_ix = lax.broadcasted_iota(jnp.int32, (C, C), 0)
    s_ix = lax.broadcasted_iota(jnp.int32, (C, C), 1)
    w_sp = [jnp.where(t_ix >= s_ix, wsp_ref[g], 0.0).astype(BF16) for g in range(GMLP_GROUPS)]
    first = lax.broadcasted_iota(jnp.int32, (1, LANES), 1) < GMLP_GROUP_DIM
    b_t = bsp_ref[...].T
    lane_group = lax.broadcasted_iota(jnp.int32, (1, D_GMLP), 1) // GMLP_GROUP_DIM
    b_sp = jnp.zeros((C, D_GMLP), F32)
    for g in range(GMLP_GROUPS):
        b_sp = jnp.where(lane_group == g, b_t[:, g:g + 1], b_sp)

    def qkv_cols(c0, c1):
        acc = jnp.dot(h, w_ref[:, c0:c1], preferred_element_type=F32)
        if c1 <= D_MOBA:
            acc = acc * Q_SCALE
        qkv_ref[:, c0:c1] = acc.astype(BF16)
        if c1 <= 2 * D_MOBA:
            sq = acc * acc
            for c in range(c0, c1, MOBA_HEAD_DIM):
                lanes = (c - c0) // LANES * LANES
                own = first if (c - c0) % LANES == 0 else jnp.logical_not(first)
                rows_sq = jnp.sum(jnp.where(own, sq[:, lanes:lanes + LANES], 0.0), axis=-1, keepdims=True)
                top = jnp.max(rows_sq, axis=0, keepdims=True)
                norm_ref[0, c // MOBA_HEAD_DIM:c // MOBA_HEAD_DIM + 1, :] = jnp.broadcast_to(top, (1, LANES))

    def gmlp_rows(c):
        rows = slice(c * C, (c + 1) * C)
        z = jax.nn.gelu(uz_scr[rows, D_GMLP:])
        mu = jnp.mean(z, axis=-1, keepdims=True)
        zc = z - mu
        var = jnp.mean(zc * zc, axis=-1, keepdims=True)
        zn = (zc * lax.rsqrt(var + EPS) * lng_ref[...] + lnb_ref[...]).astype(BF16)
        cols = []
        for gp in range(D_GMLP // LANES):
            zp = zn[:, gp * LANES:(gp + 1) * LANES]
            r0 = jnp.dot(w_sp[2 * gp], zp, preferred_element_type=F32)
            r1 = jnp.dot(w_sp[2 * gp + 1], zp, preferred_element_type=F32)
            cols.append(jnp.where(first, r0, r1))
        s = jnp.concatenate(cols, axis=1) + b_sp
        y = jax.nn.gelu(uz_scr[rows, :D_GMLP]) * s
        ms_y = jnp.mean(y * y, axis=-1, keepdims=True)
        ya_ref[rows, :] = (y * lax.rsqrt(ms_y + EPS) * gn_ref[...]).astype(BF16)

    col_blocks = [(c, c + 2 * LANES) for c in range(0, n_qkv, 2 * LANES)]
    n_chunks = tm // C
    for step in range(max(len(col_blocks), n_chunks)):
        if step < len(col_blocks):
            qkv_cols(*col_blocks[step])
        if step < n_chunks:
            gmlp_rows(step)


def _inproj(x2, g, w_bf, ln_g, ln_b, w_sp, b_sp, gn, tm):
    T, D = x2.shape
    n_all = w_bf.shape[1]
    n_qkv = 3 * D_MOBA
    C = GMLP_CHUNK
    vec = pl.BlockSpec((1, D_GMLP), lambda i: (0, 0))
    return pl.pallas_call(
        _inproj_kernel,
        grid=(T // tm,),
        in_specs=[
            pl.BlockSpec((tm, D), lambda i: (i, 0)),
            pl.BlockSpec((1, D), lambda i: (0, 0)),
            pl.BlockSpec((D, n_all), lambda i: (0, 0), pipeline_mode=pl.Buffered(1)),
            vec, vec,
            pl.BlockSpec((GMLP_GROUPS, C, C), lambda i: (0, 0, 0)),
            pl.BlockSpec((GMLP_GROUPS, C), lambda i: (0, 0)),
            vec,
        ],
        out_specs=[
            pl.BlockSpec((tm, n_qkv), lambda i: (i, 0)),
            pl.BlockSpec((tm, D_GMLP), lambda i: (i, 0)),
            pl.BlockSpec((1, 2 * MOBA_HEADS, LANES), lambda i: (i, 0, 0)),
        ],
        out_shape=[
            jax.ShapeDtypeStruct((T, n_qkv), BF16),
            jax.ShapeDtypeStruct((T, D_GMLP), BF16),
            jax.ShapeDtypeStruct((T // tm, 2 * MOBA_HEADS, LANES), F32),
        ],
        scratch_shapes=[pltpu.VMEM((tm, n_all - n_qkv), F32)],
        compiler_params=pltpu.CompilerParams(
            dimension_semantics=("arbitrary",), vmem_limit_bytes=VMEM_LIMIT),
        name="inproj_gmlp",
    )(x2, g, w_bf, ln_g, ln_b, w_sp, b_sp, gn)


def _moba_kernel(q_ref, k_ref, v_ref, norm_ref, bias_ref, stat_ref, *refs, n_cast):
    cast_src, o_ref, cast_dst = refs[:n_cast], refs[n_cast], refs[n_cast + 1:2 * n_cast + 1]
    vt_scr, qt_scr, slots = refs[2 * n_cast + 1], refs[2 * n_cast + 2], refs[2 * n_cast + 3:]
    t_scr, p_scr = slots[:T_SLOTS], slots[T_SLOTS:]
    L = MOBA_BLOCK
    S = q_ref.shape[1]
    NB = S // L
    hp = pl.program_id(0)
    lane = lax.broadcasted_iota(jnp.int32, (1, LANES), 1)
    head_lanes = [(lane >= hh * MOBA_HEAD_DIM) & (lane < (hh + 1) * MOBA_HEAD_DIM)
                  for hh in range(HEADS_PER_STEP)]
    row = lax.broadcasted_iota(jnp.int32, (NB, L), 0)
    bodies = [(i, hh) for i in range(NB) for hh in range(HEADS_PER_STEP)]
    nb = len(bodies)

    cfar = [stat_ref[hh, BIAS_FAR:BIAS_FAR + 1, :1] for hh in range(HEADS_PER_STEP)]
    bmax = [stat_ref[hh, BIAS_MAX:BIAS_MAX + 1, :1] for hh in range(HEADS_PER_STEP)]
    bmin = [stat_ref[hh, BIAS_MIN:BIAS_MIN + 1, :1] for hh in range(HEADS_PER_STEP)]

    def stage_operands():
        vt = v_ref[0].T
        for hh in range(HEADS_PER_STEP):
            vt_scr[hh * VT_ROWS:hh * VT_ROWS + MOBA_HEAD_DIM, :] = (
                vt[hh * MOBA_HEAD_DIM:(hh + 1) * MOBA_HEAD_DIM])
            vt_scr[hh * VT_ROWS + MOBA_HEAD_DIM:(hh + 1) * VT_ROWS, :] = jnp.ones((ONES_ROWS, S), BF16)
        qt = q_ref[0].T
        dim_head = lax.broadcasted_iota(jnp.int32, (LANES, 1), 0) // MOBA_HEAD_DIM
        for hh in range(HEADS_PER_STEP):
            qt_scr[hh * LANES:(hh + 1) * LANES, :] = jnp.where(dim_head == hh, qt, jnp.zeros_like(qt))
        km = jnp.concatenate(
            [jnp.sum(k_ref[0, n * L:(n + 1) * L, :].astype(F32), axis=0, keepdims=True) * (1.0 / L)
             for n in range(NB)], axis=0)
        kmm = jnp.concatenate([jnp.where(head_lanes[hh], km, 0.0) for hh in range(HEADS_PER_STEP)],
                              axis=0)
        km_hi = kmm.astype(BF16)
        return km_hi, (kmm - km_hi.astype(F32)).astype(BF16)

    shift, span = [], []
    for hh in range(HEADS_PER_STEP):
        head = hp * HEADS_PER_STEP + hh
        q_sq = jnp.max(norm_ref[:, pl.ds(head, 1), :], axis=0)[:, :1]
        k_sq = jnp.max(norm_ref[:, pl.ds(MOBA_HEADS + head, 1), :], axis=0)[:, :1]
        reach = jnp.sqrt(q_sq * k_sq) * BOUND_SLACK
        shift.append(reach + bmax[hh])
        span.append(2.0 * reach + (bmax[hh] - bmin[hh]))
    safe = jnp.max(functools.reduce(jnp.maximum, span)) <= SAFE_SPAN

    def block_select(k_means, n):
        i, hh = bodies[n]
        q_t = masked_q(n)
        gate = (jnp.dot(k_means[0], q_t, preferred_element_type=F32)
                + jnp.dot(k_means[1], q_t, preferred_element_type=F32))
        g = gate[hh * NB:(hh + 1) * NB]
        cnt = jnp.zeros((NB, L), jnp.int32)
        for m in range(i):
            gm = g[m:m + 1, :]
            beats = (gm > g) | ((gm == g) & (m < row))
            cnt = cnt + jnp.where(beats, 1, 0)
        return cnt < MOBA_TOPK

    def masked_q(n):
        i, hh = bodies[n]
        return qt_scr[hh * LANES:(hh + 1) * LANES, i * L:(i + 1) * L]

    def pv_stage(outs, n, p_ref):
        i, hh = bodies[n]
        nk = (i + 1) * L
        acc = jnp.dot(vt_scr[hh * VT_ROWS:(hh + 1) * VT_ROWS, 0:nk], p_ref[0:nk, :],
                      preferred_element_type=F32)
        outs[n] = acc[:MOBA_HEAD_DIM] / acc[MOBA_HEAD_DIM:MOBA_HEAD_DIM + 1]
        if hh == HEADS_PER_STEP - 1:
            pair = [outs.pop(n - HEADS_PER_STEP + 1 + h) for h in range(HEADS_PER_STEP)]
            o_ref[0, i * L:(i + 1) * L, :] = jnp.concatenate(pair, axis=0).T

    def exact_path():
        masks, qms, maxes, mrows, outs = {}, {}, {}, {}, {}
        _cast_slabs(cast_src, cast_dst)
        k_means = stage_operands()

        def begin_body(n):
            i, hh = bodies[n]
            qms[n] = masked_q(n)
            if i > MOBA_TOPK:
                sel = block_select(k_means, n)
                masks[n] = (jnp.where(sel, cfar[hh], NEG),
                            jnp.where(sel, 0.0, NEG))

        def score_tile(n, j):
            i, hh = bodies[n]
            t = jnp.dot(k_ref[0, j * L:(j + 1) * L, :], qms[n],
                        preferred_element_type=F32)
            if j == i:
                t = t + bias_ref[hh, 0]
            elif j == i - 1:
                t = t + bias_ref[hh, 1]
                if i > MOBA_TOPK:
                    t = t + masks[n][1][j:j + 1, :]
            elif i > MOBA_TOPK:
                t = t + masks[n][0][j:j + 1, :]
            else:
                t = t + cfar[hh]
            t_scr[n % T_SLOTS][j * L:(j + 1) * L, :] = t
            tm = jnp.max(t.reshape(L // 8, 8, L), axis=0)
            maxes[n] = tm if j == 0 else jnp.maximum(maxes[n], tm)
            if j == i:
                mrows[n] = jnp.max(maxes.pop(n), axis=0, keepdims=True)
                qms.pop(n)
                masks.pop(n, None)

        def exp_tile(n, j):
            t_ref, p_ref = t_scr[n % T_SLOTS], p_scr[n % T_SLOTS]
            p_ref[j * L:(j + 1) * L, :] = jnp.exp2(t_ref[j * L:(j + 1) * L, :] - mrows[n]).astype(BF16)

        for n0 in range(min(LOOKAHEAD, nb)):
            begin_body(n0)
            for j in range(bodies[n0][0] + 1):
                score_tile(n0, j)
        for n in range(nb):
            n_exp = bodies[n][0] + 1
            ahead = n + LOOKAHEAD
            n_score = bodies[ahead][0] + 1 if ahead < nb else 0
            if n_score:
                begin_body(ahead)
            for j in range(max(n_exp, n_score)):
                if j < n_score:
                    score_tile(ahead, j)
                if j < n_exp:
                    exp_tile(n, j)
            mrows.pop(n)
            pv_stage(outs, n, p_scr[n % T_SLOTS])

    def fast_path():
        outs = {}
        _cast_slabs(cast_src, cast_dst)
        k_means = stage_operands()
        shifted = {(hh, d): bias_ref[hh, d] - shift[hh]
                   for hh in range(HEADS_PER_STEP) for d in range(2)}

        def prob_tiles(n):
            i, hh = bodies[n]
            qm = masked_q(n)
            s_row = shift[hh]
            if i > MOBA_TOPK:
                sel = block_select(k_means, n)
                m_far = jnp.where(sel, cfar[hh], NEG) - s_row
                m_prev = jnp.where(sel, 0.0, NEG)
            for j in range(i + 1):
                e = jnp.dot(k_ref[0, j * L:(j + 1) * L, :], qm,
                            preferred_element_type=F32)
                if j == i:
                    e = e + shifted[hh, 0]
                elif j == i - 1:
                    e = e + shifted[hh, 1]
                    if i > MOBA_TOPK:
                        e = e + m_prev[j:j + 1, :]
                else:
                    e = e + (m_far[j:j + 1, :] if i > MOBA_TOPK else cfar[hh] - s_row)
                p_scr[n % P_SLOTS][j * L:(j + 1) * L, :] = jnp.exp2(e).astype(BF16)

        for n0 in range(min(FAST_LOOKAHEAD, nb)):
            prob_tiles(n0)
        for n in range(nb):
            if n + FAST_LOOKAHEAD < nb:
                prob_tiles(n + FAST_LOOKAHEAD)
            pv_stage(outs, n, p_scr[n % P_SLOTS])

    pl.when(safe)(fast_path)
    pl.when(jnp.logical_not(safe))(exact_path)


def _moba(qkv3, norms, bias_t, bias_stats, cast_weights):
    B, S, _ = qkv3.shape
    L = MOBA_BLOCK
    n_hp = MOBA_HEADS // HEADS_PER_STEP
    blk = lambda off: pl.BlockSpec((1, S, LANES), lambda hp, b: (b, 0, off + hp))
    c_in, c_out, c_shapes = _cast_specs(cast_weights, B * n_hp, lambda hp, b: hp * B + b)
    outs = pl.pallas_call(
        functools.partial(_moba_kernel, n_cast=len(cast_weights)),
        grid=(n_hp, B),
        in_specs=[
            blk(0), blk(n_hp), blk(2 * n_hp),
            pl.BlockSpec((norms.shape[0] // B,) + norms.shape[1:], lambda hp, b: (b, 0, 0)),
            pl.BlockSpec((HEADS_PER_STEP, 2, L, L), lambda hp, b: (hp, 0, 0, 0)),
            pl.BlockSpec((HEADS_PER_STEP, STAT_ROWS, LANES), lambda hp, b: (hp, 0, 0)),
        ] + c_in,
        out_specs=[pl.BlockSpec((1, S, LANES), lambda hp, b: (b, 0, hp))] + c_out,
        out_shape=[jax.ShapeDtypeStruct((B, S, D_MOBA), F32)] + c_shapes,
        scratch_shapes=[
            pltpu.VMEM((HEADS_PER_STEP * VT_ROWS, S), BF16),
            pltpu.VMEM((HEADS_PER_STEP * LANES, S), BF16),
        ] + [pltpu.VMEM((S, L), F32)] * T_SLOTS + [pltpu.VMEM((S, L), BF16)] * P_SLOTS,
        compiler_params=pltpu.CompilerParams(
            dimension_semantics=("arbitrary", "arbitrary"), vmem_limit_bytes=VMEM_LIMIT),
        name="moba",
    )(qkv3, qkv3, qkv3, norms, bias_t, bias_stats, *cast_weights)
    return outs[0], outs[1:]


def _ffn_kernel(x_ref, yb_ref, ya_ref, gb_ref, wo_ref, gf_ref, wg_ref, wu_ref, wd_ref, gl_ref,
                o_ref, a_scr, *, final_norm):
    half = x_ref.shape[0] // 2
    d_ff = wg_ref.shape[1]
    st = {}

    def attn_norm(r):
        yb = yb_ref[r:r + half, :]
        ms = jnp.mean(yb * yb, axis=-1, keepdims=True)
        st["ybn", r] = (yb * lax.rsqrt(ms + EPS) * gb_ref[...]).astype(BF16)

    def out_proj(r):
        st["x1", r] = (x_ref[r:r + half, :]
                       + jnp.dot(st.pop(("ybn", r)), wo_ref[:D_MOBA, :], preferred_element_type=F32)
                       + jnp.dot(ya_ref[r:r + half, :], wo_ref[D_MOBA:, :],
                                 preferred_element_type=F32))

    def ffn_norm(r):
        x1 = st["x1", r]
        ms = jnp.mean(x1 * x1, axis=-1, keepdims=True)
        st["h", r] = (x1 * lax.rsqrt(ms + EPS) * gf_ref[...]).astype(BF16)

    def gate_up(r):
        h = st.pop(("h", r))
        for c0 in range(0, d_ff, FFN_COLS):
            c1 = min(c0 + FFN_COLS, d_ff)
            g = jnp.dot(h, wg_ref[:, c0:c1], preferred_element_type=F32)
            u = jnp.dot(h, wu_ref[:, c0:c1], preferred_element_type=F32)
            a_scr[r:r + half, c0:c1] = (jax.nn.silu(g) * u).astype(BF16)

    def down(r):
        x2 = st.pop(("x1", r)) + jnp.dot(a_scr[r:r + half, :], wd_ref[...], preferred_element_type=F32)
        if final_norm:
            ms = jnp.mean(x2 * x2, axis=-1, keepdims=True)
            x2 = x2 * lax.rsqrt(ms + EPS) * gl_ref[...]
        o_ref[r:r + half, :] = x2

    stages = [attn_norm, out_proj, ffn_norm, gate_up, down]
    for k in range(len(stages) + 1):
        if k < len(stages):
            stages[k](0)
        if k >= 1:
            stages[k - 1](half)


def _ffn(x2, yb, ya, gb, wo, gf, wg, wu, wd, gl, tm, final_norm):
    T, D = x2.shape
    d_ff = wg.shape[1]
    const = lambda shape: pl.BlockSpec(shape, lambda i: (0, 0), pipeline_mode=pl.Buffered(1))
    return pl.pallas_call(
        functools.partial(_ffn_kernel, final_norm=final_norm),
        grid=(T // tm,),
        in_specs=[
            pl.BlockSpec((tm, D), lambda i: (i, 0)),
            pl.BlockSpec((tm, D_MOBA), lambda i: (i, 0)),
            pl.BlockSpec((tm, D_GMLP), lambda i: (i, 0)),
            pl.BlockSpec((1, D_MOBA), lambda i: (0, 0)),
            const((D_MOBA + D_GMLP, D)),
            pl.BlockSpec((1, D), lambda i: (0, 0)),
            const((D, d_ff)), const((D, d_ff)), const((d_ff, D)),
            pl.BlockSpec((1, D), lambda i: (0, 0)),
        ],
        out_specs=pl.BlockSpec((tm, D), lambda i: (i, 0)),
        out_shape=jax.ShapeDtypeStruct((T, D), F32),
        scratch_shapes=[pltpu.VMEM((tm, d_ff), BF16)],
        compiler_params=pltpu.CompilerParams(
            dimension_semantics=("arbitrary",), vmem_limit_bytes=VMEM_LIMIT),
        name="outproj_ffn",
    )(x2, yb, ya, gb, wo, gf, wg, wu, wd, gl)


def kernel(x, rel_bias, norm_mix, w_in, gmlp_ln_g, gmlp_ln_b, w_spatial, b_spatial, out_norm_b,
           out_norm_a, w_out, norm_ffn, w_gate, w_up, w_down, norm_final):
    B, S, D = x.shape
    depth = w_in.shape[0]
    assert S % MOBA_BLOCK == 0 and S // MOBA_BLOCK > MOBA_TOPK
    far = _t5_bucket_np(np.arange(MOBA_BLOCK + 1, S + MOBA_BLOCK))
    assert (far == far[0]).all()
    far_bucket = int(far[0])

    assert depth == 1, "weight casts ride along the single layer's kernels"
    T = B * S
    x2 = x.reshape(T, D)
    row = lambda v: v.reshape(1, -1)
    bias_t, bias_stats, (w_in_bf,) = _bias_tiles(rel_bias, far_bucket, [w_in[0]])
    qkv, ya, norms = _inproj(x2, row(norm_mix[0]), w_in_bf, row(gmlp_ln_g[0]), row(gmlp_ln_b[0]),
                             w_spatial[0], b_spatial[0], row(out_norm_a[0]), PROJ_ROWS)
    yb, (wo_bf, wg_bf, wu_bf, wd_bf) = _moba(qkv.reshape(B, S, -1), norms, bias_t, bias_stats,
                                             [w_out[0], w_gate[0], w_up[0], w_down[0]])
    out = _ffn(x2, yb.reshape(T, D_MOBA), ya, row(out_norm_b[0]), wo_bf, row(norm_ffn[0]),
               wg_bf, wu_bf, wd_bf, row(norm_final), FFN_ROWS, True)
    return out.reshape(B, S, D)
```

```python
import functools
import math

import numpy as np
import jax
import jax.numpy as jnp
from jax import lax
from jax.experimental import pallas as pl
from jax.experimental.pallas import tpu as pltpu

F32 = jnp.float32
BF16 = jnp.bfloat16

MOBA_HEADS = 8
MOBA_HEAD_DIM = 64
D_MOBA = MOBA_HEADS * MOBA_HEAD_DIM
GMLP_GROUPS = 8
GMLP_GROUP_DIM = 64
D_GMLP = GMLP_GROUPS * GMLP_GROUP_DIM
MOBA_BLOCK = 256
MOBA_TOPK = 3
GMLP_CHUNK = 128
NUM_BUCKETS = 32
REL_MAX_DISTANCE = 128
EPS = 1e-6
NEG = -1e9

LANES = 128
HEADS_PER_STEP = LANES // MOBA_HEAD_DIM
BF16_SUBLANES = 16
ONES_ROWS = BF16_SUBLANES
VT_ROWS = MOBA_HEAD_DIM + ONES_ROWS
VMEM_LIMIT = 56 * 1024 * 1024
PROJ_ROWS = 1024
FFN_ROWS = 512
FFN_COLS = 256
T_SLOTS = 4
P_SLOTS = 4
LOOKAHEAD = 3
FAST_LOOKAHEAD = 3

LOG2E = math.log2(math.e)
Q_SCALE = MOBA_HEAD_DIM ** -0.5 * LOG2E

BOUND_SLACK = 1.0 + 2.0 ** -6
SAFE_SPAN = 100.0


def _t5_bucket_np(dist):
    n = np.maximum(dist, 0)
    max_exact = NUM_BUCKETS // 2
    nf = np.maximum(n, max_exact).astype(np.float32)
    ratio = np.log(nf / np.float32(max_exact)) / np.float32(math.log(REL_MAX_DISTANCE / max_exact))
    large = max_exact + (ratio * np.float32(NUM_BUCKETS - max_exact)).astype(np.int32)
    large = np.minimum(large, NUM_BUCKETS - 1)
    return np.where(n < max_exact, n, large).astype(np.int32)


def _cast_specs(weights, n_steps, step_of):
    in_specs, out_specs, out_shapes = [], [], []
    for w in weights:
        rows, cols = w.shape
        period = next(p for p in range(1, n_steps + 1)
                      if n_steps % p == 0 and (rows * p) % (n_steps * BF16_SUBLANES) == 0)
        spec = pl.BlockSpec((rows * period // n_steps, cols),
                            functools.partial(lambda p, *ids: (step_of(*ids) // p, 0), period))
        in_specs.append(spec)
        out_specs.append(spec)
        out_shapes.append(jax.ShapeDtypeStruct(w.shape, BF16))
    return in_specs, out_specs, out_shapes


def _cast_slabs(src_refs, dst_refs):
    for src, dst in zip(src_refs, dst_refs):
        dst[...] = src[...].astype(BF16)


BIAS_FAR, BIAS_MAX, BIAS_MIN = 0, 1, 2
STAT_ROWS = 8
F32_SUBLANES = 8
BIAS_HEADS_PER_STEP = 4


def _bias_kernel(bucket_ref, rel_ref, *refs, far_bucket):
    n_cast = (len(refs) - 2) // 2
    out_ref, stat_ref = refs[n_cast], refs[n_cast + 1]
    _cast_slabs(refs[:n_cast], refs[n_cast + 2:])
    heads_here, L = out_ref.shape[0], out_ref.shape[2]
    span = bucket_ref.shape[1]
    bucket = bucket_ref[...]
    lane = lax.broadcasted_iota(jnp.int32, (1, span), 1)
    for hh in range(heads_here):
        h = pl.program_id(0) * heads_here + hh
        vals = [rel_ref[b, h] * LOG2E for b in range(NUM_BUCKETS)]
        prof = jnp.zeros((1, span), F32)
        for b in range(NUM_BUCKETS):
            prof = jnp.where(bucket == b, vals[b], prof)
        own = jnp.broadcast_to(jnp.where(lane < L, prof, NEG), (F32_SUBLANES, span))
        prev = jnp.broadcast_to(prof, (F32_SUBLANES, span))
        for g in range(L // F32_SUBLANES):
            k0 = g * F32_SUBLANES
            rows = slice(k0, k0 + F32_SUBLANES)
            out_ref[hh, 0, rows, :] = pltpu.roll(own, k0, 1, stride=1, stride_axis=0)[:, :L]
            out_ref[hh, 1, rows, :] = pltpu.roll(prev, (k0 + L) % span, 1, stride=1,
                                                 stride_axis=0)[:, :L]
        stats = {BIAS_FAR: vals[far_bucket], BIAS_MAX: functools.reduce(jnp.maximum, vals),
                 BIAS_MIN: functools.reduce(jnp.minimum, vals)}
        for r in range(STAT_ROWS):
            stat_ref[hh, r:r + 1, :] = jnp.full((1, LANES), stats.get(r, 0.0), F32)


def _bias_tiles(rel_bias, far_bucket, cast_weights):
    L = MOBA_BLOCK
    bucket = _t5_bucket_np(np.arange(2 * L))[None, :]
    n_steps = MOBA_HEADS // BIAS_HEADS_PER_STEP
    c_in, c_out, c_shapes = _cast_specs(cast_weights, n_steps, lambda i: i)
    outs = pl.pallas_call(
        functools.partial(_bias_kernel, far_bucket=far_bucket),
        grid=(n_steps,),
        in_specs=[
            pl.BlockSpec((1, 2 * L), lambda i: (0, 0)),
            pl.BlockSpec(memory_space=pltpu.SMEM),
        ] + c_in,
        out_specs=[pl.BlockSpec((BIAS_HEADS_PER_STEP, 2, L, L), lambda i: (i, 0, 0, 0)),
                   pl.BlockSpec((BIAS_HEADS_PER_STEP, STAT_ROWS, LANES), lambda i: (i, 0, 0))] + c_out,
        out_shape=[jax.ShapeDtypeStruct((MOBA_HEADS, 2, L, L), F32),
                   jax.ShapeDtypeStruct((MOBA_HEADS, STAT_ROWS, LANES), F32)] + c_shapes,
        compiler_params=pltpu.CompilerParams(
            dimension_semantics=("arbitrary",), vmem_limit_bytes=VMEM_LIMIT),
        name="bias_tiles",
    )(jnp.asarray(bucket), rel_bias, *cast_weights)
    return outs[0], outs[1], outs[2:]


def _inproj_kernel(x_ref, g_ref, w_ref, lng_ref, lnb_ref, wsp_ref, bsp_ref, gn_ref,
                   qkv_ref, ya_ref, norm_ref, uz_scr):
    C = GMLP_CHUNK
    tm = x_ref.shape[0]
    n_qkv = 3 * D_MOBA
    x = x_ref[...]
    ms = jnp.mean(x * x, axis=-1, keepdims=True)
    h = (x * lax.rsqrt(ms + EPS) * g_ref[...]).astype(BF16)
    uz_scr[...] = jnp.dot(h, w_ref[:, n_qkv:], preferred_element_type=F32)

    t_ix = lax.broadcasted_iota(jnp.int32, (C, C), 0)
    s_ix = lax.broadcasted_iota(jnp.int32, (C, C), 1)
    w_sp = [jnp.where(t_ix >= s_ix, wsp_ref[g], 0.0).astype(BF16) for g in range(GMLP_GROUPS)]
    first = lax.broadcasted_iota(jnp.int32, (1, LANES), 1) < GMLP_GROUP_DIM
    b_t = bsp_ref[...].T
    lane_group = lax.broadcasted_iota(jnp.int32, (1, D_GMLP), 1) // GMLP_GROUP_DIM
    b_sp = jnp.zeros((C, D_GMLP), F32)
    for g in range(GMLP_GROUPS):
        b_sp = jnp.where(lane_group == g, b_t[:, g:g + 1], b_sp)

    def qkv_cols(c0, c1):
        acc = jnp.dot(h, w_ref[:, c0:c1], preferred_element_type=F32)
        if c1 <= D_MOBA:
            acc = acc * Q_SCALE
        qkv_ref[:, c0:c1] = acc.astype(BF16)
        if c1 <= 2 * D_MOBA:
            sq = acc * acc
            for c in range(c0, c1, MOBA_HEAD_DIM):
                lanes = (c - c0) // LANES * LANES
                own = first if (c - c0) % LANES == 0 else jnp.logical_not(first)
                rows_sq = jnp.sum(jnp.where(own, sq[:, lanes:lanes + LANES], 0.0), axis=-1, keepdims=True)
                top = jnp.max(rows_sq, axis=0, keepdims=True)
                norm_ref[0, c // MOBA_HEAD_DIM:c // MOBA_HEAD_DIM + 1, :] = jnp.broadcast_to(top, (1, LANES))

    def gmlp_rows(c):
        rows = slice(c * C, (c + 1) * C)
        z = jax.nn.gelu(uz_scr[rows, D_GMLP:])
        mu = jnp.mean(z, axis=-1, keepdims=True)
        zc = z - mu
        var = jnp.mean(zc * zc, axis=-1, keepdims=True)
        zn = (zc * lax.rsqrt(var + EPS) * lng_ref[...] + lnb_ref[...]).astype(BF16)
        cols = []
        for gp in range(D_GMLP // LANES):
            zp = zn[:, gp * LANES:(gp + 1) * LANES]
            r0 = jnp.dot(w_sp[2 * gp], zp, preferred_element_type=F32)
            r1 = jnp.dot(w_sp[2 * gp + 1], zp, preferred_element_type=F32)
            cols.append(jnp.where(first, r0, r1))
        s = jnp.concatenate(cols, axis=1) + b_sp
        y = jax.nn.gelu(uz_scr[rows, :D_GMLP]) * s
        ms_y = jnp.mean(y * y, axis=-1, keepdims=True)
        ya_ref[rows, :] = (y * lax.rsqrt(ms_y + EPS) * gn_ref[...]).astype(BF16)

    col_blocks = [(c, c + 2 * LANES) for c in range(0, n_qkv, 2 * LANES)]
    n_chunks = tm // C
    for step in range(max(len(col_blocks), n_chunks)):
        if step < len(col_blocks):
            qkv_cols(*col_blocks[step])
        if step < n_chunks:
            gmlp_rows(step)


def _inproj(x2, g, w_bf, ln_g, ln_b, w_sp, b_sp, gn, tm):
    T, D = x2.shape
    n_all = w_bf.shape[1]
    n_qkv = 3 * D_MOBA
    C = GMLP_CHUNK
    vec = pl.BlockSpec((1, D_GMLP), lambda i: (0, 0))
    return pl.pallas_call(
        _inproj_kernel,
        grid=(T // tm,),
        in_specs=[
            pl.BlockSpec((tm, D), lambda i: (i, 0)),
            pl.BlockSpec((1, D), lambda i: (0, 0)),
            pl.BlockSpec((D, n_all), lambda i: (0, 0), pipeline_mode=pl.Buffered(1)),
            vec, vec,
            pl.BlockSpec((GMLP_GROUPS, C, C), lambda i: (0, 0, 0)),
            pl.BlockSpec((GMLP_GROUPS, C), lambda i: (0, 0)),
            vec,
        ],
        out_specs=[
            pl.BlockSpec((tm, n_qkv), lambda i: (i, 0)),
            pl.BlockSpec((tm, D_GMLP), lambda i: (i, 0)),
            pl.BlockSpec((1, 2 * MOBA_HEADS, LANES), lambda i: (i, 0, 0)),
        ],
        out_shape=[
            jax.ShapeDtypeStruct((T, n_qkv), BF16),
            jax.ShapeDtypeStruct((T, D_GMLP), BF16),
            jax.ShapeDtypeStruct((T // tm, 2 * MOBA_HEADS, LANES), F32),
        ],
        scratch_shapes=[pltpu.VMEM((tm, n_all - n_qkv), F32)],
        compiler_params=pltpu.CompilerParams(
            dimension_semantics=("arbitrary",), vmem_limit_bytes=VMEM_LIMIT),
        name="inproj_gmlp",
    )(x2, g, w_bf, ln_g, ln_b, w_sp, b_sp, gn)


def _moba_kernel(q_ref, k_ref, v_ref, norm_ref, bias_ref, stat_ref, *refs, n_cast):
    cast_src, o_ref, cast_dst = refs[:n_cast], refs[n_cast], refs[n_cast + 1:2 * n_cast + 1]
    vt_scr, qt_scr, slots = refs[2 * n_cast + 1], refs[2 * n_cast + 2], refs[2 * n_cast + 3:]
    t_scr, p_scr = slots[:T_SLOTS], slots[T_SLOTS:]
    L = MOBA_BLOCK
    S = q_ref.shape[1]
    NB = S // L
    hp = pl.program_id(0)
    lane = lax.broadcasted_iota(jnp.int32, (1, LANES), 1)
    head_lanes = [(lane >= hh * MOBA_HEAD_DIM) & (lane < (hh + 1) * MOBA_HEAD_DIM)
                  for hh in range(HEADS_PER_STEP)]
    row = lax.broadcasted_iota(jnp.int32, (NB, L), 0)
    bodies = [(i, hh) for i in range(NB) for hh in range(HEADS_PER_STEP)]
    nb = len(bodies)

    cfar = [stat_ref[hh, BIAS_FAR:BIAS_FAR + 1, :1] for hh in range(HEADS_PER_STEP)]
    bmax = [stat_ref[hh, BIAS_MAX:BIAS_MAX + 1, :1] for hh in range(HEADS_PER_STEP)]
    bmin = [stat_ref[hh, BIAS_MIN:BIAS_MIN + 1, :1] for hh in range(HEADS_PER_STEP)]

    def stage_operands():
        vt = v_ref[0].T
        for hh in range(HEADS_PER_STEP):
            vt_scr[hh * VT_ROWS:hh * VT_ROWS + MOBA_HEAD_DIM, :] = (
                vt[hh * MOBA_HEAD_DIM:(hh + 1) * MOBA_HEAD_DIM])
            vt_scr[hh * VT_ROWS + MOBA_HEAD_DIM:(hh + 1) * VT_ROWS, :] = jnp.ones((ONES_ROWS, S), BF16)
        qt = q_ref[0].T
        dim_head = lax.broadcasted_iota(jnp.int32, (LANES, 1), 0) // MOBA_HEAD_DIM
        for hh in range(HEADS_PER_STEP):
            qt_scr[hh * LANES:(hh + 1) * LANES, :] = jnp.where(dim_head == hh, qt, jnp.zeros_like(qt))
        km = jnp.concatenate(
            [jnp.sum(k_ref[0, n * L:(n + 1) * L, :].astype(F32), axis=0, keepdims=True) * (1.0 / L)
             for n in range(NB)], axis=0)
        kmm = jnp.concatenate([jnp.where(head_lanes[hh], km, 0.0) for hh in range(HEADS_PER_STEP)],
                              axis=0)
        km_hi = kmm.astype(BF16)
        return km_hi, (kmm - km_hi.astype(F32)).astype(BF16)

    shift, span = [], []
    for hh in range(HEADS_PER_STEP):
        head = hp * HEADS_PER_STEP + hh
        q_sq = jnp.max(norm_ref[:, pl.ds(head, 1), :], axis=0)[:, :1]
        k_sq = jnp.max(norm_ref[:, pl.ds(MOBA_HEADS + head, 1), :], axis=0)[:, :1]
        reach = jnp.sqrt(q_sq * k_sq) * BOUND_SLACK
        shift.append(reach + bmax[hh])
        span.append(2.0 * reach + (bmax[hh] - bmin[hh]))
    safe = jnp.max(functools.reduce(jnp.maximum, span)) <= SAFE_SPAN

    def block_select(k_means, n):
        i, hh = bodies[n]
        q_t = masked_q(n)
        gate = (jnp.dot(k_means[0], q_t, preferred_element_type=F32)
                + jnp.dot(k_means[1], q_t, preferred_element_type=F32))
        g = gate[hh * NB:(hh + 1) * NB]
        cnt = jnp.zeros((NB, L), jnp.int32)
        for m in range(i):
            gm = g[m:m + 1, :]
            beats = (gm > g) | ((gm == g) & (m < row))
            cnt = cnt + jnp.where(beats, 1, 0)
        return cnt < MOBA_TOPK

    def masked_q(n):
        i, hh = bodies[n]
        return qt_scr[hh * LANES:(hh + 1) * LANES, i * L:(i + 1) * L]

    def pv_stage(outs, n, p_ref):
        i, hh = bodies[n]
        nk = (i + 1) * L
        acc = jnp.dot(vt_scr[hh * VT_ROWS:(hh + 1) * VT_ROWS, 0:nk], p_ref[0:nk, :],
                      preferred_element_type=F32)
        outs[n] = acc[:MOBA_HEAD_DIM] / acc[MOBA_HEAD_DIM:MOBA_HEAD_DIM + 1]
        if hh == HEADS_PER_STEP - 1:
            pair = [outs.pop(n - HEADS_PER_STEP + 1 + h) for h in range(HEADS_PER_STEP)]
            o_ref[0, i * L:(i + 1) * L, :] = jnp.concatenate(pair, axis=0).T

    def exact_path():
        masks, qms, maxes, mrows, outs = {}, {}, {}, {}, {}
        _cast_slabs(cast_src, cast_dst)
        k_means = stage_operands()

        def begin_body(n):
            i, hh = bodies[n]
            qms[n] = masked_q(n)
            if i > MOBA_TOPK:
                sel = block_select(k_means, n)
                masks[n] = (jnp.where(sel, cfar[hh], NEG),
                            jnp.where(sel, 0.0, NEG))

        def score_tile(n, j):
            i, hh = bodies[n]
            t = jnp.dot(k_ref[0, j * L:(j + 1) * L, :], qms[n],
                        preferred_element_type=F32)
            if j == i:
                t = t + bias_ref[hh, 0]
            elif j == i - 1:
                t = t + bias_ref[hh, 1]
                if i > MOBA_TOPK:
                    t = t + masks[n][1][j:j + 1, :]
            elif i > MOBA_TOPK:
                t = t + masks[n][0][j:j + 1, :]
            else:
                t = t + cfar[hh]
            t_scr[n % T_SLOTS][j * L:(j + 1) * L, :] = t
            tm = jnp.max(t.reshape(L // 8, 8, L), axis=0)
            maxes[n] = tm if j == 0 else jnp.maximum(maxes[n], tm)
            if j == i:
                mrows[n] = jnp.max(maxes.pop(n), axis=0, keepdims=True)
                qms.pop(n)
                masks.pop(n, None)

        def exp_tile(n, j):
            t_ref, p_ref = t_scr[n % T_SLOTS], p_scr[n % T_SLOTS]
            p_ref[j * L:(j + 1) * L, :] = jnp.exp2(t_ref[j * L:(j + 1) * L, :] - mrows[n]).astype(BF16)

        for n0 in range(min(LOOKAHEAD, nb)):
            begin_body(n0)
            for j in range(bodies[n0][0] + 1):
                score_tile(n0, j)
        for n in range(nb):
            n_exp = bodies[n][0] + 1
            ahead = n + LOOKAHEAD
            n_score = bodies[ahead][0] + 1 if ahead < nb else 0
            if n_score:
                begin_body(ahead)
            for j in range(max(n_exp, n_score)):
                if j < n_score:
                    score_tile(ahead, j)
                if j < n_exp:
                    exp_tile(n, j)
            mrows.pop(n)
            pv_stage(outs, n, p_scr[n % T_SLOTS])

    def fast_path():
        outs = {}
        _cast_slabs(cast_src, cast_dst)
        k_means = stage_operands()
        shifted = {(hh, d): bias_ref[hh, d] - shift[hh]
                   for hh in range(HEADS_PER_STEP) for d in range(2)}

        def prob_tiles(n):
            i, hh = bodies[n]
            qm = masked_q(n)
            s_row = shift[hh]
            if i > MOBA_TOPK:
                sel = block_select(k_means, n)
                m_far = jnp.where(sel, cfar[hh], NEG) - s_row
                m_prev = jnp.where(sel, 0.0, NEG)
            for j in range(i + 1):
                e = jnp.dot(k_ref[0, j * L:(j + 1) * L, :], qm,
                            preferred_element_type=F32)
                if j == i:
                    e = e + shifted[hh, 0]
                elif j == i - 1:
                    e = e + shifted[hh, 1]
                    if i > MOBA_TOPK:
                        e = e + m_prev[j:j + 1, :]
                else:
                    e = e + (m_far[j:j + 1, :] if i > MOBA_TOPK else cfar[hh] - s_row)
                p_scr[n % P_SLOTS][j * L:(j + 1) * L, :] = jnp.exp2(e).astype(BF16)

        for n0 in range(min(FAST_LOOKAHEAD, nb)):
            prob_tiles(n0)
        for n in range(nb):
            if n + FAST_LOOKAHEAD < nb:
                prob_tiles(n + FAST_LOOKAHEAD)
            pv_stage(outs, n, p_scr[n % P_SLOTS])

    pl.when(safe)(fast_path)
    pl.when(jnp.logical_not(safe))(exact_path)


def _moba(qkv3, norms, bias_t, bias_stats, cast_weights):
    B, S, _ = qkv3.shape
    L = MOBA_BLOCK
    n_hp = MOBA_HEADS // HEADS_PER_STEP
    blk = lambda off: pl.BlockSpec((1, S, LANES), lambda hp, b: (b, 0, off + hp))
    c_in, c_out, c_shapes = _cast_specs(cast_weights, B * n_hp, lambda hp, b: hp * B + b)
    outs = pl.pallas_call(
        functools.partial(_moba_kernel, n_cast=len(cast_weights)),
        grid=(n_hp, B),
        in_specs=[
            blk(0), blk(n_hp), blk(2 * n_hp),
            pl.BlockSpec((norms.shape[0] // B,) + norms.shape[1:], lambda hp, b: (b, 0, 0)),
            pl.BlockSpec((HEADS_PER_STEP, 2, L, L), lambda hp, b: (hp, 0, 0, 0)),
            pl.BlockSpec((HEADS_PER_STEP, STAT_ROWS, LANES), lambda hp, b: (hp, 0, 0)),
        ] + c_in,
        out_specs=[pl.BlockSpec((1, S, LANES), lambda hp, b: (b, 0, hp))] + c_out,
        out_shape=[jax.ShapeDtypeStruct((B, S, D_MOBA), F32)] + c_shapes,
        scratch_shapes=[
            pltpu.VMEM((HEADS_PER_STEP * VT_ROWS, S), BF16),
            pltpu.VMEM((HEADS_PER_STEP * LANES, S), BF16),
        ] + [pltpu.VMEM((S, L), F32)] * T_SLOTS + [pltpu.VMEM((S, L), BF16)] * P_SLOTS,
        compiler_params=pltpu.CompilerParams(
            dimension_semantics=("arbitrary", "arbitrary"), vmem_limit_bytes=VMEM_LIMIT),
        name="moba",
    )(qkv3, qkv3, qkv3, norms, bias_t, bias_stats, *cast_weights)
    return outs[0], outs[1:]


def _ffn_kernel(x_ref, yb_ref, ya_ref, gb_ref, wo_ref, gf_ref, wg_ref, wu_ref, wd_ref, gl_ref,
                o_ref, a_scr, *, final_norm):
    half = x_ref.shape[0] // 2
    d_ff = wg_ref.shape[1]
    st = {}

    def attn_norm(r):
        yb = yb_ref[r:r + half, :]
        ms = jnp.mean(yb * yb, axis=-1, keepdims=True)
        st["ybn", r] = (yb * lax.rsqrt(ms + EPS) * gb_ref[...]).astype(BF16)

    def out_proj(r):
        st["x1", r] = (x_ref[r:r + half, :]
                       + jnp.dot(st.pop(("ybn", r)), wo_ref[:D_MOBA, :], preferred_element_type=F32)
                       + jnp.dot(ya_ref[r:r + half, :], wo_ref[D_MOBA:, :],
                                 preferred_element_type=F32))

    def ffn_norm(r):
        x1 = st["x1", r]
        ms = jnp.mean(x1 * x1, axis=-1, keepdims=True)
        st["h", r] = (x1 * lax.rsqrt(ms + EPS) * gf_ref[...]).astype(BF16)

    def gate_up(r):
        h = st.pop(("h", r))
        for c0 in range(0, d_ff, FFN_COLS):
            c1 = min(c0 + FFN_COLS, d_ff)
            g = jnp.dot(h, wg_ref[:, c0:c1], preferred_element_type=F32)
            u = jnp.dot(h, wu_ref[:, c0:c1], preferred_element_type=F32)
            a_scr[r:r + half, c0:c1] = (jax.nn.silu(g) * u).astype(BF16)

    def down(r):
        x2 = st.pop(("x1", r)) + jnp.dot(a_scr[r:r + half, :], wd_ref[...], preferred_element_type=F32)
        if final_norm:
            ms = jnp.mean(x2 * x2, axis=-1, keepdims=True)
            x2 = x2 * lax.rsqrt(ms + EPS) * gl_ref[...]
        o_ref[r:r + half, :] = x2

    stages = [attn_norm, out_proj, ffn_norm, gate_up, down]
    for k in range(len(stages) + 1):
        if k < len(stages):
            stages[k](0)
        if k >= 1:
            stages[k - 1](half)


def _ffn(x2, yb, ya, gb, wo, gf, wg, wu, wd, gl, tm, final_norm):
    T, D = x2.shape
    d_ff = wg.shape[1]
    const = lambda shape: pl.BlockSpec(shape, lambda i: (0, 0), pipeline_mode=pl.Buffered(1))
    return pl.pallas_call(
        functools.partial(_ffn_kernel, final_norm=final_norm),
        grid=(T // tm,),
        in_specs=[
            pl.BlockSpec((tm, D), lambda i: (i, 0)),
            pl.BlockSpec((tm, D_MOBA), lambda i: (i, 0)),
            pl.BlockSpec((tm, D_GMLP), lambda i: (i, 0)),
            pl.BlockSpec((1, D_MOBA), lambda i: (0, 0)),
            const((D_MOBA + D_GMLP, D)),
            pl.BlockSpec((1, D), lambda i: (0, 0)),
            const((D, d_ff)), const((D, d_ff)), const((d_ff, D)),
            pl.BlockSpec((1, D), lambda i: (0, 0)),
        ],
        out_specs=pl.BlockSpec((tm, D), lambda i: (i, 0)),
        out_shape=jax.ShapeDtypeStruct((T, D), F32),
        scratch_shapes=[pltpu.VMEM((tm, d_ff), BF16)],
        compiler_params=pltpu.CompilerParams(
            dimension_semantics=("arbitrary",), vmem_limit_bytes=VMEM_LIMIT),
        name="outproj_ffn",
    )(x2, yb, ya, gb, wo, gf, wg, wu, wd, gl)


def kernel(x, rel_bias, norm_mix, w_in, gmlp_ln_g, gmlp_ln_b, w_spatial, b_spatial, out_norm_b,
           out_norm_a, w_out, norm_ffn, w_gate, w_up, w_down, norm_final):
    B, S, D = x.shape
    depth = w_in.shape[0]
    assert S % MOBA_BLOCK == 0 and S // MOBA_BLOCK > MOBA_TOPK
    far = _t5_bucket_np(np.arange(MOBA_BLOCK + 1, S + MOBA_BLOCK))
    assert (far == far[0]).all()
    far_bucket = int(far[0])

    assert depth == 1, "weight casts ride along the single layer's kernels"
    T = B * S
    x2 = x.reshape(T, D)
    row = lambda v: v.reshape(1, -1)
    bias_t, bias_stats, (w_in_bf,) = _bias_tiles(rel_bias, far_bucket, [w_in[0]])
    qkv, ya, norms = _inproj(x2, row(norm_mix[0]), w_in_bf, row(gmlp_ln_g[0]), row(gmlp_ln_b[0]),
                             w_spatial[0], b_spatial[0], row(out_norm_a[0]), PROJ_ROWS)
    yb, (wo_bf, wg_bf, wu_bf, wd_bf) = _moba(qkv.reshape(B, S, -1), norms, bias_t, bias_stats,
                                             [w_out[0], w_gate[0], w_up[0], w_down[0]])
    out = _ffn(x2, yb.reshape(T, D_MOBA), ya, row(out_norm_b[0]), wo_bf, row(norm_ffn[0]),
               wg_bf, wu_bf, wd_bf, row(norm_final), FFN_ROWS, True)
    return out.reshape(B, S, D)
```

```python
import functools
import math

import numpy as np
import jax
import jax.numpy as jnp
from jax import lax
from jax.experimental import pallas as pl
from jax.experimental.pallas import tpu as pltpu

F32 = jnp.float32
BF16 = jnp.bfloat16

MOBA_HEADS = 8
MOBA_HEAD_DIM = 64
D_MOBA = MOBA_HEADS * MOBA_HEAD_DIM
GMLP_GROUPS = 8
GMLP_GROUP_DIM = 64
D_GMLP = GMLP_GROUPS * GMLP_GROUP_DIM
MOBA_BLOCK = 256
MOBA_TOPK = 3
GMLP_CHUNK = 128
NUM_BUCKETS = 32
REL_MAX_DISTANCE = 128
EPS = 1e-6
NEG = -1e9

LANES = 128
HEADS_PER_STEP = LANES // MOBA_HEAD_DIM
BF16_SUBLANES = 16
ONES_ROWS = BF16_SUBLANES
VT_ROWS = MOBA_HEAD_DIM + ONES_ROWS
VMEM_LIMIT = 56 * 1024 * 1024
PROJ_ROWS = 1024
FFN_ROWS = 1024
FFN_COLS = 256
T_SLOTS = 4
P_SLOTS = 4
LOOKAHEAD = 3
FAST_LOOKAHEAD = 3

LOG2E = math.log2(math.e)
Q_SCALE = MOBA_HEAD_DIM ** -0.5 * LOG2E

BOUND_SLACK = 1.0 + 2.0 ** -6
SAFE_SPAN = 100.0


def _t5_bucket_np(dist):
    n = np.maximum(dist, 0)
    max_exact = NUM_BUCKETS // 2
    nf = np.maximum(n, max_exact).astype(np.float32)
    ratio = np.log(nf / np.float32(max_exact)) / np.float32(math.log(REL_MAX_DISTANCE / max_exact))
    large = max_exact + (ratio * np.float32(NUM_BUCKETS - max_exact)).astype(np.int32)
    large = np.minimum(large, NUM_BUCKETS - 1)
    return np.where(n < max_exact, n, large).astype(np.int32)


def _cast_specs(weights, n_steps, step_of):
    in_specs, out_specs, out_shapes = [], [], []
    for w in weights:
        rows, cols = w.shape
        period = next(p for p in range(1, n_steps + 1)
                      if n_steps % p == 0 and (rows * p) % (n_steps * BF16_SUBLANES) == 0)
        spec = pl.BlockSpec((rows * period // n_steps, cols),
                            functools.partial(lambda p, *ids: (step_of(*ids) // p, 0), period))
        in_specs.append(spec)
        out_specs.append(spec)
        out_shapes.append(jax.ShapeDtypeStruct(w.shape, BF16))
    return in_specs, out_specs, out_shapes


def _cast_slabs(src_refs, dst_refs):
    for src, dst in zip(src_refs, dst_refs):
        dst[...] = src[...].astype(BF16)


BIAS_FAR, BIAS_MAX, BIAS_MIN = 0, 1, 2
STAT_ROWS = 8
F32_SUBLANES = 8
BIAS_HEADS_PER_STEP = 4


def _bias_kernel(bucket_ref, rel_ref, *refs, far_bucket):
    n_cast = (len(refs) - 2) // 2
    out_ref, stat_ref = refs[n_cast], refs[n_cast + 1]
    _cast_slabs(refs[:n_cast], refs[n_cast + 2:])
    heads_here, L = out_ref.shape[0], out_ref.shape[2]
    span = bucket_ref.shape[1]
    bucket = bucket_ref[...]
    lane = lax.broadcasted_iota(jnp.int32, (1, span), 1)
    for hh in range(heads_here):
        h = pl.program_id(0) * heads_here + hh
        vals = [rel_ref[b, h] * LOG2E for b in range(NUM_BUCKETS)]
        prof = jnp.zeros((1, span), F32)
        for b in range(NUM_BUCKETS):
            prof = jnp.where(bucket == b, vals[b], prof)
        own = jnp.broadcast_to(jnp.where(lane < L, prof, NEG), (F32_SUBLANES, span))
        prev = jnp.broadcast_to(prof, (F32_SUBLANES, span))
        for g in range(L // F32_SUBLANES):
            k0 = g * F32_SUBLANES
            rows = slice(k0, k0 + F32_SUBLANES)
            out_ref[hh, 0, rows, :] = pltpu.roll(own, k0, 1, stride=1, stride_axis=0)[:, :L]
            out_ref[hh, 1, rows, :] = pltpu.roll(prev, (k0 + L) % span, 1, stride=1,
                                                 stride_axis=0)[:, :L]
        stats = {BIAS_FAR: vals[far_bucket], BIAS_MAX: functools.reduce(jnp.maximum, vals),
                 BIAS_MIN: functools.reduce(jnp.minimum, vals)}
        for r in range(STAT_ROWS):
            stat_ref[hh, r:r + 1, :] = jnp.full((1, LANES), stats.get(r, 0.0), F32)


def _bias_tiles(rel_bias, far_bucket, cast_weights):
    L = MOBA_BLOCK
    bucket = _t5_bucket_np(np.arange(2 * L))[None, :]
    n_steps = MOBA_HEADS // BIAS_HEADS_PER_STEP
    c_in, c_out, c_shapes = _cast_specs(cast_weights, n_steps, lambda i: i)
    outs = pl.pallas_call(
        functools.partial(_bias_kernel, far_bucket=far_bucket),
        grid=(n_steps,),
        in_specs=[
            pl.BlockSpec((1, 2 * L), lambda i: (0, 0)),
            pl.BlockSpec(memory_space=pltpu.SMEM),
        ] + c_in,
        out_specs=[pl.BlockSpec((BIAS_HEADS_PER_STEP, 2, L, L), lambda i: (i, 0, 0, 0)),
                   pl.BlockSpec((BIAS_HEADS_PER_STEP, STAT_ROWS, LANES), lambda i: (i, 0, 0))] + c_out,
        out_shape=[jax.ShapeDtypeStruct((MOBA_HEADS, 2, L, L), F32),
                   jax.ShapeDtypeStruct((MOBA_HEADS, STAT_ROWS, LANES), F32)] + c_shapes,
        compiler_params=pltpu.CompilerParams(
            dimension_semantics=("arbitrary",), vmem_limit_bytes=VMEM_LIMIT),
        name="bias_tiles",
    )(jnp.asarray(bucket), rel_bias, *cast_weights)
    return outs[0], outs[1], outs[2:]


def _inproj_kernel(x_ref, g_ref, w_ref, lng_ref, lnb_ref, wsp_ref, bsp_ref, gn_ref,
                   qkv_ref, ya_ref, norm_ref, uz_scr):
    C = GMLP_CHUNK
    tm = x_ref.shape[0]
    n_qkv = 3 * D_MOBA
    x = x_ref[...]
    ms = jnp.mean(x * x, axis=-1, keepdims=True)
    h = (x * lax.rsqrt(ms + EPS) * g_ref[...]).astype(BF16)
    uz_scr[...] = jnp.dot(h, w_ref[:, n_qkv:], preferred_element_type=F32)

    t_ix = lax.broadcasted_iota(jnp.int32, (C, C), 0)
    s_ix = lax.broadcasted_iota(jnp.int32, (C, C), 1)
    w_sp = [jnp.where(t_ix >= s_ix, wsp_ref[g], 0.0).astype(BF16) for g in range(GMLP_GROUPS)]
    first = lax.broadcasted_iota(jnp.int32, (1, LANES), 1) < GMLP_GROUP_DIM
    b_t = bsp_ref[...].T
    lane_group = lax.broadcasted_iota(jnp.int32, (1, D_GMLP), 1) // GMLP_GROUP_DIM
    b_sp = jnp.zeros((C, D_GMLP), F32)
    for g in range(GMLP_GROUPS):
        b_sp = jnp.where(lane_group == g, b_t[:, g:g + 1], b_sp)

    def qkv_cols(c0, c1):
        acc = jnp.dot(h, w_ref[:, c0:c1], preferred_element_type=F32)
        if c1 <= D_MOBA:
            acc = acc * Q_SCALE
        qkv_ref[:, c0:c1] = acc.astype(BF16)
        if c1 <= 2 * D_MOBA:
            sq = acc * acc
            for c in range(c0, c1, MOBA_HEAD_DIM):
                lanes = (c - c0) // LANES * LANES
                own = first if (c - c0) % LANES == 0 else jnp.logical_not(first)
                rows_sq = jnp.sum(jnp.where(own, sq[:, lanes:lanes + LANES], 0.0), axis=-1, keepdims=True)
                top = jnp.max(rows_sq, axis=0, keepdims=True)
                norm_ref[0, c // MOBA_HEAD_DIM:c // MOBA_HEAD_DIM + 1, :] = jnp.broadcast_to(top, (1, LANES))

    def gmlp_rows(c):
        rows = slice(c * C, (c + 1) * C)
        z = jax.nn.gelu(uz_scr[rows, D_GMLP:])
        mu = jnp.mean(z, axis=-1, keepdims=True)
        zc = z - mu
        var = jnp.mean(zc * zc, axis=-1, keepdims=True)
        zn = (zc * lax.rsqrt(var + EPS) * lng_ref[...] + lnb_ref[...]).astype(BF16)
        cols = []
        for gp in range(D_GMLP // LANES):
            zp = zn[:, gp * LANES:(gp + 1) * LANES]
            r0 = jnp.dot(w_sp[2 * gp], zp, preferred_element_type=F32)
            r1 = jnp.dot(w_sp[2 * gp + 1], zp, preferred_element_type=F32)
            cols.append(jnp.where(first, r0, r1))
        s = jnp.concatenate(cols, axis=1) + b_sp
        y = jax.nn.gelu(uz_scr[rows, :D_GMLP]) * s
        ms_y = jnp.mean(y * y, axis=-1, keepdims=True)
        ya_ref[rows, :] = (y * lax.rsqrt(ms_y + EPS) * gn_ref[...]).astype(BF16)

    col_blocks = [(c, c + 2 * LANES) for c in range(0, n_qkv, 2 * LANES)]
    n_chunks = tm // C
    for step in range(max(len(col_blocks), n_chunks)):
        if step < len(col_blocks):
            qkv_cols(*col_blocks[step])
        if step < n_chunks:
            gmlp_rows(step)


def _inproj(x2, g, w_bf, ln_g, ln_b, w_sp, b_sp, gn, tm):
    T, D = x2.shape
    n_all = w_bf.shape[1]
    n_qkv = 3 * D_MOBA
    C = GMLP_CHUNK
    vec = pl.BlockSpec((1, D_GMLP), lambda i: (0, 0))
    return pl.pallas_call(
        _inproj_kernel,
        grid=(T // tm,),
        in_specs=[
            pl.BlockSpec((tm, D), lambda i: (i, 0)),
            pl.BlockSpec((1, D), lambda i: (0, 0)),
            pl.BlockSpec((D, n_all), lambda i: (0, 0), pipeline_mode=pl.Buffered(1)),
            vec, vec,
            pl.BlockSpec((GMLP_GROUPS, C, C), lambda i: (0, 0, 0)),
            pl.BlockSpec((GMLP_GROUPS, C), lambda i: (0, 0)),
            vec,
        ],
        out_specs=[
            pl.BlockSpec((tm, n_qkv), lambda i: (i, 0)),
            pl.BlockSpec((tm, D_GMLP), lambda i: (i, 0)),
            pl.BlockSpec((1, 2 * MOBA_HEADS, LANES), lambda i: (i, 0, 0)),
        ],
        out_shape=[
            jax.ShapeDtypeStruct((T, n_qkv), BF16),
            jax.ShapeDtypeStruct((T, D_GMLP), BF16),
            jax.ShapeDtypeStruct((T // tm, 2 * MOBA_HEADS, LANES), F32),
        ],
        scratch_shapes=[pltpu.VMEM((tm, n_all - n_qkv), F32)],
        compiler_params=pltpu.CompilerParams(
            dimension_semantics=("arbitrary",), vmem_limit_bytes=VMEM_LIMIT),
        name="inproj_gmlp",
    )(x2, g, w_bf, ln_g, ln_b, w_sp, b_sp, gn)


def _moba_kernel(q_ref, k_ref, v_ref, norm_ref, bias_ref, stat_ref, *refs, n_cast):
    cast_src, o_ref, cast_dst = refs[:n_cast], refs[n_cast], refs[n_cast + 1:2 * n_cast + 1]
    vt_scr, qt_scr, slots = refs[2 * n_cast + 1], refs[2 * n_cast + 2], refs[2 * n_cast + 3:]
    t_scr, p_scr = slots[:T_SLOTS], slots[T_SLOTS:]
    L = MOBA_BLOCK
    S = q_ref.shape[1]
    NB = S // L
    hp = pl.program_id(0)
    lane = lax.broadcasted_iota(jnp.int32, (1, LANES), 1)
    head_lanes = [(lane >= hh * MOBA_HEAD_DIM) & (lane < (hh + 1) * MOBA_HEAD_DIM)
                  for hh in range(HEADS_PER_STEP)]
    row = lax.broadcasted_iota(jnp.int32, (NB, L), 0)
    bodies = [(i, hh) for i in range(NB) for hh in range(HEADS_PER_STEP)]
    nb = len(bodies)

    cfar = [stat_ref[hh, BIAS_FAR:BIAS_FAR + 1, :1] for hh in range(HEADS_PER_STEP)]
    bmax = [stat_ref[hh, BIAS_MAX:BIAS_MAX + 1, :1] for hh in range(HEADS_PER_STEP)]
    bmin = [stat_ref[hh, BIAS_MIN:BIAS_MIN + 1, :1] for hh in range(HEADS_PER_STEP)]

    def stage_operands():
        vt = v_ref[0].T
        for hh in range(HEADS_PER_STEP):
            vt_scr[hh * VT_ROWS:hh * VT_ROWS + MOBA_HEAD_DIM, :] = (
                vt[hh * MOBA_HEAD_DIM:(hh + 1) * MOBA_HEAD_DIM])
            vt_scr[hh * VT_ROWS + MOBA_HEAD_DIM:(hh + 1) * VT_ROWS, :] = jnp.ones((ONES_ROWS, S), BF16)
        qt = q_ref[0].T
        dim_head = lax.broadcasted_iota(jnp.int32, (LANES, 1), 0) // MOBA_HEAD_DIM
        for hh in range(HEADS_PER_STEP):
            qt_scr[hh * LANES:(hh + 1) * LANES, :] = jnp.where(dim_head == hh, qt, jnp.zeros_like(qt))
        km = jnp.concatenate(
            [jnp.sum(k_ref[0, n * L:(n + 1) * L, :].astype(F32), axis=0, keepdims=True) * (1.0 / L)
             for n in range(NB)], axis=0)
        kmm = jnp.concatenate([jnp.where(head_lanes[hh], km, 0.0) for hh in range(HEADS_PER_STEP)],
                              axis=0)
        km_hi = kmm.astype(BF16)
        return km_hi, (kmm - km_hi.astype(F32)).astype(BF16)

    shift, span = [], []
    for hh in range(HEADS_PER_STEP):
        head = hp * HEADS_PER_STEP + hh
        q_sq = jnp.max(norm_ref[:, pl.ds(head, 1), :], axis=0)[:, :1]
        k_sq = jnp.max(norm_ref[:, pl.ds(MOBA_HEADS + head, 1), :], axis=0)[:, :1]
        reach = jnp.sqrt(q_sq * k_sq) * BOUND_SLACK
        shift.append(reach + bmax[hh])
        span.append(2.0 * reach + (bmax[hh] - bmin[hh]))
    safe = jnp.max(functools.reduce(jnp.maximum, span)) <= SAFE_SPAN

    def block_select(k_means, n):
        i, hh = bodies[n]
        q_t = masked_q(n)
        gate = (jnp.dot(k_means[0], q_t, preferred_element_type=F32)
                + jnp.dot(k_means[1], q_t, preferred_element_type=F32))
        g = gate[hh * NB:(hh + 1) * NB]
        cnt = jnp.zeros((NB, L), jnp.int32)
        for m in range(i):
            gm = g[m:m + 1, :]
            beats = (gm > g) | ((gm == g) & (m < row))
            cnt = cnt + jnp.where(beats, 1, 0)
        return cnt < MOBA_TOPK

    def masked_q(n):
        i, hh = bodies[n]
        return qt_scr[hh * LANES:(hh + 1) * LANES, i * L:(i + 1) * L]

    def pv_stage(outs, n, p_ref):
        i, hh = bodies[n]
        nk = (i + 1) * L
        acc = jnp.dot(vt_scr[hh * VT_ROWS:(hh + 1) * VT_ROWS, 0:nk], p_ref[0:nk, :],
                      preferred_element_type=F32)
        outs[n] = acc[:MOBA_HEAD_DIM] / acc[MOBA_HEAD_DIM:MOBA_HEAD_DIM + 1]
        if hh == HEADS_PER_STEP - 1:
            pair = [outs.pop(n - HEADS_PER_STEP + 1 + h) for h in range(HEADS_PER_STEP)]
            o_ref[0, i * L:(i + 1) * L, :] = jnp.concatenate(pair, axis=0).T

    def exact_path():
        masks, qms, maxes, mrows, outs = {}, {}, {}, {}, {}
        _cast_slabs(cast_src, cast_dst)
        k_means = stage_operands()

        def begin_body(n):
            i, hh = bodies[n]
            qms[n] = masked_q(n)
            if i > MOBA_TOPK:
                sel = block_select(k_means, n)
                masks[n] = (jnp.where(sel, cfar[hh], NEG),
                            jnp.where(sel, 0.0, NEG))

        def score_tile(n, j):
            i, hh = bodies[n]
            t = jnp.dot(k_ref[0, j * L:(j + 1) * L, :], qms[n],
                        preferred_element_type=F32)
            if j == i:
                t = t + bias_ref[hh, 0]
            elif j == i - 1:
                t = t + bias_ref[hh, 1]
                if i > MOBA_TOPK:
                    t = t + masks[n][1][j:j + 1, :]
            elif i > MOBA_TOPK:
                t = t + masks[n][0][j:j + 1, :]
            else:
                t = t + cfar[hh]
            t_scr[n % T_SLOTS][j * L:(j + 1) * L, :] = t
            tm = jnp.max(t.reshape(L // 8, 8, L), axis=0)
            maxes[n] = tm if j == 0 else jnp.maximum(maxes[n], tm)
            if j == i:
                mrows[n] = jnp.max(maxes.pop(n), axis=0, keepdims=True)
                qms.pop(n)
                masks.pop(n, None)

        def exp_tile(n, j):
            t_ref, p_ref = t_scr[n % T_SLOTS], p_scr[n % T_SLOTS]
            p_ref[j * L:(j + 1) * L, :] = jnp.exp2(t_ref[j * L:(j + 1) * L, :] - mrows[n]).astype(BF16)

        for n0 in range(min(LOOKAHEAD, nb)):
            begin_body(n0)
            for j in range(bodies[n0][0] + 1):
                score_tile(n0, j)
        for n in range(nb):
            n_exp = bodies[n][0] + 1
            ahead = n + LOOKAHEAD
            n_score = bodies[ahead][0] + 1 if ahead < nb else 0
            if n_score:
                begin_body(ahead)
            for j in range(max(n_exp, n_score)):
                if j < n_score:
                    score_tile(ahead, j)
                if j < n_exp:
                    exp_tile(n, j)
            mrows.pop(n)
            pv_stage(outs, n, p_scr[n % T_SLOTS])

    def fast_path():
        outs = {}
        _cast_slabs(cast_src, cast_dst)
        k_means = stage_operands()
        shifted = {(hh, d): bias_ref[hh, d] - shift[hh]
                   for hh in range(HEADS_PER_STEP) for d in range(2)}

        def prob_tiles(n):
            i, hh = bodies[n]
            qm = masked_q(n)
            s_row = shift[hh]
            if i > MOBA_TOPK:
                sel = block_select(k_means, n)
                m_far = jnp.where(sel, cfar[hh], NEG) - s_row
                m_prev = jnp.where(sel, 0.0, NEG)
            for j in range(i + 1):
                e = jnp.dot(k_ref[0, j * L:(j + 1) * L, :], qm,
                            preferred_element_type=F32)
                if j == i:
                    e = e + shifted[hh, 0]
                elif j == i - 1:
                    e = e + shifted[hh, 1]
                    if i > MOBA_TOPK:
                        e = e + m_prev[j:j + 1, :]
                else:
                    e = e + (m_far[j:j + 1, :] if i > MOBA_TOPK else cfar[hh] - s_row)
                p_scr[n % P_SLOTS][j * L:(j + 1) * L, :] = jnp.exp2(e).astype(BF16)

        for n0 in range(min(FAST_LOOKAHEAD, nb)):
            prob_tiles(n0)
        for n in range(nb):
            if n + FAST_LOOKAHEAD < nb:
                prob_tiles(n + FAST_LOOKAHEAD)
            pv_stage(outs, n, p_scr[n % P_SLOTS])

    pl.when(safe)(fast_path)
    pl.when(jnp.logical_not(safe))(exact_path)


def _moba(qkv3, norms, bias_t, bias_stats, cast_weights):
    B, S, _ = qkv3.shape
    L = MOBA_BLOCK
    n_hp = MOBA_HEADS // HEADS_PER_STEP
    blk = lambda off: pl.BlockSpec((1, S, LANES), lambda hp, b: (b, 0, off + hp))
    c_in, c_out, c_shapes = _cast_specs(cast_weights, B * n_hp, lambda hp, b: hp * B + b)
    outs = pl.pallas_call(
        functools.partial(_moba_kernel, n_cast=len(cast_weights)),
        grid=(n_hp, B),
        in_specs=[
            blk(0), blk(n_hp), blk(2 * n_hp),
            pl.BlockSpec((norms.shape[0] // B,) + norms.shape[1:], lambda hp, b: (b, 0, 0)),
            pl.BlockSpec((HEADS_PER_STEP, 2, L, L), lambda hp, b: (hp, 0, 0, 0)),
            pl.BlockSpec((HEADS_PER_STEP, STAT_ROWS, LANES), lambda hp, b: (hp, 0, 0)),
        ] + c_in,
        out_specs=[pl.BlockSpec((1, S, LANES), lambda hp, b: (b, 0, hp))] + c_out,
        out_shape=[jax.ShapeDtypeStruct((B, S, D_MOBA), F32)] + c_shapes,
        scratch_shapes=[
            pltpu.VMEM((HEADS_PER_STEP * VT_ROWS, S), BF16),
            pltpu.VMEM((HEADS_PER_STEP * LANES, S), BF16),
        ] + [pltpu.VMEM((S, L), F32)] * T_SLOTS + [pltpu.VMEM((S, L), BF16)] * P_SLOTS,
        compiler_params=pltpu.CompilerParams(
            dimension_semantics=("arbitrary", "arbitrary"), vmem_limit_bytes=VMEM_LIMIT),
        name="moba",
    )(qkv3, qkv3, qkv3, norms, bias_t, bias_stats, *cast_weights)
    return outs[0], outs[1:]


def _ffn_kernel(x_ref, yb_ref, ya_ref, gb_ref, wo_ref, gf_ref, wg_ref, wu_ref, wd_ref, gl_ref,
                o_ref, a_scr, *, final_norm):
    half = x_ref.shape[0] // 2
    d_ff = wg_ref.shape[1]
    st = {}

    def attn_norm(r):
        yb = yb_ref[r:r + half, :]
        ms = jnp.mean(yb * yb, axis=-1, keepdims=True)
        st["ybn", r] = (yb * lax.rsqrt(ms + EPS) * gb_ref[...]).astype(BF16)

    def out_proj(r):
        st["x1", r] = (x_ref[r:r + half, :]
                       + jnp.dot(st.pop(("ybn", r)), wo_ref[:D_MOBA, :], preferred_element_type=F32)
                       + jnp.dot(ya_ref[r:r + half, :], wo_ref[D_MOBA:, :],
                                 preferred_element_type=F32))

    def ffn_norm(r):
        x1 = st["x1", r]
        ms = jnp.mean(x1 * x1, axis=-1, keepdims=True)
        st["h", r] = (x1 * lax.rsqrt(ms + EPS) * gf_ref[...]).astype(BF16)

    def gate_up(r):
        h = st.pop(("h", r))
        for c0 in range(0, d_ff, FFN_COLS):
            c1 = min(c0 + FFN_COLS, d_ff)
            g = jnp.dot(h, wg_ref[:, c0:c1], preferred_element_type=F32)
            u = jnp.dot(h, wu_ref[:, c0:c1], preferred_element_type=F32)
            a_scr[r:r + half, c0:c1] = (jax.nn.silu(g) * u).astype(BF16)

    def down(r):
        x2 = st.pop(("x1", r)) + jnp.dot(a_scr[r:r + half, :], wd_ref[...], preferred_element_type=F32)
        if final_norm:
            ms = jnp.mean(x2 * x2, axis=-1, keepdims=True)
            x2 = x2 * lax.rsqrt(ms + EPS) * gl_ref[...]
        o_ref[r:r + half, :] = x2

    stages = [attn_norm, out_proj, ffn_norm, gate_up, down]
    for k in range(len(stages) + 1):
        if k < len(stages):
            stages[k](0)
        if k >= 1:
            stages[k - 1](half)


def _ffn(x2, yb, ya, gb, wo, gf, wg, wu, wd, gl, tm, final_norm):
    T, D = x2.shape
    d_ff = wg.shape[1]
    const = lambda shape: pl.BlockSpec(shape, lambda i: (0, 0), pipeline_mode=pl.Buffered(1))
    return pl.pallas_call(
        functools.partial(_ffn_kernel, final_norm=final_norm),
        grid=(T // tm,),
        in_specs=[
            pl.BlockSpec((tm, D), lambda i: (i, 0)),
            pl.BlockSpec((tm, D_MOBA), lambda i: (i, 0)),
            pl.BlockSpec((tm, D_GMLP), lambda i: (i, 0)),
            pl.BlockSpec((1, D_MOBA), lambda i: (0, 0)),
            const((D_MOBA + D_GMLP, D)),
            pl.BlockSpec((1, D), lambda i: (0, 0)),
            const((D, d_ff)), const((D, d_ff)), const((d_ff, D)),
            pl.BlockSpec((1, D), lambda i: (0, 0)),
        ],
        out_specs=pl.BlockSpec((tm, D), lambda i: (i, 0)),
        out_shape=jax.ShapeDtypeStruct((T, D), F32),
        scratch_shapes=[pltpu.VMEM((tm, d_ff), BF16)],
        compiler_params=pltpu.CompilerParams(
            dimension_semantics=("arbitrary",), vmem_limit_bytes=VMEM_LIMIT),
        name="outproj_ffn",
    )(x2, yb, ya, gb, wo, gf, wg, wu, wd, gl)


def kernel(x, rel_bias, norm_mix, w_in, gmlp_ln_g, gmlp_ln_b, w_spatial, b_spatial, out_norm_b,
           out_norm_a, w_out, norm_ffn, w_gate, w_up, w_down, norm_final):
    B, S, D = x.shape
    depth = w_in.shape[0]
    assert S % MOBA_BLOCK == 0 and S // MOBA_BLOCK > MOBA_TOPK
    far = _t5_bucket_np(np.arange(MOBA_BLOCK + 1, S + MOBA_BLOCK))
    assert (far == far[0]).all()
    far_bucket = int(far[0])

    assert depth == 1, "weight casts ride along the single layer's kernels"
    T = B * S
    x2 = x.reshape(T, D)
    row = lambda v: v.reshape(1, -1)
    bias_t, bias_stats, (w_in_bf,) = _bias_tiles(rel_bias, far_bucket, [w_in[0]])
    qkv, ya, norms = _inproj(x2, row(norm_mix[0]), w_in_bf, row(gmlp_ln_g[0]), row(gmlp_ln_b[0]),
                             w_spatial[0], b_spatial[0], row(out_norm_a[0]), PROJ_ROWS)
    yb, (wo_bf, wg_bf, wu_bf, wd_bf) = _moba(qkv.reshape(B, S, -1), norms, bias_t, bias_stats,
                                             [w_out[0], w_gate[0], w_up[0], w_down[0]])
    out = _ffn(x2, yb.reshape(T, D_MOBA), ya, row(out_norm_b[0]), wo_bf, row(norm_ffn[0]),
               wg_bf, wu_bf, wd_bf, row(norm_final), FFN_ROWS, True)
    return out.reshape(B, S, D)
```

```python
import functools
import math

import numpy as np
import jax
import jax.numpy as jnp
from jax import lax
from jax.experimental import pallas as pl
from jax.experimental.pallas import tpu as pltpu

F32 = jnp.float32
BF16 = jnp.bfloat16

MOBA_HEADS = 8
MOBA_HEAD_DIM = 64
D_MOBA = MOBA_HEADS * MOBA_HEAD_DIM
GMLP_GROUPS = 8
GMLP_GROUP_DIM = 64
D_GMLP = GMLP_GROUPS * GMLP_GROUP_DIM
MOBA_BLOCK = 256
MOBA_TOPK = 3
GMLP_CHUNK = 128
NUM_BUCKETS = 32
REL_MAX_DISTANCE = 128
EPS = 1e-6
NEG = -1e9

LANES = 128
HEADS_PER_STEP = LANES // MOBA_HEAD_DIM
BF16_SUBLANES = 16
ONES_ROWS = BF16_SUBLANES
VT_ROWS = MOBA_HEAD_DIM + ONES_ROWS
VMEM_LIMIT = 56 * 1024 * 1024
PROJ_ROWS = 1024
FFN_ROWS = 512
FFN_COLS = 256
T_SLOTS = 4
P_SLOTS = 4
LOOKAHEAD = 3
FAST_LOOKAHEAD = 3

LOG2E = math.log2(math.e)
Q_SCALE = MOBA_HEAD_DIM ** -0.5 * LOG2E

BOUND_SLACK = 1.0 + 2.0 ** -6
SAFE_SPAN = 100.0


def _t5_bucket_np(dist):
    n = np.maximum(dist, 0)
    max_exact = NUM_BUCKETS // 2
    nf = np.maximum(n, max_exact).astype(np.float32)
    ratio = np.log(nf / np.float32(max_exact)) / np.float32(math.log(REL_MAX_DISTANCE / max_exact))
    large = max_exact + (ratio * np.float32(NUM_BUCKETS - max_exact)).astype(np.int32)
    large = np.minimum(large, NUM_BUCKETS - 1)
    return np.where(n < max_exact, n, large).astype(np.int32)


def _cast_specs(weights, n_steps, step_of):
    in_specs, out_specs, out_shapes = [], [], []
    for w in weights:
        rows, cols = w.shape
        period = next(p for p in range(1, n_steps + 1)
                      if n_steps % p == 0 and (rows * p) % (n_steps * BF16_SUBLANES) == 0)
        spec = pl.BlockSpec((rows * period // n_steps, cols),
                            functools.partial(lambda p, *ids: (step_of(*ids) // p, 0), period))
        in_specs.append(spec)
        out_specs.append(spec)
        out_shapes.append(jax.ShapeDtypeStruct(w.shape, BF16))
    return in_specs, out_specs, out_shapes


def _cast_slabs(src_refs, dst_refs):
    for src, dst in zip(src_refs, dst_refs):
        dst[...] = src[...].astype(BF16)


BIAS_FAR, BIAS_MAX, BIAS_MIN = 0, 1, 2
STAT_ROWS = 8
F32_SUBLANES = 8
BIAS_HEADS_PER_STEP = 4


def _bias_kernel(bucket_ref, rel_ref, *refs, far_bucket):
    n_cast = (len(refs) - 2) // 2
    out_ref, stat_ref = refs[n_cast], refs[n_cast + 1]
    _cast_slabs(refs[:n_cast], refs[n_cast + 2:])
    heads_here, L = out_ref.shape[0], out_ref.shape[2]
    span = bucket_ref.shape[1]
    bucket = bucket_ref[...]
    lane = lax.broadcasted_iota(jnp.int32, (1, span), 1)
    for hh in range(heads_here):
        h = pl.program_id(0) * heads_here + hh
        vals = [rel_ref[b, h] * LOG2E for b in range(NUM_BUCKETS)]
        prof = jnp.zeros((1, span), F32)
        for b in range(NUM_BUCKETS):
            prof = jnp.where(bucket == b, vals[b], prof)
        own = jnp.broadcast_to(jnp.where(lane < L, prof, NEG), (F32_SUBLANES, span))
        prev = jnp.broadcast_to(prof, (F32_SUBLANES, span))
        for g in range(L // F32_SUBLANES):
            k0 = g * F32_SUBLANES
            rows = slice(k0, k0 + F32_SUBLANES)
            out_ref[hh, 0, rows, :] = pltpu.roll(own, k0, 1, stride=1, stride_axis=0)[:, :L]
            out_ref[hh, 1, rows, :] = pltpu.roll(prev, (k0 + L) % span, 1, stride=1,
                                                 stride_axis=0)[:, :L]
        stats = {BIAS_FAR: vals[far_bucket], BIAS_MAX: functools.reduce(jnp.maximum, vals),
                 BIAS_MIN: functools.reduce(jnp.minimum, vals)}
        for r in range(STAT_ROWS):
            stat_ref[hh, r:r + 1, :] = jnp.full((1, LANES), stats.get(r, 0.0), F32)


def _bias_tiles(rel_bias, far_bucket, cast_weights):
    L = MOBA_BLOCK
    bucket = _t5_bucket_np(np.arange(2 * L))[None, :]
    n_steps = MOBA_HEADS // BIAS_HEADS_PER_STEP
    c_in, c_out, c_shapes = _cast_specs(cast_weights, n_steps, lambda i: i)
    outs = pl.pallas_call(
        functools.partial(_bias_kernel, far_bucket=far_bucket),
        grid=(n_steps,),
        in_specs=[
            pl.BlockSpec((1, 2 * L), lambda i: (0, 0)),
            pl.BlockSpec(memory_space=pltpu.SMEM),
        ] + c_in,
        out_specs=[pl.BlockSpec((BIAS_HEADS_PER_STEP, 2, L, L), lambda i: (i, 0, 0, 0)),
                   pl.BlockSpec((BIAS_HEADS_PER_STEP, STAT_ROWS, LANES), lambda i: (i, 0, 0))] + c_out,
        out_shape=[jax.ShapeDtypeStruct((MOBA_HEADS, 2, L, L), F32),
                   jax.ShapeDtypeStruct((MOBA_HEADS, STAT_ROWS, LANES), F32)] + c_shapes,
        compiler_params=pltpu.CompilerParams(
            dimension_semantics=("arbitrary",), vmem_limit_bytes=VMEM_LIMIT),
        name="bias_tiles",
    )(jnp.asarray(bucket), rel_bias, *cast_weights)
    return outs[0], outs[1], outs[2:]


def _inproj_kernel(x_ref, g_ref, w_ref, lng_ref, lnb_ref, wsp_ref, bsp_ref, gn_ref,
                   qkv_ref, ya_ref, norm_ref, uz_scr):
    C = GMLP_CHUNK
    tm = x_ref.shape[0]
    n_qkv = 3 * D_MOBA
    x = x_ref[...]
    ms = jnp.mean(x * x, axis=-1, keepdims=True)
    h = (x * lax.rsqrt(ms + EPS) * g_ref[...]).astype(BF16)
    uz_scr[...] = jnp.dot(h, w_ref[:, n_qkv:], preferred_element_type=F32)

    t_ix = lax.broadcasted_iota(jnp.int32, (C, C), 0)
    s_ix = lax.broadcasted_iota(jnp.int32, (C, C), 1)
    w_sp = [jnp.where(t_ix >= s_ix, wsp_ref[g], 0.0).astype(BF16) for g in range(GMLP_GROUPS)]
    first = lax.broadcasted_iota(jnp.int32, (1, LANES), 1) < GMLP_GROUP_DIM
    b_t = bsp_ref[...].T
    lane_group = lax.broadcasted_iota(jnp.int32, (1, D_GMLP), 1) // GMLP_GROUP_DIM
    b_sp = jnp.zeros((C, D_GMLP), F32)
    for g in range(GMLP_GROUPS):
        b_sp = jnp.where(lane_group == g, b_t[:, g:g + 1], b_sp)

    def qkv_cols(c0, c1):
        acc = jnp.dot(h, w_ref[:, c0:c1], preferred_element_type=F32)
        if c1 <= D_MOBA:
            acc = acc * Q_SCALE
        qkv_ref[:, c0:c1] = acc.astype(BF16)
        if c1 <= 2 * D_MOBA:
            sq = acc * acc
            for c in range(c0, c1, MOBA_HEAD_DIM):
                lanes = (c - c0) // LANES * LANES
                own = first if (c - c0) % LANES == 0 else jnp.logical_not(first)
                rows_sq = jnp.sum(jnp.where(own, sq[:, lanes:lanes + LANES], 0.0), axis=-1, keepdims=True)
                top = jnp.max(rows_sq, axis=0, keepdims=True)
                norm_ref[0, c // MOBA_HEAD_DIM:c // MOBA_HEAD_DIM + 1, :] = jnp.broadcast_to(top, (1, LANES))

    def gmlp_rows(c):
        rows = slice(c * C, (c + 1) * C)
        z = jax.nn.gelu(uz_scr[rows, D_GMLP:])
        mu = jnp.mean(z, axis=-1, keepdims=True)
        zc = z - mu
        var = jnp.mean(zc * zc, axis=-1, keepdims=True)
        zn = (zc * lax.rsqrt(var + EPS) * lng_ref[...] + lnb_ref[...]).astype(BF16)
        cols = []
        for gp in range(D_GMLP // LANES):
            zp = zn[:, gp * LANES:(gp + 1) * LANES]
            r0 = jnp.dot(w_sp[2 * gp], zp, preferred_element_type=F32)
            r1 = jnp.dot(w_sp[2 * gp + 1], zp, preferred_element_type=F32)
            cols.append(jnp.where(first, r0, r1))
        s = jnp.concatenate(cols, axis=1) + b_sp
        y = jax.nn.gelu(uz_scr[rows, :D_GMLP]) * s
        ms_y = jnp.mean(y * y, axis=-1, keepdims=True)
        ya_ref[rows, :] = (y * lax.rsqrt(ms_y + EPS) * gn_ref[...]).astype(BF16)

    col_blocks = [(c, c + 2 * LANES) for c in range(0, n_qkv, 2 * LANES)]
    n_chunks = tm // C
    for step in range(max(len(col_blocks), n_chunks)):
        if step < len(col_blocks):
            qkv_cols(*col_blocks[step])
        if step < n_chunks:
            gmlp_rows(step)


def _inproj(x2, g, w_bf, ln_g, ln_b, w_sp, b_sp, gn, tm):
    T, D = x2.shape
    n_all = w_bf.shape[1]
    n_qkv = 3 * D_MOBA
    C = GMLP_CHUNK
    vec = pl.BlockSpec((1, D_GMLP), lambda i: (0, 0))
    return pl.pallas_call(
        _inproj_kernel,
        grid=(T // tm,),
        in_specs=[
            pl.BlockSpec((tm, D), lambda i: (i, 0)),
            pl.BlockSpec((1, D), lambda i: (0, 0)),
            pl.BlockSpec((D, n_all), lambda i: (0, 0), pipeline_mode=pl.Buffered(1)),
            vec, vec,
            pl.BlockSpec((GMLP_GROUPS, C, C), lambda i: (0, 0, 0)),
            pl.BlockSpec((GMLP_GROUPS, C), lambda i: (0, 0)),
            vec,
        ],
        out_specs=[
            pl.BlockSpec((tm, n_qkv), lambda i: (i, 0)),
            pl.BlockSpec((tm, D_GMLP), lambda i: (i, 0)),
            pl.BlockSpec((1, 2 * MOBA_HEADS, LANES), lambda i: (i, 0, 0)),
        ],
        out_shape=[
            jax.ShapeDtypeStruct((T, n_qkv), BF16),
            jax.ShapeDtypeStruct((T, D_GMLP), BF16),
            jax.ShapeDtypeStruct((T // tm, 2 * MOBA_HEADS, LANES), F32),
        ],
        scratch_shapes=[pltpu.VMEM((tm, n_all - n_qkv), F32)],
        compiler_params=pltpu.CompilerParams(
            dimension_semantics=("arbitrary",), vmem_limit_bytes=VMEM_LIMIT),
        name="inproj_gmlp",
    )(x2, g, w_bf, ln_g, ln_b, w_sp, b_sp, gn)


def _moba_kernel(q_ref, k_ref, v_ref, norm_ref, bias_ref, stat_ref, *refs, n_cast):
    cast_src, o_ref, cast_dst = refs[:n_cast], refs[n_cast], refs[n_cast + 1:2 * n_cast + 1]
    vt_scr, qt_scr, slots = refs[2 * n_cast + 1], refs[2 * n_cast + 2], refs[2 * n_cast + 3:]
    t_scr, p_scr = slots[:T_SLOTS], slots[T_SLOTS:]
    L = MOBA_BLOCK
    S = q_ref.shape[1]
    NB = S // L
    hp = pl.program_id(0)
    lane = lax.broadcasted_iota(jnp.int32, (1, LANES), 1)
    head_lanes = [(lane >= hh * MOBA_HEAD_DIM) & (lane < (hh + 1) * MOBA_HEAD_DIM)
                  for hh in range(HEADS_PER_STEP)]
    row = lax.broadcasted_iota(jnp.int32, (NB, L), 0)
    bodies = [(i, hh) for i in range(NB) for hh in range(HEADS_PER_STEP)]
    nb = len(bodies)

    cfar = [stat_ref[hh, BIAS_FAR:BIAS_FAR + 1, :1] for hh in range(HEADS_PER_STEP)]
    bmax = [stat_ref[hh, BIAS_MAX:BIAS_MAX + 1, :1] for hh in range(HEADS_PER_STEP)]
    bmin = [stat_ref[hh, BIAS_MIN:BIAS_MIN + 1, :1] for hh in range(HEADS_PER_STEP)]

    def stage_operands():
        vt = v_ref[0].T
        for hh in range(HEADS_PER_STEP):
            vt_scr[hh * VT_ROWS:hh * VT_ROWS + MOBA_HEAD_DIM, :] = (
                vt[hh * MOBA_HEAD_DIM:(hh + 1) * MOBA_HEAD_DIM])
            vt_scr[hh * VT_ROWS + MOBA_HEAD_DIM:(hh + 1) * VT_ROWS, :] = jnp.ones((ONES_ROWS, S), BF16)
        qt = q_ref[0].T
        dim_head = lax.broadcasted_iota(jnp.int32, (LANES, 1), 0) // MOBA_HEAD_DIM
        for hh in range(HEADS_PER_STEP):
            qt_scr[hh * LANES:(hh + 1) * LANES, :] = jnp.where(dim_head == hh, qt, jnp.zeros_like(qt))
        km = jnp.concatenate(
            [jnp.sum(k_ref[0, n * L:(n + 1) * L, :].astype(F32), axis=0, keepdims=True) * (1.0 / L)
             for n in range(NB)], axis=0)
        kmm = jnp.concatenate([jnp.where(head_lanes[hh], km, 0.0) for hh in range(HEADS_PER_STEP)],
                              axis=0)
        km_hi = kmm.astype(BF16)
        return km_hi, (kmm - km_hi.astype(F32)).astype(BF16)

    shift, span = [], []
    for hh in range(HEADS_PER_STEP):
        head = hp * HEADS_PER_STEP + hh
        q_sq = jnp.max(norm_ref[:, pl.ds(head, 1), :], axis=0)[:, :1]
        k_sq = jnp.max(norm_ref[:, pl.ds(MOBA_HEADS + head, 1), :], axis=0)[:, :1]
        reach = jnp.sqrt(q_sq * k_sq) * BOUND_SLACK
        shift.append(reach + bmax[hh])
        span.append(2.0 * reach + (bmax[hh] - bmin[hh]))
    safe = jnp.max(functools.reduce(jnp.maximum, span)) <= SAFE_SPAN

    def block_select(k_means, n):
        i, hh = bodies[n]
        q_t = masked_q(n)
        both = jnp.dot(jnp.concatenate(k_means, axis=0), q_t, preferred_element_type=F32)
        gate = both[:HEADS_PER_STEP * NB] + both[HEADS_PER_STEP * NB:]
        g = gate[hh * NB:(hh + 1) * NB]
        cnt = jnp.zeros((NB, L), jnp.int32)
        for m in range(i):
            gm = g[m:m + 1, :]
            beats = (gm > g) | ((gm == g) & (m < row))
            cnt = cnt + jnp.where(beats, 1, 0)
        return cnt < MOBA_TOPK

    def masked_q(n):
        i, hh = bodies[n]
        return qt_scr[hh * LANES:(hh + 1) * LANES, i * L:(i + 1) * L]

    def pv_stage(outs, n, p_ref):
        i, hh = bodies[n]
        nk = (i + 1) * L
        acc = jnp.dot(vt_scr[hh * VT_ROWS:(hh + 1) * VT_ROWS, 0:nk], p_ref[0:nk, :],
                      preferred_element_type=F32)
        outs[n] = acc[:MOBA_HEAD_DIM] / acc[MOBA_HEAD_DIM:MOBA_HEAD_DIM + 1]
        if hh == HEADS_PER_STEP - 1:
            pair = [outs.pop(n - HEADS_PER_STEP + 1 + h) for h in range(HEADS_PER_STEP)]
            o_ref[0, i * L:(i + 1) * L, :] = jnp.concatenate(pair, axis=0).T

    def exact_path():
        masks, qms, maxes, mrows, outs = {}, {}, {}, {}, {}
        _cast_slabs(cast_src, cast_dst)
        k_means = stage_operands()

        def begin_body(n):
            i, hh = bodies[n]
            qms[n] = masked_q(n)
            if i > MOBA_TOPK:
                sel = block_select(k_means, n)
                masks[n] = (jnp.where(sel, cfar[hh], NEG),
                            jnp.where(sel, 0.0, NEG))

        def score_tile(n, j):
            i, hh = bodies[n]
            t = jnp.dot(k_ref[0, j * L:(j + 1) * L, :], qms[n],
                        preferred_element_type=F32)
            if j == i:
                t = t + bias_ref[hh, 0]
            elif j == i - 1:
                t = t + bias_ref[hh, 1]
                if i > MOBA_TOPK:
                    t = t + masks[n][1][j:j + 1, :]
            elif i > MOBA_TOPK:
                t = t + masks[n][0][j:j + 1, :]
            else:
                t = t + cfar[hh]
            t_scr[n % T_SLOTS][j * L:(j + 1) * L, :] = t
            tm = jnp.max(t.reshape(L // 8, 8, L), axis=0)
            maxes[n] = tm if j == 0 else jnp.maximum(maxes[n], tm)
            if j == i:
                mrows[n] = jnp.max(maxes.pop(n), axis=0, keepdims=True)
                qms.pop(n)
                masks.pop(n, None)

        def exp_tile(n, j):
            t_ref, p_ref = t_scr[n % T_SLOTS], p_scr[n % T_SLOTS]
            p_ref[j * L:(j + 1) * L, :] = jnp.exp2(t_ref[j * L:(j + 1) * L, :] - mrows[n]).astype(BF16)

        for n0 in range(min(LOOKAHEAD, nb)):
            begin_body(n0)
            for j in range(bodies[n0][0] + 1):
                score_tile(n0, j)
        for n in range(nb):
            n_exp = bodies[n][0] + 1
            ahead = n + LOOKAHEAD
            n_score = bodies[ahead][0] + 1 if ahead < nb else 0
            if n_score:
                begin_body(ahead)
            for j in range(max(n_exp, n_score)):
                if j < n_score:
                    score_tile(ahead, j)
                if j < n_exp:
                    exp_tile(n, j)
            mrows.pop(n)
            pv_stage(outs, n, p_scr[n % T_SLOTS])

    def fast_path():
        outs = {}
        _cast_slabs(cast_src, cast_dst)
        k_means = stage_operands()
        shifted = {(hh, d): bias_ref[hh, d] - shift[hh]
                   for hh in range(HEADS_PER_STEP) for d in range(2)}

        def prob_tiles(n):
            i, hh = bodies[n]
            qm = masked_q(n)
            s_row = shift[hh]
            if i > MOBA_TOPK:
                sel = block_select(k_means, n)
                m_far = jnp.where(sel, cfar[hh], NEG) - s_row
                m_prev = jnp.where(sel, 0.0, NEG)
            for j in range(i + 1):
                e = jnp.dot(k_ref[0, j * L:(j + 1) * L, :], qm,
                            preferred_element_type=F32)
                if j == i:
                    e = e + shifted[hh, 0]
                elif j == i - 1:
                    e = e + shifted[hh, 1]
                    if i > MOBA_TOPK:
                        e = e + m_prev[j:j + 1, :]
                else:
                    e = e + (m_far[j:j + 1, :] if i > MOBA_TOPK else cfar[hh] - s_row)
                p_scr[n % P_SLOTS][j * L:(j + 1) * L, :] = jnp.exp2(e).astype(BF16)

        for n0 in range(min(FAST_LOOKAHEAD, nb)):
            prob_tiles(n0)
        for n in range(nb):
            if n + FAST_LOOKAHEAD < nb:
                prob_tiles(n + FAST_LOOKAHEAD)
            pv_stage(outs, n, p_scr[n % P_SLOTS])

    pl.when(safe)(fast_path)
    pl.when(jnp.logical_not(safe))(exact_path)


def _moba(qkv3, norms, bias_t, bias_stats, cast_weights):
    B, S, _ = qkv3.shape
    L = MOBA_BLOCK
    n_hp = MOBA_HEADS // HEADS_PER_STEP
    blk = lambda off: pl.BlockSpec((1, S, LANES), lambda hp, b: (b, 0, off + hp))
    c_in, c_out, c_shapes = _cast_specs(cast_weights, B * n_hp, lambda hp, b: hp * B + b)
    outs = pl.pallas_call(
        functools.partial(_moba_kernel, n_cast=len(cast_weights)),
        grid=(n_hp, B),
        in_specs=[
            blk(0), blk(n_hp), blk(2 * n_hp),
            pl.BlockSpec((norms.shape[0] // B,) + norms.shape[1:], lambda hp, b: (b, 0, 0)),
            pl.BlockSpec((HEADS_PER_STEP, 2, L, L), lambda hp, b: (hp, 0, 0, 0)),
            pl.BlockSpec((HEADS_PER_STEP, STAT_ROWS, LANES), lambda hp, b: (hp, 0, 0)),
        ] + c_in,
        out_specs=[pl.BlockSpec((1, S, LANES), lambda hp, b: (b, 0, hp))] + c_out,
        out_shape=[jax.ShapeDtypeStruct((B, S, D_MOBA), F32)] + c_shapes,
        scratch_shapes=[
            pltpu.VMEM((HEADS_PER_STEP * VT_ROWS, S), BF16),
            pltpu.VMEM((HEADS_PER_STEP * LANES, S), BF16),
        ] + [pltpu.VMEM((S, L), F32)] * T_SLOTS + [pltpu.VMEM((S, L), BF16)] * P_SLOTS,
        compiler_params=pltpu.CompilerParams(
            dimension_semantics=("arbitrary", "arbitrary"), vmem_limit_bytes=VMEM_LIMIT),
        name="moba",
    )(qkv3, qkv3, qkv3, norms, bias_t, bias_stats, *cast_weights)
    return outs[0], outs[1:]


def _ffn_kernel(x_ref, yb_ref, ya_ref, gb_ref, wo_ref, gf_ref, wg_ref, wu_ref, wd_ref, gl_ref,
                o_ref, a_scr, *, final_norm):
    half = x_ref.shape[0] // 2
    d_ff = wg_ref.shape[1]
    st = {}

    def attn_norm(r):
        yb = yb_ref[r:r + half, :]
        ms = jnp.mean(yb * yb, axis=-1, keepdims=True)
        st["ybn", r] = (yb * lax.rsqrt(ms + EPS) * gb_ref[...]).astype(BF16)

    def out_proj(r):
        st["x1", r] = (x_ref[r:r + half, :]
                       + jnp.dot(st.pop(("ybn", r)), wo_ref[:D_MOBA, :], preferred_element_type=F32)
                       + jnp.dot(ya_ref[r:r + half, :], wo_ref[D_MOBA:, :],
                                 preferred_element_type=F32))

    def ffn_norm(r):
        x1 = st["x1", r]
        ms = jnp.mean(x1 * x1, axis=-1, keepdims=True)
        st["h", r] = (x1 * lax.rsqrt(ms + EPS) * gf_ref[...]).astype(BF16)

    def gate_up(r):
        h = st.pop(("h", r))
        for c0 in range(0, d_ff, FFN_COLS):
            c1 = min(c0 + FFN_COLS, d_ff)
            g = jnp.dot(h, wg_ref[:, c0:c1], preferred_element_type=F32)
            u = jnp.dot(h, wu_ref[:, c0:c1], preferred_element_type=F32)
            a_scr[r:r + half, c0:c1] = (jax.nn.silu(g) * u).astype(BF16)

    def down(r):
        x2 = st.pop(("x1", r)) + jnp.dot(a_scr[r:r + half, :], wd_ref[...], preferred_element_type=F32)
        if final_norm:
            ms = jnp.mean(x2 * x2, axis=-1, keepdims=True)
            x2 = x2 * lax.rsqrt(ms + EPS) * gl_ref[...]
        o_ref[r:r + half, :] = x2

    stages = [attn_norm, out_proj, ffn_norm, gate_up, down]
    for k in range(len(stages) + 1):
        if k < len(stages):
            stages[k](0)
        if k >= 1:
            stages[k - 1](half)


def _ffn(x2, yb, ya, gb, wo, gf, wg, wu, wd, gl, tm, final_norm):
    T, D = x2.shape
    d_ff = wg.shape[1]
    const = lambda shape: pl.BlockSpec(shape, lambda i: (0, 0), pipeline_mode=pl.Buffered(1))
    return pl.pallas_call(
        functools.partial(_ffn_kernel, final_norm=final_norm),
        grid=(T // tm,),
        in_specs=[
            pl.BlockSpec((tm, D), lambda i: (i, 0)),
            pl.BlockSpec((tm, D_MOBA), lambda i: (i, 0)),
            pl.BlockSpec((tm, D_GMLP), lambda i: (i, 0)),
            pl.BlockSpec((1, D_MOBA), lambda i: (0, 0)),
            const((D_MOBA + D_GMLP, D)),
            pl.BlockSpec((1, D), lambda i: (0, 0)),
            const((D, d_ff)), const((D, d_ff)), const((d_ff, D)),
            pl.BlockSpec((1, D), lambda i: (0, 0)),
        ],
        out_specs=pl.BlockSpec((tm, D), lambda i: (i, 0)),
        out_shape=jax.ShapeDtypeStruct((T, D), F32),
        scratch_shapes=[pltpu.VMEM((tm, d_ff), BF16)],
        compiler_params=pltpu.CompilerParams(
            dimension_semantics=("arbitrary",), vmem_limit_bytes=VMEM_LIMIT),
        name="outproj_ffn",
    )(x2, yb, ya, gb, wo, gf, wg, wu, wd, gl)


def kernel(x, rel_bias, norm_mix, w_in, gmlp_ln_g, gmlp_ln_b, w_spatial, b_spatial, out_norm_b,
           out_norm_a, w_out, norm_ffn, w_gate, w_up, w_down, norm_final):
    B, S, D = x.shape
    depth = w_in.shape[0]
    assert S % MOBA_BLOCK == 0 and S // MOBA_BLOCK > MOBA_TOPK
    far = _t5_bucket_np(np.arange(MOBA_BLOCK + 1, S + MOBA_BLOCK))
    assert (far == far[0]).all()
    far_bucket = int(far[0])

    assert depth == 1, "weight casts ride along the single layer's kernels"
    T = B * S
    x2 = x.reshape(T, D)
    row = lambda v: v.reshape(1, -1)
    bias_t, bias_stats, (w_in_bf,) = _bias_tiles(rel_bias, far_bucket, [w_in[0]])
    qkv, ya, norms = _inproj(x2, row(norm_mix[0]), w_in_bf, row(gmlp_ln_g[0]), row(gmlp_ln_b[0]),
                             w_spatial[0], b_spatial[0], row(out_norm_a[0]), PROJ_ROWS)
    yb, (wo_bf, wg_bf, wu_bf, wd_bf) = _moba(qkv.reshape(B, S, -1), norms, bias_t, bias_stats,
                                             [w_out[0], w_gate[0], w_up[0], w_down[0]])
    out = _ffn(x2, yb.reshape(T, D_MOBA), ya, row(out_norm_b[0]), wo_bf, row(norm_ffn[0]),
               wg_bf, wu_bf, wd_bf, row(norm_final), FFN_ROWS, True)
    return out.reshape(B, S, D)
```

```python
import functools
import math

import numpy as np
import jax
import jax.numpy as jnp
from jax import lax
from jax.experimental import pallas as pl
from jax.experimental.pallas import tpu as pltpu

F32 = jnp.float32
BF16 = jnp.bfloat16

MOBA_HEADS = 8
MOBA_HEAD_DIM = 64
D_MOBA = MOBA_HEADS * MOBA_HEAD_DIM
GMLP_GROUPS = 8
GMLP_GROUP_DIM = 64
D_GMLP = GMLP_GROUPS * GMLP_GROUP_DIM
MOBA_BLOCK = 256
MOBA_TOPK = 3
GMLP_CHUNK = 128
NUM_BUCKETS = 32
REL_MAX_DISTANCE = 128
EPS = 1e-6
NEG = -1e9

LANES = 128
HEADS_PER_STEP = LANES // MOBA_HEAD_DIM
BF16_SUBLANES = 16
ONES_ROWS = BF16_SUBLANES
VT_ROWS = MOBA_HEAD_DIM + ONES_ROWS
VMEM_LIMIT = 56 * 1024 * 1024
PROJ_ROWS = 1024
FFN_ROWS = 512
FFN_COLS = 256
T_SLOTS = 4
P_SLOTS = 4
LOOKAHEAD = 3
FAST_LOOKAHEAD = 3

LOG2E = math.log2(math.e)
Q_SCALE = MOBA_HEAD_DIM ** -0.5 * LOG2E

BOUND_SLACK = 1.0 + 2.0 ** -6
SAFE_SPAN = 100.0


def _t5_bucket_np(dist):
    n = np.maximum(dist, 0)
    max_exact = NUM_BUCKETS // 2
    nf = np.maximum(n, max_exact).astype(np.float32)
    ratio = np.log(nf / np.float32(max_exact)) / np.float32(math.log(REL_MAX_DISTANCE / max_exact))
    large = max_exact + (ratio * np.float32(NUM_BUCKETS - max_exact)).astype(np.int32)
    large = np.minimum(large, NUM_BUCKETS - 1)
    return np.where(n < max_exact, n, large).astype(np.int32)


def _cast_specs(weights, n_steps, step_of):
    in_specs, out_specs, out_shapes = [], [], []
    for w in weights:
        rows, cols = w.shape
        period = next(p for p in range(1, n_steps + 1)
                      if n_steps % p == 0 and (rows * p) % (n_steps * BF16_SUBLANES) == 0)
        spec = pl.BlockSpec((rows * period // n_steps, cols),
                            functools.partial(lambda p, *ids: (step_of(*ids) // p, 0), period))
        in_specs.append(spec)
        out_specs.append(spec)
        out_shapes.append(jax.ShapeDtypeStruct(w.shape, BF16))
    return in_specs, out_specs, out_shapes


def _cast_slabs(src_refs, dst_refs):
    for src, dst in zip(src_refs, dst_refs):
        dst[...] = src[...].astype(BF16)


BIAS_FAR, BIAS_MAX, BIAS_MIN = 0, 1, 2
STAT_ROWS = 8
F32_SUBLANES = 8
BIAS_HEADS_PER_STEP = 4


def _bias_kernel(bucket_ref, rel_ref, *refs, far_bucket):
    n_cast = (len(refs) - 2) // 2
    out_ref, stat_ref = refs[n_cast], refs[n_cast + 1]
    _cast_slabs(refs[:n_cast], refs[n_cast + 2:])
    heads_here, L = out_ref.shape[0], out_ref.shape[2]
    span = bucket_ref.shape[1]
    bucket = bucket_ref[...]
    lane = lax.broadcasted_iota(jnp.int32, (1, span), 1)
    for hh in range(heads_here):
        h = pl.program_id(0) * heads_here + hh
        vals = [rel_ref[b, h] * LOG2E for b in range(NUM_BUCKETS)]
        prof = jnp.zeros((1, span), F32)
        for b in range(NUM_BUCKETS):
            prof = jnp.where(bucket == b, vals[b], prof)
        own = jnp.broadcast_to(jnp.where(lane < L, prof, NEG), (F32_SUBLANES, span))
        prev = jnp.broadcast_to(prof, (F32_SUBLANES, span))
        for g in range(L // F32_SUBLANES):
            k0 = g * F32_SUBLANES
            rows = slice(k0, k0 + F32_SUBLANES)
            out_ref[hh, 0, rows, :] = pltpu.roll(own, k0, 1, stride=1, stride_axis=0)[:, :L]
            out_ref[hh, 1, rows, :] = pltpu.roll(prev, (k0 + L) % span, 1, stride=1,
                                                 stride_axis=0)[:, :L]
        stats = {BIAS_FAR: vals[far_bucket], BIAS_MAX: functools.reduce(jnp.maximum, vals),
                 BIAS_MIN: functools.reduce(jnp.minimum, vals)}
        for r in range(STAT_ROWS):
            stat_ref[hh, r:r + 1, :] = jnp.full((1, LANES), stats.get(r, 0.0), F32)


def _bias_tiles(rel_bias, far_bucket, cast_weights):
    L = MOBA_BLOCK
    bucket = _t5_bucket_np(np.arange(2 * L))[None, :]
    n_steps = MOBA_HEADS // BIAS_HEADS_PER_STEP
    c_in, c_out, c_shapes = _cast_specs(cast_weights, n_steps, lambda i: i)
    outs = pl.pallas_call(
        functools.partial(_bias_kernel, far_bucket=far_bucket),
        grid=(n_steps,),
        in_specs=[
            pl.BlockSpec((1, 2 * L), lambda i: (0, 0)),
            pl.BlockSpec(memory_space=pltpu.SMEM),
        ] + c_in,
        out_specs=[pl.BlockSpec((BIAS_HEADS_PER_STEP, 2, L, L), lambda i: (i, 0, 0, 0)),
                   pl.BlockSpec((BIAS_HEADS_PER_STEP, STAT_ROWS, LANES), lambda i: (i, 0, 0))] + c_out,
        out_shape=[jax.ShapeDtypeStruct((MOBA_HEADS, 2, L, L), F32),
                   jax.ShapeDtypeStruct((MOBA_HEADS, STAT_ROWS, LANES), F32)] + c_shapes,
        compiler_params=pltpu.CompilerParams(
            dimension_semantics=("arbitrary",), vmem_limit_bytes=VMEM_LIMIT),
        name="bias_tiles",
    )(jnp.asarray(bucket), rel_bias, *cast_weights)
    return outs[0], outs[1], outs[2:]


def _inproj_kernel(x_ref, g_ref, w_ref, lng_ref, lnb_ref, wsp_ref, bsp_ref, gn_ref,
                   qkv_ref, ya_ref, norm_ref, uz_scr):
    C = GMLP_CHUNK
    tm = x_ref.shape[0]
    n_qkv = 3 * D_MOBA
    x = x_ref[...]
    ms = jnp.mean(x * x, axis=-1, keepdims=True)
    h = (x * lax.rsqrt(ms + EPS) * g_ref[...]).astype(BF16)
    uz_scr[...] = jnp.dot(h, w_ref[:, n_qkv:], preferred_element_type=F32)

    t_ix = lax.broadcasted_iota(jnp.int32, (C, C), 0)
    s_ix = lax.broadcasted_iota(jnp.int32, (C, C), 1)
    w_sp = [jnp.where(t_ix >= s_ix, wsp_ref[g], 0.0).astype(BF16) for g in range(GMLP_GROUPS)]
    first = lax.broadcasted_iota(jnp.int32, (1, LANES), 1) < GMLP_GROUP_DIM
    b_t = bsp_ref[...].T
    lane_group = lax.broadcasted_iota(jnp.int32, (1, D_GMLP), 1) // GMLP_GROUP_DIM
    b_sp = jnp.zeros((C, D_GMLP), F32)
    for g in range(GMLP_GROUPS):
        b_sp = jnp.where(lane_group == g, b_t[:, g:g + 1], b_sp)

    def qkv_cols(c0, c1):
        acc = jnp.dot(h, w_ref[:, c0:c1], preferred_element_type=F32)
        if c1 <= D_MOBA:
            acc = acc * Q_SCALE
        qkv_ref[:, c0:c1] = acc.astype(BF16)
        if c1 <= 2 * D_MOBA:
            sq = acc * acc
            for c in range(c0, c1, MOBA_HEAD_DIM):
                lanes = (c - c0) // LANES * LANES
                own = first if (c - c0) % LANES == 0 else jnp.logical_not(first)
                rows_sq = jnp.sum(jnp.where(own, sq[:, lanes:lanes + LANES], 0.0), axis=-1, keepdims=True)
                top = jnp.max(rows_sq, axis=0, keepdims=True)
                norm_ref[0, c // MOBA_HEAD_DIM:c // MOBA_HEAD_DIM + 1, :] = jnp.broadcast_to(top, (1, LANES))

    def gmlp_rows(c):
        rows = slice(c * C, (c + 1) * C)
        z = jax.nn.gelu(uz_scr[rows, D_GMLP:])
        mu = jnp.mean(z, axis=-1, keepdims=True)
        zc = z - mu
        var = jnp.mean(zc * zc, axis=-1, keepdims=True)
        zn = (zc * lax.rsqrt(var + EPS) * lng_ref[...] + lnb_ref[...]).astype(BF16)
        cols = []
        for gp in range(D_GMLP // LANES):
            zp = zn[:, gp * LANES:(gp + 1) * LANES]
            r0 = jnp.dot(w_sp[2 * gp], zp, preferred_element_type=F32)
            r1 = jnp.dot(w_sp[2 * gp + 1], zp, preferred_element_type=F32)
            cols.append(jnp.where(first, r0, r1))
        s = jnp.concatenate(cols, axis=1) + b_sp
        y = jax.nn.gelu(uz_scr[rows, :D_GMLP]) * s
        ms_y = jnp.mean(y * y, axis=-1, keepdims=True)
        ya_ref[rows, :] = (y * lax.rsqrt(ms_y + EPS) * gn_ref[...]).astype(BF16)

    col_blocks = [(c, c + 2 * LANES) for c in range(0, n_qkv, 2 * LANES)]
    n_chunks = tm // C
    for step in range(max(len(col_blocks), n_chunks)):
        if step < len(col_blocks):
            qkv_cols(*col_blocks[step])
        if step < n_chunks:
            gmlp_rows(step)


def _inproj(x2, g, w_bf, ln_g, ln_b, w_sp, b_sp, gn, tm):
    T, D = x2.shape
    n_all = w_bf.shape[1]
    n_qkv = 3 * D_MOBA
    C = GMLP_CHUNK
    vec = pl.BlockSpec((1, D_GMLP), lambda i: (0, 0))
    return pl.pallas_call(
        _inproj_kernel,
        grid=(T // tm,),
        in_specs=[
            pl.BlockSpec((tm, D), lambda i: (i, 0)),
            pl.BlockSpec((1, D), lambda i: (0, 0)),
            pl.BlockSpec((D, n_all), lambda i: (0, 0), pipeline_mode=pl.Buffered(1)),
            vec, vec,
            pl.BlockSpec((GMLP_GROUPS, C, C), lambda i: (0, 0, 0)),
            pl.BlockSpec((GMLP_GROUPS, C), lambda i: (0, 0)),
            vec,
        ],
        out_specs=[
            pl.BlockSpec((tm, n_qkv), lambda i: (i, 0)),
            pl.BlockSpec((tm, D_GMLP), lambda i: (i, 0)),
            pl.BlockSpec((1, 2 * MOBA_HEADS, LANES), lambda i: (i, 0, 0)),
        ],
        out_shape=[
            jax.ShapeDtypeStruct((T, n_qkv), BF16),
            jax.ShapeDtypeStruct((T, D_GMLP), BF16),
            jax.ShapeDtypeStruct((T // tm, 2 * MOBA_HEADS, LANES), F32),
        ],
        scratch_shapes=[pltpu.VMEM((tm, n_all - n_qkv), F32)],
        compiler_params=pltpu.CompilerParams(
            dimension_semantics=("arbitrary",), vmem_limit_bytes=VMEM_LIMIT),
        name="inproj_gmlp",
    )(x2, g, w_bf, ln_g, ln_b, w_sp, b_sp, gn)


def _moba_kernel(q_ref, k_ref, v_ref, norm_ref, bias_ref, stat_ref, *refs, n_cast):
    cast_src, o_ref, cast_dst = refs[:n_cast], refs[n_cast], refs[n_cast + 1:2 * n_cast + 1]
    vt_scr, qt_scr, slots = refs[2 * n_cast + 1], refs[2 * n_cast + 2], refs[2 * n_cast + 3:]
    t_scr, p_scr = slots[:T_SLOTS], slots[T_SLOTS:]
    L = MOBA_BLOCK
    S = q_ref.shape[1]
    NB = S // L
    hp = pl.program_id(0)
    lane = lax.broadcasted_iota(jnp.int32, (1, LANES), 1)
    head_lanes = [(lane >= hh * MOBA_HEAD_DIM) & (lane < (hh + 1) * MOBA_HEAD_DIM)
                  for hh in range(HEADS_PER_STEP)]
    row = lax.broadcasted_iota(jnp.int32, (NB, L), 0)
    bodies = [(i, hh) for i in range(NB) for hh in range(HEADS_PER_STEP)]
    nb = len(bodies)

    cfar = [stat_ref[hh, BIAS_FAR:BIAS_FAR + 1, :1] for hh in range(HEADS_PER_STEP)]
    bmax = [stat_ref[hh, BIAS_MAX:BIAS_MAX + 1, :1] for hh in range(HEADS_PER_STEP)]
    bmin = [stat_ref[hh, BIAS_MIN:BIAS_MIN + 1, :1] for hh in range(HEADS_PER_STEP)]

    def stage_operands():
        vt = v_ref[0].T
        for hh in range(HEADS_PER_STEP):
            vt_scr[hh * VT_ROWS:hh * VT_ROWS + MOBA_HEAD_DIM, :] = (
                vt[hh * MOBA_HEAD_DIM:(hh + 1) * MOBA_HEAD_DIM])
            vt_scr[hh * VT_ROWS + MOBA_HEAD_DIM:(hh + 1) * VT_ROWS, :] = jnp.ones((ONES_ROWS, S), BF16)
        qt = q_ref[0].T
        dim_head = lax.broadcasted_iota(jnp.int32, (LANES, 1), 0) // MOBA_HEAD_DIM
        for hh in range(HEADS_PER_STEP):
            qt_scr[hh * LANES:(hh + 1) * LANES, :] = jnp.where(dim_head == hh, qt, jnp.zeros_like(qt))
        qt_scr[HEADS_PER_STEP * LANES:, :] = qt
        km = jnp.concatenate(
            [jnp.sum(k_ref[0, n * L:(n + 1) * L, :].astype(F32), axis=0, keepdims=True) * (1.0 / L)
             for n in range(NB)], axis=0)
        kmm = jnp.concatenate([jnp.where(head_lanes[hh], km, 0.0) for hh in range(HEADS_PER_STEP)],
                              axis=0)
        km_hi = kmm.astype(BF16)
        return km_hi, (kmm - km_hi.astype(F32)).astype(BF16)

    shift, span = [], []
    for hh in range(HEADS_PER_STEP):
        head = hp * HEADS_PER_STEP + hh
        q_sq = jnp.max(norm_ref[:, pl.ds(head, 1), :], axis=0)[:, :1]
        k_sq = jnp.max(norm_ref[:, pl.ds(MOBA_HEADS + head, 1), :], axis=0)[:, :1]
        reach = jnp.sqrt(q_sq * k_sq) * BOUND_SLACK
        shift.append(reach + bmax[hh])
        span.append(2.0 * reach + (bmax[hh] - bmin[hh]))
    safe = jnp.max(functools.reduce(jnp.maximum, span)) <= SAFE_SPAN

    def block_select(k_means, gates, n):
        i, hh = bodies[n]
        if i not in gates:
            q_t = qt_scr[HEADS_PER_STEP * LANES:, i * L:(i + 1) * L]
            both = jnp.dot(jnp.concatenate(k_means, axis=0), q_t, preferred_element_type=F32)
            gates[i] = both[:HEADS_PER_STEP * NB] + both[HEADS_PER_STEP * NB:]
        g = gates[i][hh * NB:(hh + 1) * NB]
        cnt = jnp.zeros((NB, L), jnp.int32)
        for m in range(i):
            gm = g[m:m + 1, :]
            beats = (gm > g) | ((gm == g) & (m < row))
            cnt = cnt + jnp.where(beats, 1, 0)
        return cnt < MOBA_TOPK

    def masked_q(n):
        i, hh = bodies[n]
        return qt_scr[hh * LANES:(hh + 1) * LANES, i * L:(i + 1) * L]

    def pv_stage(outs, n, p_ref):
        i, hh = bodies[n]
        nk = (i + 1) * L
        acc = jnp.dot(vt_scr[hh * VT_ROWS:(hh + 1) * VT_ROWS, 0:nk], p_ref[0:nk, :],
                      preferred_element_type=F32)
        outs[n] = acc[:MOBA_HEAD_DIM] / acc[MOBA_HEAD_DIM:MOBA_HEAD_DIM + 1]
        if hh == HEADS_PER_STEP - 1:
            pair = [outs.pop(n - HEADS_PER_STEP + 1 + h) for h in range(HEADS_PER_STEP)]
            o_ref[0, i * L:(i + 1) * L, :] = jnp.concatenate(pair, axis=0).T

    def exact_path():
        gates, masks, qms, maxes, mrows, outs = {}, {}, {}, {}, {}, {}
        _cast_slabs(cast_src, cast_dst)
        k_means = stage_operands()

        def begin_body(n):
            i, hh = bodies[n]
            qms[n] = masked_q(n)
            if i > MOBA_TOPK:
                sel = block_select(k_means, gates, n)
                masks[n] = (jnp.where(sel, cfar[hh], NEG),
                            jnp.where(sel, 0.0, NEG))

        def score_tile(n, j):
            i, hh = bodies[n]
            t = jnp.dot(k_ref[0, j * L:(j + 1) * L, :], qms[n],
                        preferred_element_type=F32)
            if j == i:
                t = t + bias_ref[hh, 0]
            elif j == i - 1:
                t = t + bias_ref[hh, 1]
                if i > MOBA_TOPK:
                    t = t + masks[n][1][j:j + 1, :]
            elif i > MOBA_TOPK:
                t = t + masks[n][0][j:j + 1, :]
            else:
                t = t + cfar[hh]
            t_scr[n % T_SLOTS][j * L:(j + 1) * L, :] = t
            tm = jnp.max(t.reshape(L // 8, 8, L), axis=0)
            maxes[n] = tm if j == 0 else jnp.maximum(maxes[n], tm)
            if j == i:
                mrows[n] = jnp.max(maxes.pop(n), axis=0, keepdims=True)
                qms.pop(n)
                masks.pop(n, None)

        def exp_tile(n, j):
            t_ref, p_ref = t_scr[n % T_SLOTS], p_scr[n % T_SLOTS]
            p_ref[j * L:(j + 1) * L, :] = jnp.exp2(t_ref[j * L:(j + 1) * L, :] - mrows[n]).astype(BF16)

        for n0 in range(min(LOOKAHEAD, nb)):
            begin_body(n0)
            for j in range(bodies[n0][0] + 1):
                score_tile(n0, j)
        for n in range(nb):
            n_exp = bodies[n][0] + 1
            ahead = n + LOOKAHEAD
            n_score = bodies[ahead][0] + 1 if ahead < nb else 0
            if n_score:
                begin_body(ahead)
            for j in range(max(n_exp, n_score)):
                if j < n_score:
                    score_tile(ahead, j)
                if j < n_exp:
                    exp_tile(n, j)
            mrows.pop(n)
            pv_stage(outs, n, p_scr[n % T_SLOTS])

    def fast_path():
        gates, outs = {}, {}
        _cast_slabs(cast_src, cast_dst)
        k_means = stage_operands()
        shifted = {(hh, d): bias_ref[hh, d] - shift[hh]
                   for hh in range(HEADS_PER_STEP) for d in range(2)}

        def prob_tiles(n):
            i, hh = bodies[n]
            qm = masked_q(n)
            s_row = shift[hh]
            if i > MOBA_TOPK:
                sel = block_select(k_means, gates, n)
                m_far = jnp.where(sel, cfar[hh], NEG) - s_row
                m_prev = jnp.where(sel, 0.0, NEG)
            for j in range(i + 1):
                e = jnp.dot(k_ref[0, j * L:(j + 1) * L, :], qm,
                            preferred_element_type=F32)
                if j == i:
                    e = e + shifted[hh, 0]
                elif j == i - 1:
                    e = e + shifted[hh, 1]
                    if i > MOBA_TOPK:
                        e = e + m_prev[j:j + 1, :]
                else:
                    e = e + (m_far[j:j + 1, :] if i > MOBA_TOPK else cfar[hh] - s_row)
                p_scr[n % P_SLOTS][j * L:(j + 1) * L, :] = jnp.exp2(e).astype(BF16)

        for n0 in range(min(FAST_LOOKAHEAD, nb)):
            prob_tiles(n0)
        for n in range(nb):
            if n + FAST_LOOKAHEAD < nb:
                prob_tiles(n + FAST_LOOKAHEAD)
            pv_stage(outs, n, p_scr[n % P_SLOTS])

    pl.when(safe)(fast_path)
    pl.when(jnp.logical_not(safe))(exact_path)


def _moba(qkv3, norms, bias_t, bias_stats, cast_weights):
    B, S, _ = qkv3.shape
    L = MOBA_BLOCK
    n_hp = MOBA_HEADS // HEADS_PER_STEP
    blk = lambda off: pl.BlockSpec((1, S, LANES), lambda hp, b: (b, 0, off + hp))
    c_in, c_out, c_shapes = _cast_specs(cast_weights, B * n_hp, lambda hp, b: hp * B + b)
    outs = pl.pallas_call(
        functools.partial(_moba_kernel, n_cast=len(cast_weights)),
        grid=(n_hp, B),
        in_specs=[
            blk(0), blk(n_hp), blk(2 * n_hp),
            pl.BlockSpec((norms.shape[0] // B,) + norms.shape[1:], lambda hp, b: (b, 0, 0)),
            pl.BlockSpec((HEADS_PER_STEP, 2, L, L), lambda hp, b: (hp, 0, 0, 0)),
            pl.BlockSpec((HEADS_PER_STEP, STAT_ROWS, LANES), lambda hp, b: (hp, 0, 0)),
        ] + c_in,
        out_specs=[pl.BlockSpec((1, S, LANES), lambda hp, b: (b, 0, hp))] + c_out,
        out_shape=[jax.ShapeDtypeStruct((B, S, D_MOBA), F32)] + c_shapes,
        scratch_shapes=[
            pltpu.VMEM((HEADS_PER_STEP * VT_ROWS, S), BF16),
            pltpu.VMEM(((HEADS_PER_STEP + 1) * LANES, S), BF16),
        ] + [pltpu.VMEM((S, L), F32)] * T_SLOTS + [pltpu.VMEM((S, L), BF16)] * P_SLOTS,
        compiler_params=pltpu.CompilerParams(
            dimension_semantics=("arbitrary", "arbitrary"), vmem_limit_bytes=VMEM_LIMIT),
        name="moba",
    )(qkv3, qkv3, qkv3, norms, bias_t, bias_stats, *cast_weights)
    return outs[0], outs[1:]


def _ffn_kernel(x_ref, yb_ref, ya_ref, gb_ref, wo_ref, gf_ref, wg_ref, wu_ref, wd_ref, gl_ref,
                o_ref, a_scr, *, final_norm):
    half = x_ref.shape[0] // 2
    d_ff = wg_ref.shape[1]
    st = {}

    def attn_norm(r):
        yb = yb_ref[r:r + half, :]
        ms = jnp.mean(yb * yb, axis=-1, keepdims=True)
        st["ybn", r] = (yb * lax.rsqrt(ms + EPS) * gb_ref[...]).astype(BF16)

    def out_proj(r):
        st["x1", r] = (x_ref[r:r + half, :]
                       + jnp.dot(st.pop(("ybn", r)), wo_ref[:D_MOBA, :], preferred_element_type=F32)
                       + jnp.dot(ya_ref[r:r + half, :], wo_ref[D_MOBA:, :],
                                 preferred_element_type=F32))

    def ffn_norm(r):
        x1 = st["x1", r]
        ms = jnp.mean(x1 * x1, axis=-1, keepdims=True)
        st["h", r] = (x1 * lax.rsqrt(ms + EPS) * gf_ref[...]).astype(BF16)

    def gate_up(r):
        h = st.pop(("h", r))
        for c0 in range(0, d_ff, FFN_COLS):
            c1 = min(c0 + FFN_COLS, d_ff)
            g = jnp.dot(h, wg_ref[:, c0:c1], preferred_element_type=F32)
            u = jnp.dot(h, wu_ref[:, c0:c1], preferred_element_type=F32)
            a_scr[r:r + half, c0:c1] = (jax.nn.silu(g) * u).astype(BF16)

    def down(r):
        x2 = st.pop(("x1", r)) + jnp.dot(a_scr[r:r + half, :], wd_ref[...], preferred_element_type=F32)
        if final_norm:
            ms = jnp.mean(x2 * x2, axis=-1, keepdims=True)
            x2 = x2 * lax.rsqrt(ms + EPS) * gl_ref[...]
        o_ref[r:r + half, :] = x2

    stages = [attn_norm, out_proj, ffn_norm, gate_up, down]
    for k in range(len(stages) + 1):
        if k < len(stages):
            stages[k](0)
        if k >= 1:
            stages[k - 1](half)


def _ffn(x2, yb, ya, gb, wo, gf, wg, wu, wd, gl, tm, final_norm):
    T, D = x2.shape
    d_ff = wg.shape[1]
    const = lambda shape: pl.BlockSpec(shape, lambda i: (0, 0), pipeline_mode=pl.Buffered(1))
    return pl.pallas_call(
        functools.partial(_ffn_kernel, final_norm=final_norm),
        grid=(T // tm,),
        in_specs=[
            pl.BlockSpec((tm, D), lambda i: (i, 0)),
            pl.BlockSpec((tm, D_MOBA), lambda i: (i, 0)),
            pl.BlockSpec((tm, D_GMLP), lambda i: (i, 0)),
            pl.BlockSpec((1, D_MOBA), lambda i: (0, 0)),
            const((D_MOBA + D_GMLP, D)),
            pl.BlockSpec((1, D), lambda i: (0, 0)),
            const((D, d_ff)), const((D, d_ff)), const((d_ff, D)),
            pl.BlockSpec((1, D), lambda i: (0, 0)),
        ],
        out_specs=pl.BlockSpec((tm, D), lambda i: (i, 0)),
        out_shape=jax.ShapeDtypeStruct((T, D), F32),
        scratch_shapes=[pltpu.VMEM((tm, d_ff), BF16)],
        compiler_params=pltpu.CompilerParams(
            dimension_semantics=("arbitrary",), vmem_limit_bytes=VMEM_LIMIT),
        name="outproj_ffn",
    )(x2, yb, ya, gb, wo, gf, wg, wu, wd, gl)


def kernel(x, rel_bias, norm_mix, w_in, gmlp_ln_g, gmlp_ln_b, w_spatial, b_spatial, out_norm_b,
           out_norm_a, w_out, norm_ffn, w_gate, w_up, w_down, norm_final):
    B, S, D = x.shape
    depth = w_in.shape[0]
    assert S % MOBA_BLOCK == 0 and S // MOBA_BLOCK > MOBA_TOPK
    far = _t5_bucket_np(np.arange(MOBA_BLOCK + 1, S + MOBA_BLOCK))
    assert (far == far[0]).all()
    far_bucket = int(far[0])

    assert depth == 1, "weight casts ride along the single layer's kernels"
    T = B * S
    x2 = x.reshape(T, D)
    row = lambda v: v.reshape(1, -1)
    bias_t, bias_stats, (w_in_bf,) = _bias_tiles(rel_bias, far_bucket, [w_in[0]])
    qkv, ya, norms = _inproj(x2, row(norm_mix[0]), w_in_bf, row(gmlp_ln_g[0]), row(gmlp_ln_b[0]),
                             w_spatial[0], b_spatial[0], row(out_norm_a[0]), PROJ_ROWS)
    yb, (wo_bf, wg_bf, wu_bf, wd_bf) = _moba(qkv.reshape(B, S, -1), norms, bias_t, bias_stats,
                                             [w_out[0], w_gate[0], w_up[0], w_down[0]])
    out = _ffn(x2, yb.reshape(T, D_MOBA), ya, row(out_norm_b[0]), wo_bf, row(norm_ffn[0]),
               wg_bf, wu_bf, wd_bf, row(norm_final), FFN_ROWS, True)
    return out.reshape(B, S, D)
```

```python
import functools
import math

import numpy as np
import jax
import jax.numpy as jnp
from jax import lax
from jax.experimental import pallas as pl
from jax.experimental.pallas import tpu as pltpu

F32 = jnp.float32
BF16 = jnp.bfloat16

MOBA_HEADS = 8
MOBA_HEAD_DIM = 64
D_MOBA = MOBA_HEADS * MOBA_HEAD_DIM
GMLP_GROUPS = 8
GMLP_GROUP_DIM = 64
D_GMLP = GMLP_GROUPS * GMLP_GROUP_DIM
MOBA_BLOCK = 256
MOBA_TOPK = 3
GMLP_CHUNK = 128
NUM_BUCKETS = 32
REL_MAX_DISTANCE = 128
EPS = 1e-6
NEG = -1e9

LANES = 128
HEADS_PER_STEP = LANES // MOBA_HEAD_DIM
BF16_SUBLANES = 16
ONES_ROWS = BF16_SUBLANES
VT_ROWS = MOBA_HEAD_DIM + ONES_ROWS
VMEM_LIMIT = 56 * 1024 * 1024
PROJ_ROWS = 1024
FFN_ROWS = 512
FFN_COLS = 256
T_SLOTS = 4
P_SLOTS = 4
LOOKAHEAD = 3
FAST_LOOKAHEAD = 3

LOG2E = math.log2(math.e)
Q_SCALE = MOBA_HEAD_DIM ** -0.5 * LOG2E

BOUND_SLACK = 1.0 + 2.0 ** -6
SAFE_SPAN = 100.0


def _t5_bucket_np(dist):
    n = np.maximum(dist, 0)
    max_exact = NUM_BUCKETS // 2
    nf = np.maximum(n, max_exact).astype(np.float32)
    ratio = np.log(nf / np.float32(max_exact)) / np.float32(math.log(REL_MAX_DISTANCE / max_exact))
    large = max_exact + (ratio * np.float32(NUM_BUCKETS - max_exact)).astype(np.int32)
    large = np.minimum(large, NUM_BUCKETS - 1)
    return np.where(n < max_exact, n, large).astype(np.int32)


def _cast_specs(weights, n_steps, step_of):
    in_specs, out_specs, out_shapes = [], [], []
    for w in weights:
        rows, cols = w.shape
        period = next(p for p in range(1, n_steps + 1)
                      if n_steps % p == 0 and (rows * p) % (n_steps * BF16_SUBLANES) == 0)
        spec = pl.BlockSpec((rows * period // n_steps, cols),
                            functools.partial(lambda p, *ids: (step_of(*ids) // p, 0), period))
        in_specs.append(spec)
        out_specs.append(spec)
        out_shapes.append(jax.ShapeDtypeStruct(w.shape, BF16))
    return in_specs, out_specs, out_shapes


def _cast_slabs(src_refs, dst_refs):
    for src, dst in zip(src_refs, dst_refs):
        dst[...] = src[...].astype(BF16)


BIAS_FAR, BIAS_MAX, BIAS_MIN = 0, 1, 2
STAT_ROWS = 8
F32_SUBLANES = 8
BIAS_HEADS_PER_STEP = 4


def _bias_kernel(bucket_ref, rel_ref, *refs, far_bucket):
    n_cast = (len(refs) - 2) // 2
    out_ref, stat_ref = refs[n_cast], refs[n_cast + 1]
    _cast_slabs(refs[:n_cast], refs[n_cast + 2:])
    heads_here, L = out_ref.shape[0], out_ref.shape[2]
    span = bucket_ref.shape[1]
    bucket = bucket_ref[...]
    lane = lax.broadcasted_iota(jnp.int32, (1, span), 1)
    for hh in range(heads_here):
        h = pl.program_id(0) * heads_here + hh
        vals = [rel_ref[b, h] * LOG2E for b in range(NUM_BUCKETS)]
        prof = jnp.zeros((1, span), F32)
        for b in range(NUM_BUCKETS):
            prof = jnp.where(bucket == b, vals[b], prof)
        own = jnp.broadcast_to(jnp.where(lane < L, prof, NEG), (F32_SUBLANES, span))
        prev = jnp.broadcast_to(prof, (F32_SUBLANES, span))
        for g in range(L // F32_SUBLANES):
            k0 = g * F32_SUBLANES
            rows = slice(k0, k0 + F32_SUBLANES)
            out_ref[hh, 0, rows, :] = pltpu.roll(own, k0, 1, stride=1, stride_axis=0)[:, :L]
            out_ref[hh, 1, rows, :] = pltpu.roll(prev, (k0 + L) % span, 1, stride=1,
                                                 stride_axis=0)[:, :L]
        stats = {BIAS_FAR: vals[far_bucket], BIAS_MAX: functools.reduce(jnp.maximum, vals),
                 BIAS_MIN: functools.reduce(jnp.minimum, vals)}
        for r in range(STAT_ROWS):
            stat_ref[hh, r:r + 1, :] = jnp.full((1, LANES), stats.get(r, 0.0), F32)


def _bias_tiles(rel_bias, far_bucket, cast_weights):
    L = MOBA_BLOCK
    bucket = _t5_bucket_np(np.arange(2 * L))[None, :]
    n_steps = MOBA_HEADS // BIAS_HEADS_PER_STEP
    c_in, c_out, c_shapes = _cast_specs(cast_weights, n_steps, lambda i: i)
    outs = pl.pallas_call(
        functools.partial(_bias_kernel, far_bucket=far_bucket),
        grid=(n_steps,),
        in_specs=[
            pl.BlockSpec((1, 2 * L), lambda i: (0, 0)),
            pl.BlockSpec(memory_space=pltpu.SMEM),
        ] + c_in,
        out_specs=[pl.BlockSpec((BIAS_HEADS_PER_STEP, 2, L, L), lambda i: (i, 0, 0, 0)),
                   pl.BlockSpec((BIAS_HEADS_PER_STEP, STAT_ROWS, LANES), lambda i: (i, 0, 0))] + c_out,
        out_shape=[jax.ShapeDtypeStruct((MOBA_HEADS, 2, L, L), F32),
                   jax.ShapeDtypeStruct((MOBA_HEADS, STAT_ROWS, LANES), F32)] + c_shapes,
        compiler_params=pltpu.CompilerParams(
            dimension_semantics=("arbitrary",), vmem_limit_bytes=VMEM_LIMIT),
        name="bias_tiles",
    )(jnp.asarray(bucket), rel_bias, *cast_weights)
    return outs[0], outs[1], outs[2:]


def _inproj_kernel(x_ref, g_ref, w_ref, lng_ref, lnb_ref, wsp_ref, bsp_ref, gn_ref,
                   qkv_ref, ya_ref, norm_ref, uz_scr):
    C = GMLP_CHUNK
    tm = x_ref.shape[0]
    n_qkv = 3 * D_MOBA
    x = x_ref[...]
    ms = jnp.mean(x * x, axis=-1, keepdims=True)
    h = (x * lax.rsqrt(ms + EPS) * g_ref[...]).astype(BF16)
    uz_scr[...] = jnp.dot(h, w_ref[:, n_qkv:], preferred_element_type=F32)

    t_ix = lax.broadcasted_iota(jnp.int32, (C, C), 0)
    s_ix = lax.broadcasted_iota(jnp.int32, (C, C), 1)
    w_sp = [jnp.where(t_ix >= s_ix, wsp_ref[g], 0.0).astype(BF16) for g in range(GMLP_GROUPS)]
    w_pair = [jnp.concatenate(w_sp[2 * gp:2 * gp + 2], axis=0) for gp in range(GMLP_GROUPS // 2)]
    first = lax.broadcasted_iota(jnp.int32, (1, LANES), 1) < GMLP_GROUP_DIM
    b_t = bsp_ref[...].T
    lane_group = lax.broadcasted_iota(jnp.int32, (1, D_GMLP), 1) // GMLP_GROUP_DIM
    b_sp = jnp.zeros((C, D_GMLP), F32)
    for g in range(GMLP_GROUPS):
        b_sp = jnp.where(lane_group == g, b_t[:, g:g + 1], b_sp)

    def qkv_cols(c0, c1):
        acc = jnp.dot(h, w_ref[:, c0:c1], preferred_element_type=F32)
        if c1 <= D_MOBA:
            acc = acc * Q_SCALE
        qkv_ref[:, c0:c1] = acc.astype(BF16)
        if c1 <= 2 * D_MOBA:
            sq = acc * acc
            for c in range(c0, c1, MOBA_HEAD_DIM):
                lanes = (c - c0) // LANES * LANES
                own = first if (c - c0) % LANES == 0 else jnp.logical_not(first)
                rows_sq = jnp.sum(jnp.where(own, sq[:, lanes:lanes + LANES], 0.0), axis=-1, keepdims=True)
                top = jnp.max(rows_sq, axis=0, keepdims=True)
                norm_ref[0, c // MOBA_HEAD_DIM:c // MOBA_HEAD_DIM + 1, :] = jnp.broadcast_to(top, (1, LANES))

    def gmlp_rows(c):
        rows = slice(c * C, (c + 1) * C)
        z = jax.nn.gelu(uz_scr[rows, D_GMLP:])
        mu = jnp.mean(z, axis=-1, keepdims=True)
        zc = z - mu
        var = jnp.mean(zc * zc, axis=-1, keepdims=True)
        zn = (zc * lax.rsqrt(var + EPS) * lng_ref[...] + lnb_ref[...]).astype(BF16)
        cols = []
        for gp in range(D_GMLP // LANES):
            zp = zn[:, gp * LANES:(gp + 1) * LANES]
            r = jnp.dot(w_pair[gp], zp, preferred_element_type=F32)
            cols.append(jnp.where(first, r[:C], r[C:]))
        s = jnp.concatenate(cols, axis=1) + b_sp
        y = jax.nn.gelu(uz_scr[rows, :D_GMLP]) * s
        ms_y = jnp.mean(y * y, axis=-1, keepdims=True)
        ya_ref[rows, :] = (y * lax.rsqrt(ms_y + EPS) * gn_ref[...]).astype(BF16)

    col_blocks = [(c, c + 2 * LANES) for c in range(0, n_qkv, 2 * LANES)]
    n_chunks = tm // C
    for step in range(max(len(col_blocks), n_chunks)):
        if step < len(col_blocks):
            qkv_cols(*col_blocks[step])
        if step < n_chunks:
            gmlp_rows(step)


def _inproj(x2, g, w_bf, ln_g, ln_b, w_sp, b_sp, gn, tm):
    T, D = x2.shape
    n_all = w_bf.shape[1]
    n_qkv = 3 * D_MOBA
    C = GMLP_CHUNK
    vec = pl.BlockSpec((1, D_GMLP), lambda i: (0, 0))
    return pl.pallas_call(
        _inproj_kernel,
        grid=(T // tm,),
        in_specs=[
            pl.BlockSpec((tm, D), lambda i: (i, 0)),
            pl.BlockSpec((1, D), lambda i: (0, 0)),
            pl.BlockSpec((D, n_all), lambda i: (0, 0), pipeline_mode=pl.Buffered(1)),
            vec, vec,
            pl.BlockSpec((GMLP_GROUPS, C, C), lambda i: (0, 0, 0)),
            pl.BlockSpec((GMLP_GROUPS, C), lambda i: (0, 0)),
            vec,
        ],
        out_specs=[
            pl.BlockSpec((tm, n_qkv), lambda i: (i, 0)),
            pl.BlockSpec((tm, D_GMLP), lambda i: (i, 0)),
            pl.BlockSpec((1, 2 * MOBA_HEADS, LANES), lambda i: (i, 0, 0)),
        ],
        out_shape=[
            jax.ShapeDtypeStruct((T, n_qkv), BF16),
            jax.ShapeDtypeStruct((T, D_GMLP), BF16),
            jax.ShapeDtypeStruct((T // tm, 2 * MOBA_HEADS, LANES), F32),
        ],
        scratch_shapes=[pltpu.VMEM((tm, n_all - n_qkv), F32)],
        compiler_params=pltpu.CompilerParams(
            dimension_semantics=("arbitrary",), vmem_limit_bytes=VMEM_LIMIT),
        name="inproj_gmlp",
    )(x2, g, w_bf, ln_g, ln_b, w_sp, b_sp, gn)


def _moba_kernel(q_ref, k_ref, v_ref, norm_ref, bias_ref, stat_ref, *refs, n_cast):
    cast_src, o_ref, cast_dst = refs[:n_cast], refs[n_cast], refs[n_cast + 1:2 * n_cast + 1]
    vt_scr, qt_scr, slots = refs[2 * n_cast + 1], refs[2 * n_cast + 2], refs[2 * n_cast + 3:]
    t_scr, p_scr = slots[:T_SLOTS], slots[T_SLOTS:]
    L = MOBA_BLOCK
    S = q_ref.shape[1]
    NB = S // L
    hp = pl.program_id(0)
    lane = lax.broadcasted_iota(jnp.int32, (1, LANES), 1)
    head_lanes = [(lane >= hh * MOBA_HEAD_DIM) & (lane < (hh + 1) * MOBA_HEAD_DIM)
                  for hh in range(HEADS_PER_STEP)]
    row = lax.broadcasted_iota(jnp.int32, (NB, L), 0)
    bodies = [(i, hh) for i in range(NB) for hh in range(HEADS_PER_STEP)]
    nb = len(bodies)

    cfar = [stat_ref[hh, BIAS_FAR:BIAS_FAR + 1, :1] for hh in range(HEADS_PER_STEP)]
    bmax = [stat_ref[hh, BIAS_MAX:BIAS_MAX + 1, :1] for hh in range(HEADS_PER_STEP)]
    bmin = [stat_ref[hh, BIAS_MIN:BIAS_MIN + 1, :1] for hh in range(HEADS_PER_STEP)]

    def stage_operands():
        vt = v_ref[0].T
        for hh in range(HEADS_PER_STEP):
            vt_scr[hh * VT_ROWS:hh * VT_ROWS + MOBA_HEAD_DIM, :] = (
                vt[hh * MOBA_HEAD_DIM:(hh + 1) * MOBA_HEAD_DIM])
            vt_scr[hh * VT_ROWS + MOBA_HEAD_DIM:(hh + 1) * VT_ROWS, :] = jnp.ones((ONES_ROWS, S), BF16)
        qt = q_ref[0].T
        dim_head = lax.broadcasted_iota(jnp.int32, (LANES, 1), 0) // MOBA_HEAD_DIM
        for hh in range(HEADS_PER_STEP):
            qt_scr[hh * LANES:(hh + 1) * LANES, :] = jnp.where(dim_head == hh, qt, jnp.zeros_like(qt))
        qt_scr[HEADS_PER_STEP * LANES:, :] = qt
        km = jnp.concatenate(
            [jnp.sum(k_ref[0, n * L:(n + 1) * L, :].astype(F32), axis=0, keepdims=True) * (1.0 / L)
             for n in range(NB)], axis=0)
        kmm = jnp.concatenate([jnp.where(head_lanes[hh], km, 0.0) for hh in range(HEADS_PER_STEP)],
                              axis=0)
        km_hi = kmm.astype(BF16)
        return km_hi, (kmm - km_hi.astype(F32)).astype(BF16)

    shift, span = [], []
    for hh in range(HEADS_PER_STEP):
        head = hp * HEADS_PER_STEP + hh
        q_sq = jnp.max(norm_ref[:, pl.ds(head, 1), :], axis=0)[:, :1]
        k_sq = jnp.max(norm_ref[:, pl.ds(MOBA_HEADS + head, 1), :], axis=0)[:, :1]
        reach = jnp.sqrt(q_sq * k_sq) * BOUND_SLACK
        shift.append(reach + bmax[hh])
        span.append(2.0 * reach + (bmax[hh] - bmin[hh]))
    safe = jnp.max(functools.reduce(jnp.maximum, span)) <= SAFE_SPAN

    def block_select(k_means, gates, n):
        i, hh = bodies[n]
        if i not in gates:
            q_t = qt_scr[HEADS_PER_STEP * LANES:, i * L:(i + 1) * L]
            both = jnp.dot(jnp.concatenate(k_means, axis=0), q_t, preferred_element_type=F32)
            gates[i] = both[:HEADS_PER_STEP * NB] + both[HEADS_PER_STEP * NB:]
        g = gates[i][hh * NB:(hh + 1) * NB]
        cnt = jnp.zeros((NB, L), jnp.int32)
        for m in range(i):
            gm = g[m:m + 1, :]
            beats = (gm > g) | ((gm == g) & (m < row))
            cnt = cnt + jnp.where(beats, 1, 0)
        return cnt < MOBA_TOPK

    def masked_q(n):
        i, hh = bodies[n]
        return qt_scr[hh * LANES:(hh + 1) * LANES, i * L:(i + 1) * L]

    def pv_stage(outs, n, p_ref):
        i, hh = bodies[n]
        nk = (i + 1) * L
        acc = jnp.dot(vt_scr[hh * VT_ROWS:(hh + 1) * VT_ROWS, 0:nk], p_ref[0:nk, :],
                      preferred_element_type=F32)
        outs[n] = acc[:MOBA_HEAD_DIM] / acc[MOBA_HEAD_DIM:MOBA_HEAD_DIM + 1]
        if hh == HEADS_PER_STEP - 1:
            pair = [outs.pop(n - HEADS_PER_STEP + 1 + h) for h in range(HEADS_PER_STEP)]
            o_ref[0, i * L:(i + 1) * L, :] = jnp.concatenate(pair, axis=0).T

    def exact_path():
        gates, masks, qms, maxes, mrows, outs = {}, {}, {}, {}, {}, {}
        _cast_slabs(cast_src, cast_dst)
        k_means = stage_operands()

        def begin_body(n):
            i, hh = bodies[n]
            qms[n] = masked_q(n)
            if i > MOBA_TOPK:
                sel = block_select(k_means, gates, n)
                masks[n] = (jnp.where(sel, cfar[hh], NEG),
                            jnp.where(sel, 0.0, NEG))

        def score_tile(n, j):
            i, hh = bodies[n]
            t = jnp.dot(k_ref[0, j * L:(j + 1) * L, :], qms[n],
                        preferred_element_type=F32)
            if j == i:
                t = t + bias_ref[hh, 0]
            elif j == i - 1:
                t = t + bias_ref[hh, 1]
                if i > MOBA_TOPK:
                    t = t + masks[n][1][j:j + 1, :]
            elif i > MOBA_TOPK:
                t = t + masks[n][0][j:j + 1, :]
            else:
                t = t + cfar[hh]
            t_scr[n % T_SLOTS][j * L:(j + 1) * L, :] = t
            tm = jnp.max(t.reshape(L // 8, 8, L), axis=0)
            maxes[n] = tm if j == 0 else jnp.maximum(maxes[n], tm)
            if j == i:
                mrows[n] = jnp.max(maxes.pop(n), axis=0, keepdims=True)
                qms.pop(n)
                masks.pop(n, None)

        def exp_tile(n, j):
            t_ref, p_ref = t_scr[n % T_SLOTS], p_scr[n % T_SLOTS]
            p_ref[j * L:(j + 1) * L, :] = jnp.exp2(t_ref[j * L:(j + 1) * L, :] - mrows[n]).astype(BF16)

        for n0 in range(min(LOOKAHEAD, nb)):
            begin_body(n0)
            for j in range(bodies[n0][0] + 1):
                score_tile(n0, j)
        for n in range(nb):
            n_exp = bodies[n][0] + 1
            ahead = n + LOOKAHEAD
            n_score = bodies[ahead][0] + 1 if ahead < nb else 0
            if n_score:
                begin_body(ahead)
            for j in range(max(n_exp, n_score)):
                if j < n_score:
                    score_tile(ahead, j)
                if j < n_exp:
                    exp_tile(n, j)
            mrows.pop(n)
            pv_stage(outs, n, p_scr[n % T_SLOTS])

    def fast_path():
        gates, outs = {}, {}
        _cast_slabs(cast_src, cast_dst)
        k_means = stage_operands()
        shifted = {(hh, d): bias_ref[hh, d] - shift[hh]
                   for hh in range(HEADS_PER_STEP) for d in range(2)}

        def prob_tiles(n):
            i, hh = bodies[n]
            qm = masked_q(n)
            s_row = shift[hh]
            if i > MOBA_TOPK:
                sel = block_select(k_means, gates, n)
                m_far = jnp.where(sel, cfar[hh], NEG) - s_row
                m_prev = jnp.where(sel, 0.0, NEG)
            for j in range(i + 1):
                e = jnp.dot(k_ref[0, j * L:(j + 1) * L, :], qm,
                            preferred_element_type=F32)
                if j == i:
                    e = e + shifted[hh, 0]
                elif j == i - 1:
                    e = e + shifted[hh, 1]
                    if i > MOBA_TOPK:
                        e = e + m_prev[j:j + 1, :]
                else:
                    e = e + (m_far[j:j + 1, :] if i > MOBA_TOPK else cfar[hh] - s_row)
                p_scr[n % P_SLOTS][j * L:(j + 1) * L, :] = jnp.exp2(e).astype(BF16)

        for n0 in range(min(FAST_LOOKAHEAD, nb)):
            prob_tiles(n0)
        for n in range(nb):
            if n + FAST_LOOKAHEAD < nb:
                prob_tiles(n + FAST_LOOKAHEAD)
            pv_stage(outs, n, p_scr[n % P_SLOTS])

    pl.when(safe)(fast_path)
    pl.when(jnp.logical_not(safe))(exact_path)


def _moba(qkv3, norms, bias_t, bias_stats, cast_weights):
    B, S, _ = qkv3.shape
    L = MOBA_BLOCK
    n_hp = MOBA_HEADS // HEADS_PER_STEP
    blk = lambda off: pl.BlockSpec((1, S, LANES), lambda hp, b: (b, 0, off + hp))
    c_in, c_out, c_shapes = _cast_specs(cast_weights, B * n_hp, lambda hp, b: hp * B + b)
    outs = pl.pallas_call(
        functools.partial(_moba_kernel, n_cast=len(cast_weights)),
        grid=(n_hp, B),
        in_specs=[
            blk(0), blk(n_hp), blk(2 * n_hp),
            pl.BlockSpec((norms.shape[0] // B,) + norms.shape[1:], lambda hp, b: (b, 0, 0)),
            pl.BlockSpec((HEADS_PER_STEP, 2, L, L), lambda hp, b: (hp, 0, 0, 0)),
            pl.BlockSpec((HEADS_PER_STEP, STAT_ROWS, LANES), lambda hp, b: (hp, 0, 0)),
        ] + c_in,
        out_specs=[pl.BlockSpec((1, S, LANES), lambda hp, b: (b, 0, hp))] + c_out,
        out_shape=[jax.ShapeDtypeStruct((B, S, D_MOBA), F32)] + c_shapes,
        scratch_shapes=[
            pltpu.VMEM((HEADS_PER_STEP * VT_ROWS, S), BF16),
            pltpu.VMEM(((HEADS_PER_STEP + 1) * LANES, S), BF16),
        ] + [pltpu.VMEM((S, L), F32)] * T_SLOTS + [pltpu.VMEM((S, L), BF16)] * P_SLOTS,
        compiler_params=pltpu.CompilerParams(
            dimension_semantics=("arbitrary", "arbitrary"), vmem_limit_bytes=VMEM_LIMIT),
        name="moba",
    )(qkv3, qkv3, qkv3, norms, bias_t, bias_stats, *cast_weights)
    return outs[0], outs[1:]


def _ffn_kernel(x_ref, yb_ref, ya_ref, gb_ref, wo_ref, gf_ref, wg_ref, wu_ref, wd_ref, gl_ref,
                o_ref, a_scr, *, final_norm):
    half = x_ref.shape[0] // 2
    d_ff = wg_ref.shape[1]
    st = {}

    def attn_norm(r):
        yb = yb_ref[r:r + half, :]
        ms = jnp.mean(yb * yb, axis=-1, keepdims=True)
        st["ybn", r] = (yb * lax.rsqrt(ms + EPS) * gb_ref[...]).astype(BF16)

    def out_proj(r):
        st["x1", r] = (x_ref[r:r + half, :]
                       + jnp.dot(st.pop(("ybn", r)), wo_ref[:D_MOBA, :], preferred_element_type=F32)
                       + jnp.dot(ya_ref[r:r + half, :], wo_ref[D_MOBA:, :],
                                 preferred_element_type=F32))

    def ffn_norm(r):
        x1 = st["x1", r]
        ms = jnp.mean(x1 * x1, axis=-1, keepdims=True)
        st["h", r] = (x1 * lax.rsqrt(ms + EPS) * gf_ref[...]).astype(BF16)

    def gate_up(r):
        h = st.pop(("h", r))
        for c0 in range(0, d_ff, FFN_COLS):
            c1 = min(c0 + FFN_COLS, d_ff)
            g = jnp.dot(h, wg_ref[:, c0:c1], preferred_element_type=F32)
            u = jnp.dot(h, wu_ref[:, c0:c1], preferred_element_type=F32)
            a_scr[r:r + half, c0:c1] = (jax.nn.silu(g) * u).astype(BF16)

    def down(r):
        x2 = st.pop(("x1", r)) + jnp.dot(a_scr[r:r + half, :], wd_ref[...], preferred_element_type=F32)
        if final_norm:
            ms = jnp.mean(x2 * x2, axis=-1, keepdims=True)
            x2 = x2 * lax.rsqrt(ms + EPS) * gl_ref[...]
        o_ref[r:r + half, :] = x2

    stages = [attn_norm, out_proj, ffn_norm, gate_up, down]
    for k in range(len(stages) + 1):
        if k < len(stages):
            stages[k](0)
        if k >= 1:
            stages[k - 1](half)


def _ffn(x2, yb, ya, gb, wo, gf, wg, wu, wd, gl, tm, final_norm):
    T, D = x2.shape
    d_ff = wg.shape[1]
    const = lambda shape: pl.BlockSpec(shape, lambda i: (0, 0), pipeline_mode=pl.Buffered(1))
    return pl.pallas_call(
        functools.partial(_ffn_kernel, final_norm=final_norm),
        grid=(T // tm,),
        in_specs=[
            pl.BlockSpec((tm, D), lambda i: (i, 0)),
            pl.BlockSpec((tm, D_MOBA), lambda i: (i, 0)),
            pl.BlockSpec((tm, D_GMLP), lambda i: (i, 0)),
            pl.BlockSpec((1, D_MOBA), lambda i: (0, 0)),
            const((D_MOBA + D_GMLP, D)),
            pl.BlockSpec((1, D), lambda i: (0, 0)),
            const((D, d_ff)), const((D, d_ff)), const((d_ff, D)),
            pl.BlockSpec((1, D), lambda i: (0, 0)),
        ],
        out_specs=pl.BlockSpec((tm, D), lambda i: (i, 0)),
        out_shape=jax.ShapeDtypeStruct((T, D), F32),
        scratch_shapes=[pltpu.VMEM((tm, d_ff), BF16)],
        compiler_params=pltpu.CompilerParams(
            dimension_semantics=("arbitrary",), vmem_limit_bytes=VMEM_LIMIT),
        name="outproj_ffn",
    )(x2, yb, ya, gb, wo, gf, wg, wu, wd, gl)


def kernel(x, rel_bias, norm_mix, w_in, gmlp_ln_g, gmlp_ln_b, w_spatial, b_spatial, out_norm_b,
           out_norm_a, w_out, norm_ffn, w_gate, w_up, w_down, norm_final):
    B, S, D = x.shape
    depth = w_in.shape[0]
    assert S % MOBA_BLOCK == 0 and S // MOBA_BLOCK > MOBA_TOPK
    far = _t5_bucket_np(np.arange(MOBA_BLOCK + 1, S + MOBA_BLOCK))
    assert (far == far[0]).all()
    far_bucket = int(far[0])

    assert depth == 1, "weight casts ride along the single layer's kernels"
    T = B * S
    x2 = x.reshape(T, D)
    row = lambda v: v.reshape(1, -1)
    bias_t, bias_stats, (w_in_bf,) = _bias_tiles(rel_bias, far_bucket, [w_in[0]])
    qkv, ya, norms = _inproj(x2, row(norm_mix[0]), w_in_bf, row(gmlp_ln_g[0]), row(gmlp_ln_b[0]),
                             w_spatial[0], b_spatial[0], row(out_norm_a[0]), PROJ_ROWS)
    yb, (wo_bf, wg_bf, wu_bf, wd_bf) = _moba(qkv.reshape(B, S, -1), norms, bias_t, bias_stats,
                                             [w_out[0], w_gate[0], w_up[0], w_down[0]])
    out = _ffn(x2, yb.reshape(T, D_MOBA), ya, row(out_norm_b[0]), wo_bf, row(norm_ffn[0]),
               wg_bf, wu_bf, wd_bf, row(norm_final), FFN_ROWS, True)
    return out.reshape(B, S, D)
```

```python
import functools
import math

import numpy as np
import jax
import jax.numpy as jnp
from jax import lax
from jax.experimental import pallas as pl
from jax.experimental.pallas import tpu as pltpu

F32 = jnp.float32
BF16 = jnp.bfloat16

MOBA_HEADS = 8
MOBA_HEAD_DIM = 64
D_MOBA = MOBA_HEADS * MOBA_HEAD_DIM
GMLP_GROUPS = 8
GMLP_GROUP_DIM = 64
D_GMLP = GMLP_GROUPS * GMLP_GROUP_DIM
MOBA_BLOCK = 256
MOBA_TOPK = 3
GMLP_CHUNK = 128
NUM_BUCKETS = 32
REL_MAX_DISTANCE = 128
EPS = 1e-6
NEG = -1e9

LANES = 128
HEADS_PER_STEP = LANES // MOBA_HEAD_DIM
BF16_SUBLANES = 16
ONES_ROWS = BF16_SUBLANES
VT_ROWS = MOBA_HEAD_DIM + ONES_ROWS
VMEM_LIMIT = 56 * 1024 * 1024
PROJ_ROWS = 1024
FFN_ROWS = 512
FFN_COLS = 256
T_SLOTS = 4
P_SLOTS = 4
LOOKAHEAD = 3
FAST_LOOKAHEAD = 3

LOG2E = math.log2(math.e)
Q_SCALE = MOBA_HEAD_DIM ** -0.5 * LOG2E

BOUND_SLACK = 1.0 + 2.0 ** -6
SAFE_SPAN = 100.0


def _t5_bucket_np(dist):
    n = np.maximum(dist, 0)
    max_exact = NUM_BUCKETS // 2
    nf = np.maximum(n, max_exact).astype(np.float32)
    ratio = np.log(nf / np.float32(max_exact)) / np.float32(math.log(REL_MAX_DISTANCE / max_exact))
    large = max_exact + (ratio * np.float32(NUM_BUCKETS - max_exact)).astype(np.int32)
    large = np.minimum(large, NUM_BUCKETS - 1)
    return np.where(n < max_exact, n, large).astype(np.int32)


def _cast_specs(weights, n_steps, step_of):
    in_specs, out_specs, out_shapes = [], [], []
    for w in weights:
        rows, cols = w.shape
        period = next(p for p in range(1, n_steps + 1)
                      if n_steps % p == 0 and (rows * p) % (n_steps * BF16_SUBLANES) == 0)
        spec = pl.BlockSpec((rows * period // n_steps, cols),
                            functools.partial(lambda p, *ids: (step_of(*ids) // p, 0), period))
        in_specs.append(spec)
        out_specs.append(spec)
        out_shapes.append(jax.ShapeDtypeStruct(w.shape, BF16))
    return in_specs, out_specs, out_shapes


def _cast_slabs(src_refs, dst_refs):
    for src, dst in zip(src_refs, dst_refs):
        dst[...] = src[...].astype(BF16)


BIAS_FAR, BIAS_MAX, BIAS_MIN = 0, 1, 2
STAT_ROWS = 8
F32_SUBLANES = 8
BIAS_HEADS_PER_STEP = 4


def _bias_kernel(bucket_ref, rel_ref, *refs, far_bucket):
    n_cast = (len(refs) - 2) // 2
    out_ref, stat_ref = refs[n_cast], refs[n_cast + 1]
    _cast_slabs(refs[:n_cast], refs[n_cast + 2:])
    heads_here, L = out_ref.shape[0], out_ref.shape[2]
    span = bucket_ref.shape[1]
    bucket = bucket_ref[...]
    lane = lax.broadcasted_iota(jnp.int32, (1, span), 1)
    for hh in range(heads_here):
        h = pl.program_id(0) * heads_here + hh
        vals = [rel_ref[b, h] * LOG2E for b in range(NUM_BUCKETS)]
        prof = jnp.zeros((1, span), F32)
        for b in range(NUM_BUCKETS):
            prof = jnp.where(bucket == b, vals[b], prof)
        own = jnp.broadcast_to(jnp.where(lane < L, prof, NEG), (F32_SUBLANES, span))
        prev = jnp.broadcast_to(prof, (F32_SUBLANES, span))
        for g in range(L // F32_SUBLANES):
            k0 = g * F32_SUBLANES
            rows = slice(k0, k0 + F32_SUBLANES)
            out_ref[hh, 0, rows, :] = pltpu.roll(own, k0, 1, stride=1, stride_axis=0)[:, :L]
            out_ref[hh, 1, rows, :] = pltpu.roll(prev, (k0 + L) % span, 1, stride=1,
                                                 stride_axis=0)[:, :L]
        stats = {BIAS_FAR: vals[far_bucket], BIAS_MAX: functools.reduce(jnp.maximum, vals),
                 BIAS_MIN: functools.reduce(jnp.minimum, vals)}
        for r in range(STAT_ROWS):
            stat_ref[hh, r:r + 1, :] = jnp.full((1, LANES), stats.get(r, 0.0), F32)


def _bias_tiles(rel_bias, far_bucket, cast_weights):
    L = MOBA_BLOCK
    bucket = _t5_bucket_np(np.arange(2 * L))[None, :]
    n_steps = MOBA_HEADS // BIAS_HEADS_PER_STEP
    c_in, c_out, c_shapes = _cast_specs(cast_weights, n_steps, lambda i: i)
    outs = pl.pallas_call(
        functools.partial(_bias_kernel, far_bucket=far_bucket),
        grid=(n_steps,),
        in_specs=[
            pl.BlockSpec((1, 2 * L), lambda i: (0, 0)),
            pl.BlockSpec(memory_space=pltpu.SMEM),
        ] + c_in,
        out_specs=[pl.BlockSpec((BIAS_HEADS_PER_STEP, 2, L, L), lambda i: (i, 0, 0, 0)),
                   pl.BlockSpec((BIAS_HEADS_PER_STEP, STAT_ROWS, LANES), lambda i: (i, 0, 0))] + c_out,
        out_shape=[jax.ShapeDtypeStruct((MOBA_HEADS, 2, L, L), F32),
                   jax.ShapeDtypeStruct((MOBA_HEADS, STAT_ROWS, LANES), F32)] + c_shapes,
        compiler_params=pltpu.CompilerParams(
            dimension_semantics=("arbitrary",), vmem_limit_bytes=VMEM_LIMIT),
        name="bias_tiles",
    )(jnp.asarray(bucket), rel_bias, *cast_weights)
    return outs[0], outs[1], outs[2:]


def _inproj_kernel(x_ref, g_ref, w_ref, lng_ref, lnb_ref, wsp_ref, bsp_ref, gn_ref,
                   qkv_ref, ya_ref, norm_ref, uz_scr):
    C = GMLP_CHUNK
    tm = x_ref.shape[0]
    n_qkv = 3 * D_MOBA
    x = x_ref[...]
    ms = jnp.mean(x * x, axis=-1, keepdims=True)
    h = (x * lax.rsqrt(ms + EPS) * g_ref[...]).astype(BF16)
    uz_scr[...] = jnp.dot(h, w_ref[:, n_qkv:], preferred_element_type=F32)

    t_ix = lax.broadcasted_iota(jnp.int32, (C, C), 0)
    s_ix = lax.broadcasted_iota(jnp.int32, (C, C), 1)
    w_sp = [jnp.where(t_ix >= s_ix, wsp_ref[g], 0.0).astype(BF16) for g in range(GMLP_GROUPS)]
    w_pair = [jnp.concatenate(w_sp[2 * gp:2 * gp + 2], axis=0) for gp in range(GMLP_GROUPS // 2)]
    first = lax.broadcasted_iota(jnp.int32, (1, LANES), 1) < GMLP_GROUP_DIM
    b_t = bsp_ref[...].T
    lane_group = lax.broadcasted_iota(jnp.int32, (1, D_GMLP), 1) // GMLP_GROUP_DIM
    b_sp = jnp.zeros((C, D_GMLP), F32)
    for g in range(GMLP_GROUPS):
        b_sp = jnp.where(lane_group == g, b_t[:, g:g + 1], b_sp)

    def qkv_cols(c0, c1):
        acc = jnp.dot(h, w_ref[:, c0:c1], preferred_element_type=F32)
        if c1 <= D_MOBA:
            acc = acc * Q_SCALE
        qkv_ref[:, c0:c1] = acc.astype(BF16)
        if c1 <= 2 * D_MOBA:
            sq = acc * acc
            for c in range(c0, c1, MOBA_HEAD_DIM):
                lanes = (c - c0) // LANES * LANES
                own = first if (c - c0) % LANES == 0 else jnp.logical_not(first)
                rows_sq = jnp.sum(jnp.where(own, sq[:, lanes:lanes + LANES], 0.0), axis=-1, keepdims=True)
                top = jnp.max(rows_sq, axis=0, keepdims=True)
                norm_ref[0, c // MOBA_HEAD_DIM:c // MOBA_HEAD_DIM + 1, :] = jnp.broadcast_to(top, (1, LANES))

    def gmlp_rows(c):
        rows = slice(c * C, (c + 1) * C)
        z = jax.nn.gelu(uz_scr[rows, D_GMLP:])
        mu = jnp.mean(z, axis=-1, keepdims=True)
        zc = z - mu
        var = jnp.mean(zc * zc, axis=-1, keepdims=True)
        zn = (zc * lax.rsqrt(var + EPS) * lng_ref[...] + lnb_ref[...]).astype(BF16)
        cols = []
        for gp in range(D_GMLP // LANES):
            zp = zn[:, gp * LANES:(gp + 1) * LANES]
            r = jnp.dot(w_pair[gp], zp, preferred_element_type=F32)
            cols.append(jnp.where(first, r[:C], r[C:]))
        s = jnp.concatenate(cols, axis=1) + b_sp
        y = jax.nn.gelu(uz_scr[rows, :D_GMLP]) * s
        ms_y = jnp.mean(y * y, axis=-1, keepdims=True)
        ya_ref[rows, :] = (y * lax.rsqrt(ms_y + EPS) * gn_ref[...]).astype(BF16)

    col_blocks = [(c, c + 2 * LANES) for c in range(0, n_qkv, 2 * LANES)]
    n_chunks = tm // C
    for step in range(max(len(col_blocks), n_chunks)):
        if step < len(col_blocks):
            qkv_cols(*col_blocks[step])
        if step < n_chunks:
            gmlp_rows(step)


def _inproj(x2, g, w_bf, ln_g, ln_b, w_sp, b_sp, gn, tm):
    T, D = x2.shape
    n_all = w_bf.shape[1]
    n_qkv = 3 * D_MOBA
    C = GMLP_CHUNK
    vec = pl.BlockSpec((1, D_GMLP), lambda i: (0, 0))
    return pl.pallas_call(
        _inproj_kernel,
        grid=(T // tm,),
        in_specs=[
            pl.BlockSpec((tm, D), lambda i: (i, 0)),
            pl.BlockSpec((1, D), lambda i: (0, 0)),
            pl.BlockSpec((D, n_all), lambda i: (0, 0), pipeline_mode=pl.Buffered(1)),
            vec, vec,
            pl.BlockSpec((GMLP_GROUPS, C, C), lambda i: (0, 0, 0)),
            pl.BlockSpec((GMLP_GROUPS, C), lambda i: (0, 0)),
            vec,
        ],
        out_specs=[
            pl.BlockSpec((tm, n_qkv), lambda i: (i, 0)),
            pl.BlockSpec((tm, D_GMLP), lambda i: (i, 0)),
            pl.BlockSpec((1, 2 * MOBA_HEADS, LANES), lambda i: (i, 0, 0)),
        ],
        out_shape=[
            jax.ShapeDtypeStruct((T, n_qkv), BF16),
            jax.ShapeDtypeStruct((T, D_GMLP), BF16),
            jax.ShapeDtypeStruct((T // tm, 2 * MOBA_HEADS, LANES), F32),
        ],
        scratch_shapes=[pltpu.VMEM((tm, n_all - n_qkv), F32)],
        compiler_params=pltpu.CompilerParams(
            dimension_semantics=("arbitrary",), vmem_limit_bytes=VMEM_LIMIT),
        name="inproj_gmlp",
    )(x2, g, w_bf, ln_g, ln_b, w_sp, b_sp, gn)


def _moba_kernel(q_ref, k_ref, v_ref, norm_ref, bias_ref, stat_ref, *refs, n_cast):
    cast_src, o_ref, cast_dst = refs[:n_cast], refs[n_cast], refs[n_cast + 1:2 * n_cast + 1]
    vt_scr, qt_scr, slots = refs[2 * n_cast + 1], refs[2 * n_cast + 2], refs[2 * n_cast + 3:]
    t_scr, p_scr = slots[:T_SLOTS], slots[T_SLOTS:]
    L = MOBA_BLOCK
    S = q_ref.shape[1]
    NB = S // L
    hp = pl.program_id(0)
    lane = lax.broadcasted_iota(jnp.int32, (1, LANES), 1)
    head_lanes = [(lane >= hh * MOBA_HEAD_DIM) & (lane < (hh + 1) * MOBA_HEAD_DIM)
                  for hh in range(HEADS_PER_STEP)]
    row = lax.broadcasted_iota(jnp.int32, (NB, L), 0)
    bodies = [(i, hh) for i in range(NB) for hh in range(HEADS_PER_STEP)]
    nb = len(bodies)

    cfar = [stat_ref[hh, BIAS_FAR:BIAS_FAR + 1, :1] for hh in range(HEADS_PER_STEP)]
    bmax = [stat_ref[hh, BIAS_MAX:BIAS_MAX + 1, :1] for hh in range(HEADS_PER_STEP)]
    bmin = [stat_ref[hh, BIAS_MIN:BIAS_MIN + 1, :1] for hh in range(HEADS_PER_STEP)]

    def stage_operands():
        vt = v_ref[0].T
        for hh in range(HEADS_PER_STEP):
            vt_scr[hh * VT_ROWS:hh * VT_ROWS + MOBA_HEAD_DIM, :] = (
                vt[hh * MOBA_HEAD_DIM:(hh + 1) * MOBA_HEAD_DIM])
            vt_scr[hh * VT_ROWS + MOBA_HEAD_DIM:(hh + 1) * VT_ROWS, :] = jnp.ones((ONES_ROWS, S), BF16)
        qt = q_ref[0].T
        dim_head = lax.broadcasted_iota(jnp.int32, (LANES, 1), 0) // MOBA_HEAD_DIM
        for hh in range(HEADS_PER_STEP):
            qt_scr[hh * LANES:(hh + 1) * LANES, :] = jnp.where(dim_head == hh, qt, jnp.zeros_like(qt))
        qt_scr[HEADS_PER_STEP * LANES:, :] = qt
        km = jnp.concatenate(
            [jnp.sum(k_ref[0, n * L:(n + 1) * L, :].astype(F32), axis=0, keepdims=True) * (1.0 / L)
             for n in range(NB)], axis=0)
        kmm = jnp.concatenate([jnp.where(head_lanes[hh], km, 0.0) for hh in range(HEADS_PER_STEP)],
                              axis=0)
        km_hi = kmm.astype(BF16)
        return km_hi, (kmm - km_hi.astype(F32)).astype(BF16)

    shift, span = [], []
    for hh in range(HEADS_PER_STEP):
        head = hp * HEADS_PER_STEP + hh
        q_sq = jnp.max(norm_ref[:, pl.ds(head, 1), :], axis=0)[:, :1]
        k_sq = jnp.max(norm_ref[:, pl.ds(MOBA_HEADS + head, 1), :], axis=0)[:, :1]
        reach = jnp.sqrt(q_sq * k_sq) * BOUND_SLACK
        shift.append(reach + bmax[hh])
        span.append(2.0 * reach + (bmax[hh] - bmin[hh]))
    safe = jnp.max(functools.reduce(jnp.maximum, span)) <= SAFE_SPAN

    def block_select(k_means, gates, n):
        i, hh = bodies[n]
        if i not in gates:
            q_t = qt_scr[HEADS_PER_STEP * LANES:, i * L:(i + 1) * L]
            both = jnp.dot(jnp.concatenate(k_means, axis=0), q_t, preferred_element_type=F32)
            gates[i] = both[:HEADS_PER_STEP * NB] + both[HEADS_PER_STEP * NB:]
        g = gates[i][hh * NB:(hh + 1) * NB]
        cnt = jnp.zeros((NB, L), jnp.int32)
        for m in range(i):
            gm = g[m:m + 1, :]
            beats = (gm > g) | ((gm == g) & (m < row))
            cnt = cnt + jnp.where(beats, 1, 0)
        return cnt < MOBA_TOPK

    def masked_q(n):
        i, hh = bodies[n]
        return qt_scr[hh * LANES:(hh + 1) * LANES, i * L:(i + 1) * L]

    def pv_stage(outs, n, p_ref):
        i, hh = bodies[n]
        nk = (i + 1) * L
        acc = jnp.dot(vt_scr[hh * VT_ROWS:(hh + 1) * VT_ROWS, 0:nk], p_ref[0:nk, :],
                      preferred_element_type=F32)
        outs[n] = acc[:MOBA_HEAD_DIM] / acc[MOBA_HEAD_DIM:MOBA_HEAD_DIM + 1]
        if hh == HEADS_PER_STEP - 1:
            pair = [outs.pop(n - HEADS_PER_STEP + 1 + h) for h in range(HEADS_PER_STEP)]
            o_ref[0, i * L:(i + 1) * L, :] = jnp.concatenate(pair, axis=0).T

    def exact_path():
        gates, masks, qms, maxes, mrows, outs = {}, {}, {}, {}, {}, {}
        _cast_slabs(cast_src, cast_dst)
        k_means = stage_operands()

        def begin_body(n):
            i, hh = bodies[n]
            qms[n] = masked_q(n)
            if i > MOBA_TOPK:
                sel = block_select(k_means, gates, n)
                masks[n] = (jnp.where(sel, cfar[hh], NEG),
                            jnp.where(sel, 0.0, NEG))

        def score_tile(n, j):
            i, hh = bodies[n]
            t = jnp.dot(k_ref[0, j * L:(j + 1) * L, :], qms[n],
                        preferred_element_type=F32)
            if j == i:
                t = t + bias_ref[hh, 0]
            elif j == i - 1:
                t = t + bias_ref[hh, 1]
                if i > MOBA_TOPK:
                    t = t + masks[n][1][j:j + 1, :]
            elif i > MOBA_TOPK:
                t = t + masks[n][0][j:j + 1, :]
            else:
                t = t + cfar[hh]
            t_scr[n % T_SLOTS][j * L:(j + 1) * L, :] = t
            tm = jnp.max(t.reshape(L // 8, 8, L), axis=0)
            maxes[n] = tm if j == 0 else jnp.maximum(maxes[n], tm)
            if j == i:
                mrows[n] = jnp.max(maxes.pop(n), axis=0, keepdims=True)
                qms.pop(n)
                masks.pop(n, None)

        def exp_tile(n, j):
            t_ref, p_ref = t_scr[n % T_SLOTS], p_scr[n % T_SLOTS]
            p_ref[j * L:(j + 1) * L, :] = jnp.exp2(t_ref[j * L:(j + 1) * L, :] - mrows[n]).astype(BF16)

        for n0 in range(min(LOOKAHEAD, nb)):
            begin_body(n0)
            for j in range(bodies[n0][0] + 1):
                score_tile(n0, j)
        for n in range(nb):
            n_exp = bodies[n][0] + 1
            ahead = n + LOOKAHEAD
            n_score = bodies[ahead][0] + 1 if ahead < nb else 0
            if n_score:
                begin_body(ahead)
            for j in range(max(n_exp, n_score)):
                if j < n_score:
                    score_tile(ahead, j)
                if j < n_exp:
                    exp_tile(n, j)
            mrows.pop(n)
            pv_stage(outs, n, p_scr[n % T_SLOTS])

    def fast_path():
        gates, outs = {}, {}
        _cast_slabs(cast_src, cast_dst)
        k_means = stage_operands()
        shifted = {(hh, d): bias_ref[hh, d] - shift[hh]
                   for hh in range(HEADS_PER_STEP) for d in range(2)}

        def prob_tiles(n):
            i, hh = bodies[n]
            qm = masked_q(n)
            s_row = shift[hh]
            if i > MOBA_TOPK:
                sel = block_select(k_means, gates, n)
                m_far = jnp.where(sel, cfar[hh], NEG) - s_row
                m_prev = jnp.where(sel, 0.0, NEG)
            for j in range(i + 1):
                e = jnp.dot(k_ref[0, j * L:(j + 1) * L, :], qm,
                            preferred_element_type=F32)
                if j == i:
                    e = e + shifted[hh, 0]
                elif j == i - 1:
                    e = e + shifted[hh, 1]
                    if i > MOBA_TOPK:
                        e = e + m_prev[j:j + 1, :]
                else:
                    e = e + (m_far[j:j + 1, :] if i > MOBA_TOPK else cfar[hh] - s_row)
                p_scr[n % P_SLOTS][j * L:(j + 1) * L, :] = jnp.exp2(e).astype(BF16)

        for n0 in range(min(FAST_LOOKAHEAD, nb)):
            prob_tiles(n0)
        for n in range(nb):
            if n + FAST_LOOKAHEAD < nb:
                prob_tiles(n + FAST_LOOKAHEAD)
            pv_stage(outs, n, p_scr[n % P_SLOTS])

    pl.when(safe)(fast_path)
    pl.when(jnp.logical_not(safe))(exact_path)


def _moba(qkv3, norms, bias_t, bias_stats, cast_weights):
    B, S, _ = qkv3.shape
    L = MOBA_BLOCK
    n_hp = MOBA_HEADS // HEADS_PER_STEP
    blk = lambda off: pl.BlockSpec((1, S, LANES), lambda hp, b: (b, 0, off + hp))
    c_in, c_out, c_shapes = _cast_specs(cast_weights, B * n_hp, lambda hp, b: hp * B + b)
    outs = pl.pallas_call(
        functools.partial(_moba_kernel, n_cast=len(cast_weights)),
        grid=(n_hp, B),
        in_specs=[
            blk(0), blk(n_hp), blk(2 * n_hp),
            pl.BlockSpec((norms.shape[0] // B,) + norms.shape[1:], lambda hp, b: (b, 0, 0)),
            pl.BlockSpec((HEADS_PER_STEP, 2, L, L), lambda hp, b: (hp, 0, 0, 0)),
            pl.BlockSpec((HEADS_PER_STEP, STAT_ROWS, LANES), lambda hp, b: (hp, 0, 0)),
        ] + c_in,
        out_specs=[pl.BlockSpec((1, S, LANES), lambda hp, b: (b, 0, hp))] + c_out,
        out_shape=[jax.ShapeDtypeStruct((B, S, D_MOBA), F32)] + c_shapes,
        scratch_shapes=[
            pltpu.VMEM((HEADS_PER_STEP * VT_ROWS, S), BF16),
            pltpu.VMEM(((HEADS_PER_STEP + 1) * LANES, S), BF16),
        ] + [pltpu.VMEM((S, L), F32)] * T_SLOTS + [pltpu.VMEM((S, L), BF16)] * P_SLOTS,
        compiler_params=pltpu.CompilerParams(
            dimension_semantics=("arbitrary", "arbitrary"), vmem_limit_bytes=VMEM_LIMIT),
        name="moba",
    )(qkv3, qkv3, qkv3, norms, bias_t, bias_stats, *cast_weights)
    return outs[0], outs[1:]


def _ffn_kernel(x_ref, yb_ref, ya_ref, gb_ref, wo_ref, gf_ref, wg_ref, wu_ref, wd_ref, gl_ref,
                o_ref, a_scr, *, final_norm):
    half = x_ref.shape[0] // 2
    d_ff = wg_ref.shape[1]
    st = {}

    def attn_norm(r):
        yb = yb_ref[r:r + half, :]
        ms = jnp.mean(yb * yb, axis=-1, keepdims=True)
        st["ybn", r] = (yb * lax.rsqrt(ms + EPS) * gb_ref[...]).astype(BF16)

    def out_proj(r):
        y = jnp.concatenate([st.pop(("ybn", r)), ya_ref[r:r + half, :]], axis=1)
        st["x1", r] = x_ref[r:r + half, :] + jnp.dot(y, wo_ref[...], preferred_element_type=F32)

    def ffn_norm(r):
        x1 = st["x1", r]
        ms = jnp.mean(x1 * x1, axis=-1, keepdims=True)
        st["h", r] = (x1 * lax.rsqrt(ms + EPS) * gf_ref[...]).astype(BF16)

    def gate_up(r):
        h = st.pop(("h", r))
        for c0 in range(0, d_ff, FFN_COLS):
            c1 = min(c0 + FFN_COLS, d_ff)
            g = jnp.dot(h, wg_ref[:, c0:c1], preferred_element_type=F32)
            u = jnp.dot(h, wu_ref[:, c0:c1], preferred_element_type=F32)
            a_scr[r:r + half, c0:c1] = (jax.nn.silu(g) * u).astype(BF16)

    def down(r):
        x2 = st.pop(("x1", r)) + jnp.dot(a_scr[r:r + half, :], wd_ref[...], preferred_element_type=F32)
        if final_norm:
            ms = jnp.mean(x2 * x2, axis=-1, keepdims=True)
            x2 = x2 * lax.rsqrt(ms + EPS) * gl_ref[...]
        o_ref[r:r + half, :] = x2

    stages = [attn_norm, out_proj, ffn_norm, gate_up, down]
    for k in range(len(stages) + 1):
        if k < len(stages):
            stages[k](0)
        if k >= 1:
            stages[k - 1](half)


def _ffn(x2, yb, ya, gb, wo, gf, wg, wu, wd, gl, tm, final_norm):
    T, D = x2.shape
    d_ff = wg.shape[1]
    const = lambda shape: pl.BlockSpec(shape, lambda i: (0, 0), pipeline_mode=pl.Buffered(1))
    return pl.pallas_call(
        functools.partial(_ffn_kernel, final_norm=final_norm),
        grid=(T // tm,),
        in_specs=[
            pl.BlockSpec((tm, D), lambda i: (i, 0)),
            pl.BlockSpec((tm, D_MOBA), lambda i: (i, 0)),
            pl.BlockSpec((tm, D_GMLP), lambda i: (i, 0)),
            pl.BlockSpec((1, D_MOBA), lambda i: (0, 0)),
            const((D_MOBA + D_GMLP, D)),
            pl.BlockSpec((1, D), lambda i: (0, 0)),
            const((D, d_ff)), const((D, d_ff)), const((d_ff, D)),
            pl.BlockSpec((1, D), lambda i: (0, 0)),
        ],
        out_specs=pl.BlockSpec((tm, D), lambda i: (i, 0)),
        out_shape=jax.ShapeDtypeStruct((T, D), F32),
        scratch_shapes=[pltpu.VMEM((tm, d_ff), BF16)],
        compiler_params=pltpu.CompilerParams(
            dimension_semantics=("arbitrary",), vmem_limit_bytes=VMEM_LIMIT),
        name="outproj_ffn",
    )(x2, yb, ya, gb, wo, gf, wg, wu, wd, gl)


def kernel(x, rel_bias, norm_mix, w_in, gmlp_ln_g, gmlp_ln_b, w_spatial, b_spatial, out_norm_b,
           out_norm_a, w_out, norm_ffn, w_gate, w_up, w_down, norm_final):
    B, S, D = x.shape
    depth = w_in.shape[0]
    assert S % MOBA_BLOCK == 0 and S // MOBA_BLOCK > MOBA_TOPK
    far = _t5_bucket_np(np.arange(MOBA_BLOCK + 1, S + MOBA_BLOCK))
    assert (far == far[0]).all()
    far_bucket = int(far[0])

    assert depth == 1, "weight casts ride along the single layer's kernels"
    T = B * S
    x2 = x.reshape(T, D)
    row = lambda v: v.reshape(1, -1)
    bias_t, bias_stats, (w_in_bf,) = _bias_tiles(rel_bias, far_bucket, [w_in[0]])
    qkv, ya, norms = _inproj(x2, row(norm_mix[0]), w_in_bf, row(gmlp_ln_g[0]), row(gmlp_ln_b[0]),
                             w_spatial[0], b_spatial[0], row(out_norm_a[0]), PROJ_ROWS)
    yb, (wo_bf, wg_bf, wu_bf, wd_bf) = _moba(qkv.reshape(B, S, -1), norms, bias_t, bias_stats,
                                             [w_out[0], w_gate[0], w_up[0], w_down[0]])
    out = _ffn(x2, yb.reshape(T, D_MOBA), ya, row(out_norm_b[0]), wo_bf, row(norm_ffn[0]),
               wg_bf, wu_bf, wd_bf, row(norm_final), FFN_ROWS, True)
    return out.reshape(B, S, D)
```
